```python
import math
import jax, jax.numpy as jnp
from jax import lax
import numpy as np

D_MODEL = 4096
BATCH = 1
SEQ = 8192
DEPTH = 4

S5_WIDTH = D_MODEL // 2
S5_GROUP = 16
S5_GROUPS = S5_WIDTH // S5_GROUP
S5_STATE = 64
DT_MIN = 0.001
DT_MAX = 0.1
ATT_HEADS = 16
ATT_HEAD_DIM = 128
ATT_WIDTH = ATT_HEADS * ATT_HEAD_DIM
IDX_HEADS = 32
IDX_DIM = 64
TOPK_MAX = 256
Q_BLOCK = 128
REL_BUCKETS = 32
REL_MAX_DIST = 128
N_EXPERTS = 32
TOP_K_EXPERTS = 4
EXPERT_FF = 256
SWIGLU_LIMIT = 7.0
SWIGLU_ALPHA = 1.702
COND_RANK = 512
N_MOD = 6
DN_ALPHA = (2 * DEPTH) ** 0.25
DN_BETA = (8 * DEPTH) ** -0.25
LN_EPS = 1e-5
IN_SPLITS = (S5_WIDTH, ATT_WIDTH, ATT_WIDTH, ATT_WIDTH, IDX_HEADS * IDX_DIM, IDX_DIM, IDX_HEADS, D_MODEL, D_MODEL)
IN_WIDTH = sum(IN_SPLITS)

kernel_name = "hybrid_s5_dsa_moe_deepnorm"


def layer_norm(x, g, b):
    xf = x.astype(jnp.float32)
    mu = jnp.mean(xf, axis=-1, keepdims=True)
    var = jnp.mean(jnp.square(xf - mu), axis=-1, keepdims=True)
    return ((xf - mu) * lax.rsqrt(var + LN_EPS) * g.astype(jnp.float32) + b.astype(jnp.float32)).astype(x.dtype)


def t5_bucket(dist):
    exact = REL_BUCKETS // 2
    far = exact + (jnp.log(jnp.maximum(dist, 1).astype(jnp.float32) / exact)
                   / math.log(REL_MAX_DIST / exact) * (REL_BUCKETS - exact)).astype(jnp.int32)
    return jnp.where(dist < exact, dist, jnp.minimum(far, REL_BUCKETS - 1))


def _diag_combine(e1, e2):
    a1, b1 = e1
    a2, b2 = e2
    return a2 * a1, a2 * b1 + b2


def s5_branch(u, lam_re, lam_im, b_re, b_im, c_re, c_im, d, log_step, w_glu, b_glu):
    bsz, L, _ = u.shape
    f32 = jnp.float32
    uf = u.astype(f32).reshape(bsz, L, S5_GROUPS, S5_GROUP)
    lam = lax.complex(jnp.minimum(lam_re.astype(f32), -1e-4), lam_im.astype(f32))
    dt = jnp.exp(log_step.astype(f32))[:, None]
    lam_bar = jnp.exp(lam * dt)
    b_bar = ((lam_bar - 1.0) / lam)[:, :, None] * lax.complex(b_re.astype(f32), b_im.astype(f32))
    bu = jnp.einsum('blgh,gph->blgp', uf.astype(jnp.complex64), b_bar)
    a = jnp.broadcast_to(lam_bar, bu.shape)
    _, states = lax.associative_scan(_diag_combine, (a, bu), axis=1)
    c = lax.complex(c_re.astype(f32), c_im.astype(f32))
    y = jnp.einsum('blgp,ghp->blgh', states, c).real + d.astype(f32) * uf
    y = jax.nn.gelu(y.reshape(bsz, L, S5_WIDTH)).astype(u.dtype)
    return y * jax.nn.sigmoid(y @ w_glu + b_glu)


def dsa_branch(q, k, v, q_idx, k_idx, w_idx, rel_bias):
    bsz, L, _ = q.shape
    topk = min(TOPK_MAX, L // 4)
    q = q.reshape(bsz, L, ATT_HEADS, ATT_HEAD_DIM)
    k = k.reshape(bsz, L, ATT_HEADS, ATT_HEAD_DIM)
    v = v.reshape(bsz, L, ATT_HEADS, ATT_HEAD_DIM)
    q_idx = q_idx.reshape(bsz, L, IDX_HEADS, IDX_DIM)
    w_idx = w_idx * (IDX_HEADS ** -0.5)
    key_pos = jnp.arange(L, dtype=jnp.int32)

    def block(start):
        sl = lambda a: lax.dynamic_slice_in_dim(a, start, Q_BLOCK, axis=1)
        qb, qib, wb = sl(q), sl(q_idx), sl(w_idx)
        qpos = start + jnp.arange(Q_BLOCK, dtype=jnp.int32)
        s = jax.nn.relu(jnp.einsum('bqid,bkd->bqik', qib, k_idx) * (IDX_DIM ** -0.5))
        score = jnp.einsum('bqik,bqi->bqk', s, wb).astype(jnp.float32)
        causal = key_pos[None, :] <= qpos[:, None]
        score = jnp.where(causal[None], score, -jnp.inf)
        _, sel = lax.top_k(score, topk)
        k_sel = jax.vmap(lambda a, i: a[i])(k, sel)
        v_sel = jax.vmap(lambda a, i: a[i])(v, sel)
        dist = qpos[None, :, None] - sel
        valid = dist >= 0
        bias = jnp.moveaxis(rel_bias[t5_bucket(jnp.maximum(dist, 0))], -1, -2)
        logits = (jnp.einsum('bqhd,bqkhd->bqhk', qb, k_sel).astype(jnp.float32) * (ATT_HEAD_DIM ** -0.5)
                  + bias.astype(jnp.float32))
        logits = jnp.where(valid[:, :, None, :], logits, -1e30)
        p = jax.nn.softmax(logits, axis=-1).astype(v.dtype)
        return jnp.einsum('bqhk,bqkhd->bqhd', p, v_sel)

    starts = jnp.arange(L // Q_BLOCK, dtype=jnp.int32) * Q_BLOCK
    out = lax.map(block, starts)
    return jnp.moveaxis(out, 0, 1).reshape(bsz, L, ATT_WIDTH)


def token_mixer(h, w_in, w_a_out, w_b_out, w_o, w_glu, b_glu, lam_re, lam_im, b_re, b_im,
                c_re, c_im, d, log_step, rel_bias):
    proj = h @ w_in
    offs = np.cumsum(IN_SPLITS)[:-1].tolist()
    u, q, k, v, q_idx, k_idx, w_idx, g_a, g_b = jnp.split(proj, offs, axis=-1)
    y_a = s5_branch(u, lam_re, lam_im, b_re, b_im, c_re, c_im, d, log_step, w_glu, b_glu) @ w_a_out
    y_b = dsa_branch(q, k, v, q_idx, k_idx, w_idx, rel_bias) @ w_b_out
    merged = jax.nn.sigmoid(g_a) * y_a + jax.nn.sigmoid(g_b) * y_b
    return merged @ w_o


def moe(h, w_router, b_router, w_gu, b_gu, w_dn, b_dn):
    logits = (h @ w_router + b_router).astype(jnp.float32)
    top_v, top_i = lax.top_k(logits, TOP_K_EXPERTS)
    top_w = jax.nn.softmax(top_v, axis=-1)
    gates = jnp.sum(jax.nn.one_hot(top_i, N_EXPERTS, dtype=jnp.float32) * top_w[..., None], axis=-2).astype(h.dtype)
    gu = jnp.einsum('bld,edf->blef', h, w_gu) + b_gu
    gate = jnp.minimum(gu[..., :EXPERT_FF], SWIGLU_LIMIT)
    up = jnp.clip(gu[..., EXPERT_FF:], -SWIGLU_LIMIT, SWIGLU_LIMIT)
    act = (up + 1.0) * gate * jax.nn.sigmoid(SWIGLU_ALPHA * gate)
    return jnp.einsum('blef,efd->bld', act * gates[..., None], w_dn) + gates @ b_dn


def setup_inputs(seed: int = 0) -> dict:
    key = jax.random.key(seed)
    ks = jax.random.split(key, 40)
    f32 = jnp.float32
    nrm = lambda k, shape, scale: jax.random.normal(k, shape, f32) * scale
    n = jnp.arange(S5_STATE, dtype=f32)
    G, P, H = S5_GROUPS, S5_STATE, S5_GROUP
    return {
        'x': nrm(ks[0], (BATCH, SEQ, D_MODEL), 1.0),
        'c': nrm(ks[1], (BATCH, D_MODEL), 1.0),
        'w_in': nrm(ks[2], (DEPTH, D_MODEL, IN_WIDTH), D_MODEL ** -0.5),
        'w_a_out': nrm(ks[3], (DEPTH, S5_WIDTH, D_MODEL), DN_BETA * S5_WIDTH ** -0.5),
        'w_b_out': nrm(ks[4], (DEPTH, ATT_WIDTH, D_MODEL), DN_BETA * ATT_WIDTH ** -0.5),
        'w_o': nrm(ks[5], (DEPTH, D_MODEL, D_MODEL), DN_BETA * D_MODEL ** -0.5),
        'w_glu': nrm(ks[6], (DEPTH, S5_WIDTH, S5_WIDTH), S5_WIDTH ** -0.5),
        'b_glu': nrm(ks[7], (DEPTH, S5_WIDTH), 0.01),
        's5_lam_re': -0.5 + nrm(ks[8], (DEPTH, G, P), 0.01),
        's5_lam_im': math.pi * n + nrm(ks[9], (DEPTH, G, P), 0.01),
        's5_b_re': nrm(ks[10], (DEPTH, G, P, H), (2 * H) ** -0.5),
        's5_b_im': nrm(ks[11], (DEPTH, G, P, H), (2 * H) ** -0.5),
        's5_c_re': nrm(ks[12], (DEPTH, G, H, P), (2 * P) ** -0.5),
        's5_c_im': nrm(ks[13], (DEPTH, G, H, P), (2 * P) ** -0.5),
        's5_d': nrm(ks[14], (DEPTH, G, H), 1.0),
        's5_log_step': jax.random.uniform(ks[15], (DEPTH, G), f32, math.log(DT_MIN), math.log(DT_MAX)),
        'rel_bias': nrm(ks[16], (REL_BUCKETS, ATT_HEADS), 0.5),
        'w_router': nrm(ks[17], (DEPTH, D_MODEL, N_EXPERTS), D_MODEL ** -0.5),
        'b_router': nrm(ks[18], (DEPTH, N_EXPERTS), 0.01),
        'w_gu': nrm(ks[19], (DEPTH, N_EXPERTS, D_MODEL, 2 * EXPERT_FF), D_MODEL ** -0.5),
        'b_gu': nrm(ks[20], (DEPTH, N_EXPERTS, 2 * EXPERT_FF), 0.01),
        'w_dn': nrm(ks[21], (DEPTH, N_EXPERTS, EXPERT_FF, D_MODEL), DN_BETA * EXPERT_FF ** -0.5),
        'b_dn': nrm(ks[22], (DEPTH, N_EXPERTS, D_MODEL), 0.01),
        'w_c1': nrm(ks[23], (D_MODEL, COND_RANK), D_MODEL ** -0.5),
        'b_c1': nrm(ks[24], (COND_RANK,), 0.01),
        'w_c2': nrm(ks[25], (COND_RANK, N_MOD * D_MODEL), 0.1 * COND_RANK ** -0.5),
        'b_c2': nrm(ks[26], (N_MOD * D_MODEL,), 0.01),
        'mod_table': nrm(ks[27], (DEPTH, N_MOD, D_MODEL), 0.1),
        'ln_g': 1.0 + nrm(ks[28], (DEPTH, 2, D_MODEL), 0.02),
        'ln_b': nrm(ks[29], (DEPTH, 2, D_MODEL), 0.02),
    }


def reference(x, c, w_in, w_a_out, w_b_out, w_o, w_glu, b_glu, s5_lam_re, s5_lam_im, s5_b_re, s5_b_im,
              s5_c_re, s5_c_im, s5_d, s5_log_step, rel_bias, w_router, b_router, w_gu, b_gu, w_dn, b_dn,
              w_c1, b_c1, w_c2, b_c2, mod_table, ln_g, ln_b):
    bsz = c.shape[0]
    mod_shared = (jax.nn.silu(jax.nn.silu(c) @ w_c1 + b_c1) @ w_c2 + b_c2).reshape(bsz, N_MOD, D_MODEL)
    for l in range(DEPTH):
        mod = mod_shared + mod_table[l]
        sh1, sc1, g1, sh2, sc2, g2 = (mod[:, i, None, :] for i in range(N_MOD))
        h = x * (1.0 + sc1) + sh1
        y = token_mixer(h, w_in[l], w_a_out[l], w_b_out[l], w_o[l], w_glu[l], b_glu[l],
                        s5_lam_re[l], s5_lam_im[l], s5_b_re[l], s5_b_im[l], s5_c_re[l], s5_c_im[l],
                        s5_d[l], s5_log_step[l], rel_bias)
        x = layer_norm(DN_ALPHA * x + (1.0 + g1) * y, ln_g[l, 0], ln_b[l, 0])
        h = x * (1.0 + sc2) + sh2
        y = moe(h, w_router[l], b_router[l], w_gu[l], b_gu[l], w_dn[l], b_dn[l])
        x = layer_norm(DN_ALPHA * x + (1.0 + g2) * y, ln_g[l, 1], ln_b[l, 1])
    return x
```

```python
import functools
import math

import numpy as np
import jax
import jax.numpy as jnp
from jax import lax
from jax.experimental import pallas as pl
from jax.experimental.pallas import tpu as pltpu

F32 = jnp.float32
BF16 = jnp.bfloat16
I32 = jnp.int32

ATT_HEAD_DIM = 128
IDX_HEADS = 32
IDX_DIM = 64
TOPK_MAX = 256
REL_MAX_DIST = 128
TOP_K_EXPERTS = 4
SWIGLU_LIMIT = 7.0
SWIGLU_ALPHA = 1.702
N_MOD = 6
LN_EPS = 1e-5

LANES = 128
VMEM_LIMIT_BYTES = 56 * 1024 * 1024
S5_CHUNK = 16
NEG_BIG = -1e30
INT_MIN = -(2 ** 31)


def _cp(*sem):
    return pltpu.CompilerParams(dimension_semantics=sem, vmem_limit_bytes=VMEM_LIMIT_BYTES)


def _tile(n, t):
    t = min(n, t)
    while n % t:
        t //= 2
    return t


def _cond1_kernel(c_ref, w_ref, b_ref, o_ref):
    a = jax.nn.silu(c_ref[...]).astype(BF16)
    t = jnp.dot(a, w_ref[...].astype(BF16), preferred_element_type=F32) + b_ref[...]
    o_ref[...] = jax.nn.silu(t)


def _cond2_kernel(t_ref, w_ref, b_ref, tab_ref, o_ref):
    acc = jnp.dot(t_ref[...].astype(BF16), w_ref[...].astype(BF16), preferred_element_type=F32)
    o_ref[...] = acc[0:1, :] + b_ref[...] + tab_ref[...]


def _conditioning(c, w_c1, b_c1, w_c2, b_c2, mod_table):
    d = c.shape[1]
    r = w_c1.shape[1]
    depth = mod_table.shape[0]
    n = w_c2.shape[1]
    c8 = jnp.broadcast_to(c, (8, d))
    t = pl.pallas_call(
        _cond1_kernel, out_shape=jax.ShapeDtypeStruct((8, r), F32), name="cond1",
        compiler_params=_cp(),
    )(c8, w_c1, b_c1.reshape(1, r))
    tn = _tile(n, 2048)
    return pl.pallas_call(
        _cond2_kernel, grid=(n // tn,),
        in_specs=[pl.BlockSpec((8, r), lambda j: (0, 0)),
                  pl.BlockSpec((r, tn), lambda j: (0, j)),
                  pl.BlockSpec((1, tn), lambda j: (0, j)),
                  pl.BlockSpec((depth, tn), lambda j: (0, j))],
        out_specs=pl.BlockSpec((depth, tn), lambda j: (0, j)),
        out_shape=jax.ShapeDtypeStruct((depth, n), F32), name="cond2",
        compiler_params=_cp("parallel"),
    )(t, w_c2, b_c2.reshape(1, n), mod_table.reshape(depth, n))


def _modulate_kernel(x_ref, mod_ref, h_ref, *, sc_row, sh_row):
    sc = mod_ref[sc_row:sc_row + 1, :]
    sh = mod_ref[sh_row:sh_row + 1, :]
    h_ref[...] = (x_ref[...] * (1.0 + sc) + sh).astype(h_ref.dtype)


def _modulate(x, mod_l, sc_row, sh_row, tm=256):
    m, d = x.shape
    tm = _tile(m, tm)
    return pl.pallas_call(
        functools.partial(_modulate_kernel, sc_row=sc_row, sh_row=sh_row),
        grid=(m // tm,),
        in_specs=[pl.BlockSpec((tm, d), lambda i: (i, 0)),
                  pl.BlockSpec((N_MOD, d), lambda i: (0, 0))],
        out_specs=pl.BlockSpec((tm, d), lambda i: (i, 0)),
        out_shape=jax.ShapeDtypeStruct((m, d), BF16), name="modulate",
        compiler_params=_cp("parallel"),
    )(x, mod_l)


def _ln_kernel(x_ref, y_ref, mod_ref, g_ref, b_ref, nmod_ref, xo_ref, *h_refs,
               alpha, gate_row, ln_row, sc_row, sh_row):
    gate = mod_ref[gate_row:gate_row + 1, :]
    u = alpha * x_ref[...] + (1.0 + gate) * y_ref[...]
    mu = jnp.mean(u, axis=-1, keepdims=True)
    uc = u - mu
    var = jnp.mean(uc * uc, axis=-1, keepdims=True)
    xn = uc * lax.rsqrt(var + LN_EPS) * g_ref[ln_row:ln_row + 1, :] + b_ref[ln_row:ln_row + 1, :]
    xo_ref[...] = xn
    if h_refs:
        sc = nmod_ref[sc_row:sc_row + 1, :]
        sh = nmod_ref[sh_row:sh_row + 1, :]
        h_refs[0][...] = (xn * (1.0 + sc) + sh).astype(BF16)


def _ln_modulate(x, y, mod_l, ln_g_l, ln_b_l, next_mod, *, alpha, gate_row, ln_row, sc_row, sh_row,
                 want_h, tm=256):
    m, d = x.shape
    tm = _tile(m, tm)
    row = pl.BlockSpec((tm, d), lambda i: (i, 0))
    out_shape = [jax.ShapeDtypeStruct((m, d), F32)]
    out_specs = [row]
    if want_h:
        out_shape.append(jax.ShapeDtypeStruct((m, d), BF16))
        out_specs.append(row)
    res = pl.pallas_call(
        functools.partial(_ln_kernel, alpha=alpha, gate_row=gate_row, ln_row=ln_row,
                          sc_row=sc_row, sh_row=sh_row),
        grid=(m // tm,),
        in_specs=[row, row,
                  pl.BlockSpec((N_MOD, d), lambda i: (0, 0)),
                  pl.BlockSpec((2, d), lambda i: (0, 0)),
                  pl.BlockSpec((2, d), lambda i: (0, 0)),
                  pl.BlockSpec((N_MOD, d), lambda i: (0, 0))],
        out_specs=out_specs, out_shape=out_shape, name="ln_modulate",
        compiler_params=_cp("parallel"),
    )(x, y, mod_l, ln_g_l, ln_b_l, next_mod)
    return (res[0], res[1]) if want_h else (res[0], None)


def _mm_kernel(a_ref, w_ref, o_ref):
    o_ref[...] = jnp.dot(a_ref[...], w_ref[...], preferred_element_type=F32).astype(o_ref.dtype)


def _matmul(a, w, *, out_dtype, col_off=0, n_cols=None, tm=1024, tn=512, name="matmul"):
    m, k = a.shape
    n_cols = w.shape[1] - col_off if n_cols is None else n_cols
    tm = _tile(m, tm)
    tn = math.gcd(_tile(n_cols, tn), col_off) if col_off else _tile(n_cols, tn)
    off = col_off // tn
    return pl.pallas_call(
        _mm_kernel, grid=(m // tm, n_cols // tn),
        in_specs=[pl.BlockSpec((tm, k), lambda i, j: (i, 0)),
                  pl.BlockSpec((k, tn), lambda i, j: (0, j + off))],
        out_specs=pl.BlockSpec((tm, tn), lambda i, j: (i, j)),
        out_shape=jax.ShapeDtypeStruct((m, n_cols), out_dtype), name=name,
        compiler_params=_cp("parallel", "arbitrary"),
    )(a, w)


def _glu_kernel(y_ref, yt_ref, w_ref, b_ref, o_ref, ybf_ref):
    @pl.when(pl.program_id(1) == 0)
    def _():
        ybf_ref[...] = y_ref[...].astype(BF16)

    z = jnp.dot(ybf_ref[...], w_ref[...], preferred_element_type=F32) + b_ref[...]
    o_ref[...] = (yt_ref[...] * jax.nn.sigmoid(z)).astype(o_ref.dtype)


def _glu(y, w, b, tm=512, tn=512):
    m, k = y.shape
    n = w.shape[1]
    tm = _tile(m, tm)
    tn = _tile(n, tn)
    return pl.pallas_call(
        _glu_kernel, grid=(m // tm, n // tn),
        in_specs=[pl.BlockSpec((tm, k), lambda i, j: (i, 0)),
                  pl.BlockSpec((tm, tn), lambda i, j: (i, j)),
                  pl.BlockSpec((k, tn), lambda i, j: (0, j)),
                  pl.BlockSpec((1, tn), lambda i, j: (0, j))],
        out_specs=pl.BlockSpec((tm, tn), lambda i, j: (i, j)),
        out_shape=jax.ShapeDtypeStruct((m, n), BF16),
        scratch_shapes=[pltpu.VMEM((tm, k), BF16)], name="s5_glu",
        compiler_params=_cp("parallel", "arbitrary"),
    )(y, y, w, b.reshape(1, n))


def _merge_kernel(a1_ref, a2_ref, w1_ref, w2_ref, ga_ref, gb_ref, o_ref):
    ya = jnp.dot(a1_ref[...], w1_ref[...], preferred_element_type=F32)
    yb = jnp.dot(a2_ref[...], w2_ref[...], preferred_element_type=F32)
    o_ref[...] = (jax.nn.sigmoid(ga_ref[...]) * ya + jax.nn.sigmoid(gb_ref[...]) * yb).astype(o_ref.dtype)


def _merge(a1, a2, w1, w2, gates, tm=1024, tn=512):
    m, k1 = a1.shape
    k2 = a2.shape[1]
    n = w1.shape[1]
    tm = _tile(m, tm)
    tn = _tile(n, tn)
    nb = n // tn
    return pl.pallas_call(
        _merge_kernel, grid=(m // tm, nb),
        in_specs=[pl.BlockSpec((tm, k1), lambda i, j: (i, 0)),
                  pl.BlockSpec((tm, k2), lambda i, j: (i, 0)),
                  pl.BlockSpec((k1, tn), lambda i, j: (0, j)),
                  pl.BlockSpec((k2, tn), lambda i, j: (0, j)),
                  pl.BlockSpec((tm, tn), lambda i, j: (i, j)),
                  pl.BlockSpec((tm, tn), lambda i, j: (i, j + nb))],
        out_specs=pl.BlockSpec((tm, tn), lambda i, j: (i, j)),
        out_shape=jax.ShapeDtypeStruct((m, n), BF16), name="branch_merge",
        compiler_params=_cp("parallel", "arbitrary"),
    )(a1, a2, w1, w2, gates, gates)


def _s5_discretize(lam_re, lam_im, b_re, b_im, c_re, c_im, log_step, t_chunk):
    hp = lax.Precision.HIGHEST
    g, p, h = b_re.shape
    lam = lax.complex(jnp.minimum(lam_re, -1e-4), lam_im)
    dt = jnp.exp(log_step)[:, None]
    lam_bar = jnp.exp(lam * dt)
    b_bar = ((lam_bar - 1.0) / lam)[:, :, None] * lax.complex(b_re, b_im)
    c = lax.complex(c_re, c_im)
    tau = jnp.arange(t_chunk + 1, dtype=F32)
    pw = jnp.exp((lam * dt)[:, None, :] * tau[None, :, None].astype(jnp.complex64))
    cp = c[:, None, :, :] * pw[:, :, None, :]
    kern = jnp.einsum('gtop,gpi->gtoi', cp[:, :t_chunk], b_bar, precision=hp).real
    s_idx = np.arange(t_chunk)[:, None]
    t_idx = np.arange(t_chunk)[None, :]
    lag = t_idx - s_idx
    kt = kern[:, np.clip(lag, 0, None)]
    kt = jnp.where((lag >= 0)[None, :, :, None, None], kt, 0.0)
    m_mat = jnp.transpose(kt, (0, 1, 4, 2, 3)).reshape(g, t_chunk * h, t_chunk * h)
    wc = pw[:, t_chunk - 1 - np.arange(t_chunk)][:, :, :, None] * b_bar[:, None, :, :]
    wc = jnp.transpose(wc, (0, 1, 3, 2)).reshape(g, t_chunk * h, p)
    w_mat = jnp.concatenate([wc.real, wc.imag], axis=-1)
    cl = jnp.transpose(cp[:, 1:t_chunk + 1], (0, 3, 1, 2)).reshape(g, p, t_chunk * h)
    v_mat = jnp.concatenate([cl.real, -cl.imag], axis=1)
    a = pw[:, t_chunk]
    a1 = jnp.concatenate([a.real, a.real], axis=-1)
    a2 = jnp.concatenate([-a.imag, a.imag], axis=-1)
    return m_mat.astype(BF16), w_mat.astype(BF16), v_mat.astype(BF16), a1, a2


def _s5_state_in_kernel(u_ref, w_ref, x_ref, *, gb):
    for g in range(gb):
        x_ref[g] = jnp.dot(u_ref[g].astype(BF16), w_ref[g], preferred_element_type=F32)


def _s5_scan_kernel(x_ref, a1_ref, a2_ref, s_ref, *, half):
    a1 = a1_ref[...]
    a2 = a2_ref[...]
    nc = x_ref.shape[0]

    def body(c, carry):
        s_a, s_b = carry
        s_ref[c] = s_a
        x = x_ref[c]
        xs = pltpu.roll(x, half, axis=1)
        n_a = a1 * s_a + a2 * s_b + x
        n_b = a1 * s_b - a2 * s_a + xs
        return n_a, n_b

    z = jnp.zeros(a1.shape, F32)
    lax.fori_loop(0, nc, body, (z, z))


def _s5_out_kernel(u_ref, sp_ref, m_ref, v_ref, d_ref, y_ref, *, gb):
    for g in range(gb):
        u = u_ref[g]
        y = jnp.dot(u.astype(BF16), m_ref[g], preferred_element_type=F32)
        y = y + jnp.dot(sp_ref[g].astype(BF16), v_ref[g], preferred_element_type=F32)
        y = y + d_ref[g] * u
        y_ref[g] = jax.nn.gelu(y)


def _s5_branch(u, prm, d):
    m_mat, w_mat, v_mat, a1, a2 = prm
    L = u.shape[0]
    g, th, _ = m_mat.shape
    p2 = w_mat.shape[2]
    t = S5_CHUNK
    h = th // t
    nc = L // t
    gb = _tile(g, 8)
    ug = jnp.transpose(u.reshape(nc, t, g, h), (2, 0, 1, 3)).reshape(g, nc, th)
    x = pl.pallas_call(
        functools.partial(_s5_state_in_kernel, gb=gb), grid=(g // gb,),
        in_specs=[pl.BlockSpec((gb, nc, th), lambda i: (i, 0, 0)),
                  pl.BlockSpec((gb, th, p2), lambda i: (i, 0, 0))],
        out_specs=pl.BlockSpec((gb, nc, p2), lambda i: (i, 0, 0)),
        out_shape=jax.ShapeDtypeStruct((g, nc, p2), F32), name="s5_state_in",
        compiler_params=_cp("parallel"),
    )(ug, w_mat)
    xt = jnp.transpose(x, (1, 0, 2))
    gs = _tile(g, 32)
    sp = pl.pallas_call(
        functools.partial(_s5_scan_kernel, half=p2 // 2), grid=(g // gs,),
        in_specs=[pl.BlockSpec((nc, gs, p2), lambda i: (0, i, 0)),
                  pl.BlockSpec((gs, p2), lambda i: (i, 0)),
                  pl.BlockSpec((gs, p2), lambda i: (i, 0))],
        out_specs=pl.BlockSpec((nc, gs, p2), lambda i: (0, i, 0)),
        out_shape=jax.ShapeDtypeStruct((nc, g, p2), F32), name="s5_scan",
        compiler_params=_cp("parallel"),
    )(xt, a1, a2)
    spg = jnp.transpose(sp, (1, 0, 2))
    dd = jnp.tile(d, (1, t)).reshape(g, 1, th)
    y = pl.pallas_call(
        functools.partial(_s5_out_kernel, gb=gb), grid=(g // gb,),
        in_specs=[pl.BlockSpec((gb, nc, th), lambda i: (i, 0, 0)),
                  pl.BlockSpec((gb, nc, p2), lambda i: (i, 0, 0)),
                  pl.BlockSpec((gb, th, th), lambda i: (i, 0, 0)),
                  pl.BlockSpec((gb, p2, th), lambda i: (i, 0, 0)),
                  pl.BlockSpec((gb, 1, th), lambda i: (i, 0, 0))],
        out_specs=pl.BlockSpec((gb, nc, th), lambda i: (i, 0, 0)),
        out_shape=jax.ShapeDtypeStruct((g, nc, th), F32), name="s5_out",
        compiler_params=_cp("parallel"),
    )(ug, spg, m_mat, v_mat, dd)
    return jnp.transpose(y.reshape(g, nc, t, h), (1, 2, 0, 3)).reshape(L, g * h)


def _t5_bucket_starts(n_buckets):
    exact = n_buckets // 2
    dist = np.arange(0, 4 * REL_MAX_DIST).astype(np.int64)
    far = exact + (np.log(np.maximum(dist, 1).astype(np.float32) / np.float32(exact))
                   / np.float32(math.log(REL_MAX_DIST / exact)) * np.float32(n_buckets - exact)).astype(np.int64)
    bucket = np.where(dist < exact, dist, np.minimum(far, n_buckets - 1))
    return [int(np.argmax(bucket >= b)) for b in range(1, n_buckets)]


def _bias_tile_kernel(rb_ref, o_ref, *, tb, starts):
    delta = pl.program_id(0)
    head = pl.program_id(1)
    row = lax.broadcasted_iota(I32, (tb, tb), 0)
    col = lax.broadcasted_iota(I32, (tb, tb), 1)
    dist = delta * tb + row - col
    bias = jnp.full((tb, tb), rb_ref[0, head], F32)
    for b, start in enumerate(starts, start=1):
        bias = jnp.where(dist >= start, rb_ref[b, head], bias)
    o_ref[...] = jnp.where(dist >= 0, bias, NEG_BIG)


def _bias_tiles(rel_bias, tb):
    n_buckets, heads = rel_bias.shape
    starts = _t5_bucket_starts(n_buckets)
    assert 2 * tb - (tb - 1) >= starts[-1]
    return pl.pallas_call(
        functools.partial(_bias_tile_kernel, tb=tb, starts=starts), grid=(3, heads),
        in_specs=[pl.BlockSpec(memory_space=pltpu.SMEM)],
        out_specs=pl.BlockSpec((None, None, tb, tb), lambda dl, hd: (dl, hd, 0, 0)),
        out_shape=jax.ShapeDtypeStruct((3, heads, tb, tb), F32), name="rel_bias_tiles",
        compiler_params=_cp("parallel", "parallel"),
    )(rel_bias)


def _indexer_kernel(q_ref, ke_ref, ko_ref, w_ref, keys_ref, thr_ref, wb_ref, *, tq, ck, n_heads, topk, w_scale):
    i = pl.program_id(0)
    n_chunks = keys_ref.shape[1] // ck
    n_live = ((i + 1) * tq + ck - 1) // ck
    reps = ck // LANES

    wv = w_ref[...] * w_scale
    for h in range(n_heads):
        wb_ref[h] = jnp.broadcast_to(wv[:, h:h + 1], (tq, LANES))

    def score_chunk(c, carry):
        col = pl.multiple_of(c * ck, ck)
        k_even = ke_ref[:, pl.ds(col, ck)]
        k_odd = ko_ref[:, pl.ds(col, ck)]
        acc = jnp.zeros((tq, ck), F32)
        for h in range(n_heads):
            qp = q_ref[:, (h // 2) * LANES:(h // 2 + 1) * LANES]
            s = jnp.dot(qp, k_even if h % 2 == 0 else k_odd, preferred_element_type=F32)
            wbh = wb_ref[h]
            acc = acc + jnp.maximum(s, 0.0) * jnp.concatenate([wbh] * reps, axis=1)
        kpos = col + lax.broadcasted_iota(I32, (tq, ck), 1)
        qpos = i * tq + lax.broadcasted_iota(I32, (tq, ck), 0)
        score = jnp.where(kpos <= qpos, acc, -jnp.inf)
        bits = pltpu.bitcast(score, I32)
        keys_ref[:, pl.ds(col, ck)] = jnp.where(bits < 0, bits ^ 0x7FFFFFFF, bits)
        return carry

    lax.fori_loop(0, n_live, score_chunk, 0)

    def fill_chunk(c, carry):
        col = pl.multiple_of(c * ck, ck)
        keys_ref[:, pl.ds(col, ck)] = jnp.full((tq, ck), INT_MIN, I32)
        return carry

    lax.fori_loop(n_live, n_chunks, fill_chunk, 0)

    def count_ge(cand):
        def body(c, cnt):
            col = pl.multiple_of(c * ck, ck)
            blk = keys_ref[:, pl.ds(col, ck)]
            for r in range(reps):
                cnt = cnt + jnp.where(blk[:, r * LANES:(r + 1) * LANES] >= cand, 1.0, 0.0)
            return cnt
        cnt = lax.fori_loop(0, n_live, body, jnp.zeros((tq, LANES), F32))
        return jnp.sum(cnt, axis=1, keepdims=True)

    zero = jnp.zeros((tq, LANES), I32)
    thr = jnp.where(count_ge(zero) >= topk, zero, jnp.full((tq, LANES), INT_MIN, I32))

    def bit_step(it, thr):
        cand = thr | jnp.left_shift(jnp.int32(1), 30 - it)
        return jnp.where(count_ge(cand) >= topk, cand, thr)

    thr_ref[...] = lax.fori_loop(0, 31, bit_step, thr)


def _indexer(qkvi, q_col, k_even, k_odd, w_idx, topk, tq=128, ck=256):
    L = qkvi.shape[0]
    n_heads = w_idx.shape[1]
    qw = n_heads * IDX_DIM
    tq = _tile(L, tq)
    ck = _tile(L, ck)
    assert q_col % qw == 0
    w_scale = (n_heads ** -0.5) * (IDX_DIM ** -0.5)
    return pl.pallas_call(
        functools.partial(_indexer_kernel, tq=tq, ck=ck, n_heads=n_heads, topk=topk, w_scale=w_scale),
        grid=(L // tq,),
        in_specs=[pl.BlockSpec((tq, qw), lambda i: (i, q_col // qw)),
                  pl.BlockSpec((2 * IDX_DIM, L), lambda i: (0, 0)),
                  pl.BlockSpec((2 * IDX_DIM, L), lambda i: (0, 0)),
                  pl.BlockSpec((tq, n_heads), lambda i: (i, 0))],
        out_specs=[pl.BlockSpec((tq, L), lambda i: (i, 0)),
                   pl.BlockSpec((tq, LANES), lambda i: (i, 0))],
        out_shape=[jax.ShapeDtypeStruct((L, L), I32), jax.ShapeDtypeStruct((L, LANES), I32)],
        scratch_shapes=[pltpu.VMEM((n_heads, tq, LANES), F32)], name="dsa_indexer",
        compiler_params=_cp("parallel"),
    )(qkvi, k_even, k_odd, w_idx)


def _attn_kernel(qi_ref, kj_ref, q_ref, k_ref, v_ref, keys_ref, thr_ref, bias_ref, o_ref,
                 m_ref, l_ref, acc_ref, *, heads, dh, scale, reps):
    step = pl.program_id(0)
    qi = qi_ref[step]
    kj = kj_ref[step]

    @pl.when(kj == 0)
    def _():
        m_ref[...] = jnp.full(m_ref.shape, -jnp.inf, F32)
        l_ref[...] = jnp.zeros(l_ref.shape, F32)
        acc_ref[...] = jnp.zeros(acc_ref.shape, F32)

    thr = jnp.concatenate([thr_ref[...]] * reps, axis=1)
    mask_bias = jnp.where(keys_ref[...] >= thr, 0.0, NEG_BIG)
    for h in range(heads):
        hs = slice(h * dh, (h + 1) * dh)
        s = lax.dot_general(q_ref[:, hs], k_ref[:, hs], (((1,), (1,)), ((), ())),
                            preferred_element_type=F32)
        logits = s * scale + (mask_bias + bias_ref[h])
        m_prev = m_ref[h]
        m_new = jnp.maximum(m_prev, jnp.max(logits, axis=1, keepdims=True))
        alpha = jnp.exp(m_prev - m_new)
        p = jnp.exp(logits - jnp.concatenate([m_new] * reps, axis=1))
        l_ref[h] = alpha * l_ref[h] + jnp.sum(p, axis=1, keepdims=True)
        acc_ref[:, hs] = alpha * acc_ref[:, hs] + jnp.dot(p.astype(BF16), v_ref[:, hs],
                                                          preferred_element_type=F32)
        m_ref[h] = m_new

    @pl.when(kj == qi)
    def _():
        for h in range(heads):
            hs = slice(h * dh, (h + 1) * dh)
            o_ref[:, hs] = (acc_ref[:, hs] / l_ref[h]).astype(o_ref.dtype)


def _attention(qkvi, q_col, k_col, v_col, keys, thr, bias_tiles, tb):
    L = qkvi.shape[0]
    heads = bias_tiles.shape[1]
    dh = ATT_HEAD_DIM
    aw = heads * dh
    assert dh == LANES and q_col % aw == 0 and k_col % aw == 0 and v_col % aw == 0
    nb = L // tb
    pairs = [(i, j) for i in range(nb) for j in range(i + 1)]
    qi = jnp.asarray(np.array([p[0] for p in pairs], np.int32))
    kj = jnp.asarray(np.array([p[1] for p in pairs], np.int32))
    grid_spec = pltpu.PrefetchScalarGridSpec(
        num_scalar_prefetch=2, grid=(len(pairs),),
        in_specs=[pl.BlockSpec((tb, aw), lambda s, qi, kj: (qi[s], q_col // aw)),
                  pl.BlockSpec((tb, aw), lambda s, qi, kj: (kj[s], k_col // aw)),
                  pl.BlockSpec((tb, aw), lambda s, qi, kj: (kj[s], v_col // aw)),
                  pl.BlockSpec((tb, tb), lambda s, qi, kj: (qi[s], kj[s])),
                  pl.BlockSpec((tb, LANES), lambda s, qi, kj: (qi[s], 0)),
                  pl.BlockSpec((None, heads, tb, tb),
                               lambda s, qi, kj: (jnp.minimum(qi[s] - kj[s], 2), 0, 0, 0))],
        out_specs=pl.BlockSpec((tb, aw), lambda s, qi, kj: (qi[s], 0)),
        scratch_shapes=[pltpu.VMEM((heads, tb, LANES), F32),
                        pltpu.VMEM((heads, tb, LANES), F32),
                        pltpu.VMEM((tb, aw), F32)])
    return pl.pallas_call(
        functools.partial(_attn_kernel, heads=heads, dh=dh, scale=dh ** -0.5, reps=tb // LANES),
        grid_spec=grid_spec, out_shape=jax.ShapeDtypeStruct((L, aw), BF16), name="dsa_attention",
        compiler_params=_cp("arbitrary"),
    )(qi, kj, qkvi, qkvi, qkvi, keys, thr, bias_tiles)


def _router_kernel(h_ref, w_ref, b_ref, g_ref, *, n_exp, top_k):
    logits = jnp.dot(h_ref[...], w_ref[...], preferred_element_type=F32) + b_ref[...]
    lane = lax.broadcasted_iota(I32, logits.shape, 1).astype(F32)
    work = jnp.where(lane < n_exp, logits, -jnp.inf)
    gates = jnp.zeros(logits.shape, F32)
    denom = None
    top = None
    for r in range(top_k):
        m = jnp.max(work, axis=1, keepdims=True)
        first = jnp.min(jnp.where(work == m, lane, float(LANES)), axis=1, keepdims=True)
        sel = lane == first
        if r == 0:
            top = m
        e = jnp.exp(m - top)
        gates = gates + jnp.where(sel, e, 0.0)
        denom = e if denom is None else denom + e
        work = jnp.where(sel, -jnp.inf, work)
    g_ref[...] = gates / denom


def _router(h, w_pad, b_pad, n_exp, tm=512):
    m, d = h.shape
    tm = _tile(m, tm)
    return pl.pallas_call(
        functools.partial(_router_kernel, n_exp=n_exp, top_k=TOP_K_EXPERTS), grid=(m // tm,),
        in_specs=[pl.BlockSpec((tm, d), lambda i: (i, 0)),
                  pl.BlockSpec((d, LANES), lambda i: (0, 0)),
                  pl.BlockSpec((1, LANES), lambda i: (0, 0))],
        out_specs=pl.BlockSpec((tm, LANES), lambda i: (i, 0)),
        out_shape=jax.ShapeDtypeStruct((m, LANES), F32), name="moe_router",
        compiler_params=_cp("parallel"),
    )(h, w_pad, b_pad)


def _moe_kernel(h_ref, g_ref, wgu_ref, bgu_ref, wdn_ref, bdn_ref, o_ref, acc_ref, *, ff):
    e = pl.program_id(1)
    gates = g_ref[...]

    @pl.when(e == 0)
    def _():
        acc_ref[...] = jnp.dot(gates.astype(BF16), bdn_ref[...], preferred_element_type=F32)

    gu = jnp.dot(h_ref[...], wgu_ref[...], preferred_element_type=F32) + bgu_ref[...]
    gate = jnp.minimum(gu[:, :ff], SWIGLU_LIMIT)
    up = jnp.clip(gu[:, ff:], -SWIGLU_LIMIT, SWIGLU_LIMIT)
    act = (up + 1.0) * gate * jax.nn.sigmoid(SWIGLU_ALPHA * gate)
    lane = lax.broadcasted_iota(I32, gates.shape, 1)
    ge = jnp.sum(jnp.where(lane == e, gates, 0.0), axis=1, keepdims=True)
    acc_ref[...] += jnp.dot((act * ge).astype(BF16), wdn_ref[...], preferred_element_type=F32)

    @pl.when(e == pl.num_programs(1) - 1)
    def _():
        o_ref[...] = acc_ref[...]


def _moe_experts(h, gates, w_gu, b_gu, w_dn, b_dn_pad, tm=512):
    m, d = h.shape
    n_exp, _, ff2 = w_gu.shape
    tm = _tile(m, tm)
    return pl.pallas_call(
        functools.partial(_moe_kernel, ff=ff2 // 2), grid=(m // tm, n_exp),
        in_specs=[pl.BlockSpec((tm, d), lambda i, e: (i, 0)),
                  pl.BlockSpec((tm, LANES), lambda i, e: (i, 0)),
                  pl.BlockSpec((None, d, ff2), lambda i, e: (e, 0, 0)),
                  pl.BlockSpec((None, 1, ff2), lambda i, e: (e, 0, 0)),
                  pl.BlockSpec((None, ff2 // 2, d), lambda i, e: (e, 0, 0)),
                  pl.BlockSpec((LANES, d), lambda i, e: (0, 0))],
        out_specs=pl.BlockSpec((tm, d), lambda i, e: (i, 0)),
        out_shape=jax.ShapeDtypeStruct((m, d), F32),
        scratch_shapes=[pltpu.VMEM((tm, d), F32)], name="moe_experts",
        compiler_params=_cp("parallel", "arbitrary"),
    )(h, gates, w_gu, b_gu.reshape(n_exp, 1, ff2), w_dn, b_dn_pad)


def kernel(x, c, w_in, w_a_out, w_b_out, w_o, w_glu, b_glu, s5_lam_re, s5_lam_im, s5_b_re, s5_b_im, s5_c_re, s5_c_im, s5_d, s5_log_step, rel_bias, w_router, b_router, w_gu, b_gu, w_dn, b_dn, w_c1, b_c1, w_c2, b_c2, mod_table, ln_g, ln_b):
    bsz, L, d = x.shape
    assert bsz == 1 and c.shape[0] == 1
    depth = w_in.shape[0]
    g, p, h = s5_b_re.shape[1:]
    s5w = g * h
    heads = rel_bias.shape[1]
    aw = heads * ATT_HEAD_DIM
    iw = IDX_HEADS * IDX_DIM
    n_exp = w_router.shape[2]
    assert w_in.shape[2] == s5w + 3 * aw + iw + IDX_DIM + IDX_HEADS + 2 * d
    assert n_exp <= LANES and IDX_DIM + IDX_HEADS <= LANES and 2 * IDX_DIM == LANES
    o_q, o_k, o_v, o_qi = 0, aw, 2 * aw, 3 * aw
    o_ki = s5w + 3 * aw + iw
    o_gate = o_ki + IDX_DIM + IDX_HEADS
    topk = min(TOPK_MAX, L // 4)
    tb = _tile(L, 256)
    dn_alpha = (2 * depth) ** 0.25

    mod = _conditioning(c, w_c1, b_c1, w_c2, b_c2, mod_table).reshape(depth, N_MOD, d)
    bias_tiles = _bias_tiles(rel_bias, tb)

    xs = x.reshape(L, d)
    hcur = _modulate(xs, mod[0], sc_row=1, sh_row=0)
    for l in range(depth):
        w_l = w_in[l]
        w_main = w_l[:, :o_ki].astype(BF16)
        w_misc = jnp.pad(w_l[:, o_ki:o_gate], ((0, 0), (0, LANES - IDX_DIM - IDX_HEADS))).astype(BF16)
        w_gate = w_l[:, o_gate:].astype(BF16)
        u = _matmul(hcur, w_main, out_dtype=F32, col_off=0, n_cols=s5w, name="proj_u")
        qkvi = _matmul(hcur, w_main, out_dtype=BF16, col_off=s5w, n_cols=3 * aw + iw, name="proj_qkvi")
        misc = _matmul(hcur, w_misc, out_dtype=F32, tn=LANES, name="proj_misc")
        gates_ab = _matmul(hcur, w_gate, out_dtype=F32, name="proj_gates")

        prm = _s5_discretize(s5_lam_re[l], s5_lam_im[l], s5_b_re[l], s5_b_im[l], s5_c_re[l], s5_c_im[l],
                             s5_log_step[l], S5_CHUNK)
        y_s5 = _s5_branch(u, prm, s5_d[l])
        ya_in = _glu(y_s5, w_glu[l].astype(BF16), b_glu[l])

        k_t = jnp.transpose(misc[:, :IDX_DIM]).astype(BF16)
        zeros = jnp.zeros_like(k_t)
        k_even = jnp.concatenate([k_t, zeros], axis=0)
        k_odd = jnp.concatenate([zeros, k_t], axis=0)
        w_idx = misc[:, IDX_DIM:IDX_DIM + IDX_HEADS]
        keys, thr = _indexer(qkvi, o_qi, k_even, k_odd, w_idx, topk)
        attn = _attention(qkvi, o_q, o_k, o_v, keys, thr, bias_tiles, tb)

        merged = _merge(ya_in, attn, w_a_out[l].astype(BF16), w_b_out[l].astype(BF16), gates_ab)
        y = _matmul(merged, w_o[l].astype(BF16), out_dtype=F32, name="proj_out")
        xs, h2 = _ln_modulate(xs, y, mod[l], ln_g[l], ln_b[l], mod[l], alpha=dn_alpha, gate_row=2, ln_row=0,
                              sc_row=4, sh_row=3, want_h=True)

        w_r = jnp.pad(w_router[l], ((0, 0), (0, LANES - n_exp))).astype(BF16)
        b_r = jnp.pad(b_router[l], (0, LANES - n_exp)).reshape(1, LANES)
        gates = _router(h2, w_r, b_r, n_exp)
        b_dn_pad = jnp.pad(b_dn[l], ((0, LANES - n_exp), (0, 0))).astype(BF16)
        y = _moe_experts(h2, gates, w_gu[l].astype(BF16), b_gu[l], w_dn[l].astype(BF16), b_dn_pad)
        last = l == depth - 1
        xs, hcur = _ln_modulate(xs, y, mod[l], ln_g[l], ln_b[l], mod[min(l + 1, depth - 1)], alpha=dn_alpha,
                                gate_row=5, ln_row=1, sc_row=1, sh_row=0, want_h=not last)
    return xs.reshape(bsz, L, d)
```

```python
import functools
import math

import numpy as np
import jax
import jax.numpy as jnp
from jax import lax
from jax.experimental import pallas as pl
from jax.experimental.pallas import tpu as pltpu

F32 = jnp.float32
BF16 = jnp.bfloat16
I32 = jnp.int32

ATT_HEAD_DIM = 128
IDX_HEADS = 32
IDX_DIM = 64
TOPK_MAX = 256
REL_MAX_DIST = 128
TOP_K_EXPERTS = 4
SWIGLU_LIMIT = 7.0
SWIGLU_ALPHA = 1.702
N_MOD = 6
LN_EPS = 1e-5

LANES = 128
VMEM_LIMIT_BYTES = 56 * 1024 * 1024
S5_CHUNK = 16
NEG_BIG = -1e30
LOG2E = math.log2(math.e)
INT_MIN = -(2 ** 31)


def _cp(*sem):
    return pltpu.CompilerParams(dimension_semantics=sem, vmem_limit_bytes=VMEM_LIMIT_BYTES)


def _tile(n, t):
    t = min(n, t)
    while n % t:
        t //= 2
    return t


def _cond1_kernel(c_ref, w_ref, b_ref, o_ref):
    a = jax.nn.silu(c_ref[...]).astype(BF16)
    t = jnp.dot(a, w_ref[...].astype(BF16), preferred_element_type=F32) + b_ref[...]
    o_ref[...] = jax.nn.silu(t)


def _cond2_kernel(t_ref, w_ref, b_ref, tab_ref, o_ref):
    acc = jnp.dot(t_ref[...].astype(BF16), w_ref[...].astype(BF16), preferred_element_type=F32)
    o_ref[...] = acc[0:1, :] + b_ref[...] + tab_ref[...]


def _conditioning(c, w_c1, b_c1, w_c2, b_c2, mod_table):
    d = c.shape[1]
    r = w_c1.shape[1]
    depth = mod_table.shape[0]
    n = w_c2.shape[1]
    c8 = jnp.broadcast_to(c, (8, d))
    t = pl.pallas_call(
        _cond1_kernel, out_shape=jax.ShapeDtypeStruct((8, r), F32), name="cond1",
        compiler_params=_cp(),
    )(c8, w_c1, b_c1.reshape(1, r))
    tn = _tile(n, 2048)
    return pl.pallas_call(
        _cond2_kernel, grid=(n // tn,),
        in_specs=[pl.BlockSpec((8, r), lambda j: (0, 0)),
                  pl.BlockSpec((r, tn), lambda j: (0, j)),
                  pl.BlockSpec((1, tn), lambda j: (0, j)),
                  pl.BlockSpec((depth, tn), lambda j: (0, j))],
        out_specs=pl.BlockSpec((depth, tn), lambda j: (0, j)),
        out_shape=jax.ShapeDtypeStruct((depth, n), F32), name="cond2",
        compiler_params=_cp("parallel"),
    )(t, w_c2, b_c2.reshape(1, n), mod_table.reshape(depth, n))


def _modulate_kernel(x_ref, mod_ref, h_ref, *, sc_row, sh_row):
    sc = mod_ref[sc_row:sc_row + 1, :]
    sh = mod_ref[sh_row:sh_row + 1, :]
    h_ref[...] = (x_ref[...] * (1.0 + sc) + sh).astype(h_ref.dtype)


def _modulate(x, mod_l, sc_row, sh_row, tm=256):
    m, d = x.shape
    tm = _tile(m, tm)
    return pl.pallas_call(
        functools.partial(_modulate_kernel, sc_row=sc_row, sh_row=sh_row),
        grid=(m // tm,),
        in_specs=[pl.BlockSpec((tm, d), lambda i: (i, 0)),
                  pl.BlockSpec((N_MOD, d), lambda i: (0, 0))],
        out_specs=pl.BlockSpec((tm, d), lambda i: (i, 0)),
        out_shape=jax.ShapeDtypeStruct((m, d), BF16), name="modulate",
        compiler_params=_cp("parallel"),
    )(x, mod_l)


def _ln_kernel(x_ref, y_ref, mod_ref, g_ref, b_ref, nmod_ref, xo_ref, *h_refs,
               alpha, gate_row, ln_row, sc_row, sh_row):
    gate = mod_ref[gate_row:gate_row + 1, :]
    u = alpha * x_ref[...] + (1.0 + gate) * y_ref[...]
    mu = jnp.mean(u, axis=-1, keepdims=True)
    uc = u - mu
    var = jnp.mean(uc * uc, axis=-1, keepdims=True)
    xn = uc * lax.rsqrt(var + LN_EPS) * g_ref[ln_row:ln_row + 1, :] + b_ref[ln_row:ln_row + 1, :]
    xo_ref[...] = xn
    if h_refs:
        sc = nmod_ref[sc_row:sc_row + 1, :]
        sh = nmod_ref[sh_row:sh_row + 1, :]
        h_refs[0][...] = (xn * (1.0 + sc) + sh).astype(BF16)


def _ln_modulate(x, y, mod_l, ln_g_l, ln_b_l, next_mod, *, alpha, gate_row, ln_row, sc_row, sh_row,
                 want_h, tm=256):
    m, d = x.shape
    tm = _tile(m, tm)
    row = pl.BlockSpec((tm, d), lambda i: (i, 0))
    out_shape = [jax.ShapeDtypeStruct((m, d), F32)]
    out_specs = [row]
    if want_h:
        out_shape.append(jax.ShapeDtypeStruct((m, d), BF16))
        out_specs.append(row)
    res = pl.pallas_call(
        functools.partial(_ln_kernel, alpha=alpha, gate_row=gate_row, ln_row=ln_row,
                          sc_row=sc_row, sh_row=sh_row),
        grid=(m // tm,),
        in_specs=[row, row,
                  pl.BlockSpec((N_MOD, d), lambda i: (0, 0)),
                  pl.BlockSpec((2, d), lambda i: (0, 0)),
                  pl.BlockSpec((2, d), lambda i: (0, 0)),
                  pl.BlockSpec((N_MOD, d), lambda i: (0, 0))],
        out_specs=out_specs, out_shape=out_shape, name="ln_modulate",
        compiler_params=_cp("parallel"),
    )(x, y, mod_l, ln_g_l, ln_b_l, next_mod)
    return (res[0], res[1]) if want_h else (res[0], None)


def _mm_kernel(a_ref, w_ref, o_ref, wbf_ref):
    @pl.when(pl.program_id(1) == 0)
    def _():
        wbf_ref[...] = w_ref[...].astype(BF16)

    o_ref[...] = jnp.dot(a_ref[...], wbf_ref[...], preferred_element_type=F32).astype(o_ref.dtype)


def _w_spec(w, layer, k, tn, off=0):
    if w.ndim == 3:
        return pl.BlockSpec((None, k, tn), lambda j, i: (layer, 0, j + off))
    return pl.BlockSpec((k, tn), lambda j, i: (0, j + off))


def _matmul(a, w, *, out_dtype, layer=0, col_off=0, n_cols=None, tm=1024, tn=512, name="matmul"):
    m, k = a.shape
    n_cols = w.shape[-1] - col_off if n_cols is None else n_cols
    tm = _tile(m, tm)
    tn = math.gcd(_tile(n_cols, tn), col_off) if col_off else _tile(n_cols, tn)
    off = col_off // tn
    return pl.pallas_call(
        _mm_kernel, grid=(n_cols // tn, m // tm),
        in_specs=[pl.BlockSpec((tm, k), lambda j, i: (i, 0)),
                  _w_spec(w, layer, k, tn, off)],
        out_specs=pl.BlockSpec((tm, tn), lambda j, i: (i, j)),
        out_shape=jax.ShapeDtypeStruct((m, n_cols), out_dtype),
        scratch_shapes=[pltpu.VMEM((k, tn), BF16)], name=name,
        compiler_params=_cp("parallel", "arbitrary"),
    )(a, w)


def _glu_kernel(y_ref, yt_ref, w_ref, b_ref, o_ref, wbf_ref):
    @pl.when(pl.program_id(1) == 0)
    def _():
        wbf_ref[...] = w_ref[...].astype(BF16)

    z = jnp.dot(y_ref[...].astype(BF16), wbf_ref[...], preferred_element_type=F32) + b_ref[...]
    o_ref[...] = (yt_ref[...] * jax.nn.sigmoid(z)).astype(o_ref.dtype)


def _glu(y, w, b, layer, tm=1024, tn=512):
    m, k = y.shape
    n = w.shape[-1]
    tm = _tile(m, tm)
    tn = _tile(n, tn)
    return pl.pallas_call(
        _glu_kernel, grid=(n // tn, m // tm),
        in_specs=[pl.BlockSpec((tm, k), lambda j, i: (i, 0)),
                  pl.BlockSpec((tm, tn), lambda j, i: (i, j)),
                  _w_spec(w, layer, k, tn),
                  pl.BlockSpec((1, tn), lambda j, i: (0, j))],
        out_specs=pl.BlockSpec((tm, tn), lambda j, i: (i, j)),
        out_shape=jax.ShapeDtypeStruct((m, n), BF16),
        scratch_shapes=[pltpu.VMEM((k, tn), BF16)], name="s5_glu",
        compiler_params=_cp("parallel", "arbitrary"),
    )(y, y, w, b.reshape(1, n))


def _merge_kernel(a1_ref, a2_ref, w1_ref, w2_ref, ga_ref, gb_ref, o_ref, w1bf_ref, w2bf_ref):
    @pl.when(pl.program_id(1) == 0)
    def _():
        w1bf_ref[...] = w1_ref[...].astype(BF16)
        w2bf_ref[...] = w2_ref[...].astype(BF16)

    ya = jnp.dot(a1_ref[...], w1bf_ref[...], preferred_element_type=F32)
    yb = jnp.dot(a2_ref[...], w2bf_ref[...], preferred_element_type=F32)
    o_ref[...] = (jax.nn.sigmoid(ga_ref[...]) * ya + jax.nn.sigmoid(gb_ref[...]) * yb).astype(o_ref.dtype)


def _merge(a1, a2, w1, w2, gates, layer, tm=1024, tn=512):
    m, k1 = a1.shape
    k2 = a2.shape[1]
    n = w1.shape[-1]
    tm = _tile(m, tm)
    tn = _tile(n, tn)
    nb = n // tn
    return pl.pallas_call(
        _merge_kernel, grid=(nb, m // tm),
        in_specs=[pl.BlockSpec((tm, k1), lambda j, i: (i, 0)),
                  pl.BlockSpec((tm, k2), lambda j, i: (i, 0)),
                  _w_spec(w1, layer, k1, tn),
                  _w_spec(w2, layer, k2, tn),
                  pl.BlockSpec((tm, tn), lambda j, i: (i, j)),
                  pl.BlockSpec((tm, tn), lambda j, i: (i, j + nb))],
        out_specs=pl.BlockSpec((tm, tn), lambda j, i: (i, j)),
        out_shape=jax.ShapeDtypeStruct((m, n), BF16),
        scratch_shapes=[pltpu.VMEM((k1, tn), BF16), pltpu.VMEM((k2, tn), BF16)], name="branch_merge",
        compiler_params=_cp("parallel", "arbitrary"),
    )(a1, a2, w1, w2, gates, gates)


def _s5_discretize(lam_re, lam_im, b_re, b_im, c_re, c_im, d, log_step, t_chunk):
    hp = lax.Precision.HIGHEST
    g, p, h = b_re.shape
    na = LANES // h
    nx = g // na
    eye = jnp.eye(na, dtype=F32)
    lam = lax.complex(jnp.minimum(lam_re, -1e-4), lam_im)
    dt = jnp.exp(log_step)[:, None]
    lam_bar = jnp.exp(lam * dt)
    b_bar = ((lam_bar - 1.0) / lam)[:, :, None] * lax.complex(b_re, b_im)
    c = lax.complex(c_re, c_im)
    tau = jnp.arange(t_chunk + 1, dtype=F32)
    pw = jnp.exp((lam * dt)[:, None, :] * tau[None, :, None].astype(jnp.complex64))
    cp = c[:, None, :, :] * pw[:, :, None, :]
    kern = jnp.einsum('gtop,gpi->gtoi', cp[:, :t_chunk], b_bar, precision=hp).real
    s_idx = np.arange(t_chunk)[:, None]
    t_idx = np.arange(t_chunk)[None, :]
    lag = t_idx - s_idx
    kt = kern[:, np.clip(lag, 0, None)]
    kt = jnp.where((lag >= 0)[None, :, :, None, None], kt, 0.0)
    tah = t_chunk * na * h
    bd = jnp.einsum('xastoi,ab->xsaitbo', kt.reshape(nx, na, t_chunk, t_chunk, h, h), eye).reshape(nx, tah, tah)
    wc = pw[:, t_chunk - 1 - np.arange(t_chunk)][:, :, :, None] * b_bar[:, None, :, :]
    wc = jnp.transpose(wc, (0, 1, 3, 2))
    w_ri = jnp.concatenate([wc.real, wc.imag], axis=-1).reshape(nx, na, t_chunk, h, 2 * p)
    w_ir = jnp.concatenate([wc.imag, wc.real], axis=-1).reshape(nx, na, t_chunk, h, 2 * p)
    wa = jnp.einsum('xasiq,ab->xsaibq', w_ri, eye).reshape(nx, tah, na * 2 * p)
    wb = jnp.einsum('xasiq,ab->xsaibq', w_ir, eye).reshape(nx, tah, na * 2 * p)
    cl = jnp.transpose(cp[:, 1:t_chunk + 1], (0, 3, 1, 2))
    v_ri = jnp.concatenate([cl.real, -cl.imag], axis=1).reshape(nx, na, 2 * p, t_chunk, h)
    vm = jnp.einsum('xaqto,ab->xaqtbo', v_ri, eye).reshape(nx, na * 2 * p, tah)
    dd = jnp.broadcast_to(d.reshape(nx, 1, 1, na, h), (nx, 1, t_chunk, na, h)).reshape(nx, 1, tah)
    a = pw[:, t_chunk]
    a1 = jnp.concatenate([a.real, a.real], axis=-1).reshape(1, g * 2 * p)
    a2 = jnp.concatenate([-a.imag, a.imag], axis=-1).reshape(1, g * 2 * p)
    return bd.astype(BF16), wa.astype(BF16), wb.astype(BF16), vm.astype(BF16), dd, a1, a2


def _chunk_rows(u_ref, t, nc):
    return jnp.concatenate([u_ref[pl.ds(s, nc, stride=t), :] for s in range(t)], axis=1)


def _s5_state_in_kernel(u_ref, wa_ref, wb_ref, xa_ref, xb_ref, *, t, nc):
    ucat = _chunk_rows(u_ref, t, nc).astype(BF16)
    xa_ref[...] = jnp.dot(ucat, wa_ref[...], preferred_element_type=F32)
    xb_ref[...] = jnp.dot(ucat, wb_ref[...], preferred_element_type=F32)


def _s5_scan_kernel(xa_ref, xb_ref, a1_ref, a2_ref, s_ref, *, unroll):
    a1 = a1_ref[...]
    a2 = a2_ref[...]
    nc = xa_ref.shape[0]

    def body(cb, carry):
        s_a, s_b = carry
        base = pl.multiple_of(cb * unroll, unroll)
        xa = xa_ref[pl.ds(base, unroll), :]
        xb = xb_ref[pl.ds(base, unroll), :]
        rows = []
        for k in range(unroll):
            rows.append(s_a)
            s_a, s_b = (a1 * s_a + a2 * s_b + xa[k:k + 1, :],
                        a1 * s_b - a2 * s_a + xb[k:k + 1, :])
        s_ref[pl.ds(base, unroll), :] = jnp.concatenate(rows, axis=0)
        return s_a, s_b

    z = jnp.zeros(a1.shape, F32)
    lax.fori_loop(0, nc // unroll, body, (z, z))


def _s5_out_kernel(u_ref, sp_ref, bd_ref, vm_ref, dd_ref, y_ref, *, t, nc, n_parts):
    part = pl.program_id(1)
    tp = t // n_parts
    ucat = _chunk_rows(u_ref, t, nc)
    y = jnp.dot(ucat.astype(BF16), bd_ref[...], preferred_element_type=F32)
    y = y + jnp.dot(sp_ref[...].astype(BF16), vm_ref[...], preferred_element_type=F32)
    for k in range(n_parts):
        @pl.when(part == k)
        def _(k=k):
            yk = jax.nn.gelu(y + dd_ref[...] * ucat[:, k * tp * LANES:(k + 1) * tp * LANES])
            for s in range(tp):
                y_ref[pl.ds(k * tp + s, nc, stride=t), :] = yk[:, s * LANES:(s + 1) * LANES]


def _s5_branch(u, prm):
    bd, wa, wb, vm, dd, a1, a2 = prm
    L, width = u.shape
    nx, tah, sw = wa.shape
    t = tah // LANES
    nc = L // t
    u_spec = pl.BlockSpec((L, LANES), lambda x: (0, x))
    st_spec = pl.BlockSpec((nc, sw), lambda x: (0, x))
    xa, xb = pl.pallas_call(
        functools.partial(_s5_state_in_kernel, t=t, nc=nc), grid=(nx,),
        in_specs=[u_spec,
                  pl.BlockSpec((None, tah, sw), lambda x: (x, 0, 0)),
                  pl.BlockSpec((None, tah, sw), lambda x: (x, 0, 0))],
        out_specs=[st_spec, st_spec],
        out_shape=[jax.ShapeDtypeStruct((nc, nx * sw), F32)] * 2, name="s5_state_in",
        compiler_params=_cp("parallel"),
    )(u, wa, wb)
    ws = _tile(nx * sw, 2048)
    unroll = _tile(nc, 8)
    sc_spec = pl.BlockSpec((nc, ws), lambda x: (0, x))
    a_spec = pl.BlockSpec((1, ws), lambda x: (0, x))
    sp = pl.pallas_call(
        functools.partial(_s5_scan_kernel, unroll=unroll), grid=(nx * sw // ws,),
        in_specs=[sc_spec, sc_spec, a_spec, a_spec],
        out_specs=sc_spec,
        out_shape=jax.ShapeDtypeStruct((nc, nx * sw), F32), name="s5_scan",
        compiler_params=_cp("parallel"),
    )(xa, xb, a1, a2)
    n_parts = 2
    tw = tah // n_parts
    return pl.pallas_call(
        functools.partial(_s5_out_kernel, t=t, nc=nc, n_parts=n_parts), grid=(nx, n_parts),
        in_specs=[pl.BlockSpec((L, LANES), lambda x, k: (0, x)),
                  pl.BlockSpec((nc, sw), lambda x, k: (0, x)),
                  pl.BlockSpec((None, tah, tw), lambda x, k: (x, 0, k)),
                  pl.BlockSpec((None, sw, tw), lambda x, k: (x, 0, k)),
                  pl.BlockSpec((None, 1, tw), lambda x, k: (x, 0, k))],
        out_specs=pl.BlockSpec((L, LANES), lambda x, k: (0, x)),
        out_shape=jax.ShapeDtypeStruct((L, width), F32), name="s5_out",
        compiler_params=_cp("parallel", "arbitrary"),
    )(u, sp, bd, vm, dd)


def _t5_bucket_starts(n_buckets):
    exact = n_buckets // 2
    dist = np.arange(0, 4 * REL_MAX_DIST).astype(np.int64)
    far = exact + (np.log(np.maximum(dist, 1).astype(np.float32) / np.float32(exact))
                   / np.float32(math.log(REL_MAX_DIST / exact)) * np.float32(n_buckets - exact)).astype(np.int64)
    bucket = np.where(dist < exact, dist, np.minimum(far, n_buckets - 1))
    return [int(np.argmax(bucket >= b)) for b in range(1, n_buckets)]


def _bias_tile_kernel(rb_ref, o_ref, *, tb, starts):
    delta = pl.program_id(0)
    head = pl.program_id(1)
    key = lax.broadcasted_iota(I32, (tb, tb), 0)
    qry = lax.broadcasted_iota(I32, (tb, tb), 1)
    dist = delta * tb + qry - key
    bias = jnp.full((tb, tb), rb_ref[0, head], F32)
    for b, start in enumerate(starts, start=1):
        bias = jnp.where(dist >= start, rb_ref[b, head], bias)
    o_ref[...] = jnp.where(dist >= 0, bias * LOG2E, NEG_BIG)


def _bias_tiles(rel_bias, tb):
    n_buckets, heads = rel_bias.shape
    starts = _t5_bucket_starts(n_buckets)
    assert 2 * tb - (tb - 1) >= starts[-1]
    return pl.pallas_call(
        functools.partial(_bias_tile_kernel, tb=tb, starts=starts), grid=(2, heads),
        in_specs=[pl.BlockSpec(memory_space=pltpu.SMEM)],
        out_specs=pl.BlockSpec((None, None, tb, tb), lambda dl, hd: (dl, hd, 0, 0)),
        out_shape=jax.ShapeDtypeStruct((2, heads, tb, tb), F32), name="rel_bias_tiles",
        compiler_params=_cp("parallel", "parallel"),
    )(rel_bias)


def _indexer_kernel(qt_ref, k2_ref, wt_ref, keys_ref, thr_ref, *, tq, ck, n_heads, topk, w_scale):
    i = pl.program_id(0)
    n_blocks = keys_ref.shape[0] // tq
    n_live = (i + 1) * (tq // ck)
    wv = wt_ref[...] * w_scale

    def score_chunk(c, carry):
        row = pl.multiple_of(c * ck, ck)
        kk = k2_ref[c]
        acc = jnp.zeros((ck, tq), F32)
        for pr in range(n_heads // 2):
            s2 = jnp.dot(kk, qt_ref[pr * LANES:(pr + 1) * LANES, :], preferred_element_type=F32)
            acc = acc + jnp.maximum(s2[:ck], 0.0) * wv[2 * pr:2 * pr + 1, :]
            acc = acc + jnp.maximum(s2[ck:], 0.0) * wv[2 * pr + 1:2 * pr + 2, :]
        kpos = row + lax.broadcasted_iota(I32, (ck, tq), 0)
        qpos = i * tq + lax.broadcasted_iota(I32, (ck, tq), 1)
        score = jnp.where(kpos <= qpos, acc, -jnp.inf)
        bits = pltpu.bitcast(score, I32)
        keys_ref[pl.ds(row, ck), :] = jnp.where(bits < 0, bits ^ 0x7FFFFFFF, bits)
        return carry

    lax.fori_loop(0, n_live, score_chunk, 0)

    def fill_block(b, carry):
        row = pl.multiple_of(b * tq, tq)
        keys_ref[pl.ds(row, tq), :] = jnp.full((tq, tq), INT_MIN, I32)
        return carry

    lax.fori_loop(i + 1, n_blocks, fill_block, 0)

    def count_ge(cand):
        def body(b, cnt):
            row = pl.multiple_of(b * tq, tq)
            hit = jnp.where(keys_ref[pl.ds(row, tq), :] >= cand, 1.0, 0.0)
            return cnt + jnp.sum(hit.reshape(tq // 8, 8, tq), axis=0)
        cnt = lax.fori_loop(0, i + 1, body, jnp.zeros((8, tq), F32))
        return jnp.sum(cnt, axis=0, keepdims=True)

    zero = jnp.zeros((1, tq), I32)
    thr = jnp.where(count_ge(zero) >= topk, zero, jnp.full((1, tq), INT_MIN, I32))

    def bit_step(it, thr):
        cand = thr | jnp.left_shift(jnp.int32(1), 30 - it)
        return jnp.where(count_ge(cand) >= topk, cand, thr)

    thr_ref[...] = jnp.broadcast_to(lax.fori_loop(0, 31, bit_step, thr), thr_ref.shape)


def _indexer(qi_t, k2, w_t, topk, tq=256, ck=128):
    qw, L = qi_t.shape
    n_heads = w_t.shape[0]
    tq = _tile(L, tq)
    ck = k2.shape[1] // 2
    assert tq % ck == 0
    w_scale = (n_heads ** -0.5) * (IDX_DIM ** -0.5)
    return pl.pallas_call(
        functools.partial(_indexer_kernel, tq=tq, ck=ck, n_heads=n_heads, topk=topk, w_scale=w_scale),
        grid=(L // tq,),
        in_specs=[pl.BlockSpec((qw, tq), lambda i: (0, i)),
                  pl.BlockSpec(k2.shape, lambda i: (0, 0, 0)),
                  pl.BlockSpec((n_heads, tq), lambda i: (0, i))],
        out_specs=[pl.BlockSpec((L, tq), lambda i: (0, i)),
                   pl.BlockSpec((8, tq), lambda i: (0, i))],
        out_shape=[jax.ShapeDtypeStruct((L, L), I32), jax.ShapeDtypeStruct((8, L), I32)],
        name="dsa_indexer", compiler_params=_cp("parallel"),
    )(qi_t, k2, w_t)


def _attn_kernel(qi_ref, kj_ref, far_ref, qt_ref, k_ref, vt_ref, keys_ref, thr_ref, bias_ref, o_ref,
                 m_ref, l_ref, acc_ref, mb_ref, *, heads, dh, scale2):
    step = pl.program_id(0)
    qi = qi_ref[step]
    kj = kj_ref[step]

    @pl.when(kj == 0)
    def _():
        m_ref[...] = jnp.full(m_ref.shape, -jnp.inf, F32)
        l_ref[...] = jnp.zeros(l_ref.shape, F32)
        acc_ref[...] = jnp.zeros(acc_ref.shape, F32)

    mb_ref[...] = jnp.where(keys_ref[...] >= thr_ref[0:1, :], 0.0, NEG_BIG)

    def scores(h):
        hs = slice(h * dh, (h + 1) * dh)
        return jnp.dot(k_ref[:, hs], qt_ref[hs, :], preferred_element_type=F32)

    def run(near):
        ahead = 4
        pending = [scores(h) for h in range(min(ahead, heads))]
        for h in range(heads):
            hs = slice(h * dh, (h + 1) * dh)
            x = pending.pop(0) * scale2 + mb_ref[...]
            if h + ahead < heads:
                pending.append(scores(h + ahead))
            if near:
                x = x + bias_ref[h]
            far = 0.0 if near else far_ref[h]
            m_prev = m_ref[h]
            m_new = jnp.maximum(m_prev, jnp.max(x, axis=0, keepdims=True) + far)
            alpha = jnp.exp2(m_prev - m_new)
            p = jnp.exp2(x - (m_new[0:1, :] - far))
            l_ref[h] = alpha * l_ref[h] + jnp.sum(p, axis=0, keepdims=True)
            m_ref[h] = m_new
            acc_ref[hs, :] = alpha[0:1, :] * acc_ref[hs, :] + jnp.dot(vt_ref[hs, :], p.astype(BF16),
                                                                      preferred_element_type=F32)

    @pl.when(qi - kj < 2)
    def _():
        run(True)

    @pl.when(qi - kj >= 2)
    def _():
        run(False)

    @pl.when(kj == qi)
    def _():
        for h in range(heads):
            hs = slice(h * dh, (h + 1) * dh)
            o_ref[hs, :] = (acc_ref[hs, :] / l_ref[h][0:1, :]).astype(o_ref.dtype)


def _attention(q_t, qkvi, k_col, v_t, keys, thr, bias_tiles, far_bias, tb):
    L = qkvi.shape[0]
    heads = bias_tiles.shape[1]
    dh = ATT_HEAD_DIM
    aw = heads * dh
    assert k_col % aw == 0
    nb = L // tb
    pairs = [(i, j) for i in range(nb) for j in range(i + 1)]
    qi = jnp.asarray(np.array([p[0] for p in pairs], np.int32))
    kj = jnp.asarray(np.array([p[1] for p in pairs], np.int32))
    grid_spec = pltpu.PrefetchScalarGridSpec(
        num_scalar_prefetch=2, grid=(len(pairs),),
        in_specs=[pl.BlockSpec(memory_space=pltpu.SMEM),
                  pl.BlockSpec((aw, tb), lambda s, qi, kj: (0, qi[s])),
                  pl.BlockSpec((tb, aw), lambda s, qi, kj: (kj[s], k_col // aw)),
                  pl.BlockSpec((aw, tb), lambda s, qi, kj: (0, kj[s])),
                  pl.BlockSpec((tb, tb), lambda s, qi, kj: (kj[s], qi[s])),
                  pl.BlockSpec((8, tb), lambda s, qi, kj: (0, qi[s])),
                  pl.BlockSpec((None, heads, tb, tb),
                               lambda s, qi, kj: (jnp.minimum(qi[s] - kj[s], 1), 0, 0, 0))],
        out_specs=pl.BlockSpec((aw, tb), lambda s, qi, kj: (0, qi[s])),
        scratch_shapes=[pltpu.VMEM((heads, 8, tb), F32),
                        pltpu.VMEM((heads, 8, tb), F32),
                        pltpu.VMEM((aw, tb), F32),
                        pltpu.VMEM((tb, tb), F32)])
    return pl.pallas_call(
        functools.partial(_attn_kernel, heads=heads, dh=dh, scale2=dh ** -0.5 * LOG2E),
        grid_spec=grid_spec, out_shape=jax.ShapeDtypeStruct((aw, L), BF16), name="dsa_attention",
        compiler_params=_cp("arbitrary"),
    )(qi, kj, far_bias, q_t, qkvi, v_t, keys, thr, bias_tiles)


def _router_kernel(h_ref, w_ref, b_ref, g_ref, *, n_exp, top_k):
    logits = jnp.dot(h_ref[...], w_ref[...], preferred_element_type=F32) + b_ref[...]
    lane = lax.broadcasted_iota(I32, logits.shape, 1).astype(F32)
    work = jnp.where(lane < n_exp, logits, -jnp.inf)
    gates = jnp.zeros(logits.shape, F32)
    denom = None
    top = None
    for r in range(top_k):
        m = jnp.max(work, axis=1, keepdims=True)
        first = jnp.min(jnp.where(work == m, lane, float(LANES)), axis=1, keepdims=True)
        sel = lane == first
        if r == 0:
            top = m
        e = jnp.exp(m - top)
        gates = gates + jnp.where(sel, e, 0.0)
        denom = e if denom is None else denom + e
        work = jnp.where(sel, -jnp.inf, work)
    g_ref[...] = gates / denom


def _router(h, w_pad, b_pad, n_exp, tm=512):
    m, d = h.shape
    tm = _tile(m, tm)
    return pl.pallas_call(
        functools.partial(_router_kernel, n_exp=n_exp, top_k=TOP_K_EXPERTS), grid=(m // tm,),
        in_specs=[pl.BlockSpec((tm, d), lambda i: (i, 0)),
                  pl.BlockSpec((d, LANES), lambda i: (0, 0)),
                  pl.BlockSpec((1, LANES), lambda i: (0, 0))],
        out_specs=pl.BlockSpec((tm, LANES), lambda i: (i, 0)),
        out_shape=jax.ShapeDtypeStruct((m, LANES), F32), name="moe_router",
        compiler_params=_cp("parallel"),
    )(h, w_pad, b_pad)


def _moe_kernel(h_ref, g_ref, wgu_ref, bgu_ref, wdn_ref, bdn_ref, o_ref, acc_ref, *, ff):
    e = pl.program_id(1)
    gates = g_ref[...]

    @pl.when(e == 0)
    def _():
        acc_ref[...] = jnp.dot(gates.astype(BF16), bdn_ref[...], preferred_element_type=F32)

    gu = jnp.dot(h_ref[...], wgu_ref[...], preferred_element_type=F32) + bgu_ref[...]
    gate = jnp.minimum(gu[:, :ff], SWIGLU_LIMIT)
    up = jnp.clip(gu[:, ff:], -SWIGLU_LIMIT, SWIGLU_LIMIT)
    act = (up + 1.0) * gate * jax.nn.sigmoid(SWIGLU_ALPHA * gate)
    lane = lax.broadcasted_iota(I32, gates.shape, 1)
    ge = jnp.sum(jnp.where(lane == e, gates, 0.0), axis=1, keepdims=True)
    acc_ref[...] += jnp.dot((act * ge).astype(BF16), wdn_ref[...], preferred_element_type=F32)

    @pl.when(e == pl.num_programs(1) - 1)
    def _():
        o_ref[...] = acc_ref[...]


def _moe_experts(h, gates, w_gu, b_gu, w_dn, b_dn_pad, tm=512):
    m, d = h.shape
    n_exp, _, ff2 = w_gu.shape
    tm = _tile(m, tm)
    return pl.pallas_call(
        functools.partial(_moe_kernel, ff=ff2 // 2), grid=(m // tm, n_exp),
        in_specs=[pl.BlockSpec((tm, d), lambda i, e: (i, 0)),
                  pl.BlockSpec((tm, LANES), lambda i, e: (i, 0)),
                  pl.BlockSpec((None, d, ff2), lambda i, e: (e, 0, 0)),
                  pl.BlockSpec((None, 1, ff2), lambda i, e: (e, 0, 0)),
                  pl.BlockSpec((None, ff2 // 2, d), lambda i, e: (e, 0, 0)),
                  pl.BlockSpec((LANES, d), lambda i, e: (0, 0))],
        out_specs=pl.BlockSpec((tm, d), lambda i, e: (i, 0)),
        out_shape=jax.ShapeDtypeStruct((m, d), F32),
        scratch_shapes=[pltpu.VMEM((tm, d), F32)], name="moe_experts",
        compiler_params=_cp("parallel", "arbitrary"),
    )(h, gates, w_gu, b_gu.reshape(n_exp, 1, ff2), w_dn, b_dn_pad)


def kernel(x, c, w_in, w_a_out, w_b_out, w_o, w_glu, b_glu, s5_lam_re, s5_lam_im, s5_b_re, s5_b_im, s5_c_re, s5_c_im, s5_d, s5_log_step, rel_bias, w_router, b_router, w_gu, b_gu, w_dn, b_dn, w_c1, b_c1, w_c2, b_c2, mod_table, ln_g, ln_b):
    bsz, L, d = x.shape
    assert bsz == 1 and c.shape[0] == 1
    depth = w_in.shape[0]
    g, p, h = s5_b_re.shape[1:]
    s5w = g * h
    heads = rel_bias.shape[1]
    aw = heads * ATT_HEAD_DIM
    iw = IDX_HEADS * IDX_DIM
    n_exp = w_router.shape[2]
    assert w_in.shape[2] == s5w + 3 * aw + iw + IDX_DIM + IDX_HEADS + 2 * d
    assert n_exp <= LANES and IDX_DIM + IDX_HEADS <= LANES and 2 * IDX_DIM == LANES
    o_q, o_k, o_v, o_qi = 0, aw, 2 * aw, 3 * aw
    o_ki = s5w + 3 * aw + iw
    o_gate = o_ki + IDX_DIM + IDX_HEADS
    topk = min(TOPK_MAX, L // 4)
    tb = _tile(L, 256)
    ck = _tile(L, 128)
    dn_alpha = (2 * depth) ** 0.25

    mod = _conditioning(c, w_c1, b_c1, w_c2, b_c2, mod_table).reshape(depth, N_MOD, d)
    bias_tiles = _bias_tiles(rel_bias, tb)
    far_bias = rel_bias[-1] * LOG2E
    assert o_ki % LANES == 0 and (g * h) % LANES == 0 and LANES % h == 0

    xs = x.reshape(L, d)
    hcur = _modulate(xs, mod[0], sc_row=1, sh_row=0)
    for l in range(depth):
        u = _matmul(hcur, w_in, layer=l, out_dtype=F32, col_off=0, n_cols=s5w, name="proj_u")
        qkvi = _matmul(hcur, w_in, layer=l, out_dtype=BF16, col_off=s5w, n_cols=3 * aw + iw, name="proj_qkvi")
        misc = _matmul(hcur, w_in, layer=l, out_dtype=F32, col_off=o_ki, n_cols=LANES, tn=LANES, name="proj_misc")
        w_gate = w_in[l, :, o_gate:].astype(BF16)
        gates_ab = _matmul(hcur, w_gate, out_dtype=F32, name="proj_gates")

        prm = _s5_discretize(s5_lam_re[l], s5_lam_im[l], s5_b_re[l], s5_b_im[l], s5_c_re[l], s5_c_im[l],
                             s5_d[l], s5_log_step[l], S5_CHUNK)
        y_s5 = _s5_branch(u, prm)
        ya_in = _glu(y_s5, w_glu, b_glu[l], l)

        q_t = jnp.transpose(qkvi[:, o_q:o_q + aw])
        v_t = jnp.transpose(qkvi[:, o_v:o_v + aw])
        qi_t = jnp.transpose(qkvi[:, o_qi:o_qi + iw])
        k_idx = misc[:, :IDX_DIM].astype(BF16)
        zeros = jnp.zeros_like(k_idx)
        k_even = jnp.concatenate([k_idx, zeros], axis=1).reshape(L // ck, ck, LANES)
        k_odd = jnp.concatenate([zeros, k_idx], axis=1).reshape(L // ck, ck, LANES)
        k2 = jnp.concatenate([k_even, k_odd], axis=1)
        w_t = jnp.transpose(misc[:, IDX_DIM:IDX_DIM + IDX_HEADS])
        keys, thr = _indexer(qi_t, k2, w_t, topk)
        attn = jnp.transpose(_attention(q_t, qkvi, o_k, v_t, keys, thr, bias_tiles, far_bias, tb))

        merged = _merge(ya_in, attn, w_a_out, w_b_out, gates_ab, l)
        y = _matmul(merged, w_o, layer=l, out_dtype=F32, name="proj_out")
        xs, h2 = _ln_modulate(xs, y, mod[l], ln_g[l], ln_b[l], mod[l], alpha=dn_alpha, gate_row=2, ln_row=0,
                              sc_row=4, sh_row=3, want_h=True)

        w_r = jnp.pad(w_router[l], ((0, 0), (0, LANES - n_exp))).astype(BF16)
        b_r = jnp.pad(b_router[l], (0, LANES - n_exp)).reshape(1, LANES)
        gates = _router(h2, w_r, b_r, n_exp)
        b_dn_pad = jnp.pad(b_dn[l], ((0, LANES - n_exp), (0, 0))).astype(BF16)
        y = _moe_experts(h2, gates, w_gu[l].astype(BF16), b_gu[l], w_dn[l].astype(BF16), b_dn_pad)
        last = l == depth - 1
        xs, hcur = _ln_modulate(xs, y, mod[l], ln_g[l], ln_b[l], mod[min(l + 1, depth - 1)], alpha=dn_alpha,
                                gate_row=5, ln_row=1, sc_row=1, sh_row=0, want_h=not last)
    return xs.reshape(bsz, L, d)
```

```python
import functools
import math

import numpy as np
import jax
import jax.numpy as jnp
from jax import lax
from jax.experimental import pallas as pl
from jax.experimental.pallas import tpu as pltpu

F32 = jnp.float32
BF16 = jnp.bfloat16
I32 = jnp.int32

ATT_HEAD_DIM = 128
IDX_HEADS = 32
IDX_DIM = 64
TOPK_MAX = 256
REL_MAX_DIST = 128
TOP_K_EXPERTS = 4
SWIGLU_LIMIT = 7.0
SWIGLU_ALPHA = 1.702
N_MOD = 6
LN_EPS = 1e-5

LANES = 128
VMEM_LIMIT_BYTES = 56 * 1024 * 1024
S5_CHUNK = 16
NEG_BIG = -1e30
LOG2E = math.log2(math.e)
INT_MIN = -(2 ** 31)


def _cp(*sem):
    return pltpu.CompilerParams(dimension_semantics=sem, vmem_limit_bytes=VMEM_LIMIT_BYTES)


def _tile(n, t):
    t = min(n, t)
    while n % t:
        t //= 2
    return t


def _cond1_kernel(c_ref, w_ref, b_ref, o_ref):
    a = jax.nn.silu(c_ref[...]).astype(BF16)
    t = jnp.dot(a, w_ref[...].astype(BF16), preferred_element_type=F32) + b_ref[...]
    o_ref[...] = jax.nn.silu(t)


def _cond2_kernel(t_ref, w_ref, b_ref, tab_ref, o_ref):
    acc = jnp.dot(t_ref[...].astype(BF16), w_ref[...].astype(BF16), preferred_element_type=F32)
    o_ref[...] = acc[0:1, :] + b_ref[...] + tab_ref[...]


def _conditioning(c, w_c1, b_c1, w_c2, b_c2, mod_table):
    d = c.shape[1]
    r = w_c1.shape[1]
    depth = mod_table.shape[0]
    n = w_c2.shape[1]
    c8 = jnp.broadcast_to(c, (8, d))
    t = pl.pallas_call(
        _cond1_kernel, out_shape=jax.ShapeDtypeStruct((8, r), F32), name="cond1",
        compiler_params=_cp(),
    )(c8, w_c1, b_c1.reshape(1, r))
    tn = _tile(n, 2048)
    return pl.pallas_call(
        _cond2_kernel, grid=(n // tn,),
        in_specs=[pl.BlockSpec((8, r), lambda j: (0, 0)),
                  pl.BlockSpec((r, tn), lambda j: (0, j)),
                  pl.BlockSpec((1, tn), lambda j: (0, j)),
                  pl.BlockSpec((depth, tn), lambda j: (0, j))],
        out_specs=pl.BlockSpec((depth, tn), lambda j: (0, j)),
        out_shape=jax.ShapeDtypeStruct((depth, n), F32), name="cond2",
        compiler_params=_cp("parallel"),
    )(t, w_c2, b_c2.reshape(1, n), mod_table.reshape(depth, n))


def _modulate_kernel(x_ref, mod_ref, h_ref, *, sc_row, sh_row):
    sc = mod_ref[sc_row:sc_row + 1, :]
    sh = mod_ref[sh_row:sh_row + 1, :]
    h_ref[...] = (x_ref[...] * (1.0 + sc) + sh).astype(h_ref.dtype)


def _modulate(x, mod_l, sc_row, sh_row, tm=256):
    m, d = x.shape
    tm = _tile(m, tm)
    return pl.pallas_call(
        functools.partial(_modulate_kernel, sc_row=sc_row, sh_row=sh_row),
        grid=(m // tm,),
        in_specs=[pl.BlockSpec((tm, d), lambda i: (i, 0)),
                  pl.BlockSpec((N_MOD, d), lambda i: (0, 0))],
        out_specs=pl.BlockSpec((tm, d), lambda i: (i, 0)),
        out_shape=jax.ShapeDtypeStruct((m, d), BF16), name="modulate",
        compiler_params=_cp("parallel"),
    )(x, mod_l)


def _ln_kernel(x_ref, y_ref, mod_ref, g_ref, b_ref, nmod_ref, xo_ref, *h_refs,
               alpha, gate_row, ln_row, sc_row, sh_row):
    gate = mod_ref[gate_row:gate_row + 1, :]
    u = alpha * x_ref[...] + (1.0 + gate) * y_ref[...]
    mu = jnp.mean(u, axis=-1, keepdims=True)
    uc = u - mu
    var = jnp.mean(uc * uc, axis=-1, keepdims=True)
    xn = uc * lax.rsqrt(var + LN_EPS) * g_ref[ln_row:ln_row + 1, :] + b_ref[ln_row:ln_row + 1, :]
    xo_ref[...] = xn
    if h_refs:
        sc = nmod_ref[sc_row:sc_row + 1, :]
        sh = nmod_ref[sh_row:sh_row + 1, :]
        h_refs[0][...] = (xn * (1.0 + sc) + sh).astype(BF16)


def _ln_modulate(x, y, mod_l, ln_g_l, ln_b_l, next_mod, *, alpha, gate_row, ln_row, sc_row, sh_row,
                 want_h, tm=256):
    m, d = x.shape
    tm = _tile(m, tm)
    row = pl.BlockSpec((tm, d), lambda i: (i, 0))
    out_shape = [jax.ShapeDtypeStruct((m, d), F32)]
    out_specs = [row]
    if want_h:
        out_shape.append(jax.ShapeDtypeStruct((m, d), BF16))
        out_specs.append(row)
    res = pl.pallas_call(
        functools.partial(_ln_kernel, alpha=alpha, gate_row=gate_row, ln_row=ln_row,
                          sc_row=sc_row, sh_row=sh_row),
        grid=(m // tm,),
        in_specs=[row, row,
                  pl.BlockSpec((N_MOD, d), lambda i: (0, 0)),
                  pl.BlockSpec((2, d), lambda i: (0, 0)),
                  pl.BlockSpec((2, d), lambda i: (0, 0)),
                  pl.BlockSpec((N_MOD, d), lambda i: (0, 0))],
        out_specs=out_specs, out_shape=out_shape, name="ln_modulate",
        compiler_params=_cp("parallel"),
    )(x, y, mod_l, ln_g_l, ln_b_l, next_mod)
    return (res[0], res[1]) if want_h else (res[0], None)


def _mm_kernel(a_ref, w_ref, o_ref, wbf_ref):
    @pl.when(pl.program_id(1) == 0)
    def _():
        wbf_ref[...] = w_ref[...].astype(BF16)

    o_ref[...] = jnp.dot(a_ref[...], wbf_ref[...], preferred_element_type=F32).astype(o_ref.dtype)


def _w_spec(w, layer, k, tn, off=0):
    if w.ndim == 3:
        return pl.BlockSpec((None, k, tn), lambda j, i: (layer, 0, j + off))
    return pl.BlockSpec((k, tn), lambda j, i: (0, j + off))


def _matmul(a, w, *, out_dtype, layer=0, col_off=0, n_cols=None, tm=1024, tn=512, name="matmul"):
    m, k = a.shape
    n_cols = w.shape[-1] - col_off if n_cols is None else n_cols
    tm = _tile(m, tm)
    tn = math.gcd(_tile(n_cols, tn), col_off) if col_off else _tile(n_cols, tn)
    off = col_off // tn
    return pl.pallas_call(
        _mm_kernel, grid=(n_cols // tn, m // tm),
        in_specs=[pl.BlockSpec((tm, k), lambda j, i: (i, 0)),
                  _w_spec(w, layer, k, tn, off)],
        out_specs=pl.BlockSpec((tm, tn), lambda j, i: (i, j)),
        out_shape=jax.ShapeDtypeStruct((m, n_cols), out_dtype),
        scratch_shapes=[pltpu.VMEM((k, tn), BF16)], name=name,
        compiler_params=_cp("parallel", "arbitrary"),
    )(a, w)


def _glu_kernel(y_ref, yt_ref, w_ref, b_ref, o_ref, wbf_ref):
    @pl.when(pl.program_id(1) == 0)
    def _():
        wbf_ref[...] = w_ref[...].astype(BF16)

    z = jnp.dot(y_ref[...].astype(BF16), wbf_ref[...], preferred_element_type=F32) + b_ref[...]
    o_ref[...] = (yt_ref[...] * jax.nn.sigmoid(z)).astype(o_ref.dtype)


def _glu(y, w, b, layer, tm=1024, tn=512):
    m, k = y.shape
    n = w.shape[-1]
    tm = _tile(m, tm)
    tn = _tile(n, tn)
    return pl.pallas_call(
        _glu_kernel, grid=(n // tn, m // tm),
        in_specs=[pl.BlockSpec((tm, k), lambda j, i: (i, 0)),
                  pl.BlockSpec((tm, tn), lambda j, i: (i, j)),
                  _w_spec(w, layer, k, tn),
                  pl.BlockSpec((1, tn), lambda j, i: (0, j))],
        out_specs=pl.BlockSpec((tm, tn), lambda j, i: (i, j)),
        out_shape=jax.ShapeDtypeStruct((m, n), BF16),
        scratch_shapes=[pltpu.VMEM((k, tn), BF16)], name="s5_glu",
        compiler_params=_cp("parallel", "arbitrary"),
    )(y, y, w, b.reshape(1, n))


def _merge_kernel(a1_ref, a2_ref, w1_ref, w2_ref, ga_ref, gb_ref, o_ref, w1bf_ref, w2bf_ref):
    @pl.when(pl.program_id(1) == 0)
    def _():
        w1bf_ref[...] = w1_ref[...].astype(BF16)
        w2bf_ref[...] = w2_ref[...].astype(BF16)

    ya = jnp.dot(a1_ref[...], w1bf_ref[...], preferred_element_type=F32)
    yb = jnp.dot(a2_ref[...], w2bf_ref[...], preferred_element_type=F32)
    o_ref[...] = (jax.nn.sigmoid(ga_ref[...]) * ya + jax.nn.sigmoid(gb_ref[...]) * yb).astype(o_ref.dtype)


def _merge(a1, a2, w1, w2, gates, layer, tm=1024, tn=512):
    m, k1 = a1.shape
    k2 = a2.shape[1]
    n = w1.shape[-1]
    tm = _tile(m, tm)
    tn = _tile(n, tn)
    nb = n // tn
    return pl.pallas_call(
        _merge_kernel, grid=(nb, m // tm),
        in_specs=[pl.BlockSpec((tm, k1), lambda j, i: (i, 0)),
                  pl.BlockSpec((tm, k2), lambda j, i: (i, 0)),
                  _w_spec(w1, layer, k1, tn),
                  _w_spec(w2, layer, k2, tn),
                  pl.BlockSpec((tm, tn), lambda j, i: (i, j)),
                  pl.BlockSpec((tm, tn), lambda j, i: (i, j + nb))],
        out_specs=pl.BlockSpec((tm, tn), lambda j, i: (i, j)),
        out_shape=jax.ShapeDtypeStruct((m, n), BF16),
        scratch_shapes=[pltpu.VMEM((k1, tn), BF16), pltpu.VMEM((k2, tn), BF16)], name="branch_merge",
        compiler_params=_cp("parallel", "arbitrary"),
    )(a1, a2, w1, w2, gates, gates)


def _s5_discretize(lam_re, lam_im, b_re, b_im, c_re, c_im, d, log_step, t_chunk):
    hp = lax.Precision.HIGHEST
    g, p, h = b_re.shape
    na = LANES // h
    nx = g // na
    assert 2 * p == LANES
    lam = lax.complex(jnp.minimum(lam_re, -1e-4), lam_im)
    dt = jnp.exp(log_step)[:, None]
    lam_bar = jnp.exp(lam * dt)
    b_bar = ((lam_bar - 1.0) / lam)[:, :, None] * lax.complex(b_re, b_im)
    c = lax.complex(c_re, c_im)
    tau = jnp.arange(t_chunk + 1, dtype=F32)
    pw = jnp.exp((lam * dt)[:, None, :] * tau[None, :, None].astype(jnp.complex64))
    cp = c[:, None, :, :] * pw[:, :, None, :]
    kern = jnp.einsum('gtop,gpi->gtoi', cp[:, :t_chunk], b_bar, precision=hp).real
    kio = jnp.transpose(kern.reshape(nx, na, t_chunk, h, h), (0, 2, 1, 4, 3))
    kt = jnp.broadcast_to(kio[:, :, :, :, None, :], (nx, t_chunk, na, h, na, h)).reshape(nx, t_chunk, LANES, LANES)
    wcx = pw[:, t_chunk - 1 - np.arange(t_chunk)][:, :, :, None] * b_bar[:, None, :, :]
    wcx = jnp.transpose(wcx, (0, 1, 3, 2))
    w_ri = jnp.concatenate([wcx.real, wcx.imag], axis=-1).reshape(nx, na, t_chunk, h, 2 * p)
    wc = jnp.transpose(w_ri, (0, 2, 1, 3, 4)).reshape(nx, t_chunk, LANES, 2 * p)
    cl = jnp.transpose(cp[:, 1:t_chunk + 1], (0, 3, 1, 2))
    vc = jnp.concatenate([cl.real, -cl.imag], axis=1).reshape(nx, na, 2 * p, t_chunk * h)
    dd = jnp.broadcast_to(d.reshape(nx, 1, 1, na, h), (nx, 1, t_chunk, na, h)).reshape(nx, 1, t_chunk * LANES)
    a = pw[:, t_chunk]
    a1 = jnp.concatenate([a.real, a.real], axis=-1).reshape(1, g * 2 * p)
    a2 = jnp.concatenate([-a.imag, a.imag], axis=-1).reshape(1, g * 2 * p)
    return kt, wc, vc, dd, a1, a2


def _chunk_rows(u_ref, t, nc):
    return jnp.concatenate([u_ref[pl.ds(s, nc, stride=t), :] for s in range(t)], axis=1)


def _s5_state_in_kernel(u_ref, wc_ref, xa_ref, xb_ref, wa_ref, wb_ref, *, t, nc, h):
    na = LANES // h
    row_group = lax.broadcasted_iota(I32, (LANES, LANES), 0) // h
    for s in range(t):
        blk = wc_ref[s]
        swapped = pltpu.roll(blk, LANES // 2, axis=1)
        for b in range(na):
            own = row_group == b
            dst = (slice(s * LANES, (s + 1) * LANES), slice(b * LANES, (b + 1) * LANES))
            wa_ref[dst] = jnp.where(own, blk, 0.0).astype(BF16)
            wb_ref[dst] = jnp.where(own, swapped, 0.0).astype(BF16)
    ucat = _chunk_rows(u_ref, t, nc).astype(BF16)
    xa_ref[...] = jnp.dot(ucat, wa_ref[...], preferred_element_type=F32)
    xb_ref[...] = jnp.dot(ucat, wb_ref[...], preferred_element_type=F32)


def _s5_scan_kernel(xa_ref, xb_ref, a1_ref, a2_ref, s_ref, *, unroll):
    a1 = a1_ref[...]
    a2 = a2_ref[...]
    nc = xa_ref.shape[0]

    def body(cb, carry):
        s_a, s_b = carry
        base = pl.multiple_of(cb * unroll, unroll)
        xa = xa_ref[pl.ds(base, unroll), :]
        xb = xb_ref[pl.ds(base, unroll), :]
        rows = []
        for k in range(unroll):
            rows.append(s_a)
            s_a, s_b = (a1 * s_a + a2 * s_b + xa[k:k + 1, :],
                        a1 * s_b - a2 * s_a + xb[k:k + 1, :])
        s_ref[pl.ds(base, unroll), :] = jnp.concatenate(rows, axis=0)
        return s_a, s_b

    z = jnp.zeros(a1.shape, F32)
    lax.fori_loop(0, nc // unroll, body, (z, z))


def _s5_out_kernel(u_ref, sp_ref, kt_ref, vc_ref, dd_ref, y_ref, bd_ref, vm_ref, *, t, nc, h, n_parts):
    part = pl.program_id(1)
    tp = t // n_parts
    na = LANES // h
    w = tp * LANES
    same_group = (lax.broadcasted_iota(I32, (LANES, LANES), 0) // h
                  == lax.broadcasted_iota(I32, (LANES, LANES), 1) // h)
    row = lax.broadcasted_iota(I32, (LANES, w), 0)
    col = lax.broadcasted_iota(I32, (LANES, w), 1)
    landing_group = jnp.where((row // h == col // LANES) & (row % h == col % h), (col // h) % na, -1)
    ucat = _chunk_rows(u_ref, t, nc)
    for k in range(n_parts):
        @pl.when(part == k)
        def _(k=k):
            for s in range(t):
                for tt in range(tp):
                    lag = k * tp + tt - s
                    dst = (slice(s * LANES, (s + 1) * LANES), slice(tt * LANES, (tt + 1) * LANES))
                    if lag >= 0:
                        bd_ref[dst] = jnp.where(same_group, kt_ref[lag], 0.0).astype(BF16)
                    else:
                        bd_ref[dst] = jnp.zeros((LANES, LANES), BF16)
            for a in range(na):
                place = jnp.where(landing_group == a, 1.0, 0.0).astype(BF16)
                src = vc_ref[a][:, k * tp * h:(k + 1) * tp * h].astype(BF16)
                vm_ref[a * LANES:(a + 1) * LANES, :] = jnp.dot(src, place, preferred_element_type=F32).astype(BF16)
            y = jnp.dot(ucat.astype(BF16), bd_ref[...], preferred_element_type=F32)
            y = y + jnp.dot(sp_ref[...].astype(BF16), vm_ref[...], preferred_element_type=F32)
            y = jax.nn.gelu(y + dd_ref[...] * ucat[:, k * w:(k + 1) * w])
            for s in range(tp):
                y_ref[pl.ds(k * tp + s, nc, stride=t), :] = y[:, s * LANES:(s + 1) * LANES]


def _s5_branch(u, prm):
    kt, wc, vc, dd, a1, a2 = prm
    L, width = u.shape
    nx, na, _, th = vc.shape
    t = kt.shape[1]
    h = th // t
    tah = t * LANES
    sw = na * LANES
    nc = L // t
    u_spec = pl.BlockSpec((L, LANES), lambda x: (0, x))
    st_spec = pl.BlockSpec((nc, sw), lambda x: (0, x))
    xa, xb = pl.pallas_call(
        functools.partial(_s5_state_in_kernel, t=t, nc=nc, h=h), grid=(nx,),
        in_specs=[u_spec,
                  pl.BlockSpec((None, t, LANES, LANES), lambda x: (x, 0, 0, 0))],
        out_specs=[st_spec, st_spec],
        out_shape=[jax.ShapeDtypeStruct((nc, nx * sw), F32)] * 2,
        scratch_shapes=[pltpu.VMEM((tah, sw), BF16), pltpu.VMEM((tah, sw), BF16)], name="s5_state_in",
        compiler_params=_cp("parallel"),
    )(u, wc)
    ws = _tile(nx * sw, 2048)
    unroll = _tile(nc, 8)
    sc_spec = pl.BlockSpec((nc, ws), lambda x: (0, x))
    a_spec = pl.BlockSpec((1, ws), lambda x: (0, x))
    sp = pl.pallas_call(
        functools.partial(_s5_scan_kernel, unroll=unroll), grid=(nx * sw // ws,),
        in_specs=[sc_spec, sc_spec, a_spec, a_spec],
        out_specs=sc_spec,
        out_shape=jax.ShapeDtypeStruct((nc, nx * sw), F32), name="s5_scan",
        compiler_params=_cp("parallel"),
    )(xa, xb, a1, a2)
    n_parts = 2
    tw = tah // n_parts
    return pl.pallas_call(
        functools.partial(_s5_out_kernel, t=t, nc=nc, h=h, n_parts=n_parts), grid=(nx, n_parts),
        in_specs=[pl.BlockSpec((L, LANES), lambda x, k: (0, x)),
                  pl.BlockSpec((nc, sw), lambda x, k: (0, x)),
                  pl.BlockSpec((None, t, LANES, LANES), lambda x, k: (x, 0, 0, 0)),
                  pl.BlockSpec((None, na, LANES, th), lambda x, k: (x, 0, 0, 0)),
                  pl.BlockSpec((None, 1, tw), lambda x, k: (x, 0, k))],
        out_specs=pl.BlockSpec((L, LANES), lambda x, k: (0, x)),
        out_shape=jax.ShapeDtypeStruct((L, width), F32),
        scratch_shapes=[pltpu.VMEM((tah, tw), BF16), pltpu.VMEM((sw, tw), BF16)], name="s5_out",
        compiler_params=_cp("parallel", "arbitrary"),
    )(u, sp, kt, vc, dd)


def _t5_bucket_starts(n_buckets):
    exact = n_buckets // 2
    dist = np.arange(0, 4 * REL_MAX_DIST).astype(np.int64)
    far = exact + (np.log(np.maximum(dist, 1).astype(np.float32) / np.float32(exact))
                   / np.float32(math.log(REL_MAX_DIST / exact)) * np.float32(n_buckets - exact)).astype(np.int64)
    bucket = np.where(dist < exact, dist, np.minimum(far, n_buckets - 1))
    return [int(np.argmax(bucket >= b)) for b in range(1, n_buckets)]


def _bias_tile_kernel(rb_ref, o_ref, *, tb, starts):
    delta = pl.program_id(0)
    head = pl.program_id(1)
    key = lax.broadcasted_iota(I32, (tb, tb), 0)
    qry = lax.broadcasted_iota(I32, (tb, tb), 1)
    dist = delta * tb + qry - key
    bias = jnp.full((tb, tb), rb_ref[0, head], F32)
    for b, start in enumerate(starts, start=1):
        bias = jnp.where(dist >= start, rb_ref[b, head], bias)
    o_ref[...] = jnp.where(dist >= 0, bias * LOG2E, NEG_BIG)


def _bias_tiles(rel_bias, tb):
    n_buckets, heads = rel_bias.shape
    starts = _t5_bucket_starts(n_buckets)
    assert 2 * tb - (tb - 1) >= starts[-1]
    return pl.pallas_call(
        functools.partial(_bias_tile_kernel, tb=tb, starts=starts), grid=(2, heads),
        in_specs=[pl.BlockSpec(memory_space=pltpu.SMEM)],
        out_specs=pl.BlockSpec((None, None, tb, tb), lambda dl, hd: (dl, hd, 0, 0)),
        out_shape=jax.ShapeDtypeStruct((2, heads, tb, tb), F32), name="rel_bias_tiles",
        compiler_params=_cp("parallel", "parallel"),
    )(rel_bias)


def _indexer_kernel(qt_ref, k2_ref, wt_ref, keys_ref, thr_ref, *, tq, ck, n_heads, topk, w_scale):
    i = pl.program_id(0)
    n_blocks = keys_ref.shape[0] // tq
    n_live = (i + 1) * (tq // ck)
    wv = wt_ref[...] * w_scale

    def score_chunk(c, carry):
        row = pl.multiple_of(c * ck, ck)
        kk = k2_ref[c]
        acc = jnp.zeros((ck, tq), F32)
        for pr in range(n_heads // 2):
            s2 = jnp.dot(kk, qt_ref[pr * LANES:(pr + 1) * LANES, :], preferred_element_type=F32)
            acc = acc + jnp.maximum(s2[:ck], 0.0) * wv[2 * pr:2 * pr + 1, :]
            acc = acc + jnp.maximum(s2[ck:], 0.0) * wv[2 * pr + 1:2 * pr + 2, :]
        kpos = row + lax.broadcasted_iota(I32, (ck, tq), 0)
        qpos = i * tq + lax.broadcasted_iota(I32, (ck, tq), 1)
        score = jnp.where(kpos <= qpos, acc, -jnp.inf)
        bits = pltpu.bitcast(score, I32)
        keys_ref[pl.ds(row, ck), :] = jnp.where(bits < 0, bits ^ 0x7FFFFFFF, bits)
        return carry

    lax.fori_loop(0, n_live, score_chunk, 0)

    def fill_block(b, carry):
        row = pl.multiple_of(b * tq, tq)
        keys_ref[pl.ds(row, tq), :] = jnp.full((tq, tq), INT_MIN, I32)
        return carry

    lax.fori_loop(i + 1, n_blocks, fill_block, 0)

    def count_ge(cand):
        def body(b, cnt):
            row = pl.multiple_of(b * tq, tq)
            hit = jnp.where(keys_ref[pl.ds(row, tq), :] >= cand, 1.0, 0.0)
            return cnt + jnp.sum(hit.reshape(tq // 8, 8, tq), axis=0)
        cnt = lax.fori_loop(0, i + 1, body, jnp.zeros((8, tq), F32))
        return jnp.sum(cnt, axis=0, keepdims=True)

    kf = float(topk)
    zero = jnp.zeros((1, tq), I32)
    c_zero = count_ge(zero)
    thr0 = jnp.where(c_zero >= kf, zero, jnp.full((1, tq), INT_MIN, I32))
    cnt0 = jnp.where(c_zero >= kf, c_zero, ((i + 1) * tq).astype(F32))

    def undecided(st):
        it, _, cnt = st
        return (it < 31) & (jnp.max(cnt) > kf)

    def bit_step(st):
        it, thr, cnt = st
        cand = thr | jnp.left_shift(jnp.int32(1), 30 - it)
        c = count_ge(cand)
        keep = c >= kf
        return it + 1, jnp.where(keep, cand, thr), jnp.where(keep, c, cnt)

    _, thr, _ = lax.while_loop(undecided, bit_step, (jnp.int32(0), thr0, cnt0))
    thr_ref[...] = jnp.broadcast_to(thr, thr_ref.shape)


def _indexer(qi_t, k2, w_t, topk, tq=256, ck=128):
    qw, L = qi_t.shape
    n_heads = w_t.shape[0]
    tq = _tile(L, tq)
    ck = k2.shape[1] // 2
    assert tq % ck == 0
    w_scale = (n_heads ** -0.5) * (IDX_DIM ** -0.5)
    return pl.pallas_call(
        functools.partial(_indexer_kernel, tq=tq, ck=ck, n_heads=n_heads, topk=topk, w_scale=w_scale),
        grid=(L // tq,),
        in_specs=[pl.BlockSpec((qw, tq), lambda i: (0, i)),
                  pl.BlockSpec(k2.shape, lambda i: (0, 0, 0)),
                  pl.BlockSpec((n_heads, tq), lambda i: (0, i))],
        out_specs=[pl.BlockSpec((L, tq), lambda i: (0, i)),
                   pl.BlockSpec((8, tq), lambda i: (0, i))],
        out_shape=[jax.ShapeDtypeStruct((L, L), I32), jax.ShapeDtypeStruct((8, L), I32)],
        name="dsa_indexer", compiler_params=_cp("parallel"),
    )(qi_t, k2, w_t)


def _attn_kernel(qi_ref, kj_ref, far_ref, qt_ref, k_ref, vt_ref, keys_ref, thr_ref, bias_ref, o_ref,
                 m_ref, l_ref, acc_ref, mb_ref, *, heads, dh, scale2):
    step = pl.program_id(0)
    qi = qi_ref[step]
    kj = kj_ref[step]

    @pl.when(kj == 0)
    def _():
        m_ref[...] = jnp.full(m_ref.shape, -jnp.inf, F32)
        l_ref[...] = jnp.zeros(l_ref.shape, F32)
        acc_ref[...] = jnp.zeros(acc_ref.shape, F32)

    mb_ref[...] = jnp.where(keys_ref[...] >= thr_ref[0:1, :], 0.0, NEG_BIG)

    def scores(h):
        hs = slice(h * dh, (h + 1) * dh)
        return jnp.dot(k_ref[:, hs], qt_ref[hs, :], preferred_element_type=F32)

    def run(near):
        ahead = 4
        pending = [scores(h) for h in range(min(ahead, heads))]
        for h in range(heads):
            hs = slice(h * dh, (h + 1) * dh)
            x = pending.pop(0) * scale2 + mb_ref[...]
            if h + ahead < heads:
                pending.append(scores(h + ahead))
            if near:
                x = x + bias_ref[h]
            far = 0.0 if near else far_ref[h]
            m_prev = m_ref[h]
            m_new = jnp.maximum(m_prev, jnp.max(x, axis=0, keepdims=True) + far)
            alpha = jnp.exp2(m_prev - m_new)
            p = jnp.exp2(x - (m_new[0:1, :] - far))
            l_ref[h] = alpha * l_ref[h] + jnp.sum(p, axis=0, keepdims=True)
            m_ref[h] = m_new
            acc_ref[hs, :] = alpha[0:1, :] * acc_ref[hs, :] + jnp.dot(vt_ref[hs, :], p.astype(BF16),
                                                                      preferred_element_type=F32)

    @pl.when(qi - kj < 2)
    def _():
        run(True)

    @pl.when(qi - kj >= 2)
    def _():
        run(False)

    @pl.when(kj == qi)
    def _():
        for h in range(heads):
            hs = slice(h * dh, (h + 1) * dh)
            o_ref[hs, :] = (acc_ref[hs, :] / l_ref[h][0:1, :]).astype(o_ref.dtype)


def _attention(q_t, qkvi, k_col, v_t, keys, thr, bias_tiles, far_bias, tb):
    L = qkvi.shape[0]
    heads = bias_tiles.shape[1]
    dh = ATT_HEAD_DIM
    aw = heads * dh
    assert k_col % aw == 0
    nb = L // tb
    pairs = [(i, j) for i in range(nb) for j in range(i + 1)]
    qi = jnp.asarray(np.array([p[0] for p in pairs], np.int32))
    kj = jnp.asarray(np.array([p[1] for p in pairs], np.int32))
    grid_spec = pltpu.PrefetchScalarGridSpec(
        num_scalar_prefetch=2, grid=(len(pairs),),
        in_specs=[pl.BlockSpec(memory_space=pltpu.SMEM),
                  pl.BlockSpec((aw, tb), lambda s, qi, kj: (0, qi[s])),
                  pl.BlockSpec((tb, aw), lambda s, qi, kj: (kj[s], k_col // aw)),
                  pl.BlockSpec((aw, tb), lambda s, qi, kj: (0, kj[s])),
                  pl.BlockSpec((tb, tb), lambda s, qi, kj: (kj[s], qi[s])),
                  pl.BlockSpec((8, tb), lambda s, qi, kj: (0, qi[s])),
                  pl.BlockSpec((None, heads, tb, tb),
                               lambda s, qi, kj: (jnp.minimum(qi[s] - kj[s], 1), 0, 0, 0))],
        out_specs=pl.BlockSpec((aw, tb), lambda s, qi, kj: (0, qi[s])),
        scratch_shapes=[pltpu.VMEM((heads, 8, tb), F32),
                        pltpu.VMEM((heads, 8, tb), F32),
                        pltpu.VMEM((aw, tb), F32),
                        pltpu.VMEM((tb, tb), F32)])
    return pl.pallas_call(
        functools.partial(_attn_kernel, heads=heads, dh=dh, scale2=dh ** -0.5 * LOG2E),
        grid_spec=grid_spec, out_shape=jax.ShapeDtypeStruct((aw, L), BF16), name="dsa_attention",
        compiler_params=_cp("arbitrary"),
    )(qi, kj, far_bias, q_t, qkvi, v_t, keys, thr, bias_tiles)


def _router_kernel(h_ref, w_ref, b_ref, g_ref, *, n_exp, top_k):
    logits = jnp.dot(h_ref[...], w_ref[...], preferred_element_type=F32) + b_ref[...]
    lane = lax.broadcasted_iota(I32, logits.shape, 1).astype(F32)
    work = jnp.where(lane < n_exp, logits, -jnp.inf)
    gates = jnp.zeros(logits.shape, F32)
    denom = None
    top = None
    for r in range(top_k):
        m = jnp.max(work, axis=1, keepdims=True)
        first = jnp.min(jnp.where(work == m, lane, float(LANES)), axis=1, keepdims=True)
        sel = lane == first
        if r == 0:
            top = m
        e = jnp.exp(m - top)
        gates = gates + jnp.where(sel, e, 0.0)
        denom = e if denom is None else denom + e
        work = jnp.where(sel, -jnp.inf, work)
    g_ref[...] = gates / denom


def _router(h, w_pad, b_pad, n_exp, tm=512):
    m, d = h.shape
    tm = _tile(m, tm)
    return pl.pallas_call(
        functools.partial(_router_kernel, n_exp=n_exp, top_k=TOP_K_EXPERTS), grid=(m // tm,),
        in_specs=[pl.BlockSpec((tm, d), lambda i: (i, 0)),
                  pl.BlockSpec((d, LANES), lambda i: (0, 0)),
                  pl.BlockSpec((1, LANES), lambda i: (0, 0))],
        out_specs=pl.BlockSpec((tm, LANES), lambda i: (i, 0)),
        out_shape=jax.ShapeDtypeStruct((m, LANES), F32), name="moe_router",
        compiler_params=_cp("parallel"),
    )(h, w_pad, b_pad)


def _moe_kernel(h_ref, g_ref, wgu_ref, bgu_ref, wdn_ref, bdn_ref, o_ref, acc_ref, *, ff):
    e = pl.program_id(1)
    gates = g_ref[...]

    @pl.when(e == 0)
    def _():
        acc_ref[...] = jnp.dot(gates.astype(BF16), bdn_ref[...], preferred_element_type=F32)

    gu = jnp.dot(h_ref[...], wgu_ref[...], preferred_element_type=F32) + bgu_ref[...]
    gate = jnp.minimum(gu[:, :ff], SWIGLU_LIMIT)
    up = jnp.clip(gu[:, ff:], -SWIGLU_LIMIT, SWIGLU_LIMIT)
    act = (up + 1.0) * gate * jax.nn.sigmoid(SWIGLU_ALPHA * gate)
    lane = lax.broadcasted_iota(I32, gates.shape, 1)
    ge = jnp.sum(jnp.where(lane == e, gates, 0.0), axis=1, keepdims=True)
    acc_ref[...] += jnp.dot((act * ge).astype(BF16), wdn_ref[...], preferred_element_type=F32)

    @pl.when(e == pl.num_programs(1) - 1)
    def _():
        o_ref[...] = acc_ref[...]


def _moe_experts(h, gates, w_gu, b_gu, w_dn, b_dn_pad, tm=512):
    m, d = h.shape
    n_exp, _, ff2 = w_gu.shape
    tm = _tile(m, tm)
    return pl.pallas_call(
        functools.partial(_moe_kernel, ff=ff2 // 2), grid=(m // tm, n_exp),
        in_specs=[pl.BlockSpec((tm, d), lambda i, e: (i, 0)),
                  pl.BlockSpec((tm, LANES), lambda i, e: (i, 0)),
                  pl.BlockSpec((None, d, ff2), lambda i, e: (e, 0, 0)),
                  pl.BlockSpec((None, 1, ff2), lambda i, e: (e, 0, 0)),
                  pl.BlockSpec((None, ff2 // 2, d), lambda i, e: (e, 0, 0)),
                  pl.BlockSpec((LANES, d), lambda i, e: (0, 0))],
        out_specs=pl.BlockSpec((tm, d), lambda i, e: (i, 0)),
        out_shape=jax.ShapeDtypeStruct((m, d), F32),
        scratch_shapes=[pltpu.VMEM((tm, d), F32)], name="moe_experts",
        compiler_params=_cp("parallel", "arbitrary"),
    )(h, gates, w_gu, b_gu.reshape(n_exp, 1, ff2), w_dn, b_dn_pad)


def kernel(x, c, w_in, w_a_out, w_b_out, w_o, w_glu, b_glu, s5_lam_re, s5_lam_im, s5_b_re, s5_b_im, s5_c_re, s5_c_im, s5_d, s5_log_step, rel_bias, w_router, b_router, w_gu, b_gu, w_dn, b_dn, w_c1, b_c1, w_c2, b_c2, mod_table, ln_g, ln_b):
    bsz, L, d = x.shape
    assert bsz == 1 and c.shape[0] == 1
    depth = w_in.shape[0]
    g, p, h = s5_b_re.shape[1:]
    s5w = g * h
    heads = rel_bias.shape[1]
    aw = heads * ATT_HEAD_DIM
    iw = IDX_HEADS * IDX_DIM
    n_exp = w_router.shape[2]
    assert w_in.shape[2] == s5w + 3 * aw + iw + IDX_DIM + IDX_HEADS + 2 * d
    assert n_exp <= LANES and IDX_DIM + IDX_HEADS <= LANES and 2 * IDX_DIM == LANES
    o_q, o_k, o_v, o_qi = 0, aw, 2 * aw, 3 * aw
    o_ki = s5w + 3 * aw + iw
    o_gate = o_ki + IDX_DIM + IDX_HEADS
    topk = min(TOPK_MAX, L // 4)
    tb = _tile(L, 256)
    ck = _tile(L, 128)
    dn_alpha = (2 * depth) ** 0.25

    mod = _conditioning(c, w_c1, b_c1, w_c2, b_c2, mod_table).reshape(depth, N_MOD, d)
    bias_tiles = _bias_tiles(rel_bias, tb)
    far_bias = rel_bias[-1] * LOG2E
    assert o_ki % LANES == 0 and (g * h) % LANES == 0 and LANES % h == 0

    xs = x.reshape(L, d)
    hcur = _modulate(xs, mod[0], sc_row=1, sh_row=0)
    for l in range(depth):
        u = _matmul(hcur, w_in, layer=l, out_dtype=F32, col_off=0, n_cols=s5w, name="proj_u")
        qkvi = _matmul(hcur, w_in, layer=l, out_dtype=BF16, col_off=s5w, n_cols=3 * aw + iw, name="proj_qkvi")
        misc = _matmul(hcur, w_in, layer=l, out_dtype=F32, col_off=o_ki, n_cols=LANES, tn=LANES, name="proj_misc")
        w_gate = w_in[l, :, o_gate:].astype(BF16)
        gates_ab = _matmul(hcur, w_gate, out_dtype=F32, name="proj_gates")

        prm = _s5_discretize(s5_lam_re[l], s5_lam_im[l], s5_b_re[l], s5_b_im[l], s5_c_re[l], s5_c_im[l],
                             s5_d[l], s5_log_step[l], S5_CHUNK)
        y_s5 = _s5_branch(u, prm)
        ya_in = _glu(y_s5, w_glu, b_glu[l], l)

        q_t = jnp.transpose(qkvi[:, o_q:o_q + aw])
        v_t = jnp.transpose(qkvi[:, o_v:o_v + aw])
        qi_t = jnp.transpose(qkvi[:, o_qi:o_qi + iw])
        k_idx = misc[:, :IDX_DIM].astype(BF16)
        zeros = jnp.zeros_like(k_idx)
        k_even = jnp.concatenate([k_idx, zeros], axis=1).reshape(L // ck, ck, LANES)
        k_odd = jnp.concatenate([zeros, k_idx], axis=1).reshape(L // ck, ck, LANES)
        k2 = jnp.concatenate([k_even, k_odd], axis=1)
        w_t = jnp.transpose(misc[:, IDX_DIM:IDX_DIM + IDX_HEADS])
        keys, thr = _indexer(qi_t, k2, w_t, topk)
        attn = jnp.transpose(_attention(q_t, qkvi, o_k, v_t, keys, thr, bias_tiles, far_bias, tb))

        merged = _merge(ya_in, attn, w_a_out, w_b_out, gates_ab, l)
        y = _matmul(merged, w_o, layer=l, out_dtype=F32, name="proj_out")
        xs, h2 = _ln_modulate(xs, y, mod[l], ln_g[l], ln_b[l], mod[l], alpha=dn_alpha, gate_row=2, ln_row=0,
                              sc_row=4, sh_row=3, want_h=True)

        w_r = jnp.pad(w_router[l], ((0, 0), (0, LANES - n_exp))).astype(BF16)
        b_r = jnp.pad(b_router[l], (0, LANES - n_exp)).reshape(1, LANES)
        gates = _router(h2, w_r, b_r, n_exp)
        b_dn_pad = jnp.pad(b_dn[l], ((0, LANES - n_exp), (0, 0))).astype(BF16)
        y = _moe_experts(h2, gates, w_gu[l].astype(BF16), b_gu[l], w_dn[l].astype(BF16), b_dn_pad)
        last = l == depth - 1
        xs, hcur = _ln_modulate(xs, y, mod[l], ln_g[l], ln_b[l], mod[min(l + 1, depth - 1)], alpha=dn_alpha,
                                gate_row=5, ln_row=1, sc_row=1, sh_row=0, want_h=not last)
    return xs.reshape(bsz, L, d)
```

```python
import functools
import math

import numpy as np
import jax
import jax.numpy as jnp
from jax import lax
from jax.experimental import pallas as pl
from jax.experimental.pallas import tpu as pltpu

F32 = jnp.float32
BF16 = jnp.bfloat16
I32 = jnp.int32

ATT_HEAD_DIM = 128
IDX_HEADS = 32
IDX_DIM = 64
TOPK_MAX = 256
REL_MAX_DIST = 128
TOP_K_EXPERTS = 4
SWIGLU_LIMIT = 7.0
SWIGLU_ALPHA = 1.702
N_MOD = 6
LN_EPS = 1e-5

LANES = 128
VMEM_LIMIT_BYTES = 56 * 1024 * 1024
S5_CHUNK = 16
NEG_BIG = -1e30
LOG2E = math.log2(math.e)
INT_MIN = -(2 ** 31)


def _cp(*sem):
    return pltpu.CompilerParams(dimension_semantics=sem, vmem_limit_bytes=VMEM_LIMIT_BYTES)


def _tile(n, t):
    t = min(n, t)
    while n % t:
        t //= 2
    return t


def _cond1_kernel(c_ref, w_ref, b_ref, o_ref):
    a = jax.nn.silu(c_ref[...]).astype(BF16)
    t = jnp.dot(a, w_ref[...].astype(BF16), preferred_element_type=F32) + b_ref[...]
    o_ref[...] = jax.nn.silu(t)


def _cond2_kernel(t_ref, w_ref, b_ref, tab_ref, o_ref):
    acc = jnp.dot(t_ref[...].astype(BF16), w_ref[...].astype(BF16), preferred_element_type=F32)
    o_ref[...] = acc[0:1, :] + b_ref[...] + tab_ref[...]


def _conditioning(c, w_c1, b_c1, w_c2, b_c2, mod_table):
    d = c.shape[1]
    r = w_c1.shape[1]
    depth = mod_table.shape[0]
    n = w_c2.shape[1]
    c8 = jnp.broadcast_to(c, (8, d))
    t = pl.pallas_call(
        _cond1_kernel, out_shape=jax.ShapeDtypeStruct((8, r), F32), name="cond1",
        compiler_params=_cp(),
    )(c8, w_c1, b_c1.reshape(1, r))
    tn = _tile(n, 2048)
    return pl.pallas_call(
        _cond2_kernel, grid=(n // tn,),
        in_specs=[pl.BlockSpec((8, r), lambda j: (0, 0)),
                  pl.BlockSpec((r, tn), lambda j: (0, j)),
                  pl.BlockSpec((1, tn), lambda j: (0, j)),
                  pl.BlockSpec((depth, tn), lambda j: (0, j))],
        out_specs=pl.BlockSpec((depth, tn), lambda j: (0, j)),
        out_shape=jax.ShapeDtypeStruct((depth, n), F32), name="cond2",
        compiler_params=_cp("parallel"),
    )(t, w_c2, b_c2.reshape(1, n), mod_table.reshape(depth, n))


def _modulate_kernel(x_ref, mod_ref, h_ref, *, sc_row, sh_row):
    sc = mod_ref[sc_row:sc_row + 1, :]
    sh = mod_ref[sh_row:sh_row + 1, :]
    h_ref[...] = (x_ref[...] * (1.0 + sc) + sh).astype(h_ref.dtype)


def _modulate(x, mod_l, sc_row, sh_row, tm=256):
    m, d = x.shape
    tm = _tile(m, tm)
    return pl.pallas_call(
        functools.partial(_modulate_kernel, sc_row=sc_row, sh_row=sh_row),
        grid=(m // tm,),
        in_specs=[pl.BlockSpec((tm, d), lambda i: (i, 0)),
                  pl.BlockSpec((N_MOD, d), lambda i: (0, 0))],
        out_specs=pl.BlockSpec((tm, d), lambda i: (i, 0)),
        out_shape=jax.ShapeDtypeStruct((m, d), BF16), name="modulate",
        compiler_params=_cp("parallel"),
    )(x, mod_l)


def _ln_kernel(x_ref, y_ref, mod_ref, g_ref, b_ref, nmod_ref, xo_ref, *h_refs,
               alpha, gate_row, ln_row, sc_row, sh_row):
    gate = mod_ref[gate_row:gate_row + 1, :]
    u = alpha * x_ref[...] + (1.0 + gate) * y_ref[...]
    mu = jnp.mean(u, axis=-1, keepdims=True)
    uc = u - mu
    var = jnp.mean(uc * uc, axis=-1, keepdims=True)
    xn = uc * lax.rsqrt(var + LN_EPS) * g_ref[ln_row:ln_row + 1, :] + b_ref[ln_row:ln_row + 1, :]
    xo_ref[...] = xn
    if h_refs:
        sc = nmod_ref[sc_row:sc_row + 1, :]
        sh = nmod_ref[sh_row:sh_row + 1, :]
        h_refs[0][...] = (xn * (1.0 + sc) + sh).astype(BF16)


def _ln_modulate(x, y, mod_l, ln_g_l, ln_b_l, next_mod, *, alpha, gate_row, ln_row, sc_row, sh_row,
                 want_h, tm=256):
    m, d = x.shape
    tm = _tile(m, tm)
    row = pl.BlockSpec((tm, d), lambda i: (i, 0))
    out_shape = [jax.ShapeDtypeStruct((m, d), F32)]
    out_specs = [row]
    if want_h:
        out_shape.append(jax.ShapeDtypeStruct((m, d), BF16))
        out_specs.append(row)
    res = pl.pallas_call(
        functools.partial(_ln_kernel, alpha=alpha, gate_row=gate_row, ln_row=ln_row,
                          sc_row=sc_row, sh_row=sh_row),
        grid=(m // tm,),
        in_specs=[row, row,
                  pl.BlockSpec((N_MOD, d), lambda i: (0, 0)),
                  pl.BlockSpec((2, d), lambda i: (0, 0)),
                  pl.BlockSpec((2, d), lambda i: (0, 0)),
                  pl.BlockSpec((N_MOD, d), lambda i: (0, 0))],
        out_specs=out_specs, out_shape=out_shape, name="ln_modulate",
        compiler_params=_cp("parallel"),
    )(x, y, mod_l, ln_g_l, ln_b_l, next_mod)
    return (res[0], res[1]) if want_h else (res[0], None)


def _mm_kernel(a_ref, w_ref, o_ref, wbf_ref):
    @pl.when(pl.program_id(1) == 0)
    def _():
        wbf_ref[...] = w_ref[...].astype(BF16)

    o_ref[...] = jnp.dot(a_ref[...], wbf_ref[...], preferred_element_type=F32).astype(o_ref.dtype)


def _w_spec(w, layer, k, tn, off=0):
    if w.ndim == 3:
        return pl.BlockSpec((None, k, tn), lambda j, i: (layer, 0, j + off))
    return pl.BlockSpec((k, tn), lambda j, i: (0, j + off))


def _matmul(a, w, *, out_dtype, layer=0, col_off=0, n_cols=None, tm=1024, tn=512, name="matmul"):
    m, k = a.shape
    n_cols = w.shape[-1] - col_off if n_cols is None else n_cols
    tm = _tile(m, tm)
    tn = math.gcd(_tile(n_cols, tn), col_off) if col_off else _tile(n_cols, tn)
    off = col_off // tn
    return pl.pallas_call(
        _mm_kernel, grid=(n_cols // tn, m // tm),
        in_specs=[pl.BlockSpec((tm, k), lambda j, i: (i, 0)),
                  _w_spec(w, layer, k, tn, off)],
        out_specs=pl.BlockSpec((tm, tn), lambda j, i: (i, j)),
        out_shape=jax.ShapeDtypeStruct((m, n_cols), out_dtype),
        scratch_shapes=[pltpu.VMEM((k, tn), BF16)], name=name,
        compiler_params=_cp("parallel", "arbitrary"),
    )(a, w)


def _mm_t_kernel(a_ref, wt_ref, o_ref):
    o_ref[...] = lax.dot_general(wt_ref[...], a_ref[...], (((1,), (1,)), ((), ())),
                                 preferred_element_type=F32).astype(o_ref.dtype)


def _matmul_t(a, w_t, *, out_dtype, tm=1024, tn=1024, name="matmul_t"):
    m, k = a.shape
    n = w_t.shape[0]
    tm = _tile(m, tm)
    tn = _tile(n, tn)
    return pl.pallas_call(
        _mm_t_kernel, grid=(n // tn, m // tm),
        in_specs=[pl.BlockSpec((tm, k), lambda j, i: (i, 0)),
                  pl.BlockSpec((tn, k), lambda j, i: (j, 0))],
        out_specs=pl.BlockSpec((tn, tm), lambda j, i: (j, i)),
        out_shape=jax.ShapeDtypeStruct((n, m), out_dtype), name=name,
        compiler_params=_cp("parallel", "arbitrary"),
    )(a, w_t)


def _glu_kernel(y_ref, yt_ref, w_ref, b_ref, o_ref, wbf_ref):
    @pl.when(pl.program_id(1) == 0)
    def _():
        wbf_ref[...] = w_ref[...].astype(BF16)

    z = jnp.dot(y_ref[...].astype(BF16), wbf_ref[...], preferred_element_type=F32) + b_ref[...]
    o_ref[...] = (yt_ref[...] * jax.nn.sigmoid(z)).astype(o_ref.dtype)


def _glu(y, w, b, layer, tm=1024, tn=512):
    m, k = y.shape
    n = w.shape[-1]
    tm = _tile(m, tm)
    tn = _tile(n, tn)
    return pl.pallas_call(
        _glu_kernel, grid=(n // tn, m // tm),
        in_specs=[pl.BlockSpec((tm, k), lambda j, i: (i, 0)),
                  pl.BlockSpec((tm, tn), lambda j, i: (i, j)),
                  _w_spec(w, layer, k, tn),
                  pl.BlockSpec((1, tn), lambda j, i: (0, j))],
        out_specs=pl.BlockSpec((tm, tn), lambda j, i: (i, j)),
        out_shape=jax.ShapeDtypeStruct((m, n), BF16),
        scratch_shapes=[pltpu.VMEM((k, tn), BF16)], name="s5_glu",
        compiler_params=_cp("parallel", "arbitrary"),
    )(y, y, w, b.reshape(1, n))


def _merge_kernel(a1_ref, a2_ref, w1_ref, w2_ref, ga_ref, gb_ref, o_ref, w1bf_ref, w2bf_ref):
    @pl.when(pl.program_id(1) == 0)
    def _():
        w1bf_ref[...] = w1_ref[...].astype(BF16)
        w2bf_ref[...] = w2_ref[...].astype(BF16)

    ya = jnp.dot(a1_ref[...], w1bf_ref[...], preferred_element_type=F32)
    yb = jnp.dot(a2_ref[...], w2bf_ref[...], preferred_element_type=F32)
    o_ref[...] = (jax.nn.sigmoid(ga_ref[...]) * ya + jax.nn.sigmoid(gb_ref[...]) * yb).astype(o_ref.dtype)


def _merge(a1, a2, w1, w2, gates, layer, tm=1024, tn=512):
    m, k1 = a1.shape
    k2 = a2.shape[1]
    n = w1.shape[-1]
    tm = _tile(m, tm)
    tn = _tile(n, tn)
    nb = n // tn
    return pl.pallas_call(
        _merge_kernel, grid=(nb, m // tm),
        in_specs=[pl.BlockSpec((tm, k1), lambda j, i: (i, 0)),
                  pl.BlockSpec((tm, k2), lambda j, i: (i, 0)),
                  _w_spec(w1, layer, k1, tn),
                  _w_spec(w2, layer, k2, tn),
                  pl.BlockSpec((tm, tn), lambda j, i: (i, j)),
                  pl.BlockSpec((tm, tn), lambda j, i: (i, j + nb))],
        out_specs=pl.BlockSpec((tm, tn), lambda j, i: (i, j)),
        out_shape=jax.ShapeDtypeStruct((m, n), BF16),
        scratch_shapes=[pltpu.VMEM((k1, tn), BF16), pltpu.VMEM((k2, tn), BF16)], name="branch_merge",
        compiler_params=_cp("parallel", "arbitrary"),
    )(a1, a2, w1, w2, gates, gates)


def _s5_discretize(lam_re, lam_im, b_re, b_im, c_re, c_im, d, log_step, t_chunk):
    hp = lax.Precision.HIGHEST
    g, p, h = b_re.shape
    na = LANES // h
    nx = g // na
    assert 2 * p == LANES
    lam = lax.complex(jnp.minimum(lam_re, -1e-4), lam_im)
    dt = jnp.exp(log_step)[:, None]
    lam_bar = jnp.exp(lam * dt)
    b_bar = ((lam_bar - 1.0) / lam)[:, :, None] * lax.complex(b_re, b_im)
    c = lax.complex(c_re, c_im)
    tau = jnp.arange(t_chunk + 1, dtype=F32)
    pw = jnp.exp((lam * dt)[:, None, :] * tau[None, :, None].astype(jnp.complex64))
    cp = c[:, None, :, :] * pw[:, :, None, :]
    kern = jnp.einsum('gtop,gpi->gtoi', cp[:, :t_chunk], b_bar, precision=hp).real
    kio = jnp.transpose(kern.reshape(nx, na, t_chunk, h, h), (0, 2, 1, 4, 3))
    kt = jnp.broadcast_to(kio[:, :, :, :, None, :], (nx, t_chunk, na, h, na, h)).reshape(nx, t_chunk, LANES, LANES)
    wcx = pw[:, t_chunk - 1 - np.arange(t_chunk)][:, :, :, None] * b_bar[:, None, :, :]
    wcx = jnp.transpose(wcx, (0, 1, 3, 2))
    w_ri = jnp.concatenate([wcx.real, wcx.imag], axis=-1).reshape(nx, na, t_chunk, h, 2 * p)
    wc = jnp.transpose(w_ri, (0, 2, 1, 3, 4)).reshape(nx, t_chunk, LANES, 2 * p)
    cl = jnp.transpose(cp[:, 1:t_chunk + 1], (0, 3, 1, 2))
    vc = jnp.concatenate([cl.real, -cl.imag], axis=1).reshape(nx, na, 2 * p, t_chunk * h)
    dd = jnp.broadcast_to(d.reshape(nx, 1, 1, na, h), (nx, 1, t_chunk, na, h)).reshape(nx, 1, t_chunk * LANES)
    a = pw[:, t_chunk]
    a1 = jnp.concatenate([a.real, a.real], axis=-1).reshape(1, g * 2 * p)
    a2 = jnp.concatenate([-a.imag, a.imag], axis=-1).reshape(1, g * 2 * p)
    return kt, wc, vc, dd, a1, a2


def _chunk_rows(u_ref, t, nc):
    return jnp.concatenate([u_ref[pl.ds(s, nc, stride=t), :] for s in range(t)], axis=1)


def _s5_state_in_kernel(u_ref, wc_ref, xa_ref, xb_ref, wa_ref, wb_ref, *, t, nc, h):
    na = LANES // h
    row_group = lax.broadcasted_iota(I32, (LANES, LANES), 0) // h
    for s in range(t):
        blk = wc_ref[s]
        swapped = pltpu.roll(blk, LANES // 2, axis=1)
        for b in range(na):
            own = row_group == b
            dst = (slice(s * LANES, (s + 1) * LANES), slice(b * LANES, (b + 1) * LANES))
            wa_ref[dst] = jnp.where(own, blk, 0.0).astype(BF16)
            wb_ref[dst] = jnp.where(own, swapped, 0.0).astype(BF16)
    ucat = _chunk_rows(u_ref, t, nc).astype(BF16)
    xa_ref[...] = jnp.dot(ucat, wa_ref[...], preferred_element_type=F32)
    xb_ref[...] = jnp.dot(ucat, wb_ref[...], preferred_element_type=F32)


def _s5_scan_kernel(xa_ref, xb_ref, a1_ref, a2_ref, s_ref, *, unroll):
    a1 = a1_ref[...]
    a2 = a2_ref[...]
    nc = xa_ref.shape[0]

    def body(cb, carry):
        s_a, s_b = carry
        base = pl.multiple_of(cb * unroll, unroll)
        xa = xa_ref[pl.ds(base, unroll), :]
        xb = xb_ref[pl.ds(base, unroll), :]
        rows = []
        for k in range(unroll):
            rows.append(s_a)
            s_a, s_b = (a1 * s_a + a2 * s_b + xa[k:k + 1, :],
                        a1 * s_b - a2 * s_a + xb[k:k + 1, :])
        s_ref[pl.ds(base, unroll), :] = jnp.concatenate(rows, axis=0)
        return s_a, s_b

    z = jnp.zeros(a1.shape, F32)
    lax.fori_loop(0, nc // unroll, body, (z, z))


def _s5_out_kernel(u_ref, sp_ref, kt_ref, vc_ref, dd_ref, y_ref, bd_ref, vm_ref, *, t, nc, h, n_parts):
    part = pl.program_id(1)
    tp = t // n_parts
    na = LANES // h
    w = tp * LANES
    same_group = (lax.broadcasted_iota(I32, (LANES, LANES), 0) // h
                  == lax.broadcasted_iota(I32, (LANES, LANES), 1) // h)
    row = lax.broadcasted_iota(I32, (LANES, w), 0)
    col = lax.broadcasted_iota(I32, (LANES, w), 1)
    landing_group = jnp.where((row // h == col // LANES) & (row % h == col % h), (col // h) % na, -1)
    ucat = _chunk_rows(u_ref, t, nc)
    for k in range(n_parts):
        @pl.when(part == k)
        def _(k=k):
            for s in range(t):
                for tt in range(tp):
                    lag = k * tp + tt - s
                    dst = (slice(s * LANES, (s + 1) * LANES), slice(tt * LANES, (tt + 1) * LANES))
                    if lag >= 0:
                        bd_ref[dst] = jnp.where(same_group, kt_ref[lag], 0.0).astype(BF16)
                    else:
                        bd_ref[dst] = jnp.zeros((LANES, LANES), BF16)
            for a in range(na):
                place = jnp.where(landing_group == a, 1.0, 0.0).astype(BF16)
                src = vc_ref[a][:, k * tp * h:(k + 1) * tp * h].astype(BF16)
                vm_ref[a * LANES:(a + 1) * LANES, :] = jnp.dot(src, place, preferred_element_type=F32).astype(BF16)
            y = jnp.dot(ucat.astype(BF16), bd_ref[...], preferred_element_type=F32)
            y = y + jnp.dot(sp_ref[...].astype(BF16), vm_ref[...], preferred_element_type=F32)
            y = jax.nn.gelu(y + dd_ref[...] * ucat[:, k * w:(k + 1) * w])
            for s in range(tp):
                y_ref[pl.ds(k * tp + s, nc, stride=t), :] = y[:, s * LANES:(s + 1) * LANES]


def _s5_branch(u, prm):
    kt, wc, vc, dd, a1, a2 = prm
    L, width = u.shape
    nx, na, _, th = vc.shape
    t = kt.shape[1]
    h = th // t
    tah = t * LANES
    sw = na * LANES
    nc = L // t
    u_spec = pl.BlockSpec((L, LANES), lambda x: (0, x))
    st_spec = pl.BlockSpec((nc, sw), lambda x: (0, x))
    xa, xb = pl.pallas_call(
        functools.partial(_s5_state_in_kernel, t=t, nc=nc, h=h), grid=(nx,),
        in_specs=[u_spec,
                  pl.BlockSpec((None, t, LANES, LANES), lambda x: (x, 0, 0, 0))],
        out_specs=[st_spec, st_spec],
        out_shape=[jax.ShapeDtypeStruct((nc, nx * sw), F32)] * 2,
        scratch_shapes=[pltpu.VMEM((tah, sw), BF16), pltpu.VMEM((tah, sw), BF16)], name="s5_state_in",
        compiler_params=_cp("parallel"),
    )(u, wc)
    ws = _tile(nx * sw, 2048)
    unroll = _tile(nc, 8)
    sc_spec = pl.BlockSpec((nc, ws), lambda x: (0, x))
    a_spec = pl.BlockSpec((1, ws), lambda x: (0, x))
    sp = pl.pallas_call(
        functools.partial(_s5_scan_kernel, unroll=unroll), grid=(nx * sw // ws,),
        in_specs=[sc_spec, sc_spec, a_spec, a_spec],
        out_specs=sc_spec,
        out_shape=jax.ShapeDtypeStruct((nc, nx * sw), F32), name="s5_scan",
        compiler_params=_cp("parallel"),
    )(xa, xb, a1, a2)
    n_parts = 2
    tw = tah // n_parts
    return pl.pallas_call(
        functools.partial(_s5_out_kernel, t=t, nc=nc, h=h, n_parts=n_parts), grid=(nx, n_parts),
        in_specs=[pl.BlockSpec((L, LANES), lambda x, k: (0, x)),
                  pl.BlockSpec((nc, sw), lambda x, k: (0, x)),
                  pl.BlockSpec((None, t, LANES, LANES), lambda x, k: (x, 0, 0, 0)),
                  pl.BlockSpec((None, na, LANES, th), lambda x, k: (x, 0, 0, 0)),
                  pl.BlockSpec((None, 1, tw), lambda x, k: (x, 0, k))],
        out_specs=pl.BlockSpec((L, LANES), lambda x, k: (0, x)),
        out_shape=jax.ShapeDtypeStruct((L, width), F32),
        scratch_shapes=[pltpu.VMEM((tah, tw), BF16), pltpu.VMEM((sw, tw), BF16)], name="s5_out",
        compiler_params=_cp("parallel", "arbitrary"),
    )(u, sp, kt, vc, dd)


def _t5_bucket_starts(n_buckets):
    exact = n_buckets // 2
    dist = np.arange(0, 4 * REL_MAX_DIST).astype(np.int64)
    far = exact + (np.log(np.maximum(dist, 1).astype(np.float32) / np.float32(exact))
                   / np.float32(math.log(REL_MAX_DIST / exact)) * np.float32(n_buckets - exact)).astype(np.int64)
    bucket = np.where(dist < exact, dist, np.minimum(far, n_buckets - 1))
    return [int(np.argmax(bucket >= b)) for b in range(1, n_buckets)]


def _bias_tile_kernel(rb_ref, o_ref, *, tb, starts):
    delta = pl.program_id(0)
    head = pl.program_id(1)
    key = lax.broadcasted_iota(I32, (tb, tb), 0)
    qry = lax.broadcasted_iota(I32, (tb, tb), 1)
    dist = delta * tb + qry - key
    bias = jnp.full((tb, tb), rb_ref[0, head], F32)
    for b, start in enumerate(starts, start=1):
        bias = jnp.where(dist >= start, rb_ref[b, head], bias)
    o_ref[...] = jnp.where(dist >= 0, bias * LOG2E, NEG_BIG)


def _bias_tiles(rel_bias, tb):
    n_buckets, heads = rel_bias.shape
    starts = _t5_bucket_starts(n_buckets)
    assert 2 * tb - (tb - 1) >= starts[-1]
    return pl.pallas_call(
        functools.partial(_bias_tile_kernel, tb=tb, starts=starts), grid=(2, heads),
        in_specs=[pl.BlockSpec(memory_space=pltpu.SMEM)],
        out_specs=pl.BlockSpec((None, None, tb, tb), lambda dl, hd: (dl, hd, 0, 0)),
        out_shape=jax.ShapeDtypeStruct((2, heads, tb, tb), F32), name="rel_bias_tiles",
        compiler_params=_cp("parallel", "parallel"),
    )(rel_bias)


def _indexer_kernel(qt_ref, k2_ref, wt_ref, keys_ref, thr_ref, *, tq, ck, n_heads, topk, w_scale):
    i = pl.program_id(0)
    n_blocks = keys_ref.shape[0] // tq
    n_live = (i + 1) * (tq // ck)
    wv = wt_ref[...] * w_scale

    def score_chunk(c, carry):
        row = pl.multiple_of(c * ck, ck)
        kk = k2_ref[c]
        acc = jnp.zeros((ck, tq), F32)
        for pr in range(n_heads // 2):
            s2 = jnp.dot(kk, qt_ref[pr * LANES:(pr + 1) * LANES, :], preferred_element_type=F32)
            acc = acc + jnp.maximum(s2[:ck], 0.0) * wv[2 * pr:2 * pr + 1, :]
            acc = acc + jnp.maximum(s2[ck:], 0.0) * wv[2 * pr + 1:2 * pr + 2, :]
        kpos = row + lax.broadcasted_iota(I32, (ck, tq), 0)
        qpos = i * tq + lax.broadcasted_iota(I32, (ck, tq), 1)
        score = jnp.where(kpos <= qpos, acc, -jnp.inf)
        bits = pltpu.bitcast(score, I32)
        keys_ref[pl.ds(row, ck), :] = jnp.where(bits < 0, bits ^ 0x7FFFFFFF, bits)
        return carry

    lax.fori_loop(0, n_live, score_chunk, 0)

    def fill_block(b, carry):
        row = pl.multiple_of(b * tq, tq)
        keys_ref[pl.ds(row, tq), :] = jnp.full((tq, tq), INT_MIN, I32)
        return carry

    lax.fori_loop(i + 1, n_blocks, fill_block, 0)

    def count_ge(cand):
        def body(b, cnt):
            row = pl.multiple_of(b * tq, tq)
            hit = jnp.where(keys_ref[pl.ds(row, tq), :] >= cand, 1.0, 0.0)
            return cnt + jnp.sum(hit.reshape(tq // 8, 8, tq), axis=0)
        cnt = lax.fori_loop(0, i + 1, body, jnp.zeros((8, tq), F32))
        return jnp.sum(cnt, axis=0, keepdims=True)

    kf = float(topk)
    zero = jnp.zeros((1, tq), I32)
    c_zero = count_ge(zero)
    thr0 = jnp.where(c_zero >= kf, zero, jnp.full((1, tq), INT_MIN, I32))
    cnt0 = jnp.where(c_zero >= kf, c_zero, ((i + 1) * tq).astype(F32))

    def undecided(st):
        it, _, cnt = st
        return (it < 31) & (jnp.max(cnt) > kf)

    def bit_step(st):
        it, thr, cnt = st
        cand = thr | jnp.left_shift(jnp.int32(1), 30 - it)
        c = count_ge(cand)
        keep = c >= kf
        return it + 1, jnp.where(keep, cand, thr), jnp.where(keep, c, cnt)

    _, thr, _ = lax.while_loop(undecided, bit_step, (jnp.int32(0), thr0, cnt0))
    thr_ref[...] = jnp.broadcast_to(thr, thr_ref.shape)


def _indexer(qi_t, row_off, k2, w_t, topk, tq=256):
    L = qi_t.shape[1]
    n_heads = w_t.shape[0]
    qw = n_heads * IDX_DIM
    assert row_off % qw == 0
    tq = _tile(L, tq)
    ck = k2.shape[1] // 2
    assert tq % ck == 0
    w_scale = (n_heads ** -0.5) * (IDX_DIM ** -0.5)
    return pl.pallas_call(
        functools.partial(_indexer_kernel, tq=tq, ck=ck, n_heads=n_heads, topk=topk, w_scale=w_scale),
        grid=(L // tq,),
        in_specs=[pl.BlockSpec((qw, tq), lambda i: (row_off // qw, i)),
                  pl.BlockSpec(k2.shape, lambda i: (0, 0, 0)),
                  pl.BlockSpec((n_heads, tq), lambda i: (0, i))],
        out_specs=[pl.BlockSpec((L, tq), lambda i: (0, i)),
                   pl.BlockSpec((8, tq), lambda i: (0, i))],
        out_shape=[jax.ShapeDtypeStruct((L, L), I32), jax.ShapeDtypeStruct((8, L), I32)],
        name="dsa_indexer", compiler_params=_cp("parallel"),
    )(qi_t, k2, w_t)


def _attn_kernel(qi_ref, kj_ref, far_ref, qt_ref, k_ref, vt_ref, keys_ref, thr_ref, bias_ref, o_ref,
                 m_ref, l_ref, acc_ref, mb_ref, *, heads, dh, scale2):
    step = pl.program_id(0)
    qi = qi_ref[step]
    kj = kj_ref[step]

    @pl.when(kj == 0)
    def _():
        m_ref[...] = jnp.full(m_ref.shape, -jnp.inf, F32)
        l_ref[...] = jnp.zeros(l_ref.shape, F32)
        acc_ref[...] = jnp.zeros(acc_ref.shape, F32)

    mb_ref[...] = jnp.where(keys_ref[...] >= thr_ref[0:1, :], 0.0, NEG_BIG)

    def scores(h):
        hs = slice(h * dh, (h + 1) * dh)
        return jnp.dot(k_ref[:, hs], qt_ref[hs, :], preferred_element_type=F32)

    def run(near):
        ahead = 4
        pending = [scores(h) for h in range(min(ahead, heads))]
        for h in range(heads):
            hs = slice(h * dh, (h + 1) * dh)
            x = pending.pop(0) * scale2 + mb_ref[...]
            if h + ahead < heads:
                pending.append(scores(h + ahead))
            if near:
                x = x + bias_ref[h]
            far = 0.0 if near else far_ref[h]
            m_prev = m_ref[h]
            m_new = jnp.maximum(m_prev, jnp.max(x, axis=0, keepdims=True) + far)
            alpha = jnp.exp2(m_prev - m_new)
            p = jnp.exp2(x - (m_new[0:1, :] - far))
            l_ref[h] = alpha * l_ref[h] + jnp.sum(p, axis=0, keepdims=True)
            m_ref[h] = m_new
            acc_ref[hs, :] = alpha[0:1, :] * acc_ref[hs, :] + jnp.dot(vt_ref[hs, :], p.astype(BF16),
                                                                      preferred_element_type=F32)

    @pl.when(qi - kj < 2)
    def _():
        run(True)

    @pl.when(qi - kj >= 2)
    def _():
        run(False)

    @pl.when(kj == qi)
    def _():
        for h in range(heads):
            hs = slice(h * dh, (h + 1) * dh)
            o_ref[:, hs] = jnp.transpose(acc_ref[hs, :] / l_ref[h][0:1, :]).astype(o_ref.dtype)


def _attention(qv_t, q_blk, v_blk, k, keys, thr, bias_tiles, far_bias, tb):
    L = k.shape[0]
    heads = bias_tiles.shape[1]
    dh = ATT_HEAD_DIM
    aw = heads * dh
    nb = L // tb
    pairs = [(i, j) for i in range(nb) for j in range(i + 1)]
    qi = jnp.asarray(np.array([p[0] for p in pairs], np.int32))
    kj = jnp.asarray(np.array([p[1] for p in pairs], np.int32))
    grid_spec = pltpu.PrefetchScalarGridSpec(
        num_scalar_prefetch=2, grid=(len(pairs),),
        in_specs=[pl.BlockSpec(memory_space=pltpu.SMEM),
                  pl.BlockSpec((aw, tb), lambda s, qi, kj: (q_blk, qi[s])),
                  pl.BlockSpec((tb, aw), lambda s, qi, kj: (kj[s], 0)),
                  pl.BlockSpec((aw, tb), lambda s, qi, kj: (v_blk, kj[s])),
                  pl.BlockSpec((tb, tb), lambda s, qi, kj: (kj[s], qi[s])),
                  pl.BlockSpec((8, tb), lambda s, qi, kj: (0, qi[s])),
                  pl.BlockSpec((None, heads, tb, tb),
                               lambda s, qi, kj: (jnp.minimum(qi[s] - kj[s], 1), 0, 0, 0))],
        out_specs=pl.BlockSpec((tb, aw), lambda s, qi, kj: (qi[s], 0)),
        scratch_shapes=[pltpu.VMEM((heads, 8, tb), F32),
                        pltpu.VMEM((heads, 8, tb), F32),
                        pltpu.VMEM((aw, tb), F32),
                        pltpu.VMEM((tb, tb), F32)])
    return pl.pallas_call(
        functools.partial(_attn_kernel, heads=heads, dh=dh, scale2=dh ** -0.5 * LOG2E),
        grid_spec=grid_spec, out_shape=jax.ShapeDtypeStruct((L, aw), BF16), name="dsa_attention",
        compiler_params=_cp("arbitrary"),
    )(qi, kj, far_bias, qv_t, k, qv_t, keys, thr, bias_tiles)


def _router_kernel(h_ref, w_ref, b_ref, g_ref, *, n_exp, top_k):
    logits = jnp.dot(h_ref[...], w_ref[...], preferred_element_type=F32) + b_ref[...]
    lane = lax.broadcasted_iota(I32, logits.shape, 1).astype(F32)
    work = jnp.where(lane < n_exp, logits, -jnp.inf)
    gates = jnp.zeros(logits.shape, F32)
    denom = None
    top = None
    for r in range(top_k):
        m = jnp.max(work, axis=1, keepdims=True)
        first = jnp.min(jnp.where(work == m, lane, float(LANES)), axis=1, keepdims=True)
        sel = lane == first
        if r == 0:
            top = m
        e = jnp.exp(m - top)
        gates = gates + jnp.where(sel, e, 0.0)
        denom = e if denom is None else denom + e
        work = jnp.where(sel, -jnp.inf, work)
    g_ref[...] = gates / denom


def _router(h, w_pad, b_pad, n_exp, tm=512):
    m, d = h.shape
    tm = _tile(m, tm)
    return pl.pallas_call(
        functools.partial(_router_kernel, n_exp=n_exp, top_k=TOP_K_EXPERTS), grid=(m // tm,),
        in_specs=[pl.BlockSpec((tm, d), lambda i: (i, 0)),
                  pl.BlockSpec((d, LANES), lambda i: (0, 0)),
                  pl.BlockSpec((1, LANES), lambda i: (0, 0))],
        out_specs=pl.BlockSpec((tm, LANES), lambda i: (i, 0)),
        out_shape=jax.ShapeDtypeStruct((m, LANES), F32), name="moe_router",
        compiler_params=_cp("parallel"),
    )(h, w_pad, b_pad)


def _moe_kernel(h_ref, g_ref, wgu_ref, bgu_ref, wdn_ref, bdn_ref, o_ref, acc_ref, *, ff):
    e = pl.program_id(1)
    gates = g_ref[...]

    @pl.when(e == 0)
    def _():
        acc_ref[...] = jnp.dot(gates.astype(BF16), bdn_ref[...], preferred_element_type=F32)

    gu = jnp.dot(h_ref[...], wgu_ref[...], preferred_element_type=F32) + bgu_ref[...]
    gate = jnp.minimum(gu[:, :ff], SWIGLU_LIMIT)
    up = jnp.clip(gu[:, ff:], -SWIGLU_LIMIT, SWIGLU_LIMIT)
    act = (up + 1.0) * gate * jax.nn.sigmoid(SWIGLU_ALPHA * gate)
    lane = lax.broadcasted_iota(I32, gates.shape, 1)
    ge = jnp.sum(jnp.where(lane == e, gates, 0.0), axis=1, keepdims=True)
    acc_ref[...] += jnp.dot((act * ge).astype(BF16), wdn_ref[...], preferred_element_type=F32)

    @pl.when(e == pl.num_programs(1) - 1)
    def _():
        o_ref[...] = acc_ref[...]


def _moe_experts(h, gates, w_gu, b_gu, w_dn, b_dn_pad, tm=512):
    m, d = h.shape
    n_exp, _, ff2 = w_gu.shape
    tm = _tile(m, tm)
    return pl.pallas_call(
        functools.partial(_moe_kernel, ff=ff2 // 2), grid=(m // tm, n_exp),
        in_specs=[pl.BlockSpec((tm, d), lambda i, e: (i, 0)),
                  pl.BlockSpec((tm, LANES), lambda i, e: (i, 0)),
                  pl.BlockSpec((None, d, ff2), lambda i, e: (e, 0, 0)),
                  pl.BlockSpec((None, 1, ff2), lambda i, e: (e, 0, 0)),
                  pl.BlockSpec((None, ff2 // 2, d), lambda i, e: (e, 0, 0)),
                  pl.BlockSpec((LANES, d), lambda i, e: (0, 0))],
        out_specs=pl.BlockSpec((tm, d), lambda i, e: (i, 0)),
        out_shape=jax.ShapeDtypeStruct((m, d), F32),
        scratch_shapes=[pltpu.VMEM((tm, d), F32)], name="moe_experts",
        compiler_params=_cp("parallel", "arbitrary"),
    )(h, gates, w_gu, b_gu.reshape(n_exp, 1, ff2), w_dn, b_dn_pad)


def kernel(x, c, w_in, w_a_out, w_b_out, w_o, w_glu, b_glu, s5_lam_re, s5_lam_im, s5_b_re, s5_b_im, s5_c_re, s5_c_im, s5_d, s5_log_step, rel_bias, w_router, b_router, w_gu, b_gu, w_dn, b_dn, w_c1, b_c1, w_c2, b_c2, mod_table, ln_g, ln_b):
    bsz, L, d = x.shape
    assert bsz == 1 and c.shape[0] == 1
    depth = w_in.shape[0]
    g, p, h = s5_b_re.shape[1:]
    s5w = g * h
    heads = rel_bias.shape[1]
    aw = heads * ATT_HEAD_DIM
    iw = IDX_HEADS * IDX_DIM
    n_exp = w_router.shape[2]
    assert w_in.shape[2] == s5w + 3 * aw + iw + IDX_DIM + IDX_HEADS + 2 * d
    assert n_exp <= LANES and IDX_DIM + IDX_HEADS <= LANES and 2 * IDX_DIM == LANES
    o_q, o_k, o_v, o_qi = s5w, s5w + aw, s5w + 2 * aw, s5w + 3 * aw
    o_ki = s5w + 3 * aw + iw
    o_gate = o_ki + IDX_DIM + IDX_HEADS
    topk = min(TOPK_MAX, L // 4)
    tb = _tile(L, 256)
    ck = _tile(L, 128)
    dn_alpha = (2 * depth) ** 0.25

    mod = _conditioning(c, w_c1, b_c1, w_c2, b_c2, mod_table).reshape(depth, N_MOD, d)
    bias_tiles = _bias_tiles(rel_bias, tb)
    far_bias = rel_bias[-1] * LOG2E
    assert o_ki % LANES == 0 and (g * h) % LANES == 0 and LANES % h == 0

    xs = x.reshape(L, d)
    hcur = _modulate(xs, mod[0], sc_row=1, sh_row=0)
    for l in range(depth):
        w_l = w_in[l]
        w_nat = jnp.concatenate(
            [w_l[:, :s5w], w_l[:, o_k:o_k + aw], w_l[:, o_gate:],
             jnp.pad(w_l[:, o_ki:o_gate], ((0, 0), (0, LANES - IDX_DIM - IDX_HEADS)))], axis=1).astype(BF16)
        w_tr = jnp.transpose(jnp.concatenate(
            [w_l[:, o_q:o_q + aw], w_l[:, o_v:o_v + aw], w_l[:, o_qi:o_qi + iw]], axis=1)).astype(BF16)
        u = _matmul(hcur, w_nat, out_dtype=F32, col_off=0, n_cols=s5w, name="proj_u")
        k = _matmul(hcur, w_nat, out_dtype=BF16, col_off=s5w, n_cols=aw, name="proj_k")
        gates_ab = _matmul(hcur, w_nat, out_dtype=F32, col_off=s5w + aw, n_cols=2 * d, name="proj_gates")
        misc = _matmul(hcur, w_nat, out_dtype=F32, col_off=s5w + aw + 2 * d, n_cols=LANES, tn=LANES,
                       name="proj_misc")
        qvi_t = _matmul_t(hcur, w_tr, out_dtype=BF16, name="proj_qvi_t")

        prm = _s5_discretize(s5_lam_re[l], s5_lam_im[l], s5_b_re[l], s5_b_im[l], s5_c_re[l], s5_c_im[l],
                             s5_d[l], s5_log_step[l], S5_CHUNK)
        y_s5 = _s5_branch(u, prm)
        ya_in = _glu(y_s5, w_glu, b_glu[l], l)

        k_idx = misc[:, :IDX_DIM].astype(BF16)
        zeros = jnp.zeros_like(k_idx)
        k_even = jnp.concatenate([k_idx, zeros], axis=1).reshape(L // ck, ck, LANES)
        k_odd = jnp.concatenate([zeros, k_idx], axis=1).reshape(L // ck, ck, LANES)
        k2 = jnp.concatenate([k_even, k_odd], axis=1)
        w_t = jnp.transpose(misc[:, IDX_DIM:IDX_DIM + IDX_HEADS])
        keys, thr = _indexer(qvi_t, 2 * aw, k2, w_t, topk)
        attn = _attention(qvi_t, 0, 1, k, keys, thr, bias_tiles, far_bias, tb)

        merged = _merge(ya_in, attn, w_a_out, w_b_out, gates_ab, l)
        y = _matmul(merged, w_o, layer=l, out_dtype=F32, name="proj_out")
        xs, h2 = _ln_modulate(xs, y, mod[l], ln_g[l], ln_b[l], mod[l], alpha=dn_alpha, gate_row=2, ln_row=0,
                              sc_row=4, sh_row=3, want_h=True)

        w_r = jnp.pad(w_router[l], ((0, 0), (0, LANES - n_exp))).astype(BF16)
        b_r = jnp.pad(b_router[l], (0, LANES - n_exp)).reshape(1, LANES)
        gates = _router(h2, w_r, b_r, n_exp)
        b_dn_pad = jnp.pad(b_dn[l], ((0, LANES - n_exp), (0, 0))).astype(BF16)
        y = _moe_experts(h2, gates, w_gu[l].astype(BF16), b_gu[l], w_dn[l].astype(BF16), b_dn_pad)
        last = l == depth - 1
        xs, hcur = _ln_modulate(xs, y, mod[l], ln_g[l], ln_b[l], mod[min(l + 1, depth - 1)], alpha=dn_alpha,
                                gate_row=5, ln_row=1, sc_row=1, sh_row=0, want_h=not last)
    return xs.reshape(bsz, L, d)
```

```python
import functools
import math

import numpy as np
import jax
import jax.numpy as jnp
from jax import lax
from jax.experimental import pallas as pl
from jax.experimental.pallas import tpu as pltpu

F32 = jnp.float32
BF16 = jnp.bfloat16
I32 = jnp.int32

ATT_HEAD_DIM = 128
IDX_HEADS = 32
IDX_DIM = 64
TOPK_MAX = 256
REL_MAX_DIST = 128
TOP_K_EXPERTS = 4
SWIGLU_LIMIT = 7.0
SWIGLU_ALPHA = 1.702
N_MOD = 6
LN_EPS = 1e-5

LANES = 128
VMEM_LIMIT_BYTES = 56 * 1024 * 1024
S5_CHUNK = 16
NEG_BIG = -1e30
LOG2E = math.log2(math.e)
INT_MIN = -(2 ** 31)


def _cp(*sem):
    return pltpu.CompilerParams(dimension_semantics=sem, vmem_limit_bytes=VMEM_LIMIT_BYTES)


def _tile(n, t):
    t = min(n, t)
    while n % t:
        t //= 2
    return t


def _cond1_kernel(c_ref, w_ref, b_ref, o_ref):
    a = jax.nn.silu(c_ref[...]).astype(BF16)
    t = jnp.dot(a, w_ref[...].astype(BF16), preferred_element_type=F32) + b_ref[...]
    o_ref[...] = jax.nn.silu(t)


def _cond2_kernel(t_ref, w_ref, b_ref, tab_ref, o_ref):
    acc = jnp.dot(t_ref[...].astype(BF16), w_ref[...].astype(BF16), preferred_element_type=F32)
    o_ref[...] = acc[0:1, :] + b_ref[...] + tab_ref[...]


def _conditioning(c, w_c1, b_c1, w_c2, b_c2, mod_table):
    d = c.shape[1]
    r = w_c1.shape[1]
    depth = mod_table.shape[0]
    n = w_c2.shape[1]
    c8 = jnp.broadcast_to(c, (8, d))
    t = pl.pallas_call(
        _cond1_kernel, out_shape=jax.ShapeDtypeStruct((8, r), F32), name="cond1",
        compiler_params=_cp(),
    )(c8, w_c1, b_c1.reshape(1, r))
    tn = _tile(n, 2048)
    return pl.pallas_call(
        _cond2_kernel, grid=(n // tn,),
        in_specs=[pl.BlockSpec((8, r), lambda j: (0, 0)),
                  pl.BlockSpec((r, tn), lambda j: (0, j)),
                  pl.BlockSpec((1, tn), lambda j: (0, j)),
                  pl.BlockSpec((depth, tn), lambda j: (0, j))],
        out_specs=pl.BlockSpec((depth, tn), lambda j: (0, j)),
        out_shape=jax.ShapeDtypeStruct((depth, n), F32), name="cond2",
        compiler_params=_cp("parallel"),
    )(t, w_c2, b_c2.reshape(1, n), mod_table.reshape(depth, n))


def _modulate_kernel(x_ref, mod_ref, h_ref, *, sc_row, sh_row):
    sc = mod_ref[sc_row:sc_row + 1, :]
    sh = mod_ref[sh_row:sh_row + 1, :]
    h_ref[...] = (x_ref[...] * (1.0 + sc) + sh).astype(h_ref.dtype)


def _modulate(x, mod_l, sc_row, sh_row, tm=256):
    m, d = x.shape
    tm = _tile(m, tm)
    return pl.pallas_call(
        functools.partial(_modulate_kernel, sc_row=sc_row, sh_row=sh_row),
        grid=(m // tm,),
        in_specs=[pl.BlockSpec((tm, d), lambda i: (i, 0)),
                  pl.BlockSpec((N_MOD, d), lambda i: (0, 0))],
        out_specs=pl.BlockSpec((tm, d), lambda i: (i, 0)),
        out_shape=jax.ShapeDtypeStruct((m, d), BF16), name="modulate",
        compiler_params=_cp("parallel"),
    )(x, mod_l)


def _ln_kernel(x_ref, y_ref, mod_ref, g_ref, b_ref, nmod_ref, xo_ref, *h_refs,
               alpha, gate_row, ln_row, sc_row, sh_row):
    gate = mod_ref[gate_row:gate_row + 1, :]
    u = alpha * x_ref[...] + (1.0 + gate) * y_ref[...]
    mu = jnp.mean(u, axis=-1, keepdims=True)
    uc = u - mu
    var = jnp.mean(uc * uc, axis=-1, keepdims=True)
    xn = uc * lax.rsqrt(var + LN_EPS) * g_ref[ln_row:ln_row + 1, :] + b_ref[ln_row:ln_row + 1, :]
    xo_ref[...] = xn
    if h_refs:
        sc = nmod_ref[sc_row:sc_row + 1, :]
        sh = nmod_ref[sh_row:sh_row + 1, :]
        h_refs[0][...] = (xn * (1.0 + sc) + sh).astype(BF16)


def _ln_modulate(x, y, mod_l, ln_g_l, ln_b_l, next_mod, *, alpha, gate_row, ln_row, sc_row, sh_row,
                 want_h, tm=256):
    m, d = x.shape
    tm = _tile(m, tm)
    row = pl.BlockSpec((tm, d), lambda i: (i, 0))
    out_shape = [jax.ShapeDtypeStruct((m, d), F32)]
    out_specs = [row]
    if want_h:
        out_shape.append(jax.ShapeDtypeStruct((m, d), BF16))
        out_specs.append(row)
    res = pl.pallas_call(
        functools.partial(_ln_kernel, alpha=alpha, gate_row=gate_row, ln_row=ln_row,
                          sc_row=sc_row, sh_row=sh_row),
        grid=(m // tm,),
        in_specs=[row, row,
                  pl.BlockSpec((N_MOD, d), lambda i: (0, 0)),
                  pl.BlockSpec((2, d), lambda i: (0, 0)),
                  pl.BlockSpec((2, d), lambda i: (0, 0)),
                  pl.BlockSpec((N_MOD, d), lambda i: (0, 0))],
        out_specs=out_specs, out_shape=out_shape, name="ln_modulate",
        compiler_params=_cp("parallel"),
    )(x, y, mod_l, ln_g_l, ln_b_l, next_mod)
    return (res[0], res[1]) if want_h else (res[0], None)


def _mm_kernel(a_ref, w_ref, o_ref, wbf_ref):
    @pl.when(pl.program_id(1) == 0)
    def _():
        wbf_ref[...] = w_ref[...].astype(BF16)

    o_ref[...] = jnp.dot(a_ref[...], wbf_ref[...], preferred_element_type=F32).astype(o_ref.dtype)


def _w_spec(w, layer, k, tn, off=0):
    if w.ndim == 3:
        return pl.BlockSpec((None, k, tn), lambda j, i: (layer, 0, j + off))
    return pl.BlockSpec((k, tn), lambda j, i: (0, j + off))


def _matmul(a, w, *, out_dtype, layer=0, col_off=0, n_cols=None, tm=1024, tn=512, name="matmul"):
    m, k = a.shape
    n_cols = w.shape[-1] - col_off if n_cols is None else n_cols
    tm = _tile(m, tm)
    tn = math.gcd(_tile(n_cols, tn), col_off) if col_off else _tile(n_cols, tn)
    off = col_off // tn
    return pl.pallas_call(
        _mm_kernel, grid=(n_cols // tn, m // tm),
        in_specs=[pl.BlockSpec((tm, k), lambda j, i: (i, 0)),
                  _w_spec(w, layer, k, tn, off)],
        out_specs=pl.BlockSpec((tm, tn), lambda j, i: (i, j)),
        out_shape=jax.ShapeDtypeStruct((m, n_cols), out_dtype),
        scratch_shapes=[pltpu.VMEM((k, tn), BF16)], name=name,
        compiler_params=_cp("parallel", "arbitrary"),
    )(a, w)


def _mm_t_kernel(a_ref, wt_ref, o_ref):
    o_ref[...] = lax.dot_general(wt_ref[...], a_ref[...], (((1,), (1,)), ((), ())),
                                 preferred_element_type=F32).astype(o_ref.dtype)


def _matmul_t(a, w_t, *, out_dtype, tm=1024, tn=1024, name="matmul_t"):
    m, k = a.shape
    n = w_t.shape[0]
    tm = _tile(m, tm)
    tn = _tile(n, tn)
    return pl.pallas_call(
        _mm_t_kernel, grid=(n // tn, m // tm),
        in_specs=[pl.BlockSpec((tm, k), lambda j, i: (i, 0)),
                  pl.BlockSpec((tn, k), lambda j, i: (j, 0))],
        out_specs=pl.BlockSpec((tn, tm), lambda j, i: (j, i)),
        out_shape=jax.ShapeDtypeStruct((n, m), out_dtype), name=name,
        compiler_params=_cp("parallel", "arbitrary"),
    )(a, w_t)


def _glu_kernel(y_ref, yt_ref, w_ref, b_ref, o_ref, wbf_ref):
    @pl.when(pl.program_id(1) == 0)
    def _():
        wbf_ref[...] = w_ref[...].astype(BF16)

    z = jnp.dot(y_ref[...].astype(BF16), wbf_ref[...], preferred_element_type=F32) + b_ref[...]
    o_ref[...] = (yt_ref[...] * jax.nn.sigmoid(z)).astype(o_ref.dtype)


def _glu(y, w, b, layer, tm=1024, tn=512):
    m, k = y.shape
    n = w.shape[-1]
    tm = _tile(m, tm)
    tn = _tile(n, tn)
    return pl.pallas_call(
        _glu_kernel, grid=(n // tn, m // tm),
        in_specs=[pl.BlockSpec((tm, k), lambda j, i: (i, 0)),
                  pl.BlockSpec((tm, tn), lambda j, i: (i, j)),
                  _w_spec(w, layer, k, tn),
                  pl.BlockSpec((1, tn), lambda j, i: (0, j))],
        out_specs=pl.BlockSpec((tm, tn), lambda j, i: (i, j)),
        out_shape=jax.ShapeDtypeStruct((m, n), BF16),
        scratch_shapes=[pltpu.VMEM((k, tn), BF16)], name="s5_glu",
        compiler_params=_cp("parallel", "arbitrary"),
    )(y, y, w, b.reshape(1, n))


def _merge_kernel(a1_ref, a2_ref, w1_ref, w2_ref, ga_ref, gb_ref, o_ref, w1bf_ref, w2bf_ref):
    @pl.when(pl.program_id(1) == 0)
    def _():
        w1bf_ref[...] = w1_ref[...].astype(BF16)
        w2bf_ref[...] = w2_ref[...].astype(BF16)

    ya = jnp.dot(a1_ref[...], w1bf_ref[...], preferred_element_type=F32)
    yb = jnp.dot(a2_ref[...], w2bf_ref[...], preferred_element_type=F32)
    o_ref[...] = (jax.nn.sigmoid(ga_ref[...]) * ya + jax.nn.sigmoid(gb_ref[...]) * yb).astype(o_ref.dtype)


def _merge(a1, a2, w1, w2, gates, layer, tm=1024, tn=512):
    m, k1 = a1.shape
    k2 = a2.shape[1]
    n = w1.shape[-1]
    tm = _tile(m, tm)
    tn = _tile(n, tn)
    nb = n // tn
    return pl.pallas_call(
        _merge_kernel, grid=(nb, m // tm),
        in_specs=[pl.BlockSpec((tm, k1), lambda j, i: (i, 0)),
                  pl.BlockSpec((tm, k2), lambda j, i: (i, 0)),
                  _w_spec(w1, layer, k1, tn),
                  _w_spec(w2, layer, k2, tn),
                  pl.BlockSpec((tm, tn), lambda j, i: (i, j)),
                  pl.BlockSpec((tm, tn), lambda j, i: (i, j + nb))],
        out_specs=pl.BlockSpec((tm, tn), lambda j, i: (i, j)),
        out_shape=jax.ShapeDtypeStruct((m, n), BF16),
        scratch_shapes=[pltpu.VMEM((k1, tn), BF16), pltpu.VMEM((k2, tn), BF16)], name="branch_merge",
        compiler_params=_cp("parallel", "arbitrary"),
    )(a1, a2, w1, w2, gates, gates)


def _s5_discretize(lam_re, lam_im, b_re, b_im, c_re, c_im, d, log_step, t_chunk):
    hp = lax.Precision.HIGHEST
    g, p, h = b_re.shape
    na = LANES // h
    nx = g // na
    assert 2 * p == LANES
    lam = lax.complex(jnp.minimum(lam_re, -1e-4), lam_im)
    dt = jnp.exp(log_step)[:, None]
    lam_bar = jnp.exp(lam * dt)
    b_bar = ((lam_bar - 1.0) / lam)[:, :, None] * lax.complex(b_re, b_im)
    c = lax.complex(c_re, c_im)
    tau = jnp.arange(t_chunk + 1, dtype=F32)
    pw = jnp.exp((lam * dt)[:, None, :] * tau[None, :, None].astype(jnp.complex64))
    cp = c[:, None, :, :] * pw[:, :, None, :]
    kern = jnp.einsum('gtop,gpi->gtoi', cp[:, :t_chunk], b_bar, precision=hp).real
    kio = jnp.transpose(kern.reshape(nx, na, t_chunk, h, h), (0, 2, 1, 4, 3))
    kt = jnp.broadcast_to(kio[:, :, :, :, None, :], (nx, t_chunk, na, h, na, h)).reshape(nx, t_chunk, LANES, LANES)
    wcx = pw[:, t_chunk - 1 - np.arange(t_chunk)][:, :, :, None] * b_bar[:, None, :, :]
    wcx = jnp.transpose(wcx, (0, 1, 3, 2))
    w_ri = jnp.concatenate([wcx.real, wcx.imag], axis=-1).reshape(nx, na, t_chunk, h, 2 * p)
    wc = jnp.transpose(w_ri, (0, 2, 1, 3, 4)).reshape(nx, t_chunk, LANES, 2 * p)
    cl = jnp.transpose(cp[:, 1:t_chunk + 1], (0, 3, 1, 2))
    vc = jnp.concatenate([cl.real, -cl.imag], axis=1).reshape(nx, na, 2 * p, t_chunk * h)
    dd = jnp.broadcast_to(d.reshape(nx, 1, 1, na, h), (nx, 1, t_chunk, na, h)).reshape(nx, 1, t_chunk * LANES)
    a = pw[:, t_chunk]
    a1 = jnp.concatenate([a.real, a.real], axis=-1).reshape(1, g * 2 * p)
    a2 = jnp.concatenate([-a.imag, a.imag], axis=-1).reshape(1, g * 2 * p)
    return kt, wc, vc, dd, a1, a2


def _chunk_rows(u_ref, t, nc):
    return jnp.concatenate([u_ref[pl.ds(s, nc, stride=t), :] for s in range(t)], axis=1)


def _s5_state_in_kernel(u_ref, wc_ref, xa_ref, xb_ref, wa_ref, wb_ref, *, t, nc, h):
    na = LANES // h
    row_group = lax.broadcasted_iota(I32, (LANES, LANES), 0) // h
    for s in range(t):
        blk = wc_ref[s]
        swapped = pltpu.roll(blk, LANES // 2, axis=1)
        for b in range(na):
            own = row_group == b
            dst = (slice(s * LANES, (s + 1) * LANES), slice(b * LANES, (b + 1) * LANES))
            wa_ref[dst] = jnp.where(own, blk, 0.0).astype(BF16)
            wb_ref[dst] = jnp.where(own, swapped, 0.0).astype(BF16)
    ucat = _chunk_rows(u_ref, t, nc).astype(BF16)
    xa_ref[...] = jnp.dot(ucat, wa_ref[...], preferred_element_type=F32)
    xb_ref[...] = jnp.dot(ucat, wb_ref[...], preferred_element_type=F32)


def _s5_scan_kernel(xa_ref, xb_ref, a1_ref, a2_ref, s_ref, *, unroll):
    a1 = a1_ref[...]
    a2 = a2_ref[...]
    nc = xa_ref.shape[0]

    def body(cb, carry):
        s_a, s_b = carry
        base = pl.multiple_of(cb * unroll, unroll)
        xa = xa_ref[pl.ds(base, unroll), :]
        xb = xb_ref[pl.ds(base, unroll), :]
        rows = []
        for k in range(unroll):
            rows.append(s_a)
            s_a, s_b = (a1 * s_a + a2 * s_b + xa[k:k + 1, :],
                        a1 * s_b - a2 * s_a + xb[k:k + 1, :])
        s_ref[pl.ds(base, unroll), :] = jnp.concatenate(rows, axis=0)
        return s_a, s_b

    z = jnp.zeros(a1.shape, F32)
    lax.fori_loop(0, nc // unroll, body, (z, z))


def _s5_out_kernel(u_ref, sp_ref, kt_ref, vc_ref, dd_ref, y_ref, bd_ref, vm_ref, *, t, nc, h, n_parts):
    part = pl.program_id(1)
    tp = t // n_parts
    na = LANES // h
    w = tp * LANES
    same_group = (lax.broadcasted_iota(I32, (LANES, LANES), 0) // h
                  == lax.broadcasted_iota(I32, (LANES, LANES), 1) // h)
    row = lax.broadcasted_iota(I32, (LANES, w), 0)
    col = lax.broadcasted_iota(I32, (LANES, w), 1)
    landing_group = jnp.where((row // h == col // LANES) & (row % h == col % h), (col // h) % na, -1)
    ucat = _chunk_rows(u_ref, t, nc)
    for k in range(n_parts):
        @pl.when(part == k)
        def _(k=k):
            for s in range(t):
                for tt in range(tp):
                    lag = k * tp + tt - s
                    dst = (slice(s * LANES, (s + 1) * LANES), slice(tt * LANES, (tt + 1) * LANES))
                    if lag >= 0:
                        bd_ref[dst] = jnp.where(same_group, kt_ref[lag], 0.0).astype(BF16)
                    else:
                        bd_ref[dst] = jnp.zeros((LANES, LANES), BF16)
            for a in range(na):
                place = jnp.where(landing_group == a, 1.0, 0.0).astype(BF16)
                src = vc_ref[a][:, k * tp * h:(k + 1) * tp * h].astype(BF16)
                vm_ref[a * LANES:(a + 1) * LANES, :] = jnp.dot(src, place, preferred_element_type=F32).astype(BF16)
            y = jnp.dot(ucat.astype(BF16), bd_ref[...], preferred_element_type=F32)
            y = y + jnp.dot(sp_ref[...].astype(BF16), vm_ref[...], preferred_element_type=F32)
            y = jax.nn.gelu(y + dd_ref[...] * ucat[:, k * w:(k + 1) * w])
            for s in range(tp):
                y_ref[pl.ds(k * tp + s, nc, stride=t), :] = y[:, s * LANES:(s + 1) * LANES]


def _s5_branch(u, prm):
    kt, wc, vc, dd, a1, a2 = prm
    L, width = u.shape
    nx, na, _, th = vc.shape
    t = kt.shape[1]
    h = th // t
    tah = t * LANES
    sw = na * LANES
    nc = L // t
    u_spec = pl.BlockSpec((L, LANES), lambda x: (0, x))
    st_spec = pl.BlockSpec((nc, sw), lambda x: (0, x))
    xa, xb = pl.pallas_call(
        functools.partial(_s5_state_in_kernel, t=t, nc=nc, h=h), grid=(nx,),
        in_specs=[u_spec,
                  pl.BlockSpec((None, t, LANES, LANES), lambda x: (x, 0, 0, 0))],
        out_specs=[st_spec, st_spec],
        out_shape=[jax.ShapeDtypeStruct((nc, nx * sw), F32)] * 2,
        scratch_shapes=[pltpu.VMEM((tah, sw), BF16), pltpu.VMEM((tah, sw), BF16)], name="s5_state_in",
        compiler_params=_cp("parallel"),
    )(u, wc)
    ws = _tile(nx * sw, 2048)
    unroll = _tile(nc, 8)
    sc_spec = pl.BlockSpec((nc, ws), lambda x: (0, x))
    a_spec = pl.BlockSpec((1, ws), lambda x: (0, x))
    sp = pl.pallas_call(
        functools.partial(_s5_scan_kernel, unroll=unroll), grid=(nx * sw // ws,),
        in_specs=[sc_spec, sc_spec, a_spec, a_spec],
        out_specs=sc_spec,
        out_shape=jax.ShapeDtypeStruct((nc, nx * sw), F32), name="s5_scan",
        compiler_params=_cp("parallel"),
    )(xa, xb, a1, a2)
    n_parts = 2
    tw = tah // n_parts
    return pl.pallas_call(
        functools.partial(_s5_out_kernel, t=t, nc=nc, h=h, n_parts=n_parts), grid=(nx, n_parts),
        in_specs=[pl.BlockSpec((L, LANES), lambda x, k: (0, x)),
                  pl.BlockSpec((nc, sw), lambda x, k: (0, x)),
                  pl.BlockSpec((None, t, LANES, LANES), lambda x, k: (x, 0, 0, 0)),
                  pl.BlockSpec((None, na, LANES, th), lambda x, k: (x, 0, 0, 0)),
                  pl.BlockSpec((None, 1, tw), lambda x, k: (x, 0, k))],
        out_specs=pl.BlockSpec((L, LANES), lambda x, k: (0, x)),
        out_shape=jax.ShapeDtypeStruct((L, width), F32),
        scratch_shapes=[pltpu.VMEM((tah, tw), BF16), pltpu.VMEM((sw, tw), BF16)], name="s5_out",
        compiler_params=_cp("parallel", "arbitrary"),
    )(u, sp, kt, vc, dd)


def _t5_bucket_starts(n_buckets):
    exact = n_buckets // 2
    dist = np.arange(0, 4 * REL_MAX_DIST).astype(np.int64)
    far = exact + (np.log(np.maximum(dist, 1).astype(np.float32) / np.float32(exact))
                   / np.float32(math.log(REL_MAX_DIST / exact)) * np.float32(n_buckets - exact)).astype(np.int64)
    bucket = np.where(dist < exact, dist, np.minimum(far, n_buckets - 1))
    return [int(np.argmax(bucket >= b)) for b in range(1, n_buckets)]


def _bias_tile_kernel(rb_ref, o_ref, *, tb, starts):
    delta = pl.program_id(0)
    head = pl.program_id(1)
    key = lax.broadcasted_iota(I32, (tb, tb), 0)
    qry = lax.broadcasted_iota(I32, (tb, tb), 1)
    dist = delta * tb + qry - key
    bias = jnp.full((tb, tb), rb_ref[0, head], F32)
    for b, start in enumerate(starts, start=1):
        bias = jnp.where(dist >= start, rb_ref[b, head], bias)
    o_ref[...] = jnp.where(dist >= 0, bias * LOG2E, NEG_BIG)


def _bias_tiles(rel_bias, tb):
    n_buckets, heads = rel_bias.shape
    starts = _t5_bucket_starts(n_buckets)
    assert 2 * tb - (tb - 1) >= starts[-1]
    return pl.pallas_call(
        functools.partial(_bias_tile_kernel, tb=tb, starts=starts), grid=(2, heads),
        in_specs=[pl.BlockSpec(memory_space=pltpu.SMEM)],
        out_specs=pl.BlockSpec((None, None, tb, tb), lambda dl, hd: (dl, hd, 0, 0)),
        out_shape=jax.ShapeDtypeStruct((2, heads, tb, tb), F32), name="rel_bias_tiles",
        compiler_params=_cp("parallel", "parallel"),
    )(rel_bias)


def _indexer_kernel(qt_ref, k2_ref, wt_ref, keys_ref, thr_ref, *, tq, ck, n_heads, topk, w_scale):
    i = pl.program_id(0)
    n_blocks = keys_ref.shape[0] // tq
    n_live = (i + 1) * (tq // ck)
    wv = wt_ref[...] * w_scale

    def score_chunk(c, carry):
        row = pl.multiple_of(c * ck, ck)
        kk = k2_ref[c]
        acc = jnp.zeros((ck, tq), F32)
        for pr in range(n_heads // 2):
            s2 = jnp.dot(kk, qt_ref[pr * LANES:(pr + 1) * LANES, :], preferred_element_type=F32)
            acc = acc + jnp.maximum(s2[:ck], 0.0) * wv[2 * pr:2 * pr + 1, :]
            acc = acc + jnp.maximum(s2[ck:], 0.0) * wv[2 * pr + 1:2 * pr + 2, :]
        kpos = row + lax.broadcasted_iota(I32, (ck, tq), 0)
        qpos = i * tq + lax.broadcasted_iota(I32, (ck, tq), 1)
        score = jnp.where(kpos <= qpos, acc, -jnp.inf)
        bits = pltpu.bitcast(score, I32)
        keys_ref[pl.ds(row, ck), :] = jnp.where(bits < 0, bits ^ 0x7FFFFFFF, bits)
        return carry

    lax.fori_loop(0, n_live, score_chunk, 0)

    def fill_block(b, carry):
        row = pl.multiple_of(b * tq, tq)
        keys_ref[pl.ds(row, tq), :] = jnp.full((tq, tq), INT_MIN, I32)
        return carry

    lax.fori_loop(i + 1, n_blocks, fill_block, 0)

    acc_rows = 32

    def count_ge(cand):
        def body(b, cnt):
            row = pl.multiple_of(b * tq, tq)
            hit = jnp.where(keys_ref[pl.ds(row, tq), :] >= cand, 1.0, 0.0)
            return cnt + jnp.sum(hit.reshape(tq // acc_rows, acc_rows, tq), axis=0)
        cnt = lax.fori_loop(0, i + 1, body, jnp.zeros((acc_rows, tq), F32))
        return jnp.sum(cnt, axis=0, keepdims=True)

    kf = float(topk)
    zero = jnp.zeros((1, tq), I32)
    c_zero = count_ge(zero)
    thr0 = jnp.where(c_zero >= kf, zero, jnp.full((1, tq), INT_MIN, I32))
    cnt0 = jnp.where(c_zero >= kf, c_zero, ((i + 1) * tq).astype(F32))

    def undecided(st):
        it, _, cnt = st
        return (it < 31) & (jnp.max(cnt) > kf)

    def bit_step(st):
        it, thr, cnt = st
        cand = thr | jnp.left_shift(jnp.int32(1), 30 - it)
        c = count_ge(cand)
        keep = c >= kf
        return it + 1, jnp.where(keep, cand, thr), jnp.where(keep, c, cnt)

    _, thr, _ = lax.while_loop(undecided, bit_step, (jnp.int32(0), thr0, cnt0))
    thr_ref[...] = jnp.broadcast_to(thr, thr_ref.shape)


def _indexer(qi_t, row_off, k2, w_t, topk, tq=256):
    L = qi_t.shape[1]
    n_heads = w_t.shape[0]
    qw = n_heads * IDX_DIM
    assert row_off % qw == 0
    tq = _tile(L, tq)
    ck = k2.shape[1] // 2
    assert tq % ck == 0
    w_scale = (n_heads ** -0.5) * (IDX_DIM ** -0.5)
    return pl.pallas_call(
        functools.partial(_indexer_kernel, tq=tq, ck=ck, n_heads=n_heads, topk=topk, w_scale=w_scale),
        grid=(L // tq,),
        in_specs=[pl.BlockSpec((qw, tq), lambda i: (row_off // qw, i)),
                  pl.BlockSpec(k2.shape, lambda i: (0, 0, 0)),
                  pl.BlockSpec((n_heads, tq), lambda i: (0, i))],
        out_specs=[pl.BlockSpec((L, tq), lambda i: (0, i)),
                   pl.BlockSpec((8, tq), lambda i: (0, i))],
        out_shape=[jax.ShapeDtypeStruct((L, L), I32), jax.ShapeDtypeStruct((8, L), I32)],
        name="dsa_indexer", compiler_params=_cp("parallel"),
    )(qi_t, k2, w_t)


def _attn_kernel(qi_ref, kj_ref, far_ref, qt_ref, k_ref, vt_ref, keys_ref, thr_ref, bias_ref, o_ref,
                 m_ref, l_ref, acc_ref, mb_ref, *, heads, dh, scale2):
    step = pl.program_id(0)
    qi = qi_ref[step]
    kj = kj_ref[step]

    @pl.when(kj == 0)
    def _():
        m_ref[...] = jnp.full(m_ref.shape, -jnp.inf, F32)
        l_ref[...] = jnp.zeros(l_ref.shape, F32)
        acc_ref[...] = jnp.zeros(acc_ref.shape, F32)

    mb_ref[...] = jnp.where(keys_ref[...] >= thr_ref[0:1, :], 0.0, NEG_BIG)

    def scores(h):
        hs = slice(h * dh, (h + 1) * dh)
        return jnp.dot(k_ref[:, hs], qt_ref[hs, :], preferred_element_type=F32)

    def fold(reduce_fn, a, rows=32):
        return reduce_fn(a.reshape(a.shape[0] // rows, rows, a.shape[1]), axis=0)

    def run(near):
        ahead = 4
        pending = [scores(h) for h in range(min(ahead, heads))]
        for h in range(heads):
            hs = slice(h * dh, (h + 1) * dh)
            x = pending.pop(0) * scale2 + mb_ref[...]
            if h + ahead < heads:
                pending.append(scores(h + ahead))
            if near:
                x = x + bias_ref[h]
            far = 0.0 if near else far_ref[h]
            m_prev = m_ref[h]
            m_new = jnp.maximum(m_prev, jnp.max(fold(jnp.max, x), axis=0, keepdims=True) + far)
            alpha = jnp.exp2(m_prev - m_new)
            p = jnp.exp2(x - (m_new[0:1, :] - far))
            l_ref[h] = alpha * l_ref[h] + jnp.sum(fold(jnp.sum, p), axis=0, keepdims=True)
            m_ref[h] = m_new
            acc_ref[hs, :] = alpha[0:1, :] * acc_ref[hs, :] + jnp.dot(vt_ref[hs, :], p.astype(BF16),
                                                                      preferred_element_type=F32)

    @pl.when(qi - kj < 2)
    def _():
        run(True)

    @pl.when(qi - kj >= 2)
    def _():
        run(False)

    @pl.when(kj == qi)
    def _():
        for h in range(heads):
            hs = slice(h * dh, (h + 1) * dh)
            o_ref[:, hs] = jnp.transpose(acc_ref[hs, :] / l_ref[h][0:1, :]).astype(o_ref.dtype)


def _attention(qv_t, q_blk, v_blk, k, keys, thr, bias_tiles, far_bias, tb):
    L = k.shape[0]
    heads = bias_tiles.shape[1]
    dh = ATT_HEAD_DIM
    aw = heads * dh
    nb = L // tb
    pairs = [(i, j) for i in range(nb) for j in range(i + 1)]
    qi = jnp.asarray(np.array([p[0] for p in pairs], np.int32))
    kj = jnp.asarray(np.array([p[1] for p in pairs], np.int32))
    grid_spec = pltpu.PrefetchScalarGridSpec(
        num_scalar_prefetch=2, grid=(len(pairs),),
        in_specs=[pl.BlockSpec(memory_space=pltpu.SMEM),
                  pl.BlockSpec((aw, tb), lambda s, qi, kj: (q_blk, qi[s])),
                  pl.BlockSpec((tb, aw), lambda s, qi, kj: (kj[s], 0)),
                  pl.BlockSpec((aw, tb), lambda s, qi, kj: (v_blk, kj[s])),
                  pl.BlockSpec((tb, tb), lambda s, qi, kj: (kj[s], qi[s])),
                  pl.BlockSpec((8, tb), lambda s, qi, kj: (0, qi[s])),
                  pl.BlockSpec((None, heads, tb, tb),
                               lambda s, qi, kj: (jnp.minimum(qi[s] - kj[s], 1), 0, 0, 0))],
        out_specs=pl.BlockSpec((tb, aw), lambda s, qi, kj: (qi[s], 0)),
        scratch_shapes=[pltpu.VMEM((heads, 8, tb), F32),
                        pltpu.VMEM((heads, 8, tb), F32),
                        pltpu.VMEM((aw, tb), F32),
                        pltpu.VMEM((tb, tb), F32)])
    return pl.pallas_call(
        functools.partial(_attn_kernel, heads=heads, dh=dh, scale2=dh ** -0.5 * LOG2E),
        grid_spec=grid_spec, out_shape=jax.ShapeDtypeStruct((L, aw), BF16), name="dsa_attention",
        compiler_params=_cp("arbitrary"),
    )(qi, kj, far_bias, qv_t, k, qv_t, keys, thr, bias_tiles)


def _router_kernel(h_ref, w_ref, b_ref, g_ref, *, n_exp, top_k):
    logits = jnp.dot(h_ref[...], w_ref[...], preferred_element_type=F32) + b_ref[...]
    lane = lax.broadcasted_iota(I32, logits.shape, 1).astype(F32)
    work = jnp.where(lane < n_exp, logits, -jnp.inf)
    gates = jnp.zeros(logits.shape, F32)
    denom = None
    top = None
    for r in range(top_k):
        m = jnp.max(work, axis=1, keepdims=True)
        first = jnp.min(jnp.where(work == m, lane, float(LANES)), axis=1, keepdims=True)
        sel = lane == first
        if r == 0:
            top = m
        e = jnp.exp(m - top)
        gates = gates + jnp.where(sel, e, 0.0)
        denom = e if denom is None else denom + e
        work = jnp.where(sel, -jnp.inf, work)
    g_ref[...] = gates / denom


def _router(h, w_pad, b_pad, n_exp, tm=512):
    m, d = h.shape
    tm = _tile(m, tm)
    return pl.pallas_call(
        functools.partial(_router_kernel, n_exp=n_exp, top_k=TOP_K_EXPERTS), grid=(m // tm,),
        in_specs=[pl.BlockSpec((tm, d), lambda i: (i, 0)),
                  pl.BlockSpec((d, LANES), lambda i: (0, 0)),
                  pl.BlockSpec((1, LANES), lambda i: (0, 0))],
        out_specs=pl.BlockSpec((tm, LANES), lambda i: (i, 0)),
        out_shape=jax.ShapeDtypeStruct((m, LANES), F32), name="moe_router",
        compiler_params=_cp("parallel"),
    )(h, w_pad, b_pad)


def _moe_kernel(h_ref, g_ref, wgu_ref, bgu_ref, wdn_ref, bdn_ref, o_ref, *, ff, eb):
    step = pl.program_id(1)
    gates = g_ref[...]

    @pl.when(step == 0)
    def _():
        o_ref[...] = jnp.dot(gates.astype(BF16), bdn_ref[...], preferred_element_type=F32)

    lane = lax.broadcasted_iota(I32, gates.shape, 1)
    for k in range(eb):
        gu = jnp.dot(h_ref[...], wgu_ref[k], preferred_element_type=F32) + bgu_ref[k]
        gate = jnp.minimum(gu[:, :ff], SWIGLU_LIMIT)
        up = jnp.clip(gu[:, ff:], -SWIGLU_LIMIT, SWIGLU_LIMIT)
        act = (up + 1.0) * gate * jax.nn.sigmoid(SWIGLU_ALPHA * gate)
        ge = jnp.sum(jnp.where(lane == step * eb + k, gates, 0.0), axis=1, keepdims=True)
        o_ref[...] += jnp.dot((act * ge).astype(BF16), wdn_ref[k], preferred_element_type=F32)


def _moe_experts(h, gates, w_gu, b_gu, w_dn, b_dn_pad, tm=512, eb=2):
    m, d = h.shape
    n_exp, _, ff2 = w_gu.shape
    tm = _tile(m, tm)
    eb = _tile(n_exp, eb)
    return pl.pallas_call(
        functools.partial(_moe_kernel, ff=ff2 // 2, eb=eb), grid=(m // tm, n_exp // eb),
        in_specs=[pl.BlockSpec((tm, d), lambda i, e: (i, 0)),
                  pl.BlockSpec((tm, LANES), lambda i, e: (i, 0)),
                  pl.BlockSpec((eb, d, ff2), lambda i, e: (e, 0, 0)),
                  pl.BlockSpec((eb, 1, ff2), lambda i, e: (e, 0, 0)),
                  pl.BlockSpec((eb, ff2 // 2, d), lambda i, e: (e, 0, 0)),
                  pl.BlockSpec((LANES, d), lambda i, e: (0, 0))],
        out_specs=pl.BlockSpec((tm, d), lambda i, e: (i, 0)),
        out_shape=jax.ShapeDtypeStruct((m, d), F32), name="moe_experts",
        compiler_params=_cp("parallel", "arbitrary"),
    )(h, gates, w_gu, b_gu.reshape(n_exp, 1, ff2), w_dn, b_dn_pad)


def kernel(x, c, w_in, w_a_out, w_b_out, w_o, w_glu, b_glu, s5_lam_re, s5_lam_im, s5_b_re, s5_b_im, s5_c_re, s5_c_im, s5_d, s5_log_step, rel_bias, w_router, b_router, w_gu, b_gu, w_dn, b_dn, w_c1, b_c1, w_c2, b_c2, mod_table, ln_g, ln_b):
    bsz, L, d = x.shape
    assert bsz == 1 and c.shape[0] == 1
    depth = w_in.shape[0]
    g, p, h = s5_b_re.shape[1:]
    s5w = g * h
    heads = rel_bias.shape[1]
    aw = heads * ATT_HEAD_DIM
    iw = IDX_HEADS * IDX_DIM
    n_exp = w_router.shape[2]
    assert w_in.shape[2] == s5w + 3 * aw + iw + IDX_DIM + IDX_HEADS + 2 * d
    assert n_exp <= LANES and IDX_DIM + IDX_HEADS <= LANES and 2 * IDX_DIM == LANES
    o_q, o_k, o_v, o_qi = s5w, s5w + aw, s5w + 2 * aw, s5w + 3 * aw
    o_ki = s5w + 3 * aw + iw
    o_gate = o_ki + IDX_DIM + IDX_HEADS
    topk = min(TOPK_MAX, L // 4)
    tb = _tile(L, 256)
    ck = _tile(L, 128)
    dn_alpha = (2 * depth) ** 0.25

    mod = _conditioning(c, w_c1, b_c1, w_c2, b_c2, mod_table).reshape(depth, N_MOD, d)
    bias_tiles = _bias_tiles(rel_bias, tb)
    far_bias = rel_bias[-1] * LOG2E
    assert o_ki % LANES == 0 and (g * h) % LANES == 0 and LANES % h == 0

    xs = x.reshape(L, d)
    hcur = _modulate(xs, mod[0], sc_row=1, sh_row=0)
    for l in range(depth):
        w_l = w_in[l]
        w_nat = jnp.concatenate(
            [w_l[:, :s5w], w_l[:, o_k:o_k + aw], w_l[:, o_gate:],
             jnp.pad(w_l[:, o_ki:o_gate], ((0, 0), (0, LANES - IDX_DIM - IDX_HEADS)))], axis=1).astype(BF16)
        w_tr = jnp.transpose(jnp.concatenate(
            [w_l[:, o_q:o_q + aw], w_l[:, o_v:o_v + aw], w_l[:, o_qi:o_qi + iw]], axis=1)).astype(BF16)
        u = _matmul(hcur, w_nat, out_dtype=F32, col_off=0, n_cols=s5w, name="proj_u")
        k = _matmul(hcur, w_nat, out_dtype=BF16, col_off=s5w, n_cols=aw, name="proj_k")
        gates_ab = _matmul(hcur, w_nat, out_dtype=F32, col_off=s5w + aw, n_cols=2 * d, name="proj_gates")
        misc = _matmul(hcur, w_nat, out_dtype=F32, col_off=s5w + aw + 2 * d, n_cols=LANES, tn=LANES,
                       name="proj_misc")
        qvi_t = _matmul_t(hcur, w_tr, out_dtype=BF16, name="proj_qvi_t")

        prm = _s5_discretize(s5_lam_re[l], s5_lam_im[l], s5_b_re[l], s5_b_im[l], s5_c_re[l], s5_c_im[l],
                             s5_d[l], s5_log_step[l], S5_CHUNK)
        y_s5 = _s5_branch(u, prm)
        ya_in = _glu(y_s5, w_glu, b_glu[l], l)

        k_idx = misc[:, :IDX_DIM].astype(BF16)
        zeros = jnp.zeros_like(k_idx)
        k_even = jnp.concatenate([k_idx, zeros], axis=1).reshape(L // ck, ck, LANES)
        k_odd = jnp.concatenate([zeros, k_idx], axis=1).reshape(L // ck, ck, LANES)
        k2 = jnp.concatenate([k_even, k_odd], axis=1)
        w_t = jnp.transpose(misc[:, IDX_DIM:IDX_DIM + IDX_HEADS])
        keys, thr = _indexer(qvi_t, 2 * aw, k2, w_t, topk)
        attn = _attention(qvi_t, 0, 1, k, keys, thr, bias_tiles, far_bias, tb)

        merged = _merge(ya_in, attn, w_a_out, w_b_out, gates_ab, l)
        y = _matmul(merged, w_o, layer=l, out_dtype=F32, name="proj_out")
        xs, h2 = _ln_modulate(xs, y, mod[l], ln_g[l], ln_b[l], mod[l], alpha=dn_alpha, gate_row=2, ln_row=0,
                              sc_row=4, sh_row=3, want_h=True)

        w_r = jnp.pad(w_router[l], ((0, 0), (0, LANES - n_exp))).astype(BF16)
        b_r = jnp.pad(b_router[l], (0, LANES - n_exp)).reshape(1, LANES)
        gates = _router(h2, w_r, b_r, n_exp)
        b_dn_pad = jnp.pad(b_dn[l], ((0, LANES - n_exp), (0, 0))).astype(BF16)
        y = _moe_experts(h2, gates, w_gu[l].astype(BF16), b_gu[l], w_dn[l].astype(BF16), b_dn_pad)
        last = l == depth - 1
        xs, hcur = _ln_modulate(xs, y, mod[l], ln_g[l], ln_b[l], mod[min(l + 1, depth - 1)], alpha=dn_alpha,
                                gate_row=5, ln_row=1, sc_row=1, sh_row=0, want_h=not last)
    return xs.reshape(bsz, L, d)
```

```python
import functools
import math

import numpy as np
import jax
import jax.numpy as jnp
from jax import lax
from jax.experimental import pallas as pl
from jax.experimental.pallas import tpu as pltpu

F32 = jnp.float32
BF16 = jnp.bfloat16
I32 = jnp.int32

ATT_HEAD_DIM = 128
IDX_HEADS = 32
IDX_DIM = 64
TOPK_MAX = 256
REL_MAX_DIST = 128
TOP_K_EXPERTS = 4
SWIGLU_LIMIT = 7.0
SWIGLU_ALPHA = 1.702
N_MOD = 6
LN_EPS = 1e-5

LANES = 128
VMEM_LIMIT_BYTES = 56 * 1024 * 1024
S5_CHUNK = 16
NEG_BIG = -1e30
LOG2E = math.log2(math.e)
INT_MIN = -(2 ** 31)


def _cp(*sem):
    return pltpu.CompilerParams(dimension_semantics=sem, vmem_limit_bytes=VMEM_LIMIT_BYTES)


def _tile(n, t):
    t = min(n, t)
    while n % t:
        t //= 2
    return t


def _cond1_kernel(c_ref, w_ref, b_ref, o_ref):
    a = jax.nn.silu(c_ref[...]).astype(BF16)
    t = jnp.dot(a, w_ref[...].astype(BF16), preferred_element_type=F32) + b_ref[...]
    o_ref[...] = jax.nn.silu(t)


def _cond2_kernel(t_ref, w_ref, b_ref, tab_ref, o_ref):
    acc = jnp.dot(t_ref[...].astype(BF16), w_ref[...].astype(BF16), preferred_element_type=F32)
    o_ref[...] = acc[0:1, :] + b_ref[...] + tab_ref[...]


def _conditioning(c, w_c1, b_c1, w_c2, b_c2, mod_table):
    d = c.shape[1]
    r = w_c1.shape[1]
    depth = mod_table.shape[0]
    n = w_c2.shape[1]
    c8 = jnp.broadcast_to(c, (8, d))
    t = pl.pallas_call(
        _cond1_kernel, out_shape=jax.ShapeDtypeStruct((8, r), F32), name="cond1",
        compiler_params=_cp(),
    )(c8, w_c1, b_c1.reshape(1, r))
    tn = _tile(n, 2048)
    return pl.pallas_call(
        _cond2_kernel, grid=(n // tn,),
        in_specs=[pl.BlockSpec((8, r), lambda j: (0, 0)),
                  pl.BlockSpec((r, tn), lambda j: (0, j)),
                  pl.BlockSpec((1, tn), lambda j: (0, j)),
                  pl.BlockSpec((depth, tn), lambda j: (0, j))],
        out_specs=pl.BlockSpec((depth, tn), lambda j: (0, j)),
        out_shape=jax.ShapeDtypeStruct((depth, n), F32), name="cond2",
        compiler_params=_cp("parallel"),
    )(t, w_c2, b_c2.reshape(1, n), mod_table.reshape(depth, n))


def _modulate_kernel(x_ref, mod_ref, h_ref, *, sc_row, sh_row):
    sc = mod_ref[sc_row:sc_row + 1, :]
    sh = mod_ref[sh_row:sh_row + 1, :]
    h_ref[...] = (x_ref[...] * (1.0 + sc) + sh).astype(h_ref.dtype)


def _modulate(x, mod_l, sc_row, sh_row, tm=256):
    m, d = x.shape
    tm = _tile(m, tm)
    return pl.pallas_call(
        functools.partial(_modulate_kernel, sc_row=sc_row, sh_row=sh_row),
        grid=(m // tm,),
        in_specs=[pl.BlockSpec((tm, d), lambda i: (i, 0)),
                  pl.BlockSpec((N_MOD, d), lambda i: (0, 0))],
        out_specs=pl.BlockSpec((tm, d), lambda i: (i, 0)),
        out_shape=jax.ShapeDtypeStruct((m, d), BF16), name="modulate",
        compiler_params=_cp("parallel"),
    )(x, mod_l)


def _ln_kernel(x_ref, y_ref, mod_ref, g_ref, b_ref, nmod_ref, xo_ref, *h_refs,
               alpha, gate_row, ln_row, sc_row, sh_row):
    gate = mod_ref[gate_row:gate_row + 1, :]
    u = alpha * x_ref[...] + (1.0 + gate) * y_ref[...]
    mu = jnp.mean(u, axis=-1, keepdims=True)
    uc = u - mu
    var = jnp.mean(uc * uc, axis=-1, keepdims=True)
    xn = uc * lax.rsqrt(var + LN_EPS) * g_ref[ln_row:ln_row + 1, :] + b_ref[ln_row:ln_row + 1, :]
    xo_ref[...] = xn
    if h_refs:
        sc = nmod_ref[sc_row:sc_row + 1, :]
        sh = nmod_ref[sh_row:sh_row + 1, :]
        h_refs[0][...] = (xn * (1.0 + sc) + sh).astype(BF16)


def _ln_modulate(x, y, mod_l, ln_g_l, ln_b_l, next_mod, *, alpha, gate_row, ln_row, sc_row, sh_row,
                 want_h, tm=256):
    m, d = x.shape
    tm = _tile(m, tm)
    row = pl.BlockSpec((tm, d), lambda i: (i, 0))
    out_shape = [jax.ShapeDtypeStruct((m, d), F32)]
    out_specs = [row]
    if want_h:
        out_shape.append(jax.ShapeDtypeStruct((m, d), BF16))
        out_specs.append(row)
    res = pl.pallas_call(
        functools.partial(_ln_kernel, alpha=alpha, gate_row=gate_row, ln_row=ln_row,
                          sc_row=sc_row, sh_row=sh_row),
        grid=(m // tm,),
        in_specs=[row, row,
                  pl.BlockSpec((N_MOD, d), lambda i: (0, 0)),
                  pl.BlockSpec((2, d), lambda i: (0, 0)),
                  pl.BlockSpec((2, d), lambda i: (0, 0)),
                  pl.BlockSpec((N_MOD, d), lambda i: (0, 0))],
        out_specs=out_specs, out_shape=out_shape, name="ln_modulate",
        compiler_params=_cp("parallel"),
    )(x, y, mod_l, ln_g_l, ln_b_l, next_mod)
    return (res[0], res[1]) if want_h else (res[0], None)


def _mm_kernel(a_ref, w_ref, o_ref, wbf_ref):
    @pl.when(pl.program_id(1) == 0)
    def _():
        wbf_ref[...] = w_ref[...].astype(BF16)

    o_ref[...] = jnp.dot(a_ref[...], wbf_ref[...], preferred_element_type=F32).astype(o_ref.dtype)


def _w_spec(w, layer, k, tn, off=0):
    if w.ndim == 3:
        return pl.BlockSpec((None, k, tn), lambda j, i: (layer, 0, j + off))
    return pl.BlockSpec((k, tn), lambda j, i: (0, j + off))


def _matmul(a, w, *, out_dtype, layer=0, col_off=0, n_cols=None, tm=1024, tn=512, name="matmul"):
    m, k = a.shape
    n_cols = w.shape[-1] - col_off if n_cols is None else n_cols
    tm = _tile(m, tm)
    tn = math.gcd(_tile(n_cols, tn), col_off) if col_off else _tile(n_cols, tn)
    off = col_off // tn
    return pl.pallas_call(
        _mm_kernel, grid=(n_cols // tn, m // tm),
        in_specs=[pl.BlockSpec((tm, k), lambda j, i: (i, 0)),
                  _w_spec(w, layer, k, tn, off)],
        out_specs=pl.BlockSpec((tm, tn), lambda j, i: (i, j)),
        out_shape=jax.ShapeDtypeStruct((m, n_cols), out_dtype),
        scratch_shapes=[pltpu.VMEM((k, tn), BF16)], name=name,
        compiler_params=_cp("parallel", "arbitrary"),
    )(a, w)


def _mm_t_kernel(a_ref, wt_ref, o_ref):
    o_ref[...] = lax.dot_general(wt_ref[...], a_ref[...], (((1,), (1,)), ((), ())),
                                 preferred_element_type=F32).astype(o_ref.dtype)


def _matmul_t(a, w_t, *, out_dtype, tm=1024, tn=1024, name="matmul_t"):
    m, k = a.shape
    n = w_t.shape[0]
    tm = _tile(m, tm)
    tn = _tile(n, tn)
    return pl.pallas_call(
        _mm_t_kernel, grid=(n // tn, m // tm),
        in_specs=[pl.BlockSpec((tm, k), lambda j, i: (i, 0)),
                  pl.BlockSpec((tn, k), lambda j, i: (j, 0))],
        out_specs=pl.BlockSpec((tn, tm), lambda j, i: (j, i)),
        out_shape=jax.ShapeDtypeStruct((n, m), out_dtype), name=name,
        compiler_params=_cp("parallel", "arbitrary"),
    )(a, w_t)


def _glu_kernel(y_ref, yt_ref, w_ref, b_ref, o_ref, wbf_ref):
    @pl.when(pl.program_id(1) == 0)
    def _():
        wbf_ref[...] = w_ref[...].astype(BF16)

    z = jnp.dot(y_ref[...].astype(BF16), wbf_ref[...], preferred_element_type=F32) + b_ref[...]
    o_ref[...] = (yt_ref[...] * jax.nn.sigmoid(z)).astype(o_ref.dtype)


def _glu(y, w, b, layer, tm=1024, tn=512):
    m, k = y.shape
    n = w.shape[-1]
    tm = _tile(m, tm)
    tn = _tile(n, tn)
    return pl.pallas_call(
        _glu_kernel, grid=(n // tn, m // tm),
        in_specs=[pl.BlockSpec((tm, k), lambda j, i: (i, 0)),
                  pl.BlockSpec((tm, tn), lambda j, i: (i, j)),
                  _w_spec(w, layer, k, tn),
                  pl.BlockSpec((1, tn), lambda j, i: (0, j))],
        out_specs=pl.BlockSpec((tm, tn), lambda j, i: (i, j)),
        out_shape=jax.ShapeDtypeStruct((m, n), BF16),
        scratch_shapes=[pltpu.VMEM((k, tn), BF16)], name="s5_glu",
        compiler_params=_cp("parallel", "arbitrary"),
    )(y, y, w, b.reshape(1, n))


def _merge_kernel(a1_ref, a2_ref, w1_ref, w2_ref, ga_ref, gb_ref, o_ref, w1bf_ref, w2bf_ref):
    @pl.when(pl.program_id(1) == 0)
    def _():
        w1bf_ref[...] = w1_ref[...].astype(BF16)
        w2bf_ref[...] = w2_ref[...].astype(BF16)

    ya = jnp.dot(a1_ref[...], w1bf_ref[...], preferred_element_type=F32)
    yb = jnp.dot(a2_ref[...], w2bf_ref[...], preferred_element_type=F32)
    o_ref[...] = (jax.nn.sigmoid(ga_ref[...]) * ya + jax.nn.sigmoid(gb_ref[...]) * yb).astype(o_ref.dtype)


def _merge(a1, a2, w1, w2, gates, layer, tm=1024, tn=512):
    m, k1 = a1.shape
    k2 = a2.shape[1]
    n = w1.shape[-1]
    tm = _tile(m, tm)
    tn = _tile(n, tn)
    nb = n // tn
    return pl.pallas_call(
        _merge_kernel, grid=(nb, m // tm),
        in_specs=[pl.BlockSpec((tm, k1), lambda j, i: (i, 0)),
                  pl.BlockSpec((tm, k2), lambda j, i: (i, 0)),
                  _w_spec(w1, layer, k1, tn),
                  _w_spec(w2, layer, k2, tn),
                  pl.BlockSpec((tm, tn), lambda j, i: (i, j)),
                  pl.BlockSpec((tm, tn), lambda j, i: (i, j + nb))],
        out_specs=pl.BlockSpec((tm, tn), lambda j, i: (i, j)),
        out_shape=jax.ShapeDtypeStruct((m, n), BF16),
        scratch_shapes=[pltpu.VMEM((k1, tn), BF16), pltpu.VMEM((k2, tn), BF16)], name="branch_merge",
        compiler_params=_cp("parallel", "arbitrary"),
    )(a1, a2, w1, w2, gates, gates)


def _s5_discretize(lam_re, lam_im, b_re, b_im, c_re, c_im, d, log_step, t_chunk):
    hp = lax.Precision.HIGHEST
    g, p, h = b_re.shape
    na = LANES // h
    nx = g // na
    assert 2 * p == LANES
    lam = lax.complex(jnp.minimum(lam_re, -1e-4), lam_im)
    dt = jnp.exp(log_step)[:, None]
    lam_bar = jnp.exp(lam * dt)
    b_bar = ((lam_bar - 1.0) / lam)[:, :, None] * lax.complex(b_re, b_im)
    c = lax.complex(c_re, c_im)
    tau = jnp.arange(t_chunk + 1, dtype=F32)
    pw = jnp.exp((lam * dt)[:, None, :] * tau[None, :, None].astype(jnp.complex64))
    cp = c[:, None, :, :] * pw[:, :, None, :]
    kern = jnp.einsum('gtop,gpi->gtoi', cp[:, :t_chunk], b_bar, precision=hp).real
    kio = jnp.transpose(kern.reshape(nx, na, t_chunk, h, h), (0, 2, 1, 4, 3))
    kt = jnp.broadcast_to(kio[:, :, :, :, None, :], (nx, t_chunk, na, h, na, h)).reshape(nx, t_chunk, LANES, LANES)
    wcx = pw[:, t_chunk - 1 - np.arange(t_chunk)][:, :, :, None] * b_bar[:, None, :, :]
    wcx = jnp.transpose(wcx, (0, 1, 3, 2))
    w_ri = jnp.concatenate([wcx.real, wcx.imag], axis=-1).reshape(nx, na, t_chunk, h, 2 * p)
    wc = jnp.transpose(w_ri, (0, 2, 1, 3, 4)).reshape(nx, t_chunk, LANES, 2 * p)
    cl = jnp.transpose(cp[:, 1:t_chunk + 1], (0, 3, 1, 2))
    vc = jnp.concatenate([cl.real, -cl.imag], axis=1).reshape(nx, na, 2 * p, t_chunk * h)
    dd = jnp.broadcast_to(d.reshape(nx, 1, 1, na, h), (nx, 1, t_chunk, na, h)).reshape(nx, 1, t_chunk * LANES)
    a = pw[:, t_chunk]
    a1 = jnp.concatenate([a.real, a.real], axis=-1).reshape(1, g * 2 * p)
    a2 = jnp.concatenate([-a.imag, a.imag], axis=-1).reshape(1, g * 2 * p)
    return kt, wc, vc, dd, a1, a2


def _chunk_rows(u_ref, t, nc):
    return jnp.concatenate([u_ref[pl.ds(s, nc, stride=t), :] for s in range(t)], axis=1)


def _s5_state_in_kernel(u_ref, wc_ref, xa_ref, xb_ref, wa_ref, wb_ref, *, t, nc, h):
    na = LANES // h
    row_group = lax.broadcasted_iota(I32, (LANES, LANES), 0) // h
    for s in range(t):
        blk = wc_ref[s]
        swapped = pltpu.roll(blk, LANES // 2, axis=1)
        for b in range(na):
            own = row_group == b
            dst = (slice(s * LANES, (s + 1) * LANES), slice(b * LANES, (b + 1) * LANES))
            wa_ref[dst] = jnp.where(own, blk, 0.0).astype(BF16)
            wb_ref[dst] = jnp.where(own, swapped, 0.0).astype(BF16)
    ucat = _chunk_rows(u_ref, t, nc).astype(BF16)
    xa_ref[...] = jnp.dot(ucat, wa_ref[...], preferred_element_type=F32)
    xb_ref[...] = jnp.dot(ucat, wb_ref[...], preferred_element_type=F32)


def _s5_scan_kernel(xa_ref, xb_ref, a1_ref, a2_ref, s_ref, *, unroll):
    a1 = a1_ref[...]
    a2 = a2_ref[...]
    nc = xa_ref.shape[0]

    def body(cb, carry):
        s_a, s_b = carry
        base = pl.multiple_of(cb * unroll, unroll)
        xa = xa_ref[pl.ds(base, unroll), :]
        xb = xb_ref[pl.ds(base, unroll), :]
        rows = []
        for k in range(unroll):
            rows.append(s_a)
            s_a, s_b = (a1 * s_a + a2 * s_b + xa[k:k + 1, :],
                        a1 * s_b - a2 * s_a + xb[k:k + 1, :])
        s_ref[pl.ds(base, unroll), :] = jnp.concatenate(rows, axis=0)
        return s_a, s_b

    z = jnp.zeros(a1.shape, F32)
    lax.fori_loop(0, nc // unroll, body, (z, z))


def _s5_out_kernel(u_ref, sp_ref, kt_ref, vc_ref, dd_ref, y_ref, bd_ref, vm_ref, *, t, nc, h, n_parts):
    part = pl.program_id(1)
    tp = t // n_parts
    na = LANES // h
    w = tp * LANES
    same_group = (lax.broadcasted_iota(I32, (LANES, LANES), 0) // h
                  == lax.broadcasted_iota(I32, (LANES, LANES), 1) // h)
    row = lax.broadcasted_iota(I32, (LANES, w), 0)
    col = lax.broadcasted_iota(I32, (LANES, w), 1)
    landing_group = jnp.where((row // h == col // LANES) & (row % h == col % h), (col // h) % na, -1)
    ucat = _chunk_rows(u_ref, t, nc)
    for k in range(n_parts):
        @pl.when(part == k)
        def _(k=k):
            live = (k + 1) * tp
            for s in range(live):
                for tt in range(tp):
                    lag = k * tp + tt - s
                    dst = (slice(s * LANES, (s + 1) * LANES), slice(tt * LANES, (tt + 1) * LANES))
                    if lag >= 0:
                        bd_ref[dst] = jnp.where(same_group, kt_ref[lag], 0.0).astype(BF16)
                    else:
                        bd_ref[dst] = jnp.zeros((LANES, LANES), BF16)
            for a in range(na):
                place = jnp.where(landing_group == a, 1.0, 0.0).astype(BF16)
                src = vc_ref[a][:, k * tp * h:(k + 1) * tp * h].astype(BF16)
                vm_ref[a * LANES:(a + 1) * LANES, :] = jnp.dot(src, place, preferred_element_type=F32).astype(BF16)
            y = jnp.dot(ucat[:, :live * LANES].astype(BF16), bd_ref[:live * LANES, :], preferred_element_type=F32)
            y = y + jnp.dot(sp_ref[...].astype(BF16), vm_ref[...], preferred_element_type=F32)
            y = jax.nn.gelu(y + dd_ref[...] * ucat[:, k * w:(k + 1) * w])
            for s in range(tp):
                y_ref[pl.ds(k * tp + s, nc, stride=t), :] = y[:, s * LANES:(s + 1) * LANES]


def _s5_branch(u, prm):
    kt, wc, vc, dd, a1, a2 = prm
    L, width = u.shape
    nx, na, _, th = vc.shape
    t = kt.shape[1]
    h = th // t
    tah = t * LANES
    sw = na * LANES
    nc = L // t
    u_spec = pl.BlockSpec((L, LANES), lambda x: (0, x))
    st_spec = pl.BlockSpec((nc, sw), lambda x: (0, x))
    xa, xb = pl.pallas_call(
        functools.partial(_s5_state_in_kernel, t=t, nc=nc, h=h), grid=(nx,),
        in_specs=[u_spec,
                  pl.BlockSpec((None, t, LANES, LANES), lambda x: (x, 0, 0, 0))],
        out_specs=[st_spec, st_spec],
        out_shape=[jax.ShapeDtypeStruct((nc, nx * sw), F32)] * 2,
        scratch_shapes=[pltpu.VMEM((tah, sw), BF16), pltpu.VMEM((tah, sw), BF16)], name="s5_state_in",
        compiler_params=_cp("parallel"),
    )(u, wc)
    ws = _tile(nx * sw, 2048)
    unroll = _tile(nc, 8)
    sc_spec = pl.BlockSpec((nc, ws), lambda x: (0, x))
    a_spec = pl.BlockSpec((1, ws), lambda x: (0, x))
    sp = pl.pallas_call(
        functools.partial(_s5_scan_kernel, unroll=unroll), grid=(nx * sw // ws,),
        in_specs=[sc_spec, sc_spec, a_spec, a_spec],
        out_specs=sc_spec,
        out_shape=jax.ShapeDtypeStruct((nc, nx * sw), F32), name="s5_scan",
        compiler_params=_cp("parallel"),
    )(xa, xb, a1, a2)
    n_parts = 2
    tw = tah // n_parts
    return pl.pallas_call(
        functools.partial(_s5_out_kernel, t=t, nc=nc, h=h, n_parts=n_parts), grid=(nx, n_parts),
        in_specs=[pl.BlockSpec((L, LANES), lambda x, k: (0, x)),
                  pl.BlockSpec((nc, sw), lambda x, k: (0, x)),
                  pl.BlockSpec((None, t, LANES, LANES), lambda x, k: (x, 0, 0, 0)),
                  pl.BlockSpec((None, na, LANES, th), lambda x, k: (x, 0, 0, 0)),
                  pl.BlockSpec((None, 1, tw), lambda x, k: (x, 0, k))],
        out_specs=pl.BlockSpec((L, LANES), lambda x, k: (0, x)),
        out_shape=jax.ShapeDtypeStruct((L, width), F32),
        scratch_shapes=[pltpu.VMEM((tah, tw), BF16), pltpu.VMEM((sw, tw), BF16)], name="s5_out",
        compiler_params=_cp("parallel", "arbitrary"),
    )(u, sp, kt, vc, dd)


def _t5_bucket_starts(n_buckets):
    exact = n_buckets // 2
    dist = np.arange(0, 4 * REL_MAX_DIST).astype(np.int64)
    far = exact + (np.log(np.maximum(dist, 1).astype(np.float32) / np.float32(exact))
                   / np.float32(math.log(REL_MAX_DIST / exact)) * np.float32(n_buckets - exact)).astype(np.int64)
    bucket = np.where(dist < exact, dist, np.minimum(far, n_buckets - 1))
    return [int(np.argmax(bucket >= b)) for b in range(1, n_buckets)]


def _bias_tile_kernel(rb_ref, o_ref, *, tb, starts):
    delta = pl.program_id(0)
    head = pl.program_id(1)
    key = lax.broadcasted_iota(I32, (tb, tb), 0)
    qry = lax.broadcasted_iota(I32, (tb, tb), 1)
    dist = delta * tb + qry - key
    bias = jnp.full((tb, tb), rb_ref[0, head], F32)
    for b, start in enumerate(starts, start=1):
        bias = jnp.where(dist >= start, rb_ref[b, head], bias)
    o_ref[...] = jnp.where(dist >= 0, bias * LOG2E, NEG_BIG)


def _bias_tiles(rel_bias, tb):
    n_buckets, heads = rel_bias.shape
    starts = _t5_bucket_starts(n_buckets)
    assert 2 * tb - (tb - 1) >= starts[-1]
    return pl.pallas_call(
        functools.partial(_bias_tile_kernel, tb=tb, starts=starts), grid=(2, heads),
        in_specs=[pl.BlockSpec(memory_space=pltpu.SMEM)],
        out_specs=pl.BlockSpec((None, None, tb, tb), lambda dl, hd: (dl, hd, 0, 0)),
        out_shape=jax.ShapeDtypeStruct((2, heads, tb, tb), F32), name="rel_bias_tiles",
        compiler_params=_cp("parallel", "parallel"),
    )(rel_bias)


def _indexer_kernel(qt_ref, k2_ref, wt_ref, keys_ref, thr_ref, *, tq, ck, n_heads, topk, w_scale):
    i = pl.program_id(0)
    n_blocks = keys_ref.shape[0] // tq
    n_live = (i + 1) * (tq // ck)
    wv = wt_ref[...] * w_scale

    def score_chunk(c, carry):
        row = pl.multiple_of(c * ck, ck)
        kk = k2_ref[c]
        acc = jnp.zeros((ck, tq), F32)
        for pr in range(n_heads // 2):
            s2 = jnp.dot(kk, qt_ref[pr * LANES:(pr + 1) * LANES, :], preferred_element_type=F32)
            acc = acc + jnp.maximum(s2[:ck], 0.0) * wv[2 * pr:2 * pr + 1, :]
            acc = acc + jnp.maximum(s2[ck:], 0.0) * wv[2 * pr + 1:2 * pr + 2, :]
        kpos = row + lax.broadcasted_iota(I32, (ck, tq), 0)
        qpos = i * tq + lax.broadcasted_iota(I32, (ck, tq), 1)
        acc = jnp.where(acc == 0.0, 0.0, acc)
        score = jnp.where(kpos <= qpos, acc, -jnp.inf)
        bits = pltpu.bitcast(score, I32)
        keys_ref[pl.ds(row, ck), :] = jnp.where(bits < 0, bits ^ 0x7FFFFFFF, bits)
        return carry

    lax.fori_loop(0, n_live, score_chunk, 0)

    def fill_block(b, carry):
        row = pl.multiple_of(b * tq, tq)
        keys_ref[pl.ds(row, tq), :] = jnp.full((tq, tq), INT_MIN, I32)
        return carry

    lax.fori_loop(i + 1, n_blocks, fill_block, 0)

    acc_rows = 32

    def count_ge(cand):
        def body(b, cnt):
            row = pl.multiple_of(b * tq, tq)
            hit = jnp.where(keys_ref[pl.ds(row, tq), :] >= cand, 1.0, 0.0)
            return cnt + jnp.sum(hit.reshape(tq // acc_rows, acc_rows, tq), axis=0)
        cnt = lax.fori_loop(0, i + 1, body, jnp.zeros((acc_rows, tq), F32))
        return jnp.sum(cnt, axis=0, keepdims=True)

    kf = float(topk)
    zero = jnp.zeros((1, tq), I32)
    c_zero = count_ge(zero)
    thr0 = jnp.where(c_zero >= kf, zero, jnp.full((1, tq), INT_MIN, I32))
    cnt0 = jnp.where(c_zero >= kf, c_zero, ((i + 1) * tq).astype(F32))

    def undecided(st):
        it, _, cnt = st
        return (it < 31) & (jnp.max(cnt) > kf)

    def bit_step(st):
        it, thr, cnt = st
        cand = thr | jnp.left_shift(jnp.int32(1), 30 - it)
        c = count_ge(cand)
        keep = c >= kf
        return it + 1, jnp.where(keep, cand, thr), jnp.where(keep, c, cnt)

    _, thr, cnt = lax.while_loop(undecided, bit_step, (jnp.int32(0), thr0, cnt0))

    n_keys = keys_ref.shape[0]

    def position_cutoff(_):
        need = kf - count_ge(thr + 1)

        def count_eq_upto(pos):
            def body(b, acc):
                row = pl.multiple_of(b * tq, tq)
                kpos = row + lax.broadcasted_iota(I32, (tq, tq), 0)
                hit = jnp.where(keys_ref[pl.ds(row, tq), :] == thr, jnp.where(kpos <= pos, 1.0, 0.0), 0.0)
                return acc + jnp.sum(hit.reshape(tq // acc_rows, acc_rows, tq), axis=0)
            acc = lax.fori_loop(0, i + 1, body, jnp.zeros((acc_rows, tq), F32))
            return jnp.sum(acc, axis=0, keepdims=True)

        def halve(_, bounds):
            lo, hi = bounds
            mid = (lo + hi) >> 1
            enough = count_eq_upto(mid) >= need
            return jnp.where(enough, lo, mid), jnp.where(enough, mid, hi)

        lo0 = jnp.full((1, tq), -1, I32)
        hi0 = jnp.full((1, tq), n_keys - 1, I32)
        _, hi = lax.fori_loop(0, max(1, (n_keys - 1).bit_length()), halve, (lo0, hi0))
        return jnp.where(cnt > kf, hi, n_keys)

    jcut = lax.cond(jnp.max(cnt) > kf, position_cutoff, lambda _: jnp.full((1, tq), n_keys, I32), 0)
    sel_row = lax.broadcasted_iota(I32, thr_ref.shape, 0)
    thr_ref[...] = jnp.where(sel_row == 0, thr, jnp.where(sel_row == 1, jcut, 0))


def _indexer(qi_t, row_off, k2, w_t, topk, tq=256):
    L = qi_t.shape[1]
    n_heads = w_t.shape[0]
    qw = n_heads * IDX_DIM
    assert row_off % qw == 0
    tq = _tile(L, tq)
    ck = k2.shape[1] // 2
    assert tq % ck == 0
    w_scale = (n_heads ** -0.5) * (IDX_DIM ** -0.5)
    return pl.pallas_call(
        functools.partial(_indexer_kernel, tq=tq, ck=ck, n_heads=n_heads, topk=topk, w_scale=w_scale),
        grid=(L // tq,),
        in_specs=[pl.BlockSpec((qw, tq), lambda i: (row_off // qw, i)),
                  pl.BlockSpec(k2.shape, lambda i: (0, 0, 0)),
                  pl.BlockSpec((n_heads, tq), lambda i: (0, i))],
        out_specs=[pl.BlockSpec((L, tq), lambda i: (0, i)),
                   pl.BlockSpec((8, tq), lambda i: (0, i))],
        out_shape=[jax.ShapeDtypeStruct((L, L), I32), jax.ShapeDtypeStruct((8, L), I32)],
        name="dsa_indexer", compiler_params=_cp("parallel"),
    )(qi_t, k2, w_t)


def _attn_kernel(qi_ref, kj_ref, far_ref, qt_ref, k_ref, vt_ref, keys_ref, thr_ref, bias_ref, o_ref,
                 m_ref, l_ref, acc_ref, mb_ref, *, heads, dh, scale2):
    step = pl.program_id(0)
    qi = qi_ref[step]
    kj = kj_ref[step]

    @pl.when(kj == 0)
    def _():
        m_ref[...] = jnp.full(m_ref.shape, -jnp.inf, F32)
        l_ref[...] = jnp.zeros(l_ref.shape, F32)
        acc_ref[...] = jnp.zeros(acc_ref.shape, F32)

    thr = thr_ref[0:1, :]
    kpos = kj * keys_ref.shape[0] + lax.broadcasted_iota(I32, keys_ref.shape, 0)
    bar = jnp.where(kpos <= thr_ref[1:2, :], thr, thr + 1)
    mb_ref[...] = jnp.where(keys_ref[...] >= bar, 0.0, NEG_BIG)

    def scores(h):
        hs = slice(h * dh, (h + 1) * dh)
        return jnp.dot(k_ref[:, hs], qt_ref[hs, :], preferred_element_type=F32)

    def fold(reduce_fn, a, rows=32):
        return reduce_fn(a.reshape(a.shape[0] // rows, rows, a.shape[1]), axis=0)

    def run(near):
        ahead = 4
        pending = [scores(h) for h in range(min(ahead, heads))]
        for h in range(heads):
            hs = slice(h * dh, (h + 1) * dh)
            x = pending.pop(0) * scale2 + mb_ref[...]
            if h + ahead < heads:
                pending.append(scores(h + ahead))
            if near:
                x = x + bias_ref[h]
            far = 0.0 if near else far_ref[h]
            m_prev = m_ref[h]
            m_new = jnp.maximum(m_prev, jnp.max(fold(jnp.max, x), axis=0, keepdims=True) + far)
            alpha = jnp.exp2(m_prev - m_new)
            p = jnp.exp2(x - (m_new[0:1, :] - far))
            l_ref[h] = alpha * l_ref[h] + jnp.sum(fold(jnp.sum, p), axis=0, keepdims=True)
            m_ref[h] = m_new
            acc_ref[hs, :] = alpha[0:1, :] * acc_ref[hs, :] + jnp.dot(vt_ref[hs, :], p.astype(BF16),
                                                                      preferred_element_type=F32)

    @pl.when(qi - kj < 2)
    def _():
        run(True)

    @pl.when(qi - kj >= 2)
    def _():
        run(False)

    @pl.when(kj == qi)
    def _():
        for h in range(heads):
            hs = slice(h * dh, (h + 1) * dh)
            o_ref[:, hs] = jnp.transpose(acc_ref[hs, :] / l_ref[h][0:1, :]).astype(o_ref.dtype)


def _attention(qv_t, q_blk, v_blk, k, keys, thr, bias_tiles, far_bias, tb):
    L = k.shape[0]
    heads = bias_tiles.shape[1]
    dh = ATT_HEAD_DIM
    aw = heads * dh
    nb = L // tb
    pairs = [(i, j) for i in range(nb) for j in range(i + 1)]
    qi = jnp.asarray(np.array([p[0] for p in pairs], np.int32))
    kj = jnp.asarray(np.array([p[1] for p in pairs], np.int32))
    grid_spec = pltpu.PrefetchScalarGridSpec(
        num_scalar_prefetch=2, grid=(len(pairs),),
        in_specs=[pl.BlockSpec(memory_space=pltpu.SMEM),
                  pl.BlockSpec((aw, tb), lambda s, qi, kj: (q_blk, qi[s])),
                  pl.BlockSpec((tb, aw), lambda s, qi, kj: (kj[s], 0)),
                  pl.BlockSpec((aw, tb), lambda s, qi, kj: (v_blk, kj[s])),
                  pl.BlockSpec((tb, tb), lambda s, qi, kj: (kj[s], qi[s])),
                  pl.BlockSpec((8, tb), lambda s, qi, kj: (0, qi[s])),
                  pl.BlockSpec((None, heads, tb, tb),
                               lambda s, qi, kj: (jnp.minimum(qi[s] - kj[s], 1), 0, 0, 0))],
        out_specs=pl.BlockSpec((tb, aw), lambda s, qi, kj: (qi[s], 0)),
        scratch_shapes=[pltpu.VMEM((heads, 8, tb), F32),
                        pltpu.VMEM((heads, 8, tb), F32),
                        pltpu.VMEM((aw, tb), F32),
                        pltpu.VMEM((tb, tb), F32)])
    return pl.pallas_call(
        functools.partial(_attn_kernel, heads=heads, dh=dh, scale2=dh ** -0.5 * LOG2E),
        grid_spec=grid_spec, out_shape=jax.ShapeDtypeStruct((L, aw), BF16), name="dsa_attention",
        compiler_params=_cp("arbitrary"),
    )(qi, kj, far_bias, qv_t, k, qv_t, keys, thr, bias_tiles)


def _router_kernel(h_ref, w_ref, b_ref, g_ref, *, n_exp, top_k):
    logits = jnp.dot(h_ref[...], w_ref[...], preferred_element_type=F32) + b_ref[...]
    lane = lax.broadcasted_iota(I32, logits.shape, 1).astype(F32)
    work = jnp.where(lane < n_exp, logits, -jnp.inf)
    gates = jnp.zeros(logits.shape, F32)
    denom = None
    top = None
    for r in range(top_k):
        m = jnp.max(work, axis=1, keepdims=True)
        first = jnp.min(jnp.where(work == m, lane, float(LANES)), axis=1, keepdims=True)
        sel = lane == first
        if r == 0:
            top = m
        e = jnp.exp(m - top)
        gates = gates + jnp.where(sel, e, 0.0)
        denom = e if denom is None else denom + e
        work = jnp.where(sel, -jnp.inf, work)
    g_ref[...] = gates / denom


def _router(h, w_pad, b_pad, n_exp, tm=512):
    m, d = h.shape
    tm = _tile(m, tm)
    return pl.pallas_call(
        functools.partial(_router_kernel, n_exp=n_exp, top_k=TOP_K_EXPERTS), grid=(m // tm,),
        in_specs=[pl.BlockSpec((tm, d), lambda i: (i, 0)),
                  pl.BlockSpec((d, LANES), lambda i: (0, 0)),
                  pl.BlockSpec((1, LANES), lambda i: (0, 0))],
        out_specs=pl.BlockSpec((tm, LANES), lambda i: (i, 0)),
        out_shape=jax.ShapeDtypeStruct((m, LANES), F32), name="moe_router",
        compiler_params=_cp("parallel"),
    )(h, w_pad, b_pad)


def _moe_kernel(h_ref, g_ref, wgu_ref, bgu_ref, wdn_ref, bdn_ref, o_ref, *, ff, eb):
    step = pl.program_id(1)
    gates = g_ref[...]

    @pl.when(step == 0)
    def _():
        o_ref[...] = jnp.dot(gates.astype(BF16), bdn_ref[...], preferred_element_type=F32)

    lane = lax.broadcasted_iota(I32, gates.shape, 1)
    for k in range(eb):
        gu = jnp.dot(h_ref[...], wgu_ref[k], preferred_element_type=F32) + bgu_ref[k]
        gate = jnp.minimum(gu[:, :ff], SWIGLU_LIMIT)
        up = jnp.clip(gu[:, ff:], -SWIGLU_LIMIT, SWIGLU_LIMIT)
        act = (up + 1.0) * gate * jax.nn.sigmoid(SWIGLU_ALPHA * gate)
        ge = jnp.sum(jnp.where(lane == step * eb + k, gates, 0.0), axis=1, keepdims=True)
        o_ref[...] += jnp.dot((act * ge).astype(BF16), wdn_ref[k], preferred_element_type=F32)


def _moe_experts(h, gates, w_gu, b_gu, w_dn, b_dn_pad, tm=512, eb=2):
    m, d = h.shape
    n_exp, _, ff2 = w_gu.shape
    tm = _tile(m, tm)
    eb = _tile(n_exp, eb)
    return pl.pallas_call(
        functools.partial(_moe_kernel, ff=ff2 // 2, eb=eb), grid=(m // tm, n_exp // eb),
        in_specs=[pl.BlockSpec((tm, d), lambda i, e: (i, 0)),
                  pl.BlockSpec((tm, LANES), lambda i, e: (i, 0)),
                  pl.BlockSpec((eb, d, ff2), lambda i, e: (e, 0, 0)),
                  pl.BlockSpec((eb, 1, ff2), lambda i, e: (e, 0, 0)),
                  pl.BlockSpec((eb, ff2 // 2, d), lambda i, e: (e, 0, 0)),
                  pl.BlockSpec((LANES, d), lambda i, e: (0, 0))],
        out_specs=pl.BlockSpec((tm, d), lambda i, e: (i, 0)),
        out_shape=jax.ShapeDtypeStruct((m, d), F32), name="moe_experts",
        compiler_params=_cp("parallel", "arbitrary"),
    )(h, gates, w_gu, b_gu.reshape(n_exp, 1, ff2), w_dn, b_dn_pad)


def kernel(x, c, w_in, w_a_out, w_b_out, w_o, w_glu, b_glu, s5_lam_re, s5_lam_im, s5_b_re, s5_b_im, s5_c_re, s5_c_im, s5_d, s5_log_step, rel_bias, w_router, b_router, w_gu, b_gu, w_dn, b_dn, w_c1, b_c1, w_c2, b_c2, mod_table, ln_g, ln_b):
    bsz, L, d = x.shape
    assert bsz == 1 and c.shape[0] == 1
    depth = w_in.shape[0]
    g, p, h = s5_b_re.shape[1:]
    s5w = g * h
    heads = rel_bias.shape[1]
    aw = heads * ATT_HEAD_DIM
    iw = IDX_HEADS * IDX_DIM
    n_exp = w_router.shape[2]
    assert w_in.shape[2] == s5w + 3 * aw + iw + IDX_DIM + IDX_HEADS + 2 * d
    assert n_exp <= LANES and IDX_DIM + IDX_HEADS <= LANES and 2 * IDX_DIM == LANES
    o_q, o_k, o_v, o_qi = s5w, s5w + aw, s5w + 2 * aw, s5w + 3 * aw
    o_ki = s5w + 3 * aw + iw
    o_gate = o_ki + IDX_DIM + IDX_HEADS
    topk = min(TOPK_MAX, L // 4)
    tb = _tile(L, 256)
    ck = _tile(L, 128)
    dn_alpha = (2 * depth) ** 0.25

    mod = _conditioning(c, w_c1, b_c1, w_c2, b_c2, mod_table).reshape(depth, N_MOD, d)
    bias_tiles = _bias_tiles(rel_bias, tb)
    far_bias = rel_bias[-1] * LOG2E
    assert o_ki % LANES == 0 and (g * h) % LANES == 0 and LANES % h == 0

    xs = x.reshape(L, d)
    hcur = _modulate(xs, mod[0], sc_row=1, sh_row=0)
    for l in range(depth):
        w_l = w_in[l]
        w_nat = jnp.concatenate(
            [w_l[:, :s5w], w_l[:, o_k:o_k + aw], w_l[:, o_gate:],
             jnp.pad(w_l[:, o_ki:o_gate], ((0, 0), (0, LANES - IDX_DIM - IDX_HEADS)))], axis=1).astype(BF16)
        w_tr = jnp.transpose(jnp.concatenate(
            [w_l[:, o_q:o_q + aw], w_l[:, o_v:o_v + aw], w_l[:, o_qi:o_qi + iw]], axis=1)).astype(BF16)
        u = _matmul(hcur, w_nat, out_dtype=F32, col_off=0, n_cols=s5w, name="proj_u")
        k = _matmul(hcur, w_nat, out_dtype=BF16, col_off=s5w, n_cols=aw, name="proj_k")
        gates_ab = _matmul(hcur, w_nat, out_dtype=F32, col_off=s5w + aw, n_cols=2 * d, name="proj_gates")
        misc = _matmul(hcur, w_nat, out_dtype=F32, col_off=s5w + aw + 2 * d, n_cols=LANES, tn=LANES,
                       name="proj_misc")
        qvi_t = _matmul_t(hcur, w_tr, out_dtype=BF16, name="proj_qvi_t")

        prm = _s5_discretize(s5_lam_re[l], s5_lam_im[l], s5_b_re[l], s5_b_im[l], s5_c_re[l], s5_c_im[l],
                             s5_d[l], s5_log_step[l], S5_CHUNK)
        y_s5 = _s5_branch(u, prm)
        ya_in = _glu(y_s5, w_glu, b_glu[l], l)

        k_idx = misc[:, :IDX_DIM].astype(BF16)
        zeros = jnp.zeros_like(k_idx)
        k_even = jnp.concatenate([k_idx, zeros], axis=1).reshape(L // ck, ck, LANES)
        k_odd = jnp.concatenate([zeros, k_idx], axis=1).reshape(L // ck, ck, LANES)
        k2 = jnp.concatenate([k_even, k_odd], axis=1)
        w_t = jnp.transpose(misc[:, IDX_DIM:IDX_DIM + IDX_HEADS])
        keys, thr = _indexer(qvi_t, 2 * aw, k2, w_t, topk)
        attn = _attention(qvi_t, 0, 1, k, keys, thr, bias_tiles, far_bias, tb)

        merged = _merge(ya_in, attn, w_a_out, w_b_out, gates_ab, l)
        y = _matmul(merged, w_o, layer=l, out_dtype=F32, name="proj_out")
        xs, h2 = _ln_modulate(xs, y, mod[l], ln_g[l], ln_b[l], mod[l], alpha=dn_alpha, gate_row=2, ln_row=0,
                              sc_row=4, sh_row=3, want_h=True)

        w_r = jnp.pad(w_router[l], ((0, 0), (0, LANES - n_exp))).astype(BF16)
        b_r = jnp.pad(b_router[l], (0, LANES - n_exp)).reshape(1, LANES)
        gates = _router(h2, w_r, b_r, n_exp)
        b_dn_pad = jnp.pad(b_dn[l], ((0, LANES - n_exp), (0, 0))).astype(BF16)
        y = _moe_experts(h2, gates, w_gu[l].astype(BF16), b_gu[l], w_dn[l].astype(BF16), b_dn_pad)
        last = l == depth - 1
        xs, hcur = _ln_modulate(xs, y, mod[l], ln_g[l], ln_b[l], mod[min(l + 1, depth - 1)], alpha=dn_alpha,
                                gate_row=5, ln_row=1, sc_row=1, sh_row=0, want_h=not last)
    return xs.reshape(bsz, L, d)
```

```python
import functools
import math

import numpy as np
import jax
import jax.numpy as jnp
from jax import lax
from jax.experimental import pallas as pl
from jax.experimental.pallas import tpu as pltpu

F32 = jnp.float32
BF16 = jnp.bfloat16
I32 = jnp.int32

ATT_HEAD_DIM = 128
IDX_HEADS = 32
IDX_DIM = 64
TOPK_MAX = 256
REL_MAX_DIST = 128
TOP_K_EXPERTS = 4
SWIGLU_LIMIT = 7.0
SWIGLU_ALPHA = 1.702
N_MOD = 6
LN_EPS = 1e-5

LANES = 128
VMEM_LIMIT_BYTES = 56 * 1024 * 1024
S5_CHUNK = 16
NEG_BIG = -1e30
LOG2E = math.log2(math.e)
INT_MIN = -(2 ** 31)


def _cp(*sem):
    return pltpu.CompilerParams(dimension_semantics=sem, vmem_limit_bytes=VMEM_LIMIT_BYTES)


def _tile(n, t):
    t = min(n, t)
    while n % t:
        t //= 2
    return t


def _cond1_kernel(c_ref, w_ref, b_ref, o_ref):
    a = jax.nn.silu(c_ref[...]).astype(BF16)
    t = jnp.dot(a, w_ref[...].astype(BF16), preferred_element_type=F32) + b_ref[...]
    o_ref[...] = jax.nn.silu(t)


def _cond2_kernel(t_ref, w_ref, b_ref, tab_ref, o_ref):
    acc = jnp.dot(t_ref[...].astype(BF16), w_ref[...].astype(BF16), preferred_element_type=F32)
    o_ref[...] = acc[0:1, :] + b_ref[...] + tab_ref[...]


def _conditioning(c, w_c1, b_c1, w_c2, b_c2, mod_table):
    d = c.shape[1]
    r = w_c1.shape[1]
    depth = mod_table.shape[0]
    n = w_c2.shape[1]
    c8 = jnp.broadcast_to(c, (8, d))
    t = pl.pallas_call(
        _cond1_kernel, out_shape=jax.ShapeDtypeStruct((8, r), F32), name="cond1",
        compiler_params=_cp(),
    )(c8, w_c1, b_c1.reshape(1, r))
    tn = _tile(n, 2048)
    return pl.pallas_call(
        _cond2_kernel, grid=(n // tn,),
        in_specs=[pl.BlockSpec((8, r), lambda j: (0, 0)),
                  pl.BlockSpec((r, tn), lambda j: (0, j)),
                  pl.BlockSpec((1, tn), lambda j: (0, j)),
                  pl.BlockSpec((depth, tn), lambda j: (0, j))],
        out_specs=pl.BlockSpec((depth, tn), lambda j: (0, j)),
        out_shape=jax.ShapeDtypeStruct((depth, n), F32), name="cond2",
        compiler_params=_cp("parallel"),
    )(t, w_c2, b_c2.reshape(1, n), mod_table.reshape(depth, n))


def _modulate_kernel(x_ref, mod_ref, h_ref, *, sc_row, sh_row):
    sc = mod_ref[sc_row:sc_row + 1, :]
    sh = mod_ref[sh_row:sh_row + 1, :]
    h_ref[...] = (x_ref[...] * (1.0 + sc) + sh).astype(h_ref.dtype)


def _modulate(x, mod_l, sc_row, sh_row, tm=256):
    m, d = x.shape
    tm = _tile(m, tm)
    return pl.pallas_call(
        functools.partial(_modulate_kernel, sc_row=sc_row, sh_row=sh_row),
        grid=(m // tm,),
        in_specs=[pl.BlockSpec((tm, d), lambda i: (i, 0)),
                  pl.BlockSpec((N_MOD, d), lambda i: (0, 0))],
        out_specs=pl.BlockSpec((tm, d), lambda i: (i, 0)),
        out_shape=jax.ShapeDtypeStruct((m, d), BF16), name="modulate",
        compiler_params=_cp("parallel"),
    )(x, mod_l)


def _ln_kernel(x_ref, y_ref, mod_ref, g_ref, b_ref, nmod_ref, xo_ref, *h_refs,
               alpha, gate_row, ln_row, sc_row, sh_row):
    gate = mod_ref[gate_row:gate_row + 1, :]
    u = alpha * x_ref[...] + (1.0 + gate) * y_ref[...]
    mu = jnp.mean(u, axis=-1, keepdims=True)
    uc = u - mu
    var = jnp.mean(uc * uc, axis=-1, keepdims=True)
    xn = uc * lax.rsqrt(var + LN_EPS) * g_ref[ln_row:ln_row + 1, :] + b_ref[ln_row:ln_row + 1, :]
    xo_ref[...] = xn
    if h_refs:
        sc = nmod_ref[sc_row:sc_row + 1, :]
        sh = nmod_ref[sh_row:sh_row + 1, :]
        h_refs[0][...] = (xn * (1.0 + sc) + sh).astype(BF16)


def _ln_modulate(x, y, mod_l, ln_g_l, ln_b_l, next_mod, *, alpha, gate_row, ln_row, sc_row, sh_row,
                 want_h, tm=256):
    m, d = x.shape
    tm = _tile(m, tm)
    row = pl.BlockSpec((tm, d), lambda i: (i, 0))
    out_shape = [jax.ShapeDtypeStruct((m, d), F32)]
    out_specs = [row]
    if want_h:
        out_shape.append(jax.ShapeDtypeStruct((m, d), BF16))
        out_specs.append(row)
    res = pl.pallas_call(
        functools.partial(_ln_kernel, alpha=alpha, gate_row=gate_row, ln_row=ln_row,
                          sc_row=sc_row, sh_row=sh_row),
        grid=(m // tm,),
        in_specs=[row, row,
                  pl.BlockSpec((N_MOD, d), lambda i: (0, 0)),
                  pl.BlockSpec((2, d), lambda i: (0, 0)),
                  pl.BlockSpec((2, d), lambda i: (0, 0)),
                  pl.BlockSpec((N_MOD, d), lambda i: (0, 0))],
        out_specs=out_specs, out_shape=out_shape, name="ln_modulate",
        compiler_params=_cp("parallel"),
    )(x, y, mod_l, ln_g_l, ln_b_l, next_mod)
    return (res[0], res[1]) if want_h else (res[0], None)


def _mm_kernel(a_ref, w_ref, o_ref, wbf_ref):
    @pl.when(pl.program_id(1) == 0)
    def _():
        wbf_ref[...] = w_ref[...].astype(BF16)

    o_ref[...] = jnp.dot(a_ref[...], wbf_ref[...], preferred_element_type=F32).astype(o_ref.dtype)


def _w_spec(w, layer, k, tn, off=0):
    if w.ndim == 3:
        return pl.BlockSpec((None, k, tn), lambda j, i: (layer, 0, j + off))
    return pl.BlockSpec((k, tn), lambda j, i: (0, j + off))


def _matmul(a, w, *, out_dtype, layer=0, col_off=0, n_cols=None, tm=1024, tn=512, name="matmul"):
    m, k = a.shape
    n_cols = w.shape[-1] - col_off if n_cols is None else n_cols
    tm = _tile(m, tm)
    tn = math.gcd(_tile(n_cols, tn), col_off) if col_off else _tile(n_cols, tn)
    off = col_off // tn
    return pl.pallas_call(
        _mm_kernel, grid=(n_cols // tn, m // tm),
        in_specs=[pl.BlockSpec((tm, k), lambda j, i: (i, 0)),
                  _w_spec(w, layer, k, tn, off)],
        out_specs=pl.BlockSpec((tm, tn), lambda j, i: (i, j)),
        out_shape=jax.ShapeDtypeStruct((m, n_cols), out_dtype),
        scratch_shapes=[pltpu.VMEM((k, tn), BF16)], name=name,
        compiler_params=_cp("parallel", "arbitrary"),
    )(a, w)


def _mm_t_kernel(a_ref, wt_ref, o_ref):
    o_ref[...] = lax.dot_general(wt_ref[...], a_ref[...], (((1,), (1,)), ((), ())),
                                 preferred_element_type=F32).astype(o_ref.dtype)


def _matmul_t(a, w_t, *, out_dtype, tm=1024, tn=1024, name="matmul_t"):
    m, k = a.shape
    n = w_t.shape[0]
    tm = _tile(m, tm)
    tn = _tile(n, tn)
    return pl.pallas_call(
        _mm_t_kernel, grid=(n // tn, m // tm),
        in_specs=[pl.BlockSpec((tm, k), lambda j, i: (i, 0)),
                  pl.BlockSpec((tn, k), lambda j, i: (j, 0))],
        out_specs=pl.BlockSpec((tn, tm), lambda j, i: (j, i)),
        out_shape=jax.ShapeDtypeStruct((n, m), out_dtype), name=name,
        compiler_params=_cp("parallel", "arbitrary"),
    )(a, w_t)


def _glu_kernel(y_ref, yt_ref, w_ref, b_ref, o_ref, wbf_ref):
    @pl.when(pl.program_id(1) == 0)
    def _():
        wbf_ref[...] = w_ref[...].astype(BF16)

    z = jnp.dot(y_ref[...].astype(BF16), wbf_ref[...], preferred_element_type=F32) + b_ref[...]
    o_ref[...] = (yt_ref[...] * jax.nn.sigmoid(z)).astype(o_ref.dtype)


def _glu(y, w, b, layer, tm=1024, tn=512):
    m, k = y.shape
    n = w.shape[-1]
    tm = _tile(m, tm)
    tn = _tile(n, tn)
    return pl.pallas_call(
        _glu_kernel, grid=(n // tn, m // tm),
        in_specs=[pl.BlockSpec((tm, k), lambda j, i: (i, 0)),
                  pl.BlockSpec((tm, tn), lambda j, i: (i, j)),
                  _w_spec(w, layer, k, tn),
                  pl.BlockSpec((1, tn), lambda j, i: (0, j))],
        out_specs=pl.BlockSpec((tm, tn), lambda j, i: (i, j)),
        out_shape=jax.ShapeDtypeStruct((m, n), BF16),
        scratch_shapes=[pltpu.VMEM((k, tn), BF16)], name="s5_glu",
        compiler_params=_cp("parallel", "arbitrary"),
    )(y, y, w, b.reshape(1, n))


def _merge_kernel(a1_ref, a2_ref, w1_ref, w2_ref, ga_ref, gb_ref, o_ref, w1bf_ref, w2bf_ref):
    @pl.when(pl.program_id(1) == 0)
    def _():
        w1bf_ref[...] = w1_ref[...].astype(BF16)
        w2bf_ref[...] = w2_ref[...].astype(BF16)

    ya = jnp.dot(a1_ref[...], w1bf_ref[...], preferred_element_type=F32)
    yb = jnp.dot(a2_ref[...], w2bf_ref[...], preferred_element_type=F32)
    o_ref[...] = (jax.nn.sigmoid(ga_ref[...]) * ya + jax.nn.sigmoid(gb_ref[...]) * yb).astype(o_ref.dtype)


def _merge(a1, a2, w1, w2, gates, layer, tm=1024, tn=512):
    m, k1 = a1.shape
    k2 = a2.shape[1]
    n = w1.shape[-1]
    tm = _tile(m, tm)
    tn = _tile(n, tn)
    nb = n // tn
    return pl.pallas_call(
        _merge_kernel, grid=(nb, m // tm),
        in_specs=[pl.BlockSpec((tm, k1), lambda j, i: (i, 0)),
                  pl.BlockSpec((tm, k2), lambda j, i: (i, 0)),
                  _w_spec(w1, layer, k1, tn),
                  _w_spec(w2, layer, k2, tn),
                  pl.BlockSpec((tm, tn), lambda j, i: (i, j)),
                  pl.BlockSpec((tm, tn), lambda j, i: (i, j + nb))],
        out_specs=pl.BlockSpec((tm, tn), lambda j, i: (i, j)),
        out_shape=jax.ShapeDtypeStruct((m, n), BF16),
        scratch_shapes=[pltpu.VMEM((k1, tn), BF16), pltpu.VMEM((k2, tn), BF16)], name="branch_merge",
        compiler_params=_cp("parallel", "arbitrary"),
    )(a1, a2, w1, w2, gates, gates)


def _s5_discretize(lam_re, lam_im, b_re, b_im, c_re, c_im, d, log_step, t_chunk):
    hp = lax.Precision.HIGHEST
    g, p, h = b_re.shape
    na = LANES // h
    nx = g // na
    assert 2 * p == LANES
    lam = lax.complex(jnp.minimum(lam_re, -1e-4), lam_im)
    dt = jnp.exp(log_step)[:, None]
    lam_bar = jnp.exp(lam * dt)
    b_bar = ((lam_bar - 1.0) / lam)[:, :, None] * lax.complex(b_re, b_im)
    c = lax.complex(c_re, c_im)
    tau = jnp.arange(t_chunk + 1, dtype=F32)
    pw = jnp.exp((lam * dt)[:, None, :] * tau[None, :, None].astype(jnp.complex64))
    cp = c[:, None, :, :] * pw[:, :, None, :]
    kern = jnp.einsum('gtop,gpi->gtoi', cp[:, :t_chunk], b_bar, precision=hp).real
    kio = jnp.transpose(kern.reshape(nx, na, t_chunk, h, h), (0, 2, 1, 4, 3))
    kt = jnp.broadcast_to(kio[:, :, :, :, None, :], (nx, t_chunk, na, h, na, h)).reshape(nx, t_chunk, LANES, LANES)
    wcx = pw[:, t_chunk - 1 - np.arange(t_chunk)][:, :, :, None] * b_bar[:, None, :, :]
    wcx = jnp.transpose(wcx, (0, 1, 3, 2))
    w_ri = jnp.concatenate([wcx.real, wcx.imag], axis=-1).reshape(nx, na, t_chunk, h, 2 * p)
    wc = jnp.transpose(w_ri, (0, 2, 1, 3, 4)).reshape(nx, t_chunk, LANES, 2 * p)
    cl = jnp.transpose(cp[:, 1:t_chunk + 1], (0, 3, 1, 2))
    vc = jnp.concatenate([cl.real, -cl.imag], axis=1).reshape(nx, na, 2 * p, t_chunk * h)
    dd = jnp.broadcast_to(d.reshape(nx, 1, 1, na, h), (nx, 1, t_chunk, na, h)).reshape(nx, 1, t_chunk * LANES)
    a = pw[:, t_chunk]
    a1 = jnp.concatenate([a.real, a.real], axis=-1).reshape(1, g * 2 * p)
    a2 = jnp.concatenate([-a.imag, a.imag], axis=-1).reshape(1, g * 2 * p)
    return kt, wc, vc, dd, a1, a2


def _chunk_rows(u_ref, t, nc):
    return jnp.concatenate([u_ref[pl.ds(s, nc, stride=t), :] for s in range(t)], axis=1)


def _s5_state_in_kernel(u_ref, wc_ref, xa_ref, xb_ref, wa_ref, wb_ref, *, t, nc, h):
    na = LANES // h
    row_group = lax.broadcasted_iota(I32, (LANES, LANES), 0) // h
    for s in range(t):
        blk = wc_ref[s]
        swapped = pltpu.roll(blk, LANES // 2, axis=1)
        for b in range(na):
            own = row_group == b
            dst = (slice(s * LANES, (s + 1) * LANES), slice(b * LANES, (b + 1) * LANES))
            wa_ref[dst] = jnp.where(own, blk, 0.0).astype(BF16)
            wb_ref[dst] = jnp.where(own, swapped, 0.0).astype(BF16)
    ucat = _chunk_rows(u_ref, t, nc).astype(BF16)
    xa_ref[...] = jnp.dot(ucat, wa_ref[...], preferred_element_type=F32)
    xb_ref[...] = jnp.dot(ucat, wb_ref[...], preferred_element_type=F32)


def _s5_scan_kernel(xa_ref, xb_ref, a1_ref, a2_ref, s_ref, *, unroll):
    a1 = a1_ref[...]
    a2 = a2_ref[...]
    nc = xa_ref.shape[0]

    def body(cb, carry):
        s_a, s_b = carry
        base = pl.multiple_of(cb * unroll, unroll)
        xa = xa_ref[pl.ds(base, unroll), :]
        xb = xb_ref[pl.ds(base, unroll), :]
        rows = []
        for k in range(unroll):
            rows.append(s_a)
            s_a, s_b = (a1 * s_a + a2 * s_b + xa[k:k + 1, :],
                        a1 * s_b - a2 * s_a + xb[k:k + 1, :])
        s_ref[pl.ds(base, unroll), :] = jnp.concatenate(rows, axis=0)
        return s_a, s_b

    z = jnp.zeros(a1.shape, F32)
    lax.fori_loop(0, nc // unroll, body, (z, z))


def _s5_out_kernel(u_ref, sp_ref, kt_ref, vc_ref, dd_ref, y_ref, bd_ref, vm_ref, *, t, nc, h, n_parts):
    part = pl.program_id(1)
    tp = t // n_parts
    na = LANES // h
    w = tp * LANES
    same_group = (lax.broadcasted_iota(I32, (LANES, LANES), 0) // h
                  == lax.broadcasted_iota(I32, (LANES, LANES), 1) // h)
    row = lax.broadcasted_iota(I32, (LANES, w), 0)
    col = lax.broadcasted_iota(I32, (LANES, w), 1)
    landing_group = jnp.where((row // h == col // LANES) & (row % h == col % h), (col // h) % na, -1)
    ucat = _chunk_rows(u_ref, t, nc)
    for k in range(n_parts):
        @pl.when(part == k)
        def _(k=k):
            live = (k + 1) * tp
            for s in range(live):
                for tt in range(tp):
                    lag = k * tp + tt - s
                    dst = (slice(s * LANES, (s + 1) * LANES), slice(tt * LANES, (tt + 1) * LANES))
                    if lag >= 0:
                        bd_ref[dst] = jnp.where(same_group, kt_ref[lag], 0.0).astype(BF16)
                    else:
                        bd_ref[dst] = jnp.zeros((LANES, LANES), BF16)
            for a in range(na):
                place = jnp.where(landing_group == a, 1.0, 0.0).astype(BF16)
                src = vc_ref[a][:, k * tp * h:(k + 1) * tp * h].astype(BF16)
                vm_ref[a * LANES:(a + 1) * LANES, :] = jnp.dot(src, place, preferred_element_type=F32).astype(BF16)
            y = jnp.dot(ucat[:, :live * LANES].astype(BF16), bd_ref[:live * LANES, :], preferred_element_type=F32)
            y = y + jnp.dot(sp_ref[...].astype(BF16), vm_ref[...], preferred_element_type=F32)
            y = jax.nn.gelu(y + dd_ref[...] * ucat[:, k * w:(k + 1) * w])
            for s in range(tp):
                y_ref[pl.ds(k * tp + s, nc, stride=t), :] = y[:, s * LANES:(s + 1) * LANES]


def _s5_branch(u, prm):
    kt, wc, vc, dd, a1, a2 = prm
    L, width = u.shape
    nx, na, _, th = vc.shape
    t = kt.shape[1]
    h = th // t
    tah = t * LANES
    sw = na * LANES
    nc = L // t
    u_spec = pl.BlockSpec((L, LANES), lambda x: (0, x))
    st_spec = pl.BlockSpec((nc, sw), lambda x: (0, x))
    xa, xb = pl.pallas_call(
        functools.partial(_s5_state_in_kernel, t=t, nc=nc, h=h), grid=(nx,),
        in_specs=[u_spec,
                  pl.BlockSpec((None, t, LANES, LANES), lambda x: (x, 0, 0, 0))],
        out_specs=[st_spec, st_spec],
        out_shape=[jax.ShapeDtypeStruct((nc, nx * sw), F32)] * 2,
        scratch_shapes=[pltpu.VMEM((tah, sw), BF16), pltpu.VMEM((tah, sw), BF16)], name="s5_state_in",
        compiler_params=_cp("parallel"),
    )(u, wc)
    ws = _tile(nx * sw, 2048)
    unroll = _tile(nc, 8)
    sc_spec = pl.BlockSpec((nc, ws), lambda x: (0, x))
    a_spec = pl.BlockSpec((1, ws), lambda x: (0, x))
    sp = pl.pallas_call(
        functools.partial(_s5_scan_kernel, unroll=unroll), grid=(nx * sw // ws,),
        in_specs=[sc_spec, sc_spec, a_spec, a_spec],
        out_specs=sc_spec,
        out_shape=jax.ShapeDtypeStruct((nc, nx * sw), F32), name="s5_scan",
        compiler_params=_cp("parallel"),
    )(xa, xb, a1, a2)
    n_parts = 2
    tw = tah // n_parts
    return pl.pallas_call(
        functools.partial(_s5_out_kernel, t=t, nc=nc, h=h, n_parts=n_parts), grid=(nx, n_parts),
        in_specs=[pl.BlockSpec((L, LANES), lambda x, k: (0, x)),
                  pl.BlockSpec((nc, sw), lambda x, k: (0, x)),
                  pl.BlockSpec((None, t, LANES, LANES), lambda x, k: (x, 0, 0, 0)),
                  pl.BlockSpec((None, na, LANES, th), lambda x, k: (x, 0, 0, 0)),
                  pl.BlockSpec((None, 1, tw), lambda x, k: (x, 0, k))],
        out_specs=pl.BlockSpec((L, LANES), lambda x, k: (0, x)),
        out_shape=jax.ShapeDtypeStruct((L, width), F32),
        scratch_shapes=[pltpu.VMEM((tah, tw), BF16), pltpu.VMEM((sw, tw), BF16)], name="s5_out",
        compiler_params=_cp("parallel", "arbitrary"),
    )(u, sp, kt, vc, dd)


def _t5_bucket_starts(n_buckets):
    exact = n_buckets // 2
    dist = np.arange(0, 4 * REL_MAX_DIST).astype(np.int64)
    far = exact + (np.log(np.maximum(dist, 1).astype(np.float32) / np.float32(exact))
                   / np.float32(math.log(REL_MAX_DIST / exact)) * np.float32(n_buckets - exact)).astype(np.int64)
    bucket = np.where(dist < exact, dist, np.minimum(far, n_buckets - 1))
    return [int(np.argmax(bucket >= b)) for b in range(1, n_buckets)]


def _bias_tile_kernel(rb_ref, o_ref, *, tb, starts):
    delta = jnp.where(pl.program_id(0) == 3, -1, pl.program_id(0))
    head = pl.program_id(1)
    key = lax.broadcasted_iota(I32, (tb, tb), 0)
    qry = lax.broadcasted_iota(I32, (tb, tb), 1)
    dist = delta * tb + qry - key
    bias = jnp.full((tb, tb), rb_ref[0, head], F32)
    for b, start in enumerate(starts, start=1):
        bias = jnp.where(dist >= start, rb_ref[b, head], bias)
    o_ref[...] = jnp.where(dist >= 0, bias * LOG2E, NEG_BIG)


def _bias_tiles(rel_bias, tb):
    n_buckets, heads = rel_bias.shape
    starts = _t5_bucket_starts(n_buckets)
    assert 2 * tb - (tb - 1) >= starts[-1]
    return pl.pallas_call(
        functools.partial(_bias_tile_kernel, tb=tb, starts=starts), grid=(4, heads),
        in_specs=[pl.BlockSpec(memory_space=pltpu.SMEM)],
        out_specs=pl.BlockSpec((None, None, tb, tb), lambda dl, hd: (dl, hd, 0, 0)),
        out_shape=jax.ShapeDtypeStruct((4, heads, tb, tb), F32), name="rel_bias_tiles",
        compiler_params=_cp("parallel", "parallel"),
    )(rel_bias)


def _indexer_kernel(qt_ref, k2_ref, wt_ref, keys_ref, thr_ref, *, tq, ck, n_heads, topk, w_scale):
    i = pl.program_id(0)
    n_blocks = keys_ref.shape[0] // tq
    n_live = (i + 1) * (tq // ck)
    wv = wt_ref[...] * w_scale

    def score_chunk(c, carry):
        row = pl.multiple_of(c * ck, ck)
        kk = k2_ref[c]
        acc = jnp.zeros((ck, tq), F32)
        for pr in range(n_heads // 2):
            s2 = jnp.dot(kk, qt_ref[pr * LANES:(pr + 1) * LANES, :], preferred_element_type=F32)
            acc = acc + jnp.maximum(s2[:ck], 0.0) * wv[2 * pr:2 * pr + 1, :]
            acc = acc + jnp.maximum(s2[ck:], 0.0) * wv[2 * pr + 1:2 * pr + 2, :]
        kpos = row + lax.broadcasted_iota(I32, (ck, tq), 0)
        qpos = i * tq + lax.broadcasted_iota(I32, (ck, tq), 1)
        acc = jnp.where(acc == 0.0, 0.0, acc)
        score = jnp.where(kpos <= qpos, acc, -jnp.inf)
        bits = pltpu.bitcast(score, I32)
        keys_ref[pl.ds(row, ck), :] = jnp.where(bits < 0, bits ^ 0x7FFFFFFF, bits)
        return carry

    lax.fori_loop(0, n_live, score_chunk, 0)

    def fill_block(b, carry):
        row = pl.multiple_of(b * tq, tq)
        keys_ref[pl.ds(row, tq), :] = jnp.full((tq, tq), INT_MIN, I32)
        return carry

    lax.fori_loop(i + 1, n_blocks, fill_block, 0)

    acc_rows = 32

    def count_ge(cand):
        def body(b, cnt):
            row = pl.multiple_of(b * tq, tq)
            hit = jnp.where(keys_ref[pl.ds(row, tq), :] >= cand, 1.0, 0.0)
            return cnt + jnp.sum(hit.reshape(tq // acc_rows, acc_rows, tq), axis=0)
        cnt = lax.fori_loop(0, i + 1, body, jnp.zeros((acc_rows, tq), F32))
        return jnp.sum(cnt, axis=0, keepdims=True)

    kf = float(topk)
    zero = jnp.zeros((1, tq), I32)
    c_zero = count_ge(zero)
    thr0 = jnp.where(c_zero >= kf, zero, jnp.full((1, tq), INT_MIN, I32))
    cnt0 = jnp.where(c_zero >= kf, c_zero, ((i + 1) * tq).astype(F32))

    def undecided(st):
        it, _, cnt = st
        return (it < 31) & (jnp.max(cnt) > kf)

    def bit_step(st):
        it, thr, cnt = st
        cand = thr | jnp.left_shift(jnp.int32(1), 30 - it)
        c = count_ge(cand)
        keep = c >= kf
        return it + 1, jnp.where(keep, cand, thr), jnp.where(keep, c, cnt)

    _, thr, cnt = lax.while_loop(undecided, bit_step, (jnp.int32(0), thr0, cnt0))

    n_keys = keys_ref.shape[0]

    def position_cutoff(_):
        need = kf - count_ge(thr + 1)

        def count_eq_upto(pos):
            def body(b, acc):
                row = pl.multiple_of(b * tq, tq)
                kpos = row + lax.broadcasted_iota(I32, (tq, tq), 0)
                hit = jnp.where(keys_ref[pl.ds(row, tq), :] == thr, jnp.where(kpos <= pos, 1.0, 0.0), 0.0)
                return acc + jnp.sum(hit.reshape(tq // acc_rows, acc_rows, tq), axis=0)
            acc = lax.fori_loop(0, i + 1, body, jnp.zeros((acc_rows, tq), F32))
            return jnp.sum(acc, axis=0, keepdims=True)

        def halve(_, bounds):
            lo, hi = bounds
            mid = (lo + hi) >> 1
            enough = count_eq_upto(mid) >= need
            return jnp.where(enough, lo, mid), jnp.where(enough, mid, hi)

        lo0 = jnp.full((1, tq), -1, I32)
        hi0 = jnp.full((1, tq), n_keys - 1, I32)
        _, hi = lax.fori_loop(0, max(1, (n_keys - 1).bit_length()), halve, (lo0, hi0))
        return jnp.where(cnt > kf, hi, n_keys)

    jcut = lax.cond(jnp.max(cnt) > kf, position_cutoff, lambda _: jnp.full((1, tq), n_keys, I32), 0)
    sel_row = lax.broadcasted_iota(I32, thr_ref.shape, 0)
    thr_ref[...] = jnp.where(sel_row == 0, thr, jnp.where(sel_row == 1, jcut, 0))


def _indexer(qi_t, row_off, k2, w_t, topk, tq=256):
    L = qi_t.shape[1]
    n_heads = w_t.shape[0]
    qw = n_heads * IDX_DIM
    assert row_off % qw == 0
    tq = _tile(L, tq)
    ck = k2.shape[1] // 2
    assert tq % ck == 0
    w_scale = (n_heads ** -0.5) * (IDX_DIM ** -0.5)
    return pl.pallas_call(
        functools.partial(_indexer_kernel, tq=tq, ck=ck, n_heads=n_heads, topk=topk, w_scale=w_scale),
        grid=(L // tq,),
        in_specs=[pl.BlockSpec((qw, tq), lambda i: (row_off // qw, i)),
                  pl.BlockSpec(k2.shape, lambda i: (0, 0, 0)),
                  pl.BlockSpec((n_heads, tq), lambda i: (0, i))],
        out_specs=[pl.BlockSpec((L, tq), lambda i: (0, i)),
                   pl.BlockSpec((8, tq), lambda i: (0, i))],
        out_shape=[jax.ShapeDtypeStruct((L, L), I32), jax.ShapeDtypeStruct((8, L), I32)],
        name="dsa_indexer", compiler_params=_cp("parallel"),
    )(qi_t, k2, w_t)


def _attn_kernel(qi_ref, kj_ref, far_ref, qt_ref, k_ref, vt_ref, keys_ref, thr_ref, bias_a_ref, bias_b_ref, o_ref,
                 m_ref, l_ref, acc_ref, mb_ref, *, heads, dh, scale2):
    step = pl.program_id(0)
    qi = qi_ref[step]
    kj = kj_ref[step]

    @pl.when(kj == 0)
    def _():
        m_ref[...] = jnp.full(m_ref.shape, -jnp.inf, F32)
        l_ref[...] = jnp.zeros(l_ref.shape, F32)
        acc_ref[...] = jnp.zeros(acc_ref.shape, F32)

    thr = thr_ref[0:1, :]
    kpos = kj * keys_ref.shape[0] + lax.broadcasted_iota(I32, keys_ref.shape, 0)
    bar = jnp.where(kpos <= thr_ref[1:2, :], thr, thr + 1)
    mb_ref[...] = jnp.where(keys_ref[...] >= bar, 0.0, NEG_BIG)

    def scores(h):
        hs = slice(h * dh, (h + 1) * dh)
        return jnp.dot(k_ref[:, hs], qt_ref[hs, :], preferred_element_type=F32)

    def fold(reduce_fn, a, rows=32):
        return reduce_fn(a.reshape(a.shape[0] // rows, rows, a.shape[1]), axis=0)

    def run(near):
        ahead = 4
        pending = [scores(h) for h in range(min(ahead, heads))]
        for h in range(heads):
            hs = slice(h * dh, (h + 1) * dh)
            x = pending.pop(0) * scale2 + mb_ref[...]
            if h + ahead < heads:
                pending.append(scores(h + ahead))
            if near:
                x = x + jnp.concatenate([bias_a_ref[h], bias_b_ref[h]], axis=0)
            far = 0.0 if near else far_ref[h]
            m_prev = m_ref[h]
            m_new = jnp.maximum(m_prev, jnp.max(fold(jnp.max, x), axis=0, keepdims=True) + far)
            alpha = jnp.exp2(m_prev - m_new)
            p = jnp.exp2(x - (m_new[0:1, :] - far))
            l_ref[h] = alpha * l_ref[h] + jnp.sum(fold(jnp.sum, p), axis=0, keepdims=True)
            m_ref[h] = m_new
            acc_ref[hs, :] = alpha[0:1, :] * acc_ref[hs, :] + jnp.dot(vt_ref[hs, :], p.astype(BF16),
                                                                      preferred_element_type=F32)

    @pl.when(qi - 2 * kj - 1 < 2)
    def _():
        run(True)

    @pl.when(qi - 2 * kj - 1 >= 2)
    def _():
        run(False)

    @pl.when(kj == qi // 2)
    def _():
        for h in range(heads):
            hs = slice(h * dh, (h + 1) * dh)
            o_ref[:, hs] = jnp.transpose(acc_ref[hs, :] / l_ref[h][0:1, :]).astype(o_ref.dtype)


def _attention(qv_t, q_blk, v_blk, k, keys, thr, bias_tiles, far_bias, tb):
    L = k.shape[0]
    heads = bias_tiles.shape[1]
    dh = ATT_HEAD_DIM
    aw = heads * dh
    nb = L // tb
    assert nb % 2 == 0
    pairs = [(i, j) for i in range(nb) for j in range(i // 2 + 1)]
    qi = jnp.asarray(np.array([p[0] for p in pairs], np.int32))
    kj = jnp.asarray(np.array([p[1] for p in pairs], np.int32))

    def tile_of(behind):
        return jnp.where(behind < 0, 3, jnp.minimum(behind, 2))

    bias_spec_a = pl.BlockSpec((None, heads, tb, tb), lambda s, qi, kj: (tile_of(qi[s] - 2 * kj[s]), 0, 0, 0))
    bias_spec_b = pl.BlockSpec((None, heads, tb, tb), lambda s, qi, kj: (tile_of(qi[s] - 2 * kj[s] - 1), 0, 0, 0))
    grid_spec = pltpu.PrefetchScalarGridSpec(
        num_scalar_prefetch=2, grid=(len(pairs),),
        in_specs=[pl.BlockSpec(memory_space=pltpu.SMEM),
                  pl.BlockSpec((aw, tb), lambda s, qi, kj: (q_blk, qi[s])),
                  pl.BlockSpec((2 * tb, aw), lambda s, qi, kj: (kj[s], 0)),
                  pl.BlockSpec((aw, 2 * tb), lambda s, qi, kj: (v_blk, kj[s])),
                  pl.BlockSpec((2 * tb, tb), lambda s, qi, kj: (kj[s], qi[s])),
                  pl.BlockSpec((8, tb), lambda s, qi, kj: (0, qi[s])),
                  bias_spec_a, bias_spec_b],
        out_specs=pl.BlockSpec((tb, aw), lambda s, qi, kj: (qi[s], 0)),
        scratch_shapes=[pltpu.VMEM((heads, 8, tb), F32),
                        pltpu.VMEM((heads, 8, tb), F32),
                        pltpu.VMEM((aw, tb), F32),
                        pltpu.VMEM((2 * tb, tb), F32)])
    return pl.pallas_call(
        functools.partial(_attn_kernel, heads=heads, dh=dh, scale2=dh ** -0.5 * LOG2E),
        grid_spec=grid_spec, out_shape=jax.ShapeDtypeStruct((L, aw), BF16), name="dsa_attention",
        compiler_params=_cp("arbitrary"),
    )(qi, kj, far_bias, qv_t, k, qv_t, keys, thr, bias_tiles, bias_tiles)


def _router_kernel(h_ref, w_ref, b_ref, g_ref, *, n_exp, top_k):
    logits = jnp.dot(h_ref[...], w_ref[...], preferred_element_type=F32) + b_ref[...]
    lane = lax.broadcasted_iota(I32, logits.shape, 1).astype(F32)
    work = jnp.where(lane < n_exp, logits, -jnp.inf)
    gates = jnp.zeros(logits.shape, F32)
    denom = None
    top = None
    for r in range(top_k):
        m = jnp.max(work, axis=1, keepdims=True)
        first = jnp.min(jnp.where(work == m, lane, float(LANES)), axis=1, keepdims=True)
        sel = lane == first
        if r == 0:
            top = m
        e = jnp.exp(m - top)
        gates = gates + jnp.where(sel, e, 0.0)
        denom = e if denom is None else denom + e
        work = jnp.where(sel, -jnp.inf, work)
    g_ref[...] = gates / denom


def _router(h, w_pad, b_pad, n_exp, tm=512):
    m, d = h.shape
    tm = _tile(m, tm)
    return pl.pallas_call(
        functools.partial(_router_kernel, n_exp=n_exp, top_k=TOP_K_EXPERTS), grid=(m // tm,),
        in_specs=[pl.BlockSpec((tm, d), lambda i: (i, 0)),
                  pl.BlockSpec((d, LANES), lambda i: (0, 0)),
                  pl.BlockSpec((1, LANES), lambda i: (0, 0))],
        out_specs=pl.BlockSpec((tm, LANES), lambda i: (i, 0)),
        out_shape=jax.ShapeDtypeStruct((m, LANES), F32), name="moe_router",
        compiler_params=_cp("parallel"),
    )(h, w_pad, b_pad)


def _moe_kernel(h_ref, g_ref, wgu_ref, bgu_ref, wdn_ref, bdn_ref, o_ref, *, ff, eb):
    step = pl.program_id(1)
    gates = g_ref[...]

    @pl.when(step == 0)
    def _():
        o_ref[...] = jnp.dot(gates.astype(BF16), bdn_ref[...], preferred_element_type=F32)

    lane = lax.broadcasted_iota(I32, gates.shape, 1)
    for k in range(eb):
        gu = jnp.dot(h_ref[...], wgu_ref[k], preferred_element_type=F32) + bgu_ref[k]
        gate = jnp.minimum(gu[:, :ff], SWIGLU_LIMIT)
        up = jnp.clip(gu[:, ff:], -SWIGLU_LIMIT, SWIGLU_LIMIT)
        act = (up + 1.0) * gate * jax.nn.sigmoid(SWIGLU_ALPHA * gate)
        ge = jnp.sum(jnp.where(lane == step * eb + k, gates, 0.0), axis=1, keepdims=True)
        o_ref[...] += jnp.dot((act * ge).astype(BF16), wdn_ref[k], preferred_element_type=F32)


def _moe_experts(h, gates, w_gu, b_gu, w_dn, b_dn_pad, tm=512, eb=2):
    m, d = h.shape
    n_exp, _, ff2 = w_gu.shape
    tm = _tile(m, tm)
    eb = _tile(n_exp, eb)
    return pl.pallas_call(
        functools.partial(_moe_kernel, ff=ff2 // 2, eb=eb), grid=(m // tm, n_exp // eb),
        in_specs=[pl.BlockSpec((tm, d), lambda i, e: (i, 0)),
                  pl.BlockSpec((tm, LANES), lambda i, e: (i, 0)),
                  pl.BlockSpec((eb, d, ff2), lambda i, e: (e, 0, 0)),
                  pl.BlockSpec((eb, 1, ff2), lambda i, e: (e, 0, 0)),
                  pl.BlockSpec((eb, ff2 // 2, d), lambda i, e: (e, 0, 0)),
                  pl.BlockSpec((LANES, d), lambda i, e: (0, 0))],
        out_specs=pl.BlockSpec((tm, d), lambda i, e: (i, 0)),
        out_shape=jax.ShapeDtypeStruct((m, d), F32), name="moe_experts",
        compiler_params=_cp("parallel", "arbitrary"),
    )(h, gates, w_gu, b_gu.reshape(n_exp, 1, ff2), w_dn, b_dn_pad)


def kernel(x, c, w_in, w_a_out, w_b_out, w_o, w_glu, b_glu, s5_lam_re, s5_lam_im, s5_b_re, s5_b_im, s5_c_re, s5_c_im, s5_d, s5_log_step, rel_bias, w_router, b_router, w_gu, b_gu, w_dn, b_dn, w_c1, b_c1, w_c2, b_c2, mod_table, ln_g, ln_b):
    bsz, L, d = x.shape
    assert bsz == 1 and c.shape[0] == 1
    depth = w_in.shape[0]
    g, p, h = s5_b_re.shape[1:]
    s5w = g * h
    heads = rel_bias.shape[1]
    aw = heads * ATT_HEAD_DIM
    iw = IDX_HEADS * IDX_DIM
    n_exp = w_router.shape[2]
    assert w_in.shape[2] == s5w + 3 * aw + iw + IDX_DIM + IDX_HEADS + 2 * d
    assert n_exp <= LANES and IDX_DIM + IDX_HEADS <= LANES and 2 * IDX_DIM == LANES
    o_q, o_k, o_v, o_qi = s5w, s5w + aw, s5w + 2 * aw, s5w + 3 * aw
    o_ki = s5w + 3 * aw + iw
    o_gate = o_ki + IDX_DIM + IDX_HEADS
    topk = min(TOPK_MAX, L // 4)
    tb = _tile(L, 256)
    ck = _tile(L, 128)
    dn_alpha = (2 * depth) ** 0.25

    mod = _conditioning(c, w_c1, b_c1, w_c2, b_c2, mod_table).reshape(depth, N_MOD, d)
    bias_tiles = _bias_tiles(rel_bias, tb)
    far_bias = rel_bias[-1] * LOG2E
    assert o_ki % LANES == 0 and (g * h) % LANES == 0 and LANES % h == 0

    xs = x.reshape(L, d)
    hcur = _modulate(xs, mod[0], sc_row=1, sh_row=0)
    for l in range(depth):
        w_l = w_in[l]
        w_nat = jnp.concatenate(
            [w_l[:, :s5w], w_l[:, o_k:o_k + aw], w_l[:, o_gate:],
             jnp.pad(w_l[:, o_ki:o_gate], ((0, 0), (0, LANES - IDX_DIM - IDX_HEADS)))], axis=1).astype(BF16)
        w_tr = jnp.transpose(jnp.concatenate(
            [w_l[:, o_q:o_q + aw], w_l[:, o_v:o_v + aw], w_l[:, o_qi:o_qi + iw]], axis=1)).astype(BF16)
        u = _matmul(hcur, w_nat, out_dtype=F32, col_off=0, n_cols=s5w, name="proj_u")
        k = _matmul(hcur, w_nat, out_dtype=BF16, col_off=s5w, n_cols=aw, name="proj_k")
        gates_ab = _matmul(hcur, w_nat, out_dtype=F32, col_off=s5w + aw, n_cols=2 * d, name="proj_gates")
        misc = _matmul(hcur, w_nat, out_dtype=F32, col_off=s5w + aw + 2 * d, n_cols=LANES, tn=LANES,
                       name="proj_misc")
        qvi_t = _matmul_t(hcur, w_tr, out_dtype=BF16, name="proj_qvi_t")

        prm = _s5_discretize(s5_lam_re[l], s5_lam_im[l], s5_b_re[l], s5_b_im[l], s5_c_re[l], s5_c_im[l],
                             s5_d[l], s5_log_step[l], S5_CHUNK)
        y_s5 = _s5_branch(u, prm)
        ya_in = _glu(y_s5, w_glu, b_glu[l], l)

        k_idx = misc[:, :IDX_DIM].astype(BF16)
        zeros = jnp.zeros_like(k_idx)
        k_even = jnp.concatenate([k_idx, zeros], axis=1).reshape(L // ck, ck, LANES)
        k_odd = jnp.concatenate([zeros, k_idx], axis=1).reshape(L // ck, ck, LANES)
        k2 = jnp.concatenate([k_even, k_odd], axis=1)
        w_t = jnp.transpose(misc[:, IDX_DIM:IDX_DIM + IDX_HEADS])
        keys, thr = _indexer(qvi_t, 2 * aw, k2, w_t, topk)
        attn = _attention(qvi_t, 0, 1, k, keys, thr, bias_tiles, far_bias, tb)

        merged = _merge(ya_in, attn, w_a_out, w_b_out, gates_ab, l)
        y = _matmul(merged, w_o, layer=l, out_dtype=F32, name="proj_out")
        xs, h2 = _ln_modulate(xs, y, mod[l], ln_g[l], ln_b[l], mod[l], alpha=dn_alpha, gate_row=2, ln_row=0,
                              sc_row=4, sh_row=3, want_h=True)

        w_r = jnp.pad(w_router[l], ((0, 0), (0, LANES - n_exp))).astype(BF16)
        b_r = jnp.pad(b_router[l], (0, LANES - n_exp)).reshape(1, LANES)
        gates = _router(h2, w_r, b_r, n_exp)
        b_dn_pad = jnp.pad(b_dn[l], ((0, LANES - n_exp), (0, 0))).astype(BF16)
        y = _moe_experts(h2, gates, w_gu[l].astype(BF16), b_gu[l], w_dn[l].astype(BF16), b_dn_pad)
        last = l == depth - 1
        xs, hcur = _ln_modulate(xs, y, mod[l], ln_g[l], ln_b[l], mod[min(l + 1, depth - 1)], alpha=dn_alpha,
                                gate_row=5, ln_row=1, sc_row=1, sh_row=0, want_h=not last)
    return xs.reshape(bsz, L, d)
```

```python
import functools
import math

import numpy as np
import jax
import jax.numpy as jnp
from jax import lax
from jax.experimental import pallas as pl
from jax.experimental.pallas import tpu as pltpu

F32 = jnp.float32
BF16 = jnp.bfloat16
I32 = jnp.int32

ATT_HEAD_DIM = 128
IDX_HEADS = 32
IDX_DIM = 64
TOPK_MAX = 256
REL_MAX_DIST = 128
TOP_K_EXPERTS = 4
SWIGLU_LIMIT = 7.0
SWIGLU_ALPHA = 1.702
N_MOD = 6
LN_EPS = 1e-5

LANES = 128
VMEM_LIMIT_BYTES = 56 * 1024 * 1024
S5_CHUNK = 16
NEG_BIG = -1e30
LOG2E = math.log2(math.e)
INT_MIN = -(2 ** 31)


def _cp(*sem):
    return pltpu.CompilerParams(dimension_semantics=sem, vmem_limit_bytes=VMEM_LIMIT_BYTES)


def _tile(n, t):
    t = min(n, t)
    while n % t:
        t //= 2
    return t


def _cond1_kernel(c_ref, w_ref, b_ref, o_ref):
    a = jax.nn.silu(c_ref[...]).astype(BF16)
    t = jnp.dot(a, w_ref[...].astype(BF16), preferred_element_type=F32) + b_ref[...]
    o_ref[...] = jax.nn.silu(t)


def _cond2_kernel(t_ref, w_ref, b_ref, tab_ref, o_ref):
    acc = jnp.dot(t_ref[...].astype(BF16), w_ref[...].astype(BF16), preferred_element_type=F32)
    o_ref[...] = acc[0:1, :] + b_ref[...] + tab_ref[...]


def _conditioning(c, w_c1, b_c1, w_c2, b_c2, mod_table):
    d = c.shape[1]
    r = w_c1.shape[1]
    depth = mod_table.shape[0]
    n = w_c2.shape[1]
    c8 = jnp.broadcast_to(c, (8, d))
    t = pl.pallas_call(
        _cond1_kernel, out_shape=jax.ShapeDtypeStruct((8, r), F32), name="cond1",
        compiler_params=_cp(),
    )(c8, w_c1, b_c1.reshape(1, r))
    tn = _tile(n, 2048)
    return pl.pallas_call(
        _cond2_kernel, grid=(n // tn,),
        in_specs=[pl.BlockSpec((8, r), lambda j: (0, 0)),
                  pl.BlockSpec((r, tn), lambda j: (0, j)),
                  pl.BlockSpec((1, tn), lambda j: (0, j)),
                  pl.BlockSpec((depth, tn), lambda j: (0, j))],
        out_specs=pl.BlockSpec((depth, tn), lambda j: (0, j)),
        out_shape=jax.ShapeDtypeStruct((depth, n), F32), name="cond2",
        compiler_params=_cp("parallel"),
    )(t, w_c2, b_c2.reshape(1, n), mod_table.reshape(depth, n))


def _modulate_kernel(x_ref, mod_ref, h_ref, *, sc_row, sh_row):
    sc = mod_ref[sc_row:sc_row + 1, :]
    sh = mod_ref[sh_row:sh_row + 1, :]
    h_ref[...] = (x_ref[...] * (1.0 + sc) + sh).astype(h_ref.dtype)


def _modulate(x, mod_l, sc_row, sh_row, tm=256):
    m, d = x.shape
    tm = _tile(m, tm)
    return pl.pallas_call(
        functools.partial(_modulate_kernel, sc_row=sc_row, sh_row=sh_row),
        grid=(m // tm,),
        in_specs=[pl.BlockSpec((tm, d), lambda i: (i, 0)),
                  pl.BlockSpec((N_MOD, d), lambda i: (0, 0))],
        out_specs=pl.BlockSpec((tm, d), lambda i: (i, 0)),
        out_shape=jax.ShapeDtypeStruct((m, d), BF16), name="modulate",
        compiler_params=_cp("parallel"),
    )(x, mod_l)


def _ln_kernel(x_ref, y_ref, mod_ref, g_ref, b_ref, nmod_ref, xo_ref, *h_refs,
               alpha, gate_row, ln_row, sc_row, sh_row):
    gate = mod_ref[gate_row:gate_row + 1, :]
    u = alpha * x_ref[...] + (1.0 + gate) * y_ref[...]
    mu = jnp.mean(u, axis=-1, keepdims=True)
    uc = u - mu
    var = jnp.mean(uc * uc, axis=-1, keepdims=True)
    xn = uc * lax.rsqrt(var + LN_EPS) * g_ref[ln_row:ln_row + 1, :] + b_ref[ln_row:ln_row + 1, :]
    xo_ref[...] = xn
    if h_refs:
        sc = nmod_ref[sc_row:sc_row + 1, :]
        sh = nmod_ref[sh_row:sh_row + 1, :]
        h_refs[0][...] = (xn * (1.0 + sc) + sh).astype(BF16)


def _ln_modulate(x, y, mod_l, ln_g_l, ln_b_l, next_mod, *, alpha, gate_row, ln_row, sc_row, sh_row,
                 want_h, tm=256):
    m, d = x.shape
    tm = _tile(m, tm)
    row = pl.BlockSpec((tm, d), lambda i: (i, 0))
    out_shape = [jax.ShapeDtypeStruct((m, d), F32)]
    out_specs = [row]
    if want_h:
        out_shape.append(jax.ShapeDtypeStruct((m, d), BF16))
        out_specs.append(row)
    res = pl.pallas_call(
        functools.partial(_ln_kernel, alpha=alpha, gate_row=gate_row, ln_row=ln_row,
                          sc_row=sc_row, sh_row=sh_row),
        grid=(m // tm,),
        in_specs=[row, row,
                  pl.BlockSpec((N_MOD, d), lambda i: (0, 0)),
                  pl.BlockSpec((2, d), lambda i: (0, 0)),
                  pl.BlockSpec((2, d), lambda i: (0, 0)),
                  pl.BlockSpec((N_MOD, d), lambda i: (0, 0))],
        out_specs=out_specs, out_shape=out_shape, name="ln_modulate",
        compiler_params=_cp("parallel"),
    )(x, y, mod_l, ln_g_l, ln_b_l, next_mod)
    return (res[0], res[1]) if want_h else (res[0], None)


def _mm_kernel(a_ref, w_ref, o_ref, wbf_ref):
    @pl.when(pl.program_id(1) == 0)
    def _():
        wbf_ref[...] = w_ref[...].astype(BF16)

    o_ref[...] = jnp.dot(a_ref[...], wbf_ref[...], preferred_element_type=F32).astype(o_ref.dtype)


def _w_spec(w, layer, k, tn, off=0):
    if w.ndim == 3:
        return pl.BlockSpec((None, k, tn), lambda j, i: (layer, 0, j + off))
    return pl.BlockSpec((k, tn), lambda j, i: (0, j + off))


def _matmul(a, w, *, out_dtype, layer=0, col_off=0, n_cols=None, tm=1024, tn=512, name="matmul"):
    m, k = a.shape
    n_cols = w.shape[-1] - col_off if n_cols is None else n_cols
    tm = _tile(m, tm)
    tn = math.gcd(_tile(n_cols, tn), col_off) if col_off else _tile(n_cols, tn)
    off = col_off // tn
    return pl.pallas_call(
        _mm_kernel, grid=(n_cols // tn, m // tm),
        in_specs=[pl.BlockSpec((tm, k), lambda j, i: (i, 0)),
                  _w_spec(w, layer, k, tn, off)],
        out_specs=pl.BlockSpec((tm, tn), lambda j, i: (i, j)),
        out_shape=jax.ShapeDtypeStruct((m, n_cols), out_dtype),
        scratch_shapes=[pltpu.VMEM((k, tn), BF16)], name=name,
        compiler_params=_cp("parallel", "arbitrary"),
    )(a, w)


def _mm_t_kernel(a_ref, wt_ref, o_ref):
    o_ref[...] = lax.dot_general(wt_ref[...], a_ref[...], (((1,), (1,)), ((), ())),
                                 preferred_element_type=F32).astype(o_ref.dtype)


def _matmul_t(a, w_t, *, out_dtype, tm=1024, tn=1024, name="matmul_t"):
    m, k = a.shape
    n = w_t.shape[0]
    tm = _tile(m, tm)
    tn = _tile(n, tn)
    return pl.pallas_call(
        _mm_t_kernel, grid=(n // tn, m // tm),
        in_specs=[pl.BlockSpec((tm, k), lambda j, i: (i, 0)),
                  pl.BlockSpec((tn, k), lambda j, i: (j, 0))],
        out_specs=pl.BlockSpec((tn, tm), lambda j, i: (j, i)),
        out_shape=jax.ShapeDtypeStruct((n, m), out_dtype), name=name,
        compiler_params=_cp("parallel", "arbitrary"),
    )(a, w_t)


def _glu_kernel(y_ref, yt_ref, w_ref, b_ref, o_ref, wbf_ref):
    @pl.when(pl.program_id(1) == 0)
    def _():
        wbf_ref[...] = w_ref[...].astype(BF16)

    z = jnp.dot(y_ref[...].astype(BF16), wbf_ref[...], preferred_element_type=F32) + b_ref[...]
    o_ref[...] = (yt_ref[...] * jax.nn.sigmoid(z)).astype(o_ref.dtype)


def _glu(y, w, b, layer, tm=1024, tn=512):
    m, k = y.shape
    n = w.shape[-1]
    tm = _tile(m, tm)
    tn = _tile(n, tn)
    return pl.pallas_call(
        _glu_kernel, grid=(n // tn, m // tm),
        in_specs=[pl.BlockSpec((tm, k), lambda j, i: (i, 0)),
                  pl.BlockSpec((tm, tn), lambda j, i: (i, j)),
                  _w_spec(w, layer, k, tn),
                  pl.BlockSpec((1, tn), lambda j, i: (0, j))],
        out_specs=pl.BlockSpec((tm, tn), lambda j, i: (i, j)),
        out_shape=jax.ShapeDtypeStruct((m, n), BF16),
        scratch_shapes=[pltpu.VMEM((k, tn), BF16)], name="s5_glu",
        compiler_params=_cp("parallel", "arbitrary"),
    )(y, y, w, b.reshape(1, n))


def _merge_kernel(a1_ref, a2_ref, w1_ref, w2_ref, ga_ref, gb_ref, o_ref, w1bf_ref, w2bf_ref):
    @pl.when(pl.program_id(1) == 0)
    def _():
        w1bf_ref[...] = w1_ref[...].astype(BF16)
        w2bf_ref[...] = w2_ref[...].astype(BF16)

    ya = jnp.dot(a1_ref[...], w1bf_ref[...], preferred_element_type=F32)
    yb = jnp.dot(a2_ref[...], w2bf_ref[...], preferred_element_type=F32)
    o_ref[...] = (jax.nn.sigmoid(ga_ref[...]) * ya + jax.nn.sigmoid(gb_ref[...]) * yb).astype(o_ref.dtype)


def _merge(a1, a2, w1, w2, gates, layer, tm=1024, tn=512):
    m, k1 = a1.shape
    k2 = a2.shape[1]
    n = w1.shape[-1]
    tm = _tile(m, tm)
    tn = _tile(n, tn)
    nb = n // tn
    return pl.pallas_call(
        _merge_kernel, grid=(nb, m // tm),
        in_specs=[pl.BlockSpec((tm, k1), lambda j, i: (i, 0)),
                  pl.BlockSpec((tm, k2), lambda j, i: (i, 0)),
                  _w_spec(w1, layer, k1, tn),
                  _w_spec(w2, layer, k2, tn),
                  pl.BlockSpec((tm, tn), lambda j, i: (i, j)),
                  pl.BlockSpec((tm, tn), lambda j, i: (i, j + nb))],
        out_specs=pl.BlockSpec((tm, tn), lambda j, i: (i, j)),
        out_shape=jax.ShapeDtypeStruct((m, n), BF16),
        scratch_shapes=[pltpu.VMEM((k1, tn), BF16), pltpu.VMEM((k2, tn), BF16)], name="branch_merge",
        compiler_params=_cp("parallel", "arbitrary"),
    )(a1, a2, w1, w2, gates, gates)


def _s5_discretize(lam_re, lam_im, b_re, b_im, c_re, c_im, d, log_step, t_chunk):
    hp = lax.Precision.HIGHEST
    g, p, h = b_re.shape
    na = LANES // h
    nx = g // na
    assert 2 * p == LANES
    lam = lax.complex(jnp.minimum(lam_re, -1e-4), lam_im)
    dt = jnp.exp(log_step)[:, None]
    lam_bar = jnp.exp(lam * dt)
    b_bar = ((lam_bar - 1.0) / lam)[:, :, None] * lax.complex(b_re, b_im)
    c = lax.complex(c_re, c_im)
    tau = jnp.arange(t_chunk + 1, dtype=F32)
    pw = jnp.exp((lam * dt)[:, None, :] * tau[None, :, None].astype(jnp.complex64))
    cp = c[:, None, :, :] * pw[:, :, None, :]
    kern = jnp.einsum('gtop,gpi->gtoi', cp[:, :t_chunk], b_bar, precision=hp).real
    kio = jnp.transpose(kern.reshape(nx, na, t_chunk, h, h), (0, 2, 1, 4, 3))
    kt = jnp.tile(kio.reshape(nx, t_chunk, LANES, h), (1, 1, 1, na))
    wcx = pw[:, t_chunk - 1 - np.arange(t_chunk)][:, :, :, None] * b_bar[:, None, :, :]
    wcx = jnp.transpose(wcx, (0, 1, 3, 2))
    w_ri = jnp.concatenate([wcx.real, wcx.imag], axis=-1).reshape(nx, na, t_chunk, h, 2 * p)
    wc = jnp.transpose(w_ri, (0, 2, 1, 3, 4)).reshape(nx, t_chunk, LANES, 2 * p)
    cl = jnp.transpose(cp[:, 1:t_chunk + 1], (0, 3, 1, 2))
    vc = jnp.concatenate([cl.real, -cl.imag], axis=1).reshape(nx, na, 2 * p, t_chunk * h)
    dd = jnp.broadcast_to(d.reshape(nx, 1, 1, na, h), (nx, 1, t_chunk, na, h)).reshape(nx, 1, t_chunk * LANES)
    a = pw[:, t_chunk]
    a1 = jnp.concatenate([a.real, a.real], axis=-1).reshape(1, g * 2 * p)
    a2 = jnp.concatenate([-a.imag, a.imag], axis=-1).reshape(1, g * 2 * p)
    return kt, wc, vc, dd, a1, a2


def _chunk_rows(u_ref, t, nc):
    return jnp.concatenate([u_ref[pl.ds(s, nc, stride=t), :] for s in range(t)], axis=1)


def _s5_state_in_kernel(u_ref, wc_ref, xa_ref, xb_ref, wa_ref, wb_ref, *, t, nc, h):
    na = LANES // h
    row_group = lax.broadcasted_iota(I32, (LANES, LANES), 0) // h
    for s in range(t):
        blk = wc_ref[s]
        swapped = pltpu.roll(blk, LANES // 2, axis=1)
        for b in range(na):
            own = row_group == b
            dst = (slice(s * LANES, (s + 1) * LANES), slice(b * LANES, (b + 1) * LANES))
            wa_ref[dst] = jnp.where(own, blk, 0.0).astype(BF16)
            wb_ref[dst] = jnp.where(own, swapped, 0.0).astype(BF16)
    ucat = _chunk_rows(u_ref, t, nc).astype(BF16)
    xa_ref[...] = jnp.dot(ucat, wa_ref[...], preferred_element_type=F32)
    xb_ref[...] = jnp.dot(ucat, wb_ref[...], preferred_element_type=F32)


def _s5_scan_kernel(xa_ref, xb_ref, a1_ref, a2_ref, s_ref, *, unroll):
    a1 = a1_ref[...]
    a2 = a2_ref[...]
    nc = xa_ref.shape[0]

    def body(cb, carry):
        s_a, s_b = carry
        base = pl.multiple_of(cb * unroll, unroll)
        xa = xa_ref[pl.ds(base, unroll), :]
        xb = xb_ref[pl.ds(base, unroll), :]
        rows = []
        for k in range(unroll):
            rows.append(s_a)
            s_a, s_b = (a1 * s_a + a2 * s_b + xa[k:k + 1, :],
                        a1 * s_b - a2 * s_a + xb[k:k + 1, :])
        s_ref[pl.ds(base, unroll), :] = jnp.concatenate(rows, axis=0)
        return s_a, s_b

    z = jnp.zeros(a1.shape, F32)
    lax.fori_loop(0, nc // unroll, body, (z, z))


def _s5_out_kernel(u_ref, sp_ref, kt_ref, vc_ref, dd_ref, y_ref, bd_ref, vm_ref, *, t, nc, h, n_parts):
    part = pl.program_id(1)
    tp = t // n_parts
    na = LANES // h
    w = tp * LANES
    same_group = (lax.broadcasted_iota(I32, (LANES, LANES), 0) // h
                  == lax.broadcasted_iota(I32, (LANES, LANES), 1) // h)
    row = lax.broadcasted_iota(I32, (LANES, w), 0)
    col = lax.broadcasted_iota(I32, (LANES, w), 1)
    landing_group = jnp.where((row // h == col // LANES) & (row % h == col % h), (col // h) % na, -1)
    ucat = _chunk_rows(u_ref, t, nc)
    for k in range(n_parts):
        @pl.when(part == k)
        def _(k=k):
            live = (k + 1) * tp
            for s in range(live):
                for tt in range(tp):
                    lag = k * tp + tt - s
                    dst = (slice(s * LANES, (s + 1) * LANES), slice(tt * LANES, (tt + 1) * LANES))
                    if lag >= 0:
                        bd_ref[dst] = jnp.where(same_group, kt_ref[lag], 0.0).astype(BF16)
                    else:
                        bd_ref[dst] = jnp.zeros((LANES, LANES), BF16)
            for a in range(na):
                place = jnp.where(landing_group == a, 1.0, 0.0).astype(BF16)
                src = vc_ref[a][:, k * tp * h:(k + 1) * tp * h].astype(BF16)
                vm_ref[a * LANES:(a + 1) * LANES, :] = jnp.dot(src, place, preferred_element_type=F32).astype(BF16)
            y = jnp.dot(ucat[:, :live * LANES].astype(BF16), bd_ref[:live * LANES, :], preferred_element_type=F32)
            y = y + jnp.dot(sp_ref[...].astype(BF16), vm_ref[...], preferred_element_type=F32)
            y = jax.nn.gelu(y + dd_ref[...] * ucat[:, k * w:(k + 1) * w])
            for s in range(tp):
                y_ref[pl.ds(k * tp + s, nc, stride=t), :] = y[:, s * LANES:(s + 1) * LANES]


def _s5_branch(u, prm):
    kt, wc, vc, dd, a1, a2 = prm
    L, width = u.shape
    nx, na, _, th = vc.shape
    t = kt.shape[1]
    h = th // t
    tah = t * LANES
    sw = na * LANES
    nc = L // t
    u_spec = pl.BlockSpec((L, LANES), lambda x: (0, x))
    st_spec = pl.BlockSpec((nc, sw), lambda x: (0, x))
    xa, xb = pl.pallas_call(
        functools.partial(_s5_state_in_kernel, t=t, nc=nc, h=h), grid=(nx,),
        in_specs=[u_spec,
                  pl.BlockSpec((None, t, LANES, LANES), lambda x: (x, 0, 0, 0))],
        out_specs=[st_spec, st_spec],
        out_shape=[jax.ShapeDtypeStruct((nc, nx * sw), F32)] * 2,
        scratch_shapes=[pltpu.VMEM((tah, sw), BF16), pltpu.VMEM((tah, sw), BF16)], name="s5_state_in",
        compiler_params=_cp("parallel"),
    )(u, wc)
    ws = _tile(nx * sw, 2048)
    unroll = _tile(nc, 8)
    sc_spec = pl.BlockSpec((nc, ws), lambda x: (0, x))
    a_spec = pl.BlockSpec((1, ws), lambda x: (0, x))
    sp = pl.pallas_call(
        functools.partial(_s5_scan_kernel, unroll=unroll), grid=(nx * sw // ws,),
        in_specs=[sc_spec, sc_spec, a_spec, a_spec],
        out_specs=sc_spec,
        out_shape=jax.ShapeDtypeStruct((nc, nx * sw), F32), name="s5_scan",
        compiler_params=_cp("parallel"),
    )(xa, xb, a1, a2)
    n_parts = 2
    tw = tah // n_parts
    return pl.pallas_call(
        functools.partial(_s5_out_kernel, t=t, nc=nc, h=h, n_parts=n_parts), grid=(nx, n_parts),
        in_specs=[pl.BlockSpec((L, LANES), lambda x, k: (0, x)),
                  pl.BlockSpec((nc, sw), lambda x, k: (0, x)),
                  pl.BlockSpec((None, t, LANES, LANES), lambda x, k: (x, 0, 0, 0)),
                  pl.BlockSpec((None, na, LANES, th), lambda x, k: (x, 0, 0, 0)),
                  pl.BlockSpec((None, 1, tw), lambda x, k: (x, 0, k))],
        out_specs=pl.BlockSpec((L, LANES), lambda x, k: (0, x)),
        out_shape=jax.ShapeDtypeStruct((L, width), F32),
        scratch_shapes=[pltpu.VMEM((tah, tw), BF16), pltpu.VMEM((sw, tw), BF16)], name="s5_out",
        compiler_params=_cp("parallel", "arbitrary"),
    )(u, sp, kt, vc, dd)


def _t5_bucket_starts(n_buckets):
    exact = n_buckets // 2
    dist = np.arange(0, 4 * REL_MAX_DIST).astype(np.int64)
    far = exact + (np.log(np.maximum(dist, 1).astype(np.float32) / np.float32(exact))
                   / np.float32(math.log(REL_MAX_DIST / exact)) * np.float32(n_buckets - exact)).astype(np.int64)
    bucket = np.where(dist < exact, dist, np.minimum(far, n_buckets - 1))
    return [int(np.argmax(bucket >= b)) for b in range(1, n_buckets)]


def _bias_tile_kernel(rb_ref, o_ref, *, tb, starts):
    delta = jnp.where(pl.program_id(0) == 3, -1, pl.program_id(0))
    head = pl.program_id(1)
    key = lax.broadcasted_iota(I32, (tb, tb), 0)
    qry = lax.broadcasted_iota(I32, (tb, tb), 1)
    dist = delta * tb + qry - key
    bias = jnp.full((tb, tb), rb_ref[0, head], F32)
    for b, start in enumerate(starts, start=1):
        bias = jnp.where(dist >= start, rb_ref[b, head], bias)
    o_ref[...] = jnp.where(dist >= 0, bias * LOG2E, NEG_BIG)


def _bias_tiles(rel_bias, tb):
    n_buckets, heads = rel_bias.shape
    starts = _t5_bucket_starts(n_buckets)
    assert 2 * tb - (tb - 1) >= starts[-1]
    return pl.pallas_call(
        functools.partial(_bias_tile_kernel, tb=tb, starts=starts), grid=(4, heads),
        in_specs=[pl.BlockSpec(memory_space=pltpu.SMEM)],
        out_specs=pl.BlockSpec((None, None, tb, tb), lambda dl, hd: (dl, hd, 0, 0)),
        out_shape=jax.ShapeDtypeStruct((4, heads, tb, tb), F32), name="rel_bias_tiles",
        compiler_params=_cp("parallel", "parallel"),
    )(rel_bias)


def _indexer_kernel(qt_ref, k2_ref, wt_ref, keys_ref, thr_ref, *, tq, ck, n_heads, topk, w_scale):
    i = pl.program_id(0)
    n_blocks = keys_ref.shape[0] // tq
    n_live = (i + 1) * (tq // ck)
    wv = wt_ref[...] * w_scale

    def score_chunk(c, carry):
        row = pl.multiple_of(c * ck, ck)
        kk = k2_ref[c]
        acc = jnp.zeros((ck, tq), F32)
        for pr in range(n_heads // 2):
            s2 = jnp.dot(kk, qt_ref[pr * LANES:(pr + 1) * LANES, :], preferred_element_type=F32)
            acc = acc + jnp.maximum(s2[:ck], 0.0) * wv[2 * pr:2 * pr + 1, :]
            acc = acc + jnp.maximum(s2[ck:], 0.0) * wv[2 * pr + 1:2 * pr + 2, :]
        kpos = row + lax.broadcasted_iota(I32, (ck, tq), 0)
        qpos = i * tq + lax.broadcasted_iota(I32, (ck, tq), 1)
        acc = jnp.where(acc == 0.0, 0.0, acc)
        score = jnp.where(kpos <= qpos, acc, -jnp.inf)
        bits = pltpu.bitcast(score, I32)
        keys_ref[pl.ds(row, ck), :] = jnp.where(bits < 0, bits ^ 0x7FFFFFFF, bits)
        return carry

    lax.fori_loop(0, n_live, score_chunk, 0)

    def fill_block(b, carry):
        row = pl.multiple_of(b * tq, tq)
        keys_ref[pl.ds(row, tq), :] = jnp.full((tq, tq), INT_MIN, I32)
        return carry

    lax.fori_loop(i + 1, n_blocks, fill_block, 0)

    acc_rows = 32

    def count_ge(cand):
        def body(b, cnt):
            row = pl.multiple_of(b * tq, tq)
            hit = jnp.where(keys_ref[pl.ds(row, tq), :] >= cand, 1.0, 0.0)
            return cnt + jnp.sum(hit.reshape(tq // acc_rows, acc_rows, tq), axis=0)
        cnt = lax.fori_loop(0, i + 1, body, jnp.zeros((acc_rows, tq), F32))
        return jnp.sum(cnt, axis=0, keepdims=True)

    kf = float(topk)
    zero = jnp.zeros((1, tq), I32)
    c_zero = count_ge(zero)
    thr0 = jnp.where(c_zero >= kf, zero, jnp.full((1, tq), INT_MIN, I32))
    cnt0 = jnp.where(c_zero >= kf, c_zero, ((i + 1) * tq).astype(F32))

    def undecided(st):
        it, _, cnt = st
        return (it < 31) & (jnp.max(cnt) > kf)

    def bit_step(st):
        it, thr, cnt = st
        cand = thr | jnp.left_shift(jnp.int32(1), 30 - it)
        c = count_ge(cand)
        keep = c >= kf
        return it + 1, jnp.where(keep, cand, thr), jnp.where(keep, c, cnt)

    _, thr, cnt = lax.while_loop(undecided, bit_step, (jnp.int32(0), thr0, cnt0))

    n_keys = keys_ref.shape[0]

    def position_cutoff(_):
        need = kf - count_ge(thr + 1)

        def count_eq_upto(pos):
            def body(b, acc):
                row = pl.multiple_of(b * tq, tq)
                kpos = row + lax.broadcasted_iota(I32, (tq, tq), 0)
                hit = jnp.where(keys_ref[pl.ds(row, tq), :] == thr, jnp.where(kpos <= pos, 1.0, 0.0), 0.0)
                return acc + jnp.sum(hit.reshape(tq // acc_rows, acc_rows, tq), axis=0)
            acc = lax.fori_loop(0, i + 1, body, jnp.zeros((acc_rows, tq), F32))
            return jnp.sum(acc, axis=0, keepdims=True)

        def halve(_, bounds):
            lo, hi = bounds
            mid = (lo + hi) >> 1
            enough = count_eq_upto(mid) >= need
            return jnp.where(enough, lo, mid), jnp.where(enough, mid, hi)

        lo0 = jnp.full((1, tq), -1, I32)
        hi0 = jnp.full((1, tq), n_keys - 1, I32)
        _, hi = lax.fori_loop(0, max(1, (n_keys - 1).bit_length()), halve, (lo0, hi0))
        return jnp.where(cnt > kf, hi, n_keys)

    jcut = lax.cond(jnp.max(cnt) > kf, position_cutoff, lambda _: jnp.full((1, tq), n_keys, I32), 0)
    sel_row = lax.broadcasted_iota(I32, thr_ref.shape, 0)
    thr_ref[...] = jnp.where(sel_row == 0, thr, jnp.where(sel_row == 1, jcut, 0))


def _indexer(qi_t, row_off, k2, w_t, topk, tq=256):
    L = qi_t.shape[1]
    n_heads = w_t.shape[0]
    qw = n_heads * IDX_DIM
    assert row_off % qw == 0
    tq = _tile(L, tq)
    ck = k2.shape[1] // 2
    assert tq % ck == 0
    w_scale = (n_heads ** -0.5) * (IDX_DIM ** -0.5)
    return pl.pallas_call(
        functools.partial(_indexer_kernel, tq=tq, ck=ck, n_heads=n_heads, topk=topk, w_scale=w_scale),
        grid=(L // tq,),
        in_specs=[pl.BlockSpec((qw, tq), lambda i: (row_off // qw, i)),
                  pl.BlockSpec(k2.shape, lambda i: (0, 0, 0)),
                  pl.BlockSpec((n_heads, tq), lambda i: (0, i))],
        out_specs=[pl.BlockSpec((L, tq), lambda i: (0, i)),
                   pl.BlockSpec((8, tq), lambda i: (0, i))],
        out_shape=[jax.ShapeDtypeStruct((L, L), I32), jax.ShapeDtypeStruct((8, L), I32)],
        name="dsa_indexer", compiler_params=_cp("parallel"),
    )(qi_t, k2, w_t)


def _attn_kernel(qi_ref, kj_ref, far_ref, qt_ref, k_ref, vt_ref, keys_ref, thr_ref, bias_a_ref, bias_b_ref, o_ref,
                 m_ref, l_ref, acc_ref, mb_ref, *, heads, dh, scale2):
    step = pl.program_id(0)
    qi = qi_ref[step]
    kj = kj_ref[step]

    @pl.when(kj == 0)
    def _():
        m_ref[...] = jnp.full(m_ref.shape, -jnp.inf, F32)
        l_ref[...] = jnp.zeros(l_ref.shape, F32)
        acc_ref[...] = jnp.zeros(acc_ref.shape, F32)

    thr = thr_ref[0:1, :]
    kpos = kj * keys_ref.shape[0] + lax.broadcasted_iota(I32, keys_ref.shape, 0)
    bar = jnp.where(kpos <= thr_ref[1:2, :], thr, thr + 1)
    mb_ref[...] = jnp.where(keys_ref[...] >= bar, 0.0, NEG_BIG)

    def scores(h):
        hs = slice(h * dh, (h + 1) * dh)
        return jnp.dot(k_ref[:, hs], qt_ref[hs, :], preferred_element_type=F32)

    def fold(reduce_fn, a, rows=32):
        return reduce_fn(a.reshape(a.shape[0] // rows, rows, a.shape[1]), axis=0)

    def run(near):
        ahead = 4
        pending = [scores(h) for h in range(min(ahead, heads))]
        for h in range(heads):
            hs = slice(h * dh, (h + 1) * dh)
            x = pending.pop(0) * scale2 + mb_ref[...]
            if h + ahead < heads:
                pending.append(scores(h + ahead))
            if near:
                x = x + jnp.concatenate([bias_a_ref[h], bias_b_ref[h]], axis=0)
            far = 0.0 if near else far_ref[h]
            m_prev = m_ref[h]
            m_new = jnp.maximum(m_prev, jnp.max(fold(jnp.max, x), axis=0, keepdims=True) + far)
            alpha = jnp.exp2(m_prev - m_new)
            p = jnp.exp2(x - (m_new[0:1, :] - far))
            l_ref[h] = alpha * l_ref[h] + jnp.sum(fold(jnp.sum, p), axis=0, keepdims=True)
            m_ref[h] = m_new
            acc_ref[hs, :] = alpha[0:1, :] * acc_ref[hs, :] + jnp.dot(vt_ref[hs, :], p.astype(BF16),
                                                                      preferred_element_type=F32)

    @pl.when(qi - 2 * kj - 1 < 2)
    def _():
        run(True)

    @pl.when(qi - 2 * kj - 1 >= 2)
    def _():
        run(False)

    @pl.when(kj == qi // 2)
    def _():
        for h in range(heads):
            hs = slice(h * dh, (h + 1) * dh)
            o_ref[:, hs] = jnp.transpose(acc_ref[hs, :] / l_ref[h][0:1, :]).astype(o_ref.dtype)


def _attention(qv_t, q_blk, v_blk, k, keys, thr, bias_tiles, far_bias, tb):
    L = k.shape[0]
    heads = bias_tiles.shape[1]
    dh = ATT_HEAD_DIM
    aw = heads * dh
    nb = L // tb
    assert nb % 2 == 0
    pairs = [(i, j) for i in range(nb) for j in range(i // 2 + 1)]
    qi = jnp.asarray(np.array([p[0] for p in pairs], np.int32))
    kj = jnp.asarray(np.array([p[1] for p in pairs], np.int32))

    def tile_of(behind):
        return jnp.where(behind < 0, 3, jnp.minimum(behind, 2))

    bias_spec_a = pl.BlockSpec((None, heads, tb, tb), lambda s, qi, kj: (tile_of(qi[s] - 2 * kj[s]), 0, 0, 0))
    bias_spec_b = pl.BlockSpec((None, heads, tb, tb), lambda s, qi, kj: (tile_of(qi[s] - 2 * kj[s] - 1), 0, 0, 0))
    grid_spec = pltpu.PrefetchScalarGridSpec(
        num_scalar_prefetch=2, grid=(len(pairs),),
        in_specs=[pl.BlockSpec(memory_space=pltpu.SMEM),
                  pl.BlockSpec((aw, tb), lambda s, qi, kj: (q_blk, qi[s])),
                  pl.BlockSpec((2 * tb, aw), lambda s, qi, kj: (kj[s], 0)),
                  pl.BlockSpec((aw, 2 * tb), lambda s, qi, kj: (v_blk, kj[s])),
                  pl.BlockSpec((2 * tb, tb), lambda s, qi, kj: (kj[s], qi[s])),
                  pl.BlockSpec((8, tb), lambda s, qi, kj: (0, qi[s])),
                  bias_spec_a, bias_spec_b],
        out_specs=pl.BlockSpec((tb, aw), lambda s, qi, kj: (qi[s], 0)),
        scratch_shapes=[pltpu.VMEM((heads, 8, tb), F32),
                        pltpu.VMEM((heads, 8, tb), F32),
                        pltpu.VMEM((aw, tb), F32),
                        pltpu.VMEM((2 * tb, tb), F32)])
    return pl.pallas_call(
        functools.partial(_attn_kernel, heads=heads, dh=dh, scale2=dh ** -0.5 * LOG2E),
        grid_spec=grid_spec, out_shape=jax.ShapeDtypeStruct((L, aw), BF16), name="dsa_attention",
        compiler_params=_cp("arbitrary"),
    )(qi, kj, far_bias, qv_t, k, qv_t, keys, thr, bias_tiles, bias_tiles)


def _router_kernel(h_ref, w_ref, b_ref, g_ref, *, n_exp, top_k):
    logits = jnp.dot(h_ref[...], w_ref[...], preferred_element_type=F32) + b_ref[...]
    lane = lax.broadcasted_iota(I32, logits.shape, 1).astype(F32)
    work = jnp.where(lane < n_exp, logits, -jnp.inf)
    gates = jnp.zeros(logits.shape, F32)
    denom = None
    top = None
    for r in range(top_k):
        m = jnp.max(work, axis=1, keepdims=True)
        first = jnp.min(jnp.where(work == m, lane, float(LANES)), axis=1, keepdims=True)
        sel = lane == first
        if r == 0:
            top = m
        e = jnp.exp(m - top)
        gates = gates + jnp.where(sel, e, 0.0)
        denom = e if denom is None else denom + e
        work = jnp.where(sel, -jnp.inf, work)
    g_ref[...] = gates / denom


def _router(h, w_pad, b_pad, n_exp, tm=512):
    m, d = h.shape
    tm = _tile(m, tm)
    return pl.pallas_call(
        functools.partial(_router_kernel, n_exp=n_exp, top_k=TOP_K_EXPERTS), grid=(m // tm,),
        in_specs=[pl.BlockSpec((tm, d), lambda i: (i, 0)),
                  pl.BlockSpec((d, LANES), lambda i: (0, 0)),
                  pl.BlockSpec((1, LANES), lambda i: (0, 0))],
        out_specs=pl.BlockSpec((tm, LANES), lambda i: (i, 0)),
        out_shape=jax.ShapeDtypeStruct((m, LANES), F32), name="moe_router",
        compiler_params=_cp("parallel"),
    )(h, w_pad, b_pad)


def _moe_kernel(h_ref, g_ref, wgu_ref, bgu_ref, wdn_ref, bdn_ref, o_ref, *, ff, eb):
    step = pl.program_id(1)
    gates = g_ref[...]

    @pl.when(step == 0)
    def _():
        o_ref[...] = jnp.dot(gates.astype(BF16), bdn_ref[...], preferred_element_type=F32)

    lane = lax.broadcasted_iota(I32, gates.shape, 1)
    for k in range(eb):
        gu = jnp.dot(h_ref[...], wgu_ref[k], preferred_element_type=F32) + bgu_ref[k]
        gate = jnp.minimum(gu[:, :ff], SWIGLU_LIMIT)
        up = jnp.clip(gu[:, ff:], -SWIGLU_LIMIT, SWIGLU_LIMIT)
        act = (up + 1.0) * gate * jax.nn.sigmoid(SWIGLU_ALPHA * gate)
        ge = jnp.sum(jnp.where(lane == step * eb + k, gates, 0.0), axis=1, keepdims=True)
        o_ref[...] += jnp.dot((act * ge).astype(BF16), wdn_ref[k], preferred_element_type=F32)


def _moe_experts(h, gates, w_gu, b_gu, w_dn, b_dn_pad, tm=512, eb=2):
    m, d = h.shape
    n_exp, _, ff2 = w_gu.shape
    tm = _tile(m, tm)
    eb = _tile(n_exp, eb)
    return pl.pallas_call(
        functools.partial(_moe_kernel, ff=ff2 // 2, eb=eb), grid=(m // tm, n_exp // eb),
        in_specs=[pl.BlockSpec((tm, d), lambda i, e: (i, 0)),
                  pl.BlockSpec((tm, LANES), lambda i, e: (i, 0)),
                  pl.BlockSpec((eb, d, ff2), lambda i, e: (e, 0, 0)),
                  pl.BlockSpec((eb, 1, ff2), lambda i, e: (e, 0, 0)),
                  pl.BlockSpec((eb, ff2 // 2, d), lambda i, e: (e, 0, 0)),
                  pl.BlockSpec((LANES, d), lambda i, e: (0, 0))],
        out_specs=pl.BlockSpec((tm, d), lambda i, e: (i, 0)),
        out_shape=jax.ShapeDtypeStruct((m, d), F32), name="moe_experts",
        compiler_params=_cp("parallel", "arbitrary"),
    )(h, gates, w_gu, b_gu.reshape(n_exp, 1, ff2), w_dn, b_dn_pad)


def kernel(x, c, w_in, w_a_out, w_b_out, w_o, w_glu, b_glu, s5_lam_re, s5_lam_im, s5_b_re, s5_b_im, s5_c_re, s5_c_im, s5_d, s5_log_step, rel_bias, w_router, b_router, w_gu, b_gu, w_dn, b_dn, w_c1, b_c1, w_c2, b_c2, mod_table, ln_g, ln_b):
    bsz, L, d = x.shape
    assert bsz == 1 and c.shape[0] == 1
    depth = w_in.shape[0]
    g, p, h = s5_b_re.shape[1:]
    s5w = g * h
    heads = rel_bias.shape[1]
    aw = heads * ATT_HEAD_DIM
    iw = IDX_HEADS * IDX_DIM
    n_exp = w_router.shape[2]
    assert w_in.shape[2] == s5w + 3 * aw + iw + IDX_DIM + IDX_HEADS + 2 * d
    assert n_exp <= LANES and IDX_DIM + IDX_HEADS <= LANES and 2 * IDX_DIM == LANES
    o_q, o_k, o_v, o_qi = s5w, s5w + aw, s5w + 2 * aw, s5w + 3 * aw
    o_ki = s5w + 3 * aw + iw
    o_gate = o_ki + IDX_DIM + IDX_HEADS
    topk = min(TOPK_MAX, L // 4)
    tb = _tile(L, 256)
    ck = _tile(L, 128)
    dn_alpha = (2 * depth) ** 0.25

    mod = _conditioning(c, w_c1, b_c1, w_c2, b_c2, mod_table).reshape(depth, N_MOD, d)
    bias_tiles = _bias_tiles(rel_bias, tb)
    far_bias = rel_bias[-1] * LOG2E
    assert o_ki % LANES == 0 and (g * h) % LANES == 0 and LANES % h == 0

    xs = x.reshape(L, d)
    hcur = _modulate(xs, mod[0], sc_row=1, sh_row=0)
    for l in range(depth):
        w_l = w_in[l]
        w_nat = jnp.concatenate(
            [w_l[:, :s5w], w_l[:, o_k:o_k + aw], w_l[:, o_gate:],
             jnp.pad(w_l[:, o_ki:o_gate], ((0, 0), (0, LANES - IDX_DIM - IDX_HEADS)))], axis=1).astype(BF16)
        w_tr = jnp.transpose(jnp.concatenate(
            [w_l[:, o_q:o_q + aw], w_l[:, o_v:o_v + aw], w_l[:, o_qi:o_qi + iw]], axis=1)).astype(BF16)
        u = _matmul(hcur, w_nat, out_dtype=F32, col_off=0, n_cols=s5w, name="proj_u")
        k = _matmul(hcur, w_nat, out_dtype=BF16, col_off=s5w, n_cols=aw, name="proj_k")
        gates_ab = _matmul(hcur, w_nat, out_dtype=F32, col_off=s5w + aw, n_cols=2 * d, name="proj_gates")
        misc = _matmul(hcur, w_nat, out_dtype=F32, col_off=s5w + aw + 2 * d, n_cols=LANES, tn=LANES,
                       name="proj_misc")
        qvi_t = _matmul_t(hcur, w_tr, out_dtype=BF16, name="proj_qvi_t")

        prm = _s5_discretize(s5_lam_re[l], s5_lam_im[l], s5_b_re[l], s5_b_im[l], s5_c_re[l], s5_c_im[l],
                             s5_d[l], s5_log_step[l], S5_CHUNK)
        y_s5 = _s5_branch(u, prm)
        ya_in = _glu(y_s5, w_glu, b_glu[l], l)

        k_idx = misc[:, :IDX_DIM].astype(BF16)
        zeros = jnp.zeros_like(k_idx)
        k_even = jnp.concatenate([k_idx, zeros], axis=1).reshape(L // ck, ck, LANES)
        k_odd = jnp.concatenate([zeros, k_idx], axis=1).reshape(L // ck, ck, LANES)
        k2 = jnp.concatenate([k_even, k_odd], axis=1)
        w_t = jnp.transpose(misc[:, IDX_DIM:IDX_DIM + IDX_HEADS])
        keys, thr = _indexer(qvi_t, 2 * aw, k2, w_t, topk)
        attn = _attention(qvi_t, 0, 1, k, keys, thr, bias_tiles, far_bias, tb)

        merged = _merge(ya_in, attn, w_a_out, w_b_out, gates_ab, l)
        y = _matmul(merged, w_o, layer=l, out_dtype=F32, name="proj_out")
        xs, h2 = _ln_modulate(xs, y, mod[l], ln_g[l], ln_b[l], mod[l], alpha=dn_alpha, gate_row=2, ln_row=0,
                              sc_row=4, sh_row=3, want_h=True)

        w_r = jnp.pad(w_router[l], ((0, 0), (0, LANES - n_exp))).astype(BF16)
        b_r = jnp.pad(b_router[l], (0, LANES - n_exp)).reshape(1, LANES)
        gates = _router(h2, w_r, b_r, n_exp)
        b_dn_pad = jnp.pad(b_dn[l], ((0, LANES - n_exp), (0, 0))).astype(BF16)
        y = _moe_experts(h2, gates, w_gu[l].astype(BF16), b_gu[l], w_dn[l].astype(BF16), b_dn_pad)
        last = l == depth - 1
        xs, hcur = _ln_modulate(xs, y, mod[l], ln_g[l], ln_b[l], mod[min(l + 1, depth - 1)], alpha=dn_alpha,
                                gate_row=5, ln_row=1, sc_row=1, sh_row=0, want_h=not last)
    return xs.reshape(bsz, L, d)
```

```python
import functools
import math

import numpy as np
import jax
import jax.numpy as jnp
from jax import lax
from jax.experimental import pallas as pl
from jax.experimental.pallas import tpu as pltpu

F32 = jnp.float32
BF16 = jnp.bfloat16
I32 = jnp.int32

ATT_HEAD_DIM = 128
IDX_HEADS = 32
IDX_DIM = 64
TOPK_MAX = 256
REL_MAX_DIST = 128
TOP_K_EXPERTS = 4
SWIGLU_LIMIT = 7.0
SWIGLU_ALPHA = 1.702
N_MOD = 6
LN_EPS = 1e-5

LANES = 128
VMEM_LIMIT_BYTES = 56 * 1024 * 1024
S5_CHUNK = 16
NEG_BIG = -1e30
LOG2E = math.log2(math.e)
INT_MIN = -(2 ** 31)


def _cp(*sem):
    return pltpu.CompilerParams(dimension_semantics=sem, vmem_limit_bytes=VMEM_LIMIT_BYTES)


def _tile(n, t):
    t = min(n, t)
    while n % t:
        t //= 2
    return t


def _cond1_kernel(c_ref, w_ref, b_ref, o_ref):
    a = jax.nn.silu(c_ref[...]).astype(BF16)
    t = jnp.dot(a, w_ref[...].astype(BF16), preferred_element_type=F32) + b_ref[...]
    o_ref[...] = jax.nn.silu(t)


def _cond2_kernel(t_ref, w_ref, b_ref, tab_ref, o_ref):
    acc = jnp.dot(t_ref[...].astype(BF16), w_ref[...].astype(BF16), preferred_element_type=F32)
    o_ref[...] = acc[0:1, :] + b_ref[...] + tab_ref[...]


def _conditioning(c, w_c1, b_c1, w_c2, b_c2, mod_table):
    d = c.shape[1]
    r = w_c1.shape[1]
    depth = mod_table.shape[0]
    n = w_c2.shape[1]
    c8 = jnp.broadcast_to(c, (8, d))
    t = pl.pallas_call(
        _cond1_kernel, out_shape=jax.ShapeDtypeStruct((8, r), F32), name="cond1",
        compiler_params=_cp(),
    )(c8, w_c1, b_c1.reshape(1, r))
    tn = _tile(n, 2048)
    return pl.pallas_call(
        _cond2_kernel, grid=(n // tn,),
        in_specs=[pl.BlockSpec((8, r), lambda j: (0, 0)),
                  pl.BlockSpec((r, tn), lambda j: (0, j)),
                  pl.BlockSpec((1, tn), lambda j: (0, j)),
                  pl.BlockSpec((depth, tn), lambda j: (0, j))],
        out_specs=pl.BlockSpec((depth, tn), lambda j: (0, j)),
        out_shape=jax.ShapeDtypeStruct((depth, n), F32), name="cond2",
        compiler_params=_cp("parallel"),
    )(t, w_c2, b_c2.reshape(1, n), mod_table.reshape(depth, n))


def _modulate_kernel(x_ref, mod_ref, h_ref, *, sc_row, sh_row):
    sc = mod_ref[sc_row:sc_row + 1, :]
    sh = mod_ref[sh_row:sh_row + 1, :]
    h_ref[...] = (x_ref[...] * (1.0 + sc) + sh).astype(h_ref.dtype)


def _modulate(x, mod_l, sc_row, sh_row, tm=256):
    m, d = x.shape
    tm = _tile(m, tm)
    return pl.pallas_call(
        functools.partial(_modulate_kernel, sc_row=sc_row, sh_row=sh_row),
        grid=(m // tm,),
        in_specs=[pl.BlockSpec((tm, d), lambda i: (i, 0)),
                  pl.BlockSpec((N_MOD, d), lambda i: (0, 0))],
        out_specs=pl.BlockSpec((tm, d), lambda i: (i, 0)),
        out_shape=jax.ShapeDtypeStruct((m, d), BF16), name="modulate",
        compiler_params=_cp("parallel"),
    )(x, mod_l)


def _ln_kernel(x_ref, y_ref, mod_ref, g_ref, b_ref, nmod_ref, xo_ref, *h_refs,
               alpha, gate_row, ln_row, sc_row, sh_row):
    gate = mod_ref[gate_row:gate_row + 1, :]
    u = alpha * x_ref[...] + (1.0 + gate) * y_ref[...]
    mu = jnp.mean(u, axis=-1, keepdims=True)
    uc = u - mu
    var = jnp.mean(uc * uc, axis=-1, keepdims=True)
    xn = uc * lax.rsqrt(var + LN_EPS) * g_ref[ln_row:ln_row + 1, :] + b_ref[ln_row:ln_row + 1, :]
    xo_ref[...] = xn
    if h_refs:
        sc = nmod_ref[sc_row:sc_row + 1, :]
        sh = nmod_ref[sh_row:sh_row + 1, :]
        h_refs[0][...] = (xn * (1.0 + sc) + sh).astype(BF16)


def _ln_modulate(x, y, mod_l, ln_g_l, ln_b_l, next_mod, *, alpha, gate_row, ln_row, sc_row, sh_row,
                 want_h, tm=256):
    m, d = x.shape
    tm = _tile(m, tm)
    row = pl.BlockSpec((tm, d), lambda i: (i, 0))
    out_shape = [jax.ShapeDtypeStruct((m, d), F32)]
    out_specs = [row]
    if want_h:
        out_shape.append(jax.ShapeDtypeStruct((m, d), BF16))
        out_specs.append(row)
    res = pl.pallas_call(
        functools.partial(_ln_kernel, alpha=alpha, gate_row=gate_row, ln_row=ln_row,
                          sc_row=sc_row, sh_row=sh_row),
        grid=(m // tm,),
        in_specs=[row, row,
                  pl.BlockSpec((N_MOD, d), lambda i: (0, 0)),
                  pl.BlockSpec((2, d), lambda i: (0, 0)),
                  pl.BlockSpec((2, d), lambda i: (0, 0)),
                  pl.BlockSpec((N_MOD, d), lambda i: (0, 0))],
        out_specs=out_specs, out_shape=out_shape, name="ln_modulate",
        compiler_params=_cp("parallel"),
    )(x, y, mod_l, ln_g_l, ln_b_l, next_mod)
    return (res[0], res[1]) if want_h else (res[0], None)


def _mm_kernel(a_ref, w_ref, o_ref, wbf_ref):
    @pl.when(pl.program_id(1) == 0)
    def _():
        wbf_ref[...] = w_ref[...].astype(BF16)

    o_ref[...] = jnp.dot(a_ref[...], wbf_ref[...], preferred_element_type=F32).astype(o_ref.dtype)


def _w_spec(w, layer, k, tn, off=0):
    if w.ndim == 3:
        return pl.BlockSpec((None, k, tn), lambda j, i: (layer, 0, j + off))
    return pl.BlockSpec((k, tn), lambda j, i: (0, j + off))


def _matmul(a, w, *, out_dtype, layer=0, col_off=0, n_cols=None, tm=1024, tn=512, name="matmul"):
    m, k = a.shape
    n_cols = w.shape[-1] - col_off if n_cols is None else n_cols
    tm = _tile(m, tm)
    tn = math.gcd(_tile(n_cols, tn), col_off) if col_off else _tile(n_cols, tn)
    off = col_off // tn
    return pl.pallas_call(
        _mm_kernel, grid=(n_cols // tn, m // tm),
        in_specs=[pl.BlockSpec((tm, k), lambda j, i: (i, 0)),
                  _w_spec(w, layer, k, tn, off)],
        out_specs=pl.BlockSpec((tm, tn), lambda j, i: (i, j)),
        out_shape=jax.ShapeDtypeStruct((m, n_cols), out_dtype),
        scratch_shapes=[pltpu.VMEM((k, tn), BF16)], name=name,
        compiler_params=_cp("parallel", "arbitrary"),
    )(a, w)


def _mm_t_kernel(a_ref, w_ref, o_ref, wt_ref):
    @pl.when(pl.program_id(1) == 0)
    def _():
        wt_ref[...] = jnp.transpose(w_ref[...])

    o_ref[...] = lax.dot_general(wt_ref[...], a_ref[...], (((1,), (1,)), ((), ())),
                                 preferred_element_type=F32).astype(o_ref.dtype)


def _matmul_t(a, w, *, out_dtype, col_off, n_cols, tm=1024, tn=1024, name="matmul_t"):
    m, k = a.shape
    tm = _tile(m, tm)
    tn = math.gcd(_tile(n_cols, tn), col_off) if col_off else _tile(n_cols, tn)
    off = col_off // tn
    return pl.pallas_call(
        _mm_t_kernel, grid=(n_cols // tn, m // tm),
        in_specs=[pl.BlockSpec((tm, k), lambda j, i: (i, 0)),
                  pl.BlockSpec((k, tn), lambda j, i: (0, j + off))],
        out_specs=pl.BlockSpec((tn, tm), lambda j, i: (j, i)),
        out_shape=jax.ShapeDtypeStruct((n_cols, m), out_dtype),
        scratch_shapes=[pltpu.VMEM((tn, k), w.dtype)], name=name,
        compiler_params=_cp("parallel", "arbitrary"),
    )(a, w)


def _glu_kernel(y_ref, yt_ref, w_ref, b_ref, o_ref, wbf_ref):
    @pl.when(pl.program_id(1) == 0)
    def _():
        wbf_ref[...] = w_ref[...].astype(BF16)

    z = jnp.dot(y_ref[...].astype(BF16), wbf_ref[...], preferred_element_type=F32) + b_ref[...]
    o_ref[...] = (yt_ref[...] * jax.nn.sigmoid(z)).astype(o_ref.dtype)


def _glu(y, w, b, layer, tm=1024, tn=512):
    m, k = y.shape
    n = w.shape[-1]
    tm = _tile(m, tm)
    tn = _tile(n, tn)
    return pl.pallas_call(
        _glu_kernel, grid=(n // tn, m // tm),
        in_specs=[pl.BlockSpec((tm, k), lambda j, i: (i, 0)),
                  pl.BlockSpec((tm, tn), lambda j, i: (i, j)),
                  _w_spec(w, layer, k, tn),
                  pl.BlockSpec((1, tn), lambda j, i: (0, j))],
        out_specs=pl.BlockSpec((tm, tn), lambda j, i: (i, j)),
        out_shape=jax.ShapeDtypeStruct((m, n), BF16),
        scratch_shapes=[pltpu.VMEM((k, tn), BF16)], name="s5_glu",
        compiler_params=_cp("parallel", "arbitrary"),
    )(y, y, w, b.reshape(1, n))


def _merge_kernel(a1_ref, a2_ref, w1_ref, w2_ref, ga_ref, gb_ref, o_ref, w1bf_ref, w2bf_ref):
    @pl.when(pl.program_id(1) == 0)
    def _():
        w1bf_ref[...] = w1_ref[...].astype(BF16)
        w2bf_ref[...] = w2_ref[...].astype(BF16)

    ya = jnp.dot(a1_ref[...], w1bf_ref[...], preferred_element_type=F32)
    yb = jnp.dot(a2_ref[...], w2bf_ref[...], preferred_element_type=F32)
    o_ref[...] = (jax.nn.sigmoid(ga_ref[...]) * ya + jax.nn.sigmoid(gb_ref[...]) * yb).astype(o_ref.dtype)


def _merge(a1, a2, w1, w2, gates, layer, tm=1024, tn=512):
    m, k1 = a1.shape
    k2 = a2.shape[1]
    n = w1.shape[-1]
    tm = _tile(m, tm)
    tn = _tile(n, tn)
    nb = n // tn
    return pl.pallas_call(
        _merge_kernel, grid=(nb, m // tm),
        in_specs=[pl.BlockSpec((tm, k1), lambda j, i: (i, 0)),
                  pl.BlockSpec((tm, k2), lambda j, i: (i, 0)),
                  _w_spec(w1, layer, k1, tn),
                  _w_spec(w2, layer, k2, tn),
                  pl.BlockSpec((tm, tn), lambda j, i: (i, j)),
                  pl.BlockSpec((tm, tn), lambda j, i: (i, j + nb))],
        out_specs=pl.BlockSpec((tm, tn), lambda j, i: (i, j)),
        out_shape=jax.ShapeDtypeStruct((m, n), BF16),
        scratch_shapes=[pltpu.VMEM((k1, tn), BF16), pltpu.VMEM((k2, tn), BF16)], name="branch_merge",
        compiler_params=_cp("parallel", "arbitrary"),
    )(a1, a2, w1, w2, gates, gates)


def _s5_discretize(lam_re, lam_im, b_re, b_im, c_re, c_im, d, log_step, t_chunk):
    hp = lax.Precision.HIGHEST
    g, p, h = b_re.shape
    na = LANES // h
    nx = g // na
    assert 2 * p == LANES
    lam = lax.complex(jnp.minimum(lam_re, -1e-4), lam_im)
    dt = jnp.exp(log_step)[:, None]
    lam_bar = jnp.exp(lam * dt)
    b_bar = ((lam_bar - 1.0) / lam)[:, :, None] * lax.complex(b_re, b_im)
    c = lax.complex(c_re, c_im)
    tau = jnp.arange(t_chunk + 1, dtype=F32)
    pw = jnp.exp((lam * dt)[:, None, :] * tau[None, :, None].astype(jnp.complex64))
    cp = c[:, None, :, :] * pw[:, :, None, :]
    kern = jnp.einsum('gtop,gpi->gtoi', cp[:, :t_chunk], b_bar, precision=hp).real
    kio = jnp.transpose(kern.reshape(nx, na, t_chunk, h, h), (0, 2, 1, 4, 3))
    kt = jnp.tile(kio.reshape(nx, t_chunk, LANES, h), (1, 1, 1, na))
    wcx = pw[:, t_chunk - 1 - np.arange(t_chunk)][:, :, :, None] * b_bar[:, None, :, :]
    wcx = jnp.transpose(wcx, (0, 1, 3, 2))
    w_ri = jnp.concatenate([wcx.real, wcx.imag], axis=-1).reshape(nx, na, t_chunk, h, 2 * p)
    wc = jnp.transpose(w_ri, (0, 2, 1, 3, 4)).reshape(nx, t_chunk, LANES, 2 * p)
    cl = jnp.transpose(cp[:, 1:t_chunk + 1], (0, 3, 1, 2))
    vc = jnp.concatenate([cl.real, -cl.imag], axis=1).reshape(nx, na, 2 * p, t_chunk * h)
    dd = jnp.broadcast_to(d.reshape(nx, 1, 1, na, h), (nx, 1, t_chunk, na, h)).reshape(nx, 1, t_chunk * LANES)
    a = pw[:, t_chunk]
    a1 = jnp.concatenate([a.real, a.real], axis=-1).reshape(1, g * 2 * p)
    a2 = jnp.concatenate([-a.imag, a.imag], axis=-1).reshape(1, g * 2 * p)
    return kt, wc, vc, dd, a1, a2


def _chunk_rows(u_ref, t, nc):
    return jnp.concatenate([u_ref[pl.ds(s, nc, stride=t), :] for s in range(t)], axis=1)


def _s5_state_in_kernel(u_ref, wc_ref, xa_ref, xb_ref, wa_ref, wb_ref, *, t, nc, h):
    na = LANES // h
    row_group = lax.broadcasted_iota(I32, (LANES, LANES), 0) // h
    for s in range(t):
        blk = wc_ref[s]
        swapped = pltpu.roll(blk, LANES // 2, axis=1)
        for b in range(na):
            own = row_group == b
            dst = (slice(s * LANES, (s + 1) * LANES), slice(b * LANES, (b + 1) * LANES))
            wa_ref[dst] = jnp.where(own, blk, 0.0).astype(BF16)
            wb_ref[dst] = jnp.where(own, swapped, 0.0).astype(BF16)
    ucat = _chunk_rows(u_ref, t, nc).astype(BF16)
    xa_ref[...] = jnp.dot(ucat, wa_ref[...], preferred_element_type=F32)
    xb_ref[...] = jnp.dot(ucat, wb_ref[...], preferred_element_type=F32)


def _s5_scan_kernel(xa_ref, xb_ref, a1_ref, a2_ref, s_ref, *, unroll):
    a1 = a1_ref[...]
    a2 = a2_ref[...]
    nc = xa_ref.shape[0]

    def body(cb, carry):
        s_a, s_b = carry
        base = pl.multiple_of(cb * unroll, unroll)
        xa = xa_ref[pl.ds(base, unroll), :]
        xb = xb_ref[pl.ds(base, unroll), :]
        rows = []
        for k in range(unroll):
            rows.append(s_a)
            s_a, s_b = (a1 * s_a + a2 * s_b + xa[k:k + 1, :],
                        a1 * s_b - a2 * s_a + xb[k:k + 1, :])
        s_ref[pl.ds(base, unroll), :] = jnp.concatenate(rows, axis=0)
        return s_a, s_b

    z = jnp.zeros(a1.shape, F32)
    lax.fori_loop(0, nc // unroll, body, (z, z))


def _s5_out_kernel(u_ref, sp_ref, kt_ref, vc_ref, dd_ref, y_ref, bd_ref, vm_ref, *, t, nc, h, n_parts):
    part = pl.program_id(1)
    tp = t // n_parts
    na = LANES // h
    w = tp * LANES
    same_group = (lax.broadcasted_iota(I32, (LANES, LANES), 0) // h
                  == lax.broadcasted_iota(I32, (LANES, LANES), 1) // h)
    row = lax.broadcasted_iota(I32, (LANES, w), 0)
    col = lax.broadcasted_iota(I32, (LANES, w), 1)
    landing_group = jnp.where((row // h == col // LANES) & (row % h == col % h), (col // h) % na, -1)
    ucat = _chunk_rows(u_ref, t, nc)
    for k in range(n_parts):
        @pl.when(part == k)
        def _(k=k):
            live = (k + 1) * tp
            for s in range(live):
                for tt in range(tp):
                    lag = k * tp + tt - s
                    dst = (slice(s * LANES, (s + 1) * LANES), slice(tt * LANES, (tt + 1) * LANES))
                    if lag >= 0:
                        bd_ref[dst] = jnp.where(same_group, kt_ref[lag], 0.0).astype(BF16)
                    else:
                        bd_ref[dst] = jnp.zeros((LANES, LANES), BF16)
            for a in range(na):
                place = jnp.where(landing_group == a, 1.0, 0.0).astype(BF16)
                src = vc_ref[a][:, k * tp * h:(k + 1) * tp * h].astype(BF16)
                vm_ref[a * LANES:(a + 1) * LANES, :] = jnp.dot(src, place, preferred_element_type=F32).astype(BF16)
            y = jnp.dot(ucat[:, :live * LANES].astype(BF16), bd_ref[:live * LANES, :], preferred_element_type=F32)
            y = y + jnp.dot(sp_ref[...].astype(BF16), vm_ref[...], preferred_element_type=F32)
            y = jax.nn.gelu(y + dd_ref[...] * ucat[:, k * w:(k + 1) * w])
            for s in range(tp):
                y_ref[pl.ds(k * tp + s, nc, stride=t), :] = y[:, s * LANES:(s + 1) * LANES]


def _s5_branch(u, prm):
    kt, wc, vc, dd, a1, a2 = prm
    L, width = u.shape
    nx, na, _, th = vc.shape
    t = kt.shape[1]
    h = th // t
    tah = t * LANES
    sw = na * LANES
    nc = L // t
    u_spec = pl.BlockSpec((L, LANES), lambda x: (0, x))
    st_spec = pl.BlockSpec((nc, sw), lambda x: (0, x))
    xa, xb = pl.pallas_call(
        functools.partial(_s5_state_in_kernel, t=t, nc=nc, h=h), grid=(nx,),
        in_specs=[u_spec,
                  pl.BlockSpec((None, t, LANES, LANES), lambda x: (x, 0, 0, 0))],
        out_specs=[st_spec, st_spec],
        out_shape=[jax.ShapeDtypeStruct((nc, nx * sw), F32)] * 2,
        scratch_shapes=[pltpu.VMEM((tah, sw), BF16), pltpu.VMEM((tah, sw), BF16)], name="s5_state_in",
        compiler_params=_cp("parallel"),
    )(u, wc)
    ws = _tile(nx * sw, 2048)
    unroll = _tile(nc, 8)
    sc_spec = pl.BlockSpec((nc, ws), lambda x: (0, x))
    a_spec = pl.BlockSpec((1, ws), lambda x: (0, x))
    sp = pl.pallas_call(
        functools.partial(_s5_scan_kernel, unroll=unroll), grid=(nx * sw // ws,),
        in_specs=[sc_spec, sc_spec, a_spec, a_spec],
        out_specs=sc_spec,
        out_shape=jax.ShapeDtypeStruct((nc, nx * sw), F32), name="s5_scan",
        compiler_params=_cp("parallel"),
    )(xa, xb, a1, a2)
    n_parts = 2
    tw = tah // n_parts
    return pl.pallas_call(
        functools.partial(_s5_out_kernel, t=t, nc=nc, h=h, n_parts=n_parts), grid=(nx, n_parts),
        in_specs=[pl.BlockSpec((L, LANES), lambda x, k: (0, x)),
                  pl.BlockSpec((nc, sw), lambda x, k: (0, x)),
                  pl.BlockSpec((None, t, LANES, LANES), lambda x, k: (x, 0, 0, 0)),
                  pl.BlockSpec((None, na, LANES, th), lambda x, k: (x, 0, 0, 0)),
                  pl.BlockSpec((None, 1, tw), lambda x, k: (x, 0, k))],
        out_specs=pl.BlockSpec((L, LANES), lambda x, k: (0, x)),
        out_shape=jax.ShapeDtypeStruct((L, width), F32),
        scratch_shapes=[pltpu.VMEM((tah, tw), BF16), pltpu.VMEM((sw, tw), BF16)], name="s5_out",
        compiler_params=_cp("parallel", "arbitrary"),
    )(u, sp, kt, vc, dd)


def _t5_bucket_starts(n_buckets):
    exact = n_buckets // 2
    dist = np.arange(0, 4 * REL_MAX_DIST).astype(np.int64)
    far = exact + (np.log(np.maximum(dist, 1).astype(np.float32) / np.float32(exact))
                   / np.float32(math.log(REL_MAX_DIST / exact)) * np.float32(n_buckets - exact)).astype(np.int64)
    bucket = np.where(dist < exact, dist, np.minimum(far, n_buckets - 1))
    return [int(np.argmax(bucket >= b)) for b in range(1, n_buckets)]


def _bias_tile_kernel(rb_ref, o_ref, *, tb, starts):
    delta = jnp.where(pl.program_id(0) == 3, -1, pl.program_id(0))
    head = pl.program_id(1)
    key = lax.broadcasted_iota(I32, (tb, tb), 0)
    qry = lax.broadcasted_iota(I32, (tb, tb), 1)
    dist = delta * tb + qry - key
    bias = jnp.full((tb, tb), rb_ref[0, head], F32)
    for b, start in enumerate(starts, start=1):
        bias = jnp.where(dist >= start, rb_ref[b, head], bias)
    o_ref[...] = jnp.where(dist >= 0, bias * LOG2E, NEG_BIG)


def _bias_tiles(rel_bias, tb):
    n_buckets, heads = rel_bias.shape
    starts = _t5_bucket_starts(n_buckets)
    assert 2 * tb - (tb - 1) >= starts[-1]
    return pl.pallas_call(
        functools.partial(_bias_tile_kernel, tb=tb, starts=starts), grid=(4, heads),
        in_specs=[pl.BlockSpec(memory_space=pltpu.SMEM)],
        out_specs=pl.BlockSpec((None, None, tb, tb), lambda dl, hd: (dl, hd, 0, 0)),
        out_shape=jax.ShapeDtypeStruct((4, heads, tb, tb), F32), name="rel_bias_tiles",
        compiler_params=_cp("parallel", "parallel"),
    )(rel_bias)


def _indexer_kernel(qt_ref, k2_ref, wt_ref, keys_ref, thr_ref, *, tq, ck, n_heads, topk, w_scale):
    i = pl.program_id(0)
    n_blocks = keys_ref.shape[0] // tq
    n_live = (i + 1) * (tq // ck)
    wv = wt_ref[...] * w_scale

    def score_chunk(c, carry):
        row = pl.multiple_of(c * ck, ck)
        kk = k2_ref[c]
        acc = jnp.zeros((ck, tq), F32)
        for pr in range(n_heads // 2):
            s2 = jnp.dot(kk, qt_ref[pr * LANES:(pr + 1) * LANES, :], preferred_element_type=F32)
            acc = acc + jnp.maximum(s2[:ck], 0.0) * wv[2 * pr:2 * pr + 1, :]
            acc = acc + jnp.maximum(s2[ck:], 0.0) * wv[2 * pr + 1:2 * pr + 2, :]
        kpos = row + lax.broadcasted_iota(I32, (ck, tq), 0)
        qpos = i * tq + lax.broadcasted_iota(I32, (ck, tq), 1)
        acc = jnp.where(acc == 0.0, 0.0, acc)
        score = jnp.where(kpos <= qpos, acc, -jnp.inf)
        bits = pltpu.bitcast(score, I32)
        keys_ref[pl.ds(row, ck), :] = jnp.where(bits < 0, bits ^ 0x7FFFFFFF, bits)
        return carry

    lax.fori_loop(0, n_live, score_chunk, 0)

    def fill_block(b, carry):
        row = pl.multiple_of(b * tq, tq)
        keys_ref[pl.ds(row, tq), :] = jnp.full((tq, tq), INT_MIN, I32)
        return carry

    lax.fori_loop(i + 1, n_blocks, fill_block, 0)

    acc_rows = 32

    def count_ge(cand):
        def body(b, cnt):
            row = pl.multiple_of(b * tq, tq)
            hit = jnp.where(keys_ref[pl.ds(row, tq), :] >= cand, 1.0, 0.0)
            return cnt + jnp.sum(hit.reshape(tq // acc_rows, acc_rows, tq), axis=0)
        cnt = lax.fori_loop(0, i + 1, body, jnp.zeros((acc_rows, tq), F32))
        return jnp.sum(cnt, axis=0, keepdims=True)

    kf = float(topk)
    zero = jnp.zeros((1, tq), I32)
    c_zero = count_ge(zero)
    thr0 = jnp.where(c_zero >= kf, zero, jnp.full((1, tq), INT_MIN, I32))
    cnt0 = jnp.where(c_zero >= kf, c_zero, ((i + 1) * tq).astype(F32))

    def undecided(st):
        it, _, cnt = st
        return (it < 31) & (jnp.max(cnt) > kf)

    def bit_step(st):
        it, thr, cnt = st
        cand = thr | jnp.left_shift(jnp.int32(1), 30 - it)
        c = count_ge(cand)
        keep = c >= kf
        return it + 1, jnp.where(keep, cand, thr), jnp.where(keep, c, cnt)

    _, thr, cnt = lax.while_loop(undecided, bit_step, (jnp.int32(0), thr0, cnt0))

    n_keys = keys_ref.shape[0]

    def position_cutoff(_):
        need = kf - count_ge(thr + 1)

        def count_eq_upto(pos):
            def body(b, acc):
                row = pl.multiple_of(b * tq, tq)
                kpos = row + lax.broadcasted_iota(I32, (tq, tq), 0)
                hit = jnp.where(keys_ref[pl.ds(row, tq), :] == thr, jnp.where(kpos <= pos, 1.0, 0.0), 0.0)
                return acc + jnp.sum(hit.reshape(tq // acc_rows, acc_rows, tq), axis=0)
            acc = lax.fori_loop(0, i + 1, body, jnp.zeros((acc_rows, tq), F32))
            return jnp.sum(acc, axis=0, keepdims=True)

        def halve(_, bounds):
            lo, hi = bounds
            mid = (lo + hi) >> 1
            enough = count_eq_upto(mid) >= need
            return jnp.where(enough, lo, mid), jnp.where(enough, mid, hi)

        lo0 = jnp.full((1, tq), -1, I32)
        hi0 = jnp.full((1, tq), n_keys - 1, I32)
        _, hi = lax.fori_loop(0, max(1, (n_keys - 1).bit_length()), halve, (lo0, hi0))
        return jnp.where(cnt > kf, hi, n_keys)

    jcut = lax.cond(jnp.max(cnt) > kf, position_cutoff, lambda _: jnp.full((1, tq), n_keys, I32), 0)
    sel_row = lax.broadcasted_iota(I32, thr_ref.shape, 0)
    thr_ref[...] = jnp.where(sel_row == 0, thr, jnp.where(sel_row == 1, jcut, 0))


def _indexer(qi_t, row_off, k2, w_t, topk, tq=256):
    L = qi_t.shape[1]
    n_heads = w_t.shape[0]
    qw = n_heads * IDX_DIM
    assert row_off % qw == 0
    tq = _tile(L, tq)
    ck = k2.shape[1] // 2
    assert tq % ck == 0
    w_scale = (n_heads ** -0.5) * (IDX_DIM ** -0.5)
    return pl.pallas_call(
        functools.partial(_indexer_kernel, tq=tq, ck=ck, n_heads=n_heads, topk=topk, w_scale=w_scale),
        grid=(L // tq,),
        in_specs=[pl.BlockSpec((qw, tq), lambda i: (row_off // qw, i)),
                  pl.BlockSpec(k2.shape, lambda i: (0, 0, 0)),
                  pl.BlockSpec((n_heads, tq), lambda i: (0, i))],
        out_specs=[pl.BlockSpec((L, tq), lambda i: (0, i)),
                   pl.BlockSpec((8, tq), lambda i: (0, i))],
        out_shape=[jax.ShapeDtypeStruct((L, L), I32), jax.ShapeDtypeStruct((8, L), I32)],
        name="dsa_indexer", compiler_params=_cp("parallel"),
    )(qi_t, k2, w_t)


def _attn_kernel(qi_ref, kj_ref, far_ref, qt_ref, k_ref, vt_ref, keys_ref, thr_ref, bias_a_ref, bias_b_ref, o_ref,
                 m_ref, l_ref, acc_ref, mb_ref, *, heads, dh, scale2):
    step = pl.program_id(0)
    qi = qi_ref[step]
    kj = kj_ref[step]

    @pl.when(kj == 0)
    def _():
        m_ref[...] = jnp.full(m_ref.shape, -jnp.inf, F32)
        l_ref[...] = jnp.zeros(l_ref.shape, F32)
        acc_ref[...] = jnp.zeros(acc_ref.shape, F32)

    thr = thr_ref[0:1, :]
    kpos = kj * keys_ref.shape[0] + lax.broadcasted_iota(I32, keys_ref.shape, 0)
    bar = jnp.where(kpos <= thr_ref[1:2, :], thr, thr + 1)
    mb_ref[...] = jnp.where(keys_ref[...] >= bar, 0.0, NEG_BIG)

    def scores(h):
        hs = slice(h * dh, (h + 1) * dh)
        return jnp.dot(k_ref[:, hs], qt_ref[hs, :], preferred_element_type=F32)

    def fold(reduce_fn, a, rows=32):
        return reduce_fn(a.reshape(a.shape[0] // rows, rows, a.shape[1]), axis=0)

    def run(near):
        ahead = 4
        pending = [scores(h) for h in range(min(ahead, heads))]
        for h in range(heads):
            hs = slice(h * dh, (h + 1) * dh)
            x = pending.pop(0) * scale2 + mb_ref[...]
            if h + ahead < heads:
                pending.append(scores(h + ahead))
            if near:
                x = x + jnp.concatenate([bias_a_ref[h], bias_b_ref[h]], axis=0)
            far = 0.0 if near else far_ref[h]
            m_prev = m_ref[h]
            m_new = jnp.maximum(m_prev, jnp.max(fold(jnp.max, x), axis=0, keepdims=True) + far)
            alpha = jnp.exp2(m_prev - m_new)
            p = jnp.exp2(x - (m_new[0:1, :] - far))
            l_ref[h] = alpha * l_ref[h] + jnp.sum(fold(jnp.sum, p), axis=0, keepdims=True)
            m_ref[h] = m_new
            acc_ref[hs, :] = alpha[0:1, :] * acc_ref[hs, :] + jnp.dot(vt_ref[hs, :], p.astype(BF16),
                                                                      preferred_element_type=F32)

    @pl.when(qi - 2 * kj - 1 < 2)
    def _():
        run(True)

    @pl.when(qi - 2 * kj - 1 >= 2)
    def _():
        run(False)

    @pl.when(kj == qi // 2)
    def _():
        for h in range(heads):
            hs = slice(h * dh, (h + 1) * dh)
            o_ref[:, hs] = jnp.transpose(acc_ref[hs, :] / l_ref[h][0:1, :]).astype(o_ref.dtype)


def _attention(qv_t, q_blk, v_blk, k, keys, thr, bias_tiles, far_bias, tb):
    L = k.shape[0]
    heads = bias_tiles.shape[1]
    dh = ATT_HEAD_DIM
    aw = heads * dh
    nb = L // tb
    assert nb % 2 == 0
    pairs = [(i, j) for i in range(nb) for j in range(i // 2 + 1)]
    qi = jnp.asarray(np.array([p[0] for p in pairs], np.int32))
    kj = jnp.asarray(np.array([p[1] for p in pairs], np.int32))

    def tile_of(behind):
        return jnp.where(behind < 0, 3, jnp.minimum(behind, 2))

    bias_spec_a = pl.BlockSpec((None, heads, tb, tb), lambda s, qi, kj: (tile_of(qi[s] - 2 * kj[s]), 0, 0, 0))
    bias_spec_b = pl.BlockSpec((None, heads, tb, tb), lambda s, qi, kj: (tile_of(qi[s] - 2 * kj[s] - 1), 0, 0, 0))
    grid_spec = pltpu.PrefetchScalarGridSpec(
        num_scalar_prefetch=2, grid=(len(pairs),),
        in_specs=[pl.BlockSpec(memory_space=pltpu.SMEM),
                  pl.BlockSpec((aw, tb), lambda s, qi, kj: (q_blk, qi[s])),
                  pl.BlockSpec((2 * tb, aw), lambda s, qi, kj: (kj[s], 0)),
                  pl.BlockSpec((aw, 2 * tb), lambda s, qi, kj: (v_blk, kj[s])),
                  pl.BlockSpec((2 * tb, tb), lambda s, qi, kj: (kj[s], qi[s])),
                  pl.BlockSpec((8, tb), lambda s, qi, kj: (0, qi[s])),
                  bias_spec_a, bias_spec_b],
        out_specs=pl.BlockSpec((tb, aw), lambda s, qi, kj: (qi[s], 0)),
        scratch_shapes=[pltpu.VMEM((heads, 8, tb), F32),
                        pltpu.VMEM((heads, 8, tb), F32),
                        pltpu.VMEM((aw, tb), F32),
                        pltpu.VMEM((2 * tb, tb), F32)])
    return pl.pallas_call(
        functools.partial(_attn_kernel, heads=heads, dh=dh, scale2=dh ** -0.5 * LOG2E),
        grid_spec=grid_spec, out_shape=jax.ShapeDtypeStruct((L, aw), BF16), name="dsa_attention",
        compiler_params=_cp("arbitrary"),
    )(qi, kj, far_bias, qv_t, k, qv_t, keys, thr, bias_tiles, bias_tiles)


def _router_kernel(h_ref, w_ref, b_ref, g_ref, *, n_exp, top_k):
    logits = jnp.dot(h_ref[...], w_ref[...], preferred_element_type=F32) + b_ref[...]
    lane = lax.broadcasted_iota(I32, logits.shape, 1).astype(F32)
    work = jnp.where(lane < n_exp, logits, -jnp.inf)
    gates = jnp.zeros(logits.shape, F32)
    denom = None
    top = None
    for r in range(top_k):
        m = jnp.max(work, axis=1, keepdims=True)
        first = jnp.min(jnp.where(work == m, lane, float(LANES)), axis=1, keepdims=True)
        sel = lane == first
        if r == 0:
            top = m
        e = jnp.exp(m - top)
        gates = gates + jnp.where(sel, e, 0.0)
        denom = e if denom is None else denom + e
        work = jnp.where(sel, -jnp.inf, work)
    g_ref[...] = gates / denom


def _router(h, w_pad, b_pad, n_exp, tm=512):
    m, d = h.shape
    tm = _tile(m, tm)
    return pl.pallas_call(
        functools.partial(_router_kernel, n_exp=n_exp, top_k=TOP_K_EXPERTS), grid=(m // tm,),
        in_specs=[pl.BlockSpec((tm, d), lambda i: (i, 0)),
                  pl.BlockSpec((d, LANES), lambda i: (0, 0)),
                  pl.BlockSpec((1, LANES), lambda i: (0, 0))],
        out_specs=pl.BlockSpec((tm, LANES), lambda i: (i, 0)),
        out_shape=jax.ShapeDtypeStruct((m, LANES), F32), name="moe_router",
        compiler_params=_cp("parallel"),
    )(h, w_pad, b_pad)


def _moe_kernel(h_ref, g_ref, wgu_ref, bgu_ref, wdn_ref, bdn_ref, o_ref, *, ff, eb):
    step = pl.program_id(1)
    gates = g_ref[...]

    @pl.when(step == 0)
    def _():
        o_ref[...] = jnp.dot(gates.astype(BF16), bdn_ref[...], preferred_element_type=F32)

    lane = lax.broadcasted_iota(I32, gates.shape, 1)
    for k in range(eb):
        gu = jnp.dot(h_ref[...], wgu_ref[k], preferred_element_type=F32) + bgu_ref[k]
        gate = jnp.minimum(gu[:, :ff], SWIGLU_LIMIT)
        up = jnp.clip(gu[:, ff:], -SWIGLU_LIMIT, SWIGLU_LIMIT)
        act = (up + 1.0) * gate * jax.nn.sigmoid(SWIGLU_ALPHA * gate)
        ge = jnp.sum(jnp.where(lane == step * eb + k, gates, 0.0), axis=1, keepdims=True)
        o_ref[...] += jnp.dot((act * ge).astype(BF16), wdn_ref[k], preferred_element_type=F32)


def _moe_experts(h, gates, w_gu, b_gu, w_dn, b_dn_pad, tm=512, eb=2):
    m, d = h.shape
    n_exp, _, ff2 = w_gu.shape
    tm = _tile(m, tm)
    eb = _tile(n_exp, eb)
    return pl.pallas_call(
        functools.partial(_moe_kernel, ff=ff2 // 2, eb=eb), grid=(m // tm, n_exp // eb),
        in_specs=[pl.BlockSpec((tm, d), lambda i, e: (i, 0)),
                  pl.BlockSpec((tm, LANES), lambda i, e: (i, 0)),
                  pl.BlockSpec((eb, d, ff2), lambda i, e: (e, 0, 0)),
                  pl.BlockSpec((eb, 1, ff2), lambda i, e: (e, 0, 0)),
                  pl.BlockSpec((eb, ff2 // 2, d), lambda i, e: (e, 0, 0)),
                  pl.BlockSpec((LANES, d), lambda i, e: (0, 0))],
        out_specs=pl.BlockSpec((tm, d), lambda i, e: (i, 0)),
        out_shape=jax.ShapeDtypeStruct((m, d), F32), name="moe_experts",
        compiler_params=_cp("parallel", "arbitrary"),
    )(h, gates, w_gu, b_gu.reshape(n_exp, 1, ff2), w_dn, b_dn_pad)


def kernel(x, c, w_in, w_a_out, w_b_out, w_o, w_glu, b_glu, s5_lam_re, s5_lam_im, s5_b_re, s5_b_im, s5_c_re, s5_c_im, s5_d, s5_log_step, rel_bias, w_router, b_router, w_gu, b_gu, w_dn, b_dn, w_c1, b_c1, w_c2, b_c2, mod_table, ln_g, ln_b):
    bsz, L, d = x.shape
    assert bsz == 1 and c.shape[0] == 1
    depth = w_in.shape[0]
    g, p, h = s5_b_re.shape[1:]
    s5w = g * h
    heads = rel_bias.shape[1]
    aw = heads * ATT_HEAD_DIM
    iw = IDX_HEADS * IDX_DIM
    n_exp = w_router.shape[2]
    assert w_in.shape[2] == s5w + 3 * aw + iw + IDX_DIM + IDX_HEADS + 2 * d
    assert n_exp <= LANES and IDX_DIM + IDX_HEADS <= LANES and 2 * IDX_DIM == LANES
    o_q, o_k, o_v, o_qi = s5w, s5w + aw, s5w + 2 * aw, s5w + 3 * aw
    o_ki = s5w + 3 * aw + iw
    o_gate = o_ki + IDX_DIM + IDX_HEADS
    topk = min(TOPK_MAX, L // 4)
    tb = _tile(L, 256)
    ck = _tile(L, 128)
    dn_alpha = (2 * depth) ** 0.25

    mod = _conditioning(c, w_c1, b_c1, w_c2, b_c2, mod_table).reshape(depth, N_MOD, d)
    bias_tiles = _bias_tiles(rel_bias, tb)
    far_bias = rel_bias[-1] * LOG2E
    assert o_ki % LANES == 0 and (g * h) % LANES == 0 and LANES % h == 0

    xs = x.reshape(L, d)
    hcur = _modulate(xs, mod[0], sc_row=1, sh_row=0)
    for l in range(depth):
        w_l = w_in[l]
        w_nat = jnp.concatenate(
            [w_l[:, :s5w], w_l[:, o_k:o_k + aw], w_l[:, o_gate:],
             w_l[:, o_q:o_q + aw], w_l[:, o_v:o_v + aw], w_l[:, o_qi:o_qi + iw],
             jnp.pad(w_l[:, o_ki:o_gate], ((0, 0), (0, LANES - IDX_DIM - IDX_HEADS)))], axis=1).astype(BF16)
        c_gate, c_qvi, c_misc = s5w + aw, s5w + aw + 2 * d, s5w + aw + 2 * d + 2 * aw + iw
        u = _matmul(hcur, w_nat, out_dtype=F32, col_off=0, n_cols=s5w, name="proj_u")
        k = _matmul(hcur, w_nat, out_dtype=BF16, col_off=s5w, n_cols=aw, name="proj_k")
        gates_ab = _matmul(hcur, w_nat, out_dtype=F32, col_off=c_gate, n_cols=2 * d, name="proj_gates")
        misc = _matmul(hcur, w_nat, out_dtype=F32, col_off=c_misc, n_cols=LANES, tn=LANES, name="proj_misc")
        qvi_t = _matmul_t(hcur, w_nat, out_dtype=BF16, col_off=c_qvi, n_cols=2 * aw + iw,
                          name="proj_qvi_t")

        prm = _s5_discretize(s5_lam_re[l], s5_lam_im[l], s5_b_re[l], s5_b_im[l], s5_c_re[l], s5_c_im[l],
                             s5_d[l], s5_log_step[l], S5_CHUNK)
        y_s5 = _s5_branch(u, prm)
        ya_in = _glu(y_s5, w_glu, b_glu[l], l)

        k_idx = misc[:, :IDX_DIM].astype(BF16)
        zeros = jnp.zeros_like(k_idx)
        k_even = jnp.concatenate([k_idx, zeros], axis=1).reshape(L // ck, ck, LANES)
        k_odd = jnp.concatenate([zeros, k_idx], axis=1).reshape(L // ck, ck, LANES)
        k2 = jnp.concatenate([k_even, k_odd], axis=1)
        w_t = jnp.transpose(misc[:, IDX_DIM:IDX_DIM + IDX_HEADS])
        keys, thr = _indexer(qvi_t, 2 * aw, k2, w_t, topk)
        attn = _attention(qvi_t, 0, 1, k, keys, thr, bias_tiles, far_bias, tb)

        merged = _merge(ya_in, attn, w_a_out, w_b_out, gates_ab, l)
        y = _matmul(merged, w_o, layer=l, out_dtype=F32, name="proj_out")
        xs, h2 = _ln_modulate(xs, y, mod[l], ln_g[l], ln_b[l], mod[l], alpha=dn_alpha, gate_row=2, ln_row=0,
                              sc_row=4, sh_row=3, want_h=True)

        w_r = jnp.pad(w_router[l], ((0, 0), (0, LANES - n_exp))).astype(BF16)
        b_r = jnp.pad(b_router[l], (0, LANES - n_exp)).reshape(1, LANES)
        gates = _router(h2, w_r, b_r, n_exp)
        b_dn_pad = jnp.pad(b_dn[l], ((0, LANES - n_exp), (0, 0))).astype(BF16)
        y = _moe_experts(h2, gates, w_gu[l].astype(BF16), b_gu[l], w_dn[l].astype(BF16), b_dn_pad)
        last = l == depth - 1
        xs, hcur = _ln_modulate(xs, y, mod[l], ln_g[l], ln_b[l], mod[min(l + 1, depth - 1)], alpha=dn_alpha,
                                gate_row=5, ln_row=1, sc_row=1, sh_row=0, want_h=not last)
    return xs.reshape(bsz, L, d)
```

```python
import functools
import math

import numpy as np
import jax
import jax.numpy as jnp
from jax import lax
from jax.experimental import pallas as pl
from jax.experimental.pallas import tpu as pltpu

F32 = jnp.float32
BF16 = jnp.bfloat16
I32 = jnp.int32

ATT_HEAD_DIM = 128
IDX_HEADS = 32
IDX_DIM = 64
TOPK_MAX = 256
REL_MAX_DIST = 128
TOP_K_EXPERTS = 4
SWIGLU_LIMIT = 7.0
SWIGLU_ALPHA = 1.702
N_MOD = 6
LN_EPS = 1e-5

LANES = 128
VMEM_LIMIT_BYTES = 56 * 1024 * 1024
S5_CHUNK = 16
NEG_BIG = -1e30
LOG2E = math.log2(math.e)
INT_MIN = -(2 ** 31)


def _cp(*sem):
    return pltpu.CompilerParams(dimension_semantics=sem, vmem_limit_bytes=VMEM_LIMIT_BYTES)


def _tile(n, t):
    t = min(n, t)
    while n % t:
        t //= 2
    return t


def _cond1_kernel(c_ref, w_ref, b_ref, o_ref):
    a = jax.nn.silu(c_ref[...]).astype(BF16)
    t = jnp.dot(a, w_ref[...].astype(BF16), preferred_element_type=F32) + b_ref[...]
    o_ref[...] = jax.nn.silu(t)


def _cond2_kernel(t_ref, w_ref, b_ref, tab_ref, o_ref):
    acc = jnp.dot(t_ref[...].astype(BF16), w_ref[...].astype(BF16), preferred_element_type=F32)
    o_ref[...] = acc[0:1, :] + b_ref[...] + tab_ref[...]


def _conditioning(c, w_c1, b_c1, w_c2, b_c2, mod_table):
    d = c.shape[1]
    r = w_c1.shape[1]
    depth = mod_table.shape[0]
    n = w_c2.shape[1]
    c8 = jnp.broadcast_to(c, (8, d))
    t = pl.pallas_call(
        _cond1_kernel, out_shape=jax.ShapeDtypeStruct((8, r), F32), name="cond1",
        compiler_params=_cp(),
    )(c8, w_c1, b_c1.reshape(1, r))
    tn = _tile(n, 2048)
    return pl.pallas_call(
        _cond2_kernel, grid=(n // tn,),
        in_specs=[pl.BlockSpec((8, r), lambda j: (0, 0)),
                  pl.BlockSpec((r, tn), lambda j: (0, j)),
                  pl.BlockSpec((1, tn), lambda j: (0, j)),
                  pl.BlockSpec((depth, tn), lambda j: (0, j))],
        out_specs=pl.BlockSpec((depth, tn), lambda j: (0, j)),
        out_shape=jax.ShapeDtypeStruct((depth, n), F32), name="cond2",
        compiler_params=_cp("parallel"),
    )(t, w_c2, b_c2.reshape(1, n), mod_table.reshape(depth, n))


def _modulate_kernel(x_ref, mod_ref, h_ref, *, sc_row, sh_row):
    sc = mod_ref[sc_row:sc_row + 1, :]
    sh = mod_ref[sh_row:sh_row + 1, :]
    h_ref[...] = (x_ref[...] * (1.0 + sc) + sh).astype(h_ref.dtype)


def _modulate(x, mod_l, sc_row, sh_row, tm=256):
    m, d = x.shape
    tm = _tile(m, tm)
    return pl.pallas_call(
        functools.partial(_modulate_kernel, sc_row=sc_row, sh_row=sh_row),
        grid=(m // tm,),
        in_specs=[pl.BlockSpec((tm, d), lambda i: (i, 0)),
                  pl.BlockSpec((N_MOD, d), lambda i: (0, 0))],
        out_specs=pl.BlockSpec((tm, d), lambda i: (i, 0)),
        out_shape=jax.ShapeDtypeStruct((m, d), BF16), name="modulate",
        compiler_params=_cp("parallel"),
    )(x, mod_l)


def _ln_kernel(x_ref, y_ref, mod_ref, g_ref, b_ref, nmod_ref, xo_ref, *h_refs,
               alpha, gate_row, ln_row, sc_row, sh_row):
    gate = mod_ref[gate_row:gate_row + 1, :]
    u = alpha * x_ref[...] + (1.0 + gate) * y_ref[...]
    mu = jnp.mean(u, axis=-1, keepdims=True)
    uc = u - mu
    var = jnp.mean(uc * uc, axis=-1, keepdims=True)
    xn = uc * lax.rsqrt(var + LN_EPS) * g_ref[ln_row:ln_row + 1, :] + b_ref[ln_row:ln_row + 1, :]
    xo_ref[...] = xn
    if h_refs:
        sc = nmod_ref[sc_row:sc_row + 1, :]
        sh = nmod_ref[sh_row:sh_row + 1, :]
        h_refs[0][...] = (xn * (1.0 + sc) + sh).astype(BF16)


def _ln_modulate(x, y, mod_l, ln_g_l, ln_b_l, next_mod, *, alpha, gate_row, ln_row, sc_row, sh_row,
                 want_h, tm=256):
    m, d = x.shape
    tm = _tile(m, tm)
    row = pl.BlockSpec((tm, d), lambda i: (i, 0))
    out_shape = [jax.ShapeDtypeStruct((m, d), F32)]
    out_specs = [row]
    if want_h:
        out_shape.append(jax.ShapeDtypeStruct((m, d), BF16))
        out_specs.append(row)
    res = pl.pallas_call(
        functools.partial(_ln_kernel, alpha=alpha, gate_row=gate_row, ln_row=ln_row,
                          sc_row=sc_row, sh_row=sh_row),
        grid=(m // tm,),
        in_specs=[row, row,
                  pl.BlockSpec((N_MOD, d), lambda i: (0, 0)),
                  pl.BlockSpec((2, d), lambda i: (0, 0)),
                  pl.BlockSpec((2, d), lambda i: (0, 0)),
                  pl.BlockSpec((N_MOD, d), lambda i: (0, 0))],
        out_specs=out_specs, out_shape=out_shape, name="ln_modulate",
        compiler_params=_cp("parallel"),
    )(x, y, mod_l, ln_g_l, ln_b_l, next_mod)
    return (res[0], res[1]) if want_h else (res[0], None)


def _mm_kernel(a_ref, w_ref, o_ref, wbf_ref):
    @pl.when(pl.program_id(1) == 0)
    def _():
        wbf_ref[...] = w_ref[...].astype(BF16)

    o_ref[...] = jnp.dot(a_ref[...], wbf_ref[...], preferred_element_type=F32).astype(o_ref.dtype)


def _w_spec(w, layer, k, tn, off=0):
    if w.ndim == 3:
        return pl.BlockSpec((None, k, tn), lambda j, i: (layer, 0, j + off))
    return pl.BlockSpec((k, tn), lambda j, i: (0, j + off))


def _matmul(a, w, *, out_dtype, layer=0, col_off=0, n_cols=None, tm=1024, tn=512, name="matmul"):
    m, k = a.shape
    n_cols = w.shape[-1] - col_off if n_cols is None else n_cols
    tm = _tile(m, tm)
    tn = math.gcd(_tile(n_cols, tn), col_off) if col_off else _tile(n_cols, tn)
    off = col_off // tn
    return pl.pallas_call(
        _mm_kernel, grid=(n_cols // tn, m // tm),
        in_specs=[pl.BlockSpec((tm, k), lambda j, i: (i, 0)),
                  _w_spec(w, layer, k, tn, off)],
        out_specs=pl.BlockSpec((tm, tn), lambda j, i: (i, j)),
        out_shape=jax.ShapeDtypeStruct((m, n_cols), out_dtype),
        scratch_shapes=[pltpu.VMEM((k, tn), BF16)], name=name,
        compiler_params=_cp("parallel", "arbitrary"),
    )(a, w)


def _mm_t_kernel(a_ref, w_ref, o_ref, wt_ref):
    @pl.when(pl.program_id(1) == 0)
    def _():
        wt_ref[...] = jnp.transpose(w_ref[...])

    o_ref[...] = lax.dot_general(wt_ref[...], a_ref[...], (((1,), (1,)), ((), ())),
                                 preferred_element_type=F32).astype(o_ref.dtype)


def _matmul_t(a, w, *, out_dtype, col_off, n_cols, tm=1024, tn=1024, name="matmul_t"):
    m, k = a.shape
    tm = _tile(m, tm)
    tn = math.gcd(_tile(n_cols, tn), col_off) if col_off else _tile(n_cols, tn)
    off = col_off // tn
    return pl.pallas_call(
        _mm_t_kernel, grid=(n_cols // tn, m // tm),
        in_specs=[pl.BlockSpec((tm, k), lambda j, i: (i, 0)),
                  pl.BlockSpec((k, tn), lambda j, i: (0, j + off))],
        out_specs=pl.BlockSpec((tn, tm), lambda j, i: (j, i)),
        out_shape=jax.ShapeDtypeStruct((n_cols, m), out_dtype),
        scratch_shapes=[pltpu.VMEM((tn, k), w.dtype)], name=name,
        compiler_params=_cp("parallel", "arbitrary"),
    )(a, w)


def _glu_kernel(y_ref, yt_ref, w_ref, b_ref, o_ref, wbf_ref):
    @pl.when(pl.program_id(1) == 0)
    def _():
        wbf_ref[...] = w_ref[...].astype(BF16)

    z = jnp.dot(y_ref[...].astype(BF16), wbf_ref[...], preferred_element_type=F32) + b_ref[...]
    o_ref[...] = (yt_ref[...] * jax.nn.sigmoid(z)).astype(o_ref.dtype)


def _glu(y, w, b, layer, tm=1024, tn=512):
    m, k = y.shape
    n = w.shape[-1]
    tm = _tile(m, tm)
    tn = _tile(n, tn)
    return pl.pallas_call(
        _glu_kernel, grid=(n // tn, m // tm),
        in_specs=[pl.BlockSpec((tm, k), lambda j, i: (i, 0)),
                  pl.BlockSpec((tm, tn), lambda j, i: (i, j)),
                  _w_spec(w, layer, k, tn),
                  pl.BlockSpec((1, tn), lambda j, i: (0, j))],
        out_specs=pl.BlockSpec((tm, tn), lambda j, i: (i, j)),
        out_shape=jax.ShapeDtypeStruct((m, n), BF16),
        scratch_shapes=[pltpu.VMEM((k, tn), BF16)], name="s5_glu",
        compiler_params=_cp("parallel", "arbitrary"),
    )(y, y, w, b.reshape(1, n))


def _merge_kernel(a1_ref, a2_ref, w1_ref, w2_ref, ga_ref, gb_ref, o_ref, w1bf_ref, w2bf_ref):
    @pl.when(pl.program_id(1) == 0)
    def _():
        w1bf_ref[...] = w1_ref[...].astype(BF16)
        w2bf_ref[...] = w2_ref[...].astype(BF16)

    ya = jnp.dot(a1_ref[...], w1bf_ref[...], preferred_element_type=F32)
    yb = jnp.dot(a2_ref[...], w2bf_ref[...], preferred_element_type=F32)
    o_ref[...] = (jax.nn.sigmoid(ga_ref[...]) * ya + jax.nn.sigmoid(gb_ref[...]) * yb).astype(o_ref.dtype)


def _merge(a1, a2, w1, w2, gates, layer, tm=1024, tn=512):
    m, k1 = a1.shape
    k2 = a2.shape[1]
    n = w1.shape[-1]
    tm = _tile(m, tm)
    tn = _tile(n, tn)
    nb = n // tn
    return pl.pallas_call(
        _merge_kernel, grid=(nb, m // tm),
        in_specs=[pl.BlockSpec((tm, k1), lambda j, i: (i, 0)),
                  pl.BlockSpec((tm, k2), lambda j, i: (i, 0)),
                  _w_spec(w1, layer, k1, tn),
                  _w_spec(w2, layer, k2, tn),
                  pl.BlockSpec((tm, tn), lambda j, i: (i, j)),
                  pl.BlockSpec((tm, tn), lambda j, i: (i, j + nb))],
        out_specs=pl.BlockSpec((tm, tn), lambda j, i: (i, j)),
        out_shape=jax.ShapeDtypeStruct((m, n), BF16),
        scratch_shapes=[pltpu.VMEM((k1, tn), BF16), pltpu.VMEM((k2, tn), BF16)], name="branch_merge",
        compiler_params=_cp("parallel", "arbitrary"),
    )(a1, a2, w1, w2, gates, gates)


def _s5_discretize(lam_re, lam_im, b_re, b_im, c_re, c_im, d, log_step, t_chunk):
    hp = lax.Precision.HIGHEST
    g, p, h = b_re.shape
    na = LANES // h
    nx = g // na
    assert 2 * p == LANES
    lam = lax.complex(jnp.minimum(lam_re, -1e-4), lam_im)
    dt = jnp.exp(log_step)[:, None]
    lam_bar = jnp.exp(lam * dt)
    b_bar = ((lam_bar - 1.0) / lam)[:, :, None] * lax.complex(b_re, b_im)
    c = lax.complex(c_re, c_im)
    tau = jnp.arange(t_chunk + 1, dtype=F32)
    pw = jnp.exp((lam * dt)[:, None, :] * tau[None, :, None].astype(jnp.complex64))
    cp = c[:, None, :, :] * pw[:, :, None, :]
    kern = jnp.einsum('gtop,gpi->gtoi', cp[:, :t_chunk], b_bar, precision=hp).real
    kio = jnp.transpose(kern.reshape(nx, na, t_chunk, h, h), (0, 2, 1, 4, 3))
    kt = jnp.tile(kio.reshape(nx, t_chunk, LANES, h), (1, 1, 1, na))
    wcx = pw[:, t_chunk - 1 - np.arange(t_chunk)][:, :, :, None] * b_bar[:, None, :, :]
    wcx = jnp.transpose(wcx, (0, 1, 3, 2))
    w_ri = jnp.concatenate([wcx.real, wcx.imag], axis=-1).reshape(nx, na, t_chunk, h, 2 * p)
    wc = jnp.transpose(w_ri, (0, 2, 1, 3, 4)).reshape(nx, t_chunk, LANES, 2 * p)
    cl = jnp.transpose(cp[:, 1:t_chunk + 1], (0, 3, 1, 2))
    vc = jnp.concatenate([cl.real, -cl.imag], axis=1).reshape(nx, na, 2 * p, t_chunk * h)
    dd = jnp.broadcast_to(d.reshape(nx, 1, 1, na, h), (nx, 1, t_chunk, na, h)).reshape(nx, 1, t_chunk * LANES)
    a = pw[:, t_chunk]
    a1 = jnp.concatenate([a.real, a.real], axis=-1).reshape(1, g * 2 * p)
    a2 = jnp.concatenate([-a.imag, a.imag], axis=-1).reshape(1, g * 2 * p)
    return kt, wc, vc, dd, a1, a2


def _chunk_rows(u_ref, t, nc):
    return jnp.concatenate([u_ref[pl.ds(s, nc, stride=t), :] for s in range(t)], axis=1)


def _s5_state_in_kernel(u_ref, wc_ref, xa_ref, xb_ref, wa_ref, wb_ref, *, t, nc, h):
    na = LANES // h
    row_group = lax.broadcasted_iota(I32, (LANES, LANES), 0) // h
    for s in range(t):
        blk = wc_ref[s]
        swapped = pltpu.roll(blk, LANES // 2, axis=1)
        for b in range(na):
            own = row_group == b
            dst = (slice(s * LANES, (s + 1) * LANES), slice(b * LANES, (b + 1) * LANES))
            wa_ref[dst] = jnp.where(own, blk, 0.0).astype(BF16)
            wb_ref[dst] = jnp.where(own, swapped, 0.0).astype(BF16)
    ucat = _chunk_rows(u_ref, t, nc).astype(BF16)
    xa_ref[...] = jnp.dot(ucat, wa_ref[...], preferred_element_type=F32)
    xb_ref[...] = jnp.dot(ucat, wb_ref[...], preferred_element_type=F32)


def _s5_scan_kernel(xa_ref, xb_ref, a1_ref, a2_ref, s_ref, *, unroll):
    a1 = a1_ref[...]
    a2 = a2_ref[...]
    nc = xa_ref.shape[0]

    def body(cb, carry):
        s_a, s_b = carry
        base = pl.multiple_of(cb * unroll, unroll)
        xa = xa_ref[pl.ds(base, unroll), :]
        xb = xb_ref[pl.ds(base, unroll), :]
        rows = []
        for k in range(unroll):
            rows.append(s_a)
            s_a, s_b = (a1 * s_a + a2 * s_b + xa[k:k + 1, :],
                        a1 * s_b - a2 * s_a + xb[k:k + 1, :])
        s_ref[pl.ds(base, unroll), :] = jnp.concatenate(rows, axis=0)
        return s_a, s_b

    z = jnp.zeros(a1.shape, F32)
    lax.fori_loop(0, nc // unroll, body, (z, z))


def _s5_out_kernel(u_ref, sp_ref, kt_ref, vc_ref, dd_ref, y_ref, bd_ref, vm_ref, *, t, nc, h, n_parts):
    part = pl.program_id(1)
    tp = t // n_parts
    na = LANES // h
    w = tp * LANES
    same_group = (lax.broadcasted_iota(I32, (LANES, LANES), 0) // h
                  == lax.broadcasted_iota(I32, (LANES, LANES), 1) // h)
    row = lax.broadcasted_iota(I32, (LANES, w), 0)
    col = lax.broadcasted_iota(I32, (LANES, w), 1)
    landing_group = jnp.where((row // h == col // LANES) & (row % h == col % h), (col // h) % na, -1)
    ucat = _chunk_rows(u_ref, t, nc)
    for k in range(n_parts):
        @pl.when(part == k)
        def _(k=k):
            live = (k + 1) * tp
            for s in range(live):
                for tt in range(tp):
                    lag = k * tp + tt - s
                    dst = (slice(s * LANES, (s + 1) * LANES), slice(tt * LANES, (tt + 1) * LANES))
                    if lag >= 0:
                        bd_ref[dst] = jnp.where(same_group, kt_ref[lag], 0.0).astype(BF16)
                    else:
                        bd_ref[dst] = jnp.zeros((LANES, LANES), BF16)
            for a in range(na):
                place = jnp.where(landing_group == a, 1.0, 0.0).astype(BF16)
                src = vc_ref[a][:, k * tp * h:(k + 1) * tp * h].astype(BF16)
                vm_ref[a * LANES:(a + 1) * LANES, :] = jnp.dot(src, place, preferred_element_type=F32).astype(BF16)
            y = jnp.dot(ucat[:, :live * LANES].astype(BF16), bd_ref[:live * LANES, :], preferred_element_type=F32)
            y = y + jnp.dot(sp_ref[...].astype(BF16), vm_ref[...], preferred_element_type=F32)
            y = jax.nn.gelu(y + dd_ref[...] * ucat[:, k * w:(k + 1) * w])
            for s in range(tp):
                y_ref[pl.ds(k * tp + s, nc, stride=t), :] = y[:, s * LANES:(s + 1) * LANES]


def _s5_branch(u, prm):
    kt, wc, vc, dd, a1, a2 = prm
    L, width = u.shape
    nx, na, _, th = vc.shape
    t = kt.shape[1]
    h = th // t
    tah = t * LANES
    sw = na * LANES
    nc = L // t
    u_spec = pl.BlockSpec((L, LANES), lambda x: (0, x))
    st_spec = pl.BlockSpec((nc, sw), lambda x: (0, x))
    xa, xb = pl.pallas_call(
        functools.partial(_s5_state_in_kernel, t=t, nc=nc, h=h), grid=(nx,),
        in_specs=[u_spec,
                  pl.BlockSpec((None, t, LANES, LANES), lambda x: (x, 0, 0, 0))],
        out_specs=[st_spec, st_spec],
        out_shape=[jax.ShapeDtypeStruct((nc, nx * sw), F32)] * 2,
        scratch_shapes=[pltpu.VMEM((tah, sw), BF16), pltpu.VMEM((tah, sw), BF16)], name="s5_state_in",
        compiler_params=_cp("parallel"),
    )(u, wc)
    ws = _tile(nx * sw, 2048)
    unroll = _tile(nc, 8)
    sc_spec = pl.BlockSpec((nc, ws), lambda x: (0, x))
    a_spec = pl.BlockSpec((1, ws), lambda x: (0, x))
    sp = pl.pallas_call(
        functools.partial(_s5_scan_kernel, unroll=unroll), grid=(nx * sw // ws,),
        in_specs=[sc_spec, sc_spec, a_spec, a_spec],
        out_specs=sc_spec,
        out_shape=jax.ShapeDtypeStruct((nc, nx * sw), F32), name="s5_scan",
        compiler_params=_cp("parallel"),
    )(xa, xb, a1, a2)
    n_parts = 2
    tw = tah // n_parts
    return pl.pallas_call(
        functools.partial(_s5_out_kernel, t=t, nc=nc, h=h, n_parts=n_parts), grid=(nx, n_parts),
        in_specs=[pl.BlockSpec((L, LANES), lambda x, k: (0, x)),
                  pl.BlockSpec((nc, sw), lambda x, k: (0, x)),
                  pl.BlockSpec((None, t, LANES, LANES), lambda x, k: (x, 0, 0, 0)),
                  pl.BlockSpec((None, na, LANES, th), lambda x, k: (x, 0, 0, 0)),
                  pl.BlockSpec((None, 1, tw), lambda x, k: (x, 0, k))],
        out_specs=pl.BlockSpec((L, LANES), lambda x, k: (0, x)),
        out_shape=jax.ShapeDtypeStruct((L, width), F32),
        scratch_shapes=[pltpu.VMEM((tah, tw), BF16), pltpu.VMEM((sw, tw), BF16)], name="s5_out",
        compiler_params=_cp("parallel", "arbitrary"),
    )(u, sp, kt, vc, dd)


def _t5_bucket_starts(n_buckets):
    exact = n_buckets // 2
    dist = np.arange(0, 4 * REL_MAX_DIST).astype(np.int64)
    far = exact + (np.log(np.maximum(dist, 1).astype(np.float32) / np.float32(exact))
                   / np.float32(math.log(REL_MAX_DIST / exact)) * np.float32(n_buckets - exact)).astype(np.int64)
    bucket = np.where(dist < exact, dist, np.minimum(far, n_buckets - 1))
    return [int(np.argmax(bucket >= b)) for b in range(1, n_buckets)]


def _bias_tile_kernel(rb_ref, o_ref, *, tb, starts):
    delta = jnp.where(pl.program_id(0) == 3, -1, pl.program_id(0))
    head = pl.program_id(1)
    key = lax.broadcasted_iota(I32, (tb, tb), 0)
    qry = lax.broadcasted_iota(I32, (tb, tb), 1)
    dist = delta * tb + qry - key
    bias = jnp.full((tb, tb), rb_ref[0, head], F32)
    for b, start in enumerate(starts, start=1):
        bias = jnp.where(dist >= start, rb_ref[b, head], bias)
    o_ref[...] = jnp.where(dist >= 0, bias * LOG2E, NEG_BIG)


def _bias_tiles(rel_bias, tb):
    n_buckets, heads = rel_bias.shape
    starts = _t5_bucket_starts(n_buckets)
    assert 2 * tb - (tb - 1) >= starts[-1]
    return pl.pallas_call(
        functools.partial(_bias_tile_kernel, tb=tb, starts=starts), grid=(4, heads),
        in_specs=[pl.BlockSpec(memory_space=pltpu.SMEM)],
        out_specs=pl.BlockSpec((None, None, tb, tb), lambda dl, hd: (dl, hd, 0, 0)),
        out_shape=jax.ShapeDtypeStruct((4, heads, tb, tb), F32), name="rel_bias_tiles",
        compiler_params=_cp("parallel", "parallel"),
    )(rel_bias)


def _indexer_kernel(qt_ref, k2_ref, wt_ref, keys_ref, thr_ref, *, tq, ck, n_heads, topk, w_scale):
    i = pl.program_id(0)
    n_blocks = keys_ref.shape[0] // tq
    n_live = (i + 1) * (tq // ck)
    wv = wt_ref[...] * w_scale

    def score_chunk(c, carry):
        row = pl.multiple_of(c * ck, ck)
        kk = k2_ref[c]
        acc = jnp.zeros((ck, tq), F32)
        for pr in range(n_heads // 2):
            s2 = jnp.dot(kk, qt_ref[pr * LANES:(pr + 1) * LANES, :], preferred_element_type=F32)
            acc = acc + jnp.maximum(s2[:ck], 0.0) * wv[2 * pr:2 * pr + 1, :]
            acc = acc + jnp.maximum(s2[ck:], 0.0) * wv[2 * pr + 1:2 * pr + 2, :]
        kpos = row + lax.broadcasted_iota(I32, (ck, tq), 0)
        qpos = i * tq + lax.broadcasted_iota(I32, (ck, tq), 1)
        acc = jnp.where(acc == 0.0, 0.0, acc)
        score = jnp.where(kpos <= qpos, acc, -jnp.inf)
        bits = pltpu.bitcast(score, I32)
        keys_ref[pl.ds(row, ck), :] = jnp.where(bits < 0, bits ^ 0x7FFFFFFF, bits)
        return carry

    lax.fori_loop(0, n_live, score_chunk, 0)

    def fill_block(b, carry):
        row = pl.multiple_of(b * tq, tq)
        keys_ref[pl.ds(row, tq), :] = jnp.full((tq, tq), INT_MIN, I32)
        return carry

    lax.fori_loop(i + 1, n_blocks, fill_block, 0)

    acc_rows = 32

    def count_ge(cand):
        def body(b, cnt):
            row = pl.multiple_of(b * tq, tq)
            hit = jnp.where(keys_ref[pl.ds(row, tq), :] >= cand, 1.0, 0.0)
            return cnt + jnp.sum(hit.reshape(tq // acc_rows, acc_rows, tq), axis=0)
        cnt = lax.fori_loop(0, i + 1, body, jnp.zeros((acc_rows, tq), F32))
        return jnp.sum(cnt, axis=0, keepdims=True)

    kf = float(topk)
    zero = jnp.zeros((1, tq), I32)
    c_zero = count_ge(zero)
    thr0 = jnp.where(c_zero >= kf, zero, jnp.full((1, tq), INT_MIN, I32))
    cnt0 = jnp.where(c_zero >= kf, c_zero, ((i + 1) * tq).astype(F32))

    def undecided(st):
        it, _, cnt = st
        return (it < 31) & (jnp.max(cnt) > kf)

    def bit_step(st):
        it, thr, cnt = st
        cand = thr | jnp.left_shift(jnp.int32(1), 30 - it)
        c = count_ge(cand)
        keep = c >= kf
        return it + 1, jnp.where(keep, cand, thr), jnp.where(keep, c, cnt)

    _, thr, cnt = lax.while_loop(undecided, bit_step, (jnp.int32(0), thr0, cnt0))

    n_keys = keys_ref.shape[0]

    def position_cutoff(_):
        need = kf - count_ge(thr + 1)

        def count_eq_upto(pos):
            def body(b, acc):
                row = pl.multiple_of(b * tq, tq)
                kpos = row + lax.broadcasted_iota(I32, (tq, tq), 0)
                hit = jnp.where(keys_ref[pl.ds(row, tq), :] == thr, jnp.where(kpos <= pos, 1.0, 0.0), 0.0)
                return acc + jnp.sum(hit.reshape(tq // acc_rows, acc_rows, tq), axis=0)
            acc = lax.fori_loop(0, i + 1, body, jnp.zeros((acc_rows, tq), F32))
            return jnp.sum(acc, axis=0, keepdims=True)

        def halve(_, bounds):
            lo, hi = bounds
            mid = (lo + hi) >> 1
            enough = count_eq_upto(mid) >= need
            return jnp.where(enough, lo, mid), jnp.where(enough, mid, hi)

        lo0 = jnp.full((1, tq), -1, I32)
        hi0 = jnp.full((1, tq), n_keys - 1, I32)
        _, hi = lax.fori_loop(0, max(1, (n_keys - 1).bit_length()), halve, (lo0, hi0))
        return jnp.where(cnt > kf, hi, n_keys)

    jcut = lax.cond(jnp.max(cnt) > kf, position_cutoff, lambda _: jnp.full((1, tq), n_keys, I32), 0)
    sel_row = lax.broadcasted_iota(I32, thr_ref.shape, 0)
    thr_ref[...] = jnp.where(sel_row == 0, thr, jnp.where(sel_row == 1, jcut, 0))


def _indexer(qi_t, row_off, k2, w_t, topk, tq=256):
    L = qi_t.shape[1]
    n_heads = w_t.shape[0]
    qw = n_heads * IDX_DIM
    assert row_off % qw == 0
    tq = _tile(L, tq)
    ck = k2.shape[1] // 2
    assert tq % ck == 0
    w_scale = (n_heads ** -0.5) * (IDX_DIM ** -0.5)
    return pl.pallas_call(
        functools.partial(_indexer_kernel, tq=tq, ck=ck, n_heads=n_heads, topk=topk, w_scale=w_scale),
        grid=(L // tq,),
        in_specs=[pl.BlockSpec((qw, tq), lambda i: (row_off // qw, i)),
                  pl.BlockSpec(k2.shape, lambda i: (0, 0, 0)),
                  pl.BlockSpec((n_heads, tq), lambda i: (0, i))],
        out_specs=[pl.BlockSpec((L, tq), lambda i: (0, i)),
                   pl.BlockSpec((8, tq), lambda i: (0, i))],
        out_shape=[jax.ShapeDtypeStruct((L, L), I32), jax.ShapeDtypeStruct((8, L), I32)],
        name="dsa_indexer", compiler_params=_cp("parallel"),
    )(qi_t, k2, w_t)


def _attn_kernel(qi_ref, kj_ref, far_ref, qt_ref, k_ref, vt_ref, keys_ref, thr_ref, bias_a_ref, bias_b_ref, o_ref,
                 m_ref, l_ref, acc_ref, mb_ref, *, heads, dh, scale2):
    step = pl.program_id(0)
    qi = qi_ref[step]
    kj = kj_ref[step]

    @pl.when(kj == 0)
    def _():
        m_ref[...] = jnp.full(m_ref.shape, -jnp.inf, F32)
        l_ref[...] = jnp.zeros(l_ref.shape, F32)
        acc_ref[...] = jnp.zeros(acc_ref.shape, F32)

    thr = thr_ref[0:1, :]
    kpos = kj * keys_ref.shape[0] + lax.broadcasted_iota(I32, keys_ref.shape, 0)
    bar = jnp.where(kpos <= thr_ref[1:2, :], thr, thr + 1)
    mb_ref[...] = jnp.where(keys_ref[...] >= bar, 0.0, NEG_BIG)

    def scores(h):
        hs = slice(h * dh, (h + 1) * dh)
        return jnp.dot(k_ref[:, hs], qt_ref[hs, :], preferred_element_type=F32)

    def fold(reduce_fn, a, rows=32):
        return reduce_fn(a.reshape(a.shape[0] // rows, rows, a.shape[1]), axis=0)

    def run(near):
        ahead = 4
        pending = [scores(h) for h in range(min(ahead, heads))]
        for h in range(heads):
            hs = slice(h * dh, (h + 1) * dh)
            x = pending.pop(0) * scale2 + mb_ref[...]
            if h + ahead < heads:
                pending.append(scores(h + ahead))
            if near:
                x = x + jnp.concatenate([bias_a_ref[h], bias_b_ref[h]], axis=0)
            far = 0.0 if near else far_ref[h]
            m_prev = m_ref[h]
            m_new = jnp.maximum(m_prev, jnp.max(fold(jnp.max, x), axis=0, keepdims=True) + far)
            alpha = jnp.exp2(m_prev - m_new)
            p = jnp.exp2(x - (m_new[0:1, :] - far))
            l_ref[h] = alpha * l_ref[h] + jnp.sum(fold(jnp.sum, p), axis=0, keepdims=True)
            m_ref[h] = m_new
            acc_ref[hs, :] = alpha[0:1, :] * acc_ref[hs, :] + jnp.dot(vt_ref[hs, :], p.astype(BF16),
                                                                      preferred_element_type=F32)

    @pl.when(qi - 2 * kj - 1 < 2)
    def _():
        run(True)

    @pl.when(qi - 2 * kj - 1 >= 2)
    def _():
        run(False)

    @pl.when(kj == qi // 2)
    def _():
        for h in range(heads):
            hs = slice(h * dh, (h + 1) * dh)
            o_ref[:, hs] = jnp.transpose(acc_ref[hs, :] / l_ref[h][0:1, :]).astype(o_ref.dtype)


def _attention(q_t, v_t, k, keys, thr, bias_tiles, far_bias, tb):
    L = k.shape[0]
    heads = bias_tiles.shape[1]
    dh = ATT_HEAD_DIM
    aw = heads * dh
    nb = L // tb
    assert nb % 2 == 0
    pairs = [(i, j) for i in range(nb) for j in range(i // 2 + 1)]
    qi = jnp.asarray(np.array([p[0] for p in pairs], np.int32))
    kj = jnp.asarray(np.array([p[1] for p in pairs], np.int32))

    def tile_of(behind):
        return jnp.where(behind < 0, 3, jnp.minimum(behind, 2))

    bias_spec_a = pl.BlockSpec((None, heads, tb, tb), lambda s, qi, kj: (tile_of(qi[s] - 2 * kj[s]), 0, 0, 0))
    bias_spec_b = pl.BlockSpec((None, heads, tb, tb), lambda s, qi, kj: (tile_of(qi[s] - 2 * kj[s] - 1), 0, 0, 0))
    grid_spec = pltpu.PrefetchScalarGridSpec(
        num_scalar_prefetch=2, grid=(len(pairs),),
        in_specs=[pl.BlockSpec(memory_space=pltpu.SMEM),
                  pl.BlockSpec((aw, tb), lambda s, qi, kj: (0, qi[s])),
                  pl.BlockSpec((2 * tb, aw), lambda s, qi, kj: (kj[s], 0)),
                  pl.BlockSpec((aw, 2 * tb), lambda s, qi, kj: (0, kj[s])),
                  pl.BlockSpec((2 * tb, tb), lambda s, qi, kj: (kj[s], qi[s])),
                  pl.BlockSpec((8, tb), lambda s, qi, kj: (0, qi[s])),
                  bias_spec_a, bias_spec_b],
        out_specs=pl.BlockSpec((tb, aw), lambda s, qi, kj: (qi[s], 0)),
        scratch_shapes=[pltpu.VMEM((heads, 8, tb), F32),
                        pltpu.VMEM((heads, 8, tb), F32),
                        pltpu.VMEM((aw, tb), F32),
                        pltpu.VMEM((2 * tb, tb), F32)])
    return pl.pallas_call(
        functools.partial(_attn_kernel, heads=heads, dh=dh, scale2=dh ** -0.5 * LOG2E),
        grid_spec=grid_spec, out_shape=jax.ShapeDtypeStruct((L, aw), BF16), name="dsa_attention",
        compiler_params=_cp("arbitrary"),
    )(qi, kj, far_bias, q_t, k, v_t, keys, thr, bias_tiles, bias_tiles)


def _router_kernel(h_ref, w_ref, b_ref, g_ref, *, n_exp, top_k):
    logits = jnp.dot(h_ref[...], w_ref[...], preferred_element_type=F32) + b_ref[...]
    lane = lax.broadcasted_iota(I32, logits.shape, 1).astype(F32)
    work = jnp.where(lane < n_exp, logits, -jnp.inf)
    gates = jnp.zeros(logits.shape, F32)
    denom = None
    top = None
    for r in range(top_k):
        m = jnp.max(work, axis=1, keepdims=True)
        first = jnp.min(jnp.where(work == m, lane, float(LANES)), axis=1, keepdims=True)
        sel = lane == first
        if r == 0:
            top = m
        e = jnp.exp(m - top)
        gates = gates + jnp.where(sel, e, 0.0)
        denom = e if denom is None else denom + e
        work = jnp.where(sel, -jnp.inf, work)
    g_ref[...] = gates / denom


def _router(h, w_pad, b_pad, n_exp, tm=512):
    m, d = h.shape
    tm = _tile(m, tm)
    return pl.pallas_call(
        functools.partial(_router_kernel, n_exp=n_exp, top_k=TOP_K_EXPERTS), grid=(m // tm,),
        in_specs=[pl.BlockSpec((tm, d), lambda i: (i, 0)),
                  pl.BlockSpec((d, LANES), lambda i: (0, 0)),
                  pl.BlockSpec((1, LANES), lambda i: (0, 0))],
        out_specs=pl.BlockSpec((tm, LANES), lambda i: (i, 0)),
        out_shape=jax.ShapeDtypeStruct((m, LANES), F32), name="moe_router",
        compiler_params=_cp("parallel"),
    )(h, w_pad, b_pad)


def _moe_kernel(h_ref, g_ref, wgu_ref, bgu_ref, wdn_ref, bdn_ref, o_ref, *, ff, eb):
    step = pl.program_id(1)
    gates = g_ref[...]

    @pl.when(step == 0)
    def _():
        o_ref[...] = jnp.dot(gates.astype(BF16), bdn_ref[...], preferred_element_type=F32)

    lane = lax.broadcasted_iota(I32, gates.shape, 1)
    for k in range(eb):
        gu = jnp.dot(h_ref[...], wgu_ref[k], preferred_element_type=F32) + bgu_ref[k]
        gate = jnp.minimum(gu[:, :ff], SWIGLU_LIMIT)
        up = jnp.clip(gu[:, ff:], -SWIGLU_LIMIT, SWIGLU_LIMIT)
        act = (up + 1.0) * gate * jax.nn.sigmoid(SWIGLU_ALPHA * gate)
        ge = jnp.sum(jnp.where(lane == step * eb + k, gates, 0.0), axis=1, keepdims=True)
        o_ref[...] += jnp.dot((act * ge).astype(BF16), wdn_ref[k], preferred_element_type=F32)


def _moe_experts(h, gates, w_gu, b_gu, w_dn, b_dn_pad, tm=512, eb=2):
    m, d = h.shape
    n_exp, _, ff2 = w_gu.shape
    tm = _tile(m, tm)
    eb = _tile(n_exp, eb)
    return pl.pallas_call(
        functools.partial(_moe_kernel, ff=ff2 // 2, eb=eb), grid=(m // tm, n_exp // eb),
        in_specs=[pl.BlockSpec((tm, d), lambda i, e: (i, 0)),
                  pl.BlockSpec((tm, LANES), lambda i, e: (i, 0)),
                  pl.BlockSpec((eb, d, ff2), lambda i, e: (e, 0, 0)),
                  pl.BlockSpec((eb, 1, ff2), lambda i, e: (e, 0, 0)),
                  pl.BlockSpec((eb, ff2 // 2, d), lambda i, e: (e, 0, 0)),
                  pl.BlockSpec((LANES, d), lambda i, e: (0, 0))],
        out_specs=pl.BlockSpec((tm, d), lambda i, e: (i, 0)),
        out_shape=jax.ShapeDtypeStruct((m, d), F32), name="moe_experts",
        compiler_params=_cp("parallel", "arbitrary"),
    )(h, gates, w_gu, b_gu.reshape(n_exp, 1, ff2), w_dn, b_dn_pad)


def kernel(x, c, w_in, w_a_out, w_b_out, w_o, w_glu, b_glu, s5_lam_re, s5_lam_im, s5_b_re, s5_b_im, s5_c_re, s5_c_im, s5_d, s5_log_step, rel_bias, w_router, b_router, w_gu, b_gu, w_dn, b_dn, w_c1, b_c1, w_c2, b_c2, mod_table, ln_g, ln_b):
    bsz, L, d = x.shape
    assert bsz == 1 and c.shape[0] == 1
    depth = w_in.shape[0]
    g, p, h = s5_b_re.shape[1:]
    s5w = g * h
    heads = rel_bias.shape[1]
    aw = heads * ATT_HEAD_DIM
    iw = IDX_HEADS * IDX_DIM
    n_exp = w_router.shape[2]
    assert w_in.shape[2] == s5w + 3 * aw + iw + IDX_DIM + IDX_HEADS + 2 * d
    assert n_exp <= LANES and IDX_DIM + IDX_HEADS <= LANES and 2 * IDX_DIM == LANES
    o_q, o_k, o_v, o_qi = s5w, s5w + aw, s5w + 2 * aw, s5w + 3 * aw
    o_ki = s5w + 3 * aw + iw
    o_gate = o_ki + IDX_DIM + IDX_HEADS
    topk = min(TOPK_MAX, L // 4)
    tb = _tile(L, 256)
    ck = _tile(L, 128)
    dn_alpha = (2 * depth) ** 0.25

    mod = _conditioning(c, w_c1, b_c1, w_c2, b_c2, mod_table).reshape(depth, N_MOD, d)
    bias_tiles = _bias_tiles(rel_bias, tb)
    far_bias = rel_bias[-1] * LOG2E
    assert o_ki % LANES == 0 and (g * h) % LANES == 0 and LANES % h == 0

    xs = x.reshape(L, d)
    hcur = _modulate(xs, mod[0], sc_row=1, sh_row=0)
    for l in range(depth):
        w_l = w_in[l]
        w_main = w_l[:, :o_ki].astype(BF16)
        w_gate = w_l[:, o_gate:].astype(BF16)
        w_misc = jnp.pad(w_l[:, o_ki:o_gate], ((0, 0), (0, LANES - IDX_DIM - IDX_HEADS))).astype(BF16)
        u = _matmul(hcur, w_main, out_dtype=F32, col_off=0, n_cols=s5w, name="proj_u")
        k = _matmul(hcur, w_main, out_dtype=BF16, col_off=o_k, n_cols=aw, name="proj_k")
        gates_ab = _matmul(hcur, w_gate, out_dtype=F32, name="proj_gates")
        misc = _matmul(hcur, w_misc, out_dtype=F32, tn=LANES, name="proj_misc")
        q_t = _matmul_t(hcur, w_main, out_dtype=BF16, col_off=o_q, n_cols=aw, name="proj_q_t")
        vqi_t = _matmul_t(hcur, w_main, out_dtype=BF16, col_off=o_v, n_cols=aw + iw, name="proj_vqi_t")

        prm = _s5_discretize(s5_lam_re[l], s5_lam_im[l], s5_b_re[l], s5_b_im[l], s5_c_re[l], s5_c_im[l],
                             s5_d[l], s5_log_step[l], S5_CHUNK)
        y_s5 = _s5_branch(u, prm)
        ya_in = _glu(y_s5, w_glu, b_glu[l], l)

        k_idx = misc[:, :IDX_DIM].astype(BF16)
        zeros = jnp.zeros_like(k_idx)
        k_even = jnp.concatenate([k_idx, zeros], axis=1).reshape(L // ck, ck, LANES)
        k_odd = jnp.concatenate([zeros, k_idx], axis=1).reshape(L // ck, ck, LANES)
        k2 = jnp.concatenate([k_even, k_odd], axis=1)
        w_t = jnp.transpose(misc[:, IDX_DIM:IDX_DIM + IDX_HEADS])
        keys, thr = _indexer(vqi_t, aw, k2, w_t, topk)
        attn = _attention(q_t, vqi_t, k, keys, thr, bias_tiles, far_bias, tb)

        merged = _merge(ya_in, attn, w_a_out, w_b_out, gates_ab, l)
        y = _matmul(merged, w_o, layer=l, out_dtype=F32, name="proj_out")
        xs, h2 = _ln_modulate(xs, y, mod[l], ln_g[l], ln_b[l], mod[l], alpha=dn_alpha, gate_row=2, ln_row=0,
                              sc_row=4, sh_row=3, want_h=True)

        w_r = jnp.pad(w_router[l], ((0, 0), (0, LANES - n_exp))).astype(BF16)
        b_r = jnp.pad(b_router[l], (0, LANES - n_exp)).reshape(1, LANES)
        gates = _router(h2, w_r, b_r, n_exp)
        b_dn_pad = jnp.pad(b_dn[l], ((0, LANES - n_exp), (0, 0))).astype(BF16)
        y = _moe_experts(h2, gates, w_gu[l].astype(BF16), b_gu[l], w_dn[l].astype(BF16), b_dn_pad)
        last = l == depth - 1
        xs, hcur = _ln_modulate(xs, y, mod[l], ln_g[l], ln_b[l], mod[min(l + 1, depth - 1)], alpha=dn_alpha,
                                gate_row=5, ln_row=1, sc_row=1, sh_row=0, want_h=not last)
    return xs.reshape(bsz, L, d)
```

```python
import functools
import math

import numpy as np
import jax
import jax.numpy as jnp
from jax import lax
from jax.experimental import pallas as pl
from jax.experimental.pallas import tpu as pltpu

F32 = jnp.float32
BF16 = jnp.bfloat16
I32 = jnp.int32

ATT_HEAD_DIM = 128
IDX_HEADS = 32
IDX_DIM = 64
TOPK_MAX = 256
REL_MAX_DIST = 128
TOP_K_EXPERTS = 4
SWIGLU_LIMIT = 7.0
SWIGLU_ALPHA = 1.702
N_MOD = 6
LN_EPS = 1e-5

LANES = 128
VMEM_LIMIT_BYTES = 56 * 1024 * 1024
S5_CHUNK = 16
NEG_BIG = -1e30
LOG2E = math.log2(math.e)
INT_MIN = -(2 ** 31)


def _cp(*sem):
    return pltpu.CompilerParams(dimension_semantics=sem, vmem_limit_bytes=VMEM_LIMIT_BYTES)


def _tile(n, t):
    t = min(n, t)
    while n % t:
        t //= 2
    return t


def _cond1_kernel(c_ref, w_ref, b_ref, o_ref):
    a = jax.nn.silu(c_ref[...]).astype(BF16)
    t = jnp.dot(a, w_ref[...].astype(BF16), preferred_element_type=F32) + b_ref[...]
    o_ref[...] = jax.nn.silu(t)


def _cond2_kernel(t_ref, w_ref, b_ref, tab_ref, o_ref):
    acc = jnp.dot(t_ref[...].astype(BF16), w_ref[...].astype(BF16), preferred_element_type=F32)
    o_ref[...] = acc[0:1, :] + b_ref[...] + tab_ref[...]


def _conditioning(c, w_c1, b_c1, w_c2, b_c2, mod_table):
    d = c.shape[1]
    r = w_c1.shape[1]
    depth = mod_table.shape[0]
    n = w_c2.shape[1]
    c8 = jnp.broadcast_to(c, (8, d))
    t = pl.pallas_call(
        _cond1_kernel, out_shape=jax.ShapeDtypeStruct((8, r), F32), name="cond1",
        compiler_params=_cp(),
    )(c8, w_c1, b_c1.reshape(1, r))
    tn = _tile(n, 2048)
    return pl.pallas_call(
        _cond2_kernel, grid=(n // tn,),
        in_specs=[pl.BlockSpec((8, r), lambda j: (0, 0)),
                  pl.BlockSpec((r, tn), lambda j: (0, j)),
                  pl.BlockSpec((1, tn), lambda j: (0, j)),
                  pl.BlockSpec((depth, tn), lambda j: (0, j))],
        out_specs=pl.BlockSpec((depth, tn), lambda j: (0, j)),
        out_shape=jax.ShapeDtypeStruct((depth, n), F32), name="cond2",
        compiler_params=_cp("parallel"),
    )(t, w_c2, b_c2.reshape(1, n), mod_table.reshape(depth, n))


def _modulate_kernel(x_ref, mod_ref, h_ref, *, sc_row, sh_row):
    sc = mod_ref[sc_row:sc_row + 1, :]
    sh = mod_ref[sh_row:sh_row + 1, :]
    h_ref[...] = (x_ref[...] * (1.0 + sc) + sh).astype(h_ref.dtype)


def _modulate(x, mod_l, sc_row, sh_row, tm=256):
    m, d = x.shape
    tm = _tile(m, tm)
    return pl.pallas_call(
        functools.partial(_modulate_kernel, sc_row=sc_row, sh_row=sh_row),
        grid=(m // tm,),
        in_specs=[pl.BlockSpec((tm, d), lambda i: (i, 0)),
                  pl.BlockSpec((N_MOD, d), lambda i: (0, 0))],
        out_specs=pl.BlockSpec((tm, d), lambda i: (i, 0)),
        out_shape=jax.ShapeDtypeStruct((m, d), BF16), name="modulate",
        compiler_params=_cp("parallel"),
    )(x, mod_l)


def _ln_kernel(x_ref, y_ref, mod_ref, g_ref, b_ref, nmod_ref, xo_ref, *h_refs,
               alpha, gate_row, ln_row, sc_row, sh_row):
    gate = mod_ref[gate_row:gate_row + 1, :]
    u = alpha * x_ref[...] + (1.0 + gate) * y_ref[...]
    mu = jnp.mean(u, axis=-1, keepdims=True)
    uc = u - mu
    var = jnp.mean(uc * uc, axis=-1, keepdims=True)
    xn = uc * lax.rsqrt(var + LN_EPS) * g_ref[ln_row:ln_row + 1, :] + b_ref[ln_row:ln_row + 1, :]
    xo_ref[...] = xn
    if h_refs:
        sc = nmod_ref[sc_row:sc_row + 1, :]
        sh = nmod_ref[sh_row:sh_row + 1, :]
        h_refs[0][...] = (xn * (1.0 + sc) + sh).astype(BF16)


def _ln_modulate(x, y, mod_l, ln_g_l, ln_b_l, next_mod, *, alpha, gate_row, ln_row, sc_row, sh_row,
                 want_h, tm=256):
    m, d = x.shape
    tm = _tile(m, tm)
    row = pl.BlockSpec((tm, d), lambda i: (i, 0))
    out_shape = [jax.ShapeDtypeStruct((m, d), F32)]
    out_specs = [row]
    if want_h:
        out_shape.append(jax.ShapeDtypeStruct((m, d), BF16))
        out_specs.append(row)
    res = pl.pallas_call(
        functools.partial(_ln_kernel, alpha=alpha, gate_row=gate_row, ln_row=ln_row,
                          sc_row=sc_row, sh_row=sh_row),
        grid=(m // tm,),
        in_specs=[row, row,
                  pl.BlockSpec((N_MOD, d), lambda i: (0, 0)),
                  pl.BlockSpec((2, d), lambda i: (0, 0)),
                  pl.BlockSpec((2, d), lambda i: (0, 0)),
                  pl.BlockSpec((N_MOD, d), lambda i: (0, 0))],
        out_specs=out_specs, out_shape=out_shape, name="ln_modulate",
        compiler_params=_cp("parallel"),
    )(x, y, mod_l, ln_g_l, ln_b_l, next_mod)
    return (res[0], res[1]) if want_h else (res[0], None)


def _mm_kernel(a_ref, w_ref, o_ref, wbf_ref):
    @pl.when(pl.program_id(1) == 0)
    def _():
        wbf_ref[...] = w_ref[...].astype(BF16)

    o_ref[...] = jnp.dot(a_ref[...], wbf_ref[...], preferred_element_type=F32).astype(o_ref.dtype)


def _w_spec(w, layer, k, tn, off=0):
    if w.ndim == 3:
        return pl.BlockSpec((None, k, tn), lambda j, i: (layer, 0, j + off))
    return pl.BlockSpec((k, tn), lambda j, i: (0, j + off))


def _matmul(a, w, *, out_dtype, layer=0, col_off=0, n_cols=None, tm=1024, tn=512, name="matmul"):
    m, k = a.shape
    n_cols = w.shape[-1] - col_off if n_cols is None else n_cols
    tm = _tile(m, tm)
    tn = math.gcd(_tile(n_cols, tn), col_off) if col_off else _tile(n_cols, tn)
    off = col_off // tn
    return pl.pallas_call(
        _mm_kernel, grid=(n_cols // tn, m // tm),
        in_specs=[pl.BlockSpec((tm, k), lambda j, i: (i, 0)),
                  _w_spec(w, layer, k, tn, off)],
        out_specs=pl.BlockSpec((tm, tn), lambda j, i: (i, j)),
        out_shape=jax.ShapeDtypeStruct((m, n_cols), out_dtype),
        scratch_shapes=[pltpu.VMEM((k, tn), BF16)], name=name,
        compiler_params=_cp("parallel", "arbitrary"),
    )(a, w)


def _mm_t_kernel(a_ref, w_ref, o_ref, wt_ref):
    @pl.when(pl.program_id(1) == 0)
    def _():
        wt_ref[...] = jnp.transpose(w_ref[...])

    o_ref[...] = lax.dot_general(wt_ref[...], a_ref[...], (((1,), (1,)), ((), ())),
                                 preferred_element_type=F32).astype(o_ref.dtype)


def _matmul_t(a, w, *, out_dtype, col_off, n_cols, tm=1024, tn=1024, name="matmul_t"):
    m, k = a.shape
    tm = _tile(m, tm)
    tn = math.gcd(_tile(n_cols, tn), col_off) if col_off else _tile(n_cols, tn)
    off = col_off // tn
    return pl.pallas_call(
        _mm_t_kernel, grid=(n_cols // tn, m // tm),
        in_specs=[pl.BlockSpec((tm, k), lambda j, i: (i, 0)),
                  pl.BlockSpec((k, tn), lambda j, i: (0, j + off))],
        out_specs=pl.BlockSpec((tn, tm), lambda j, i: (j, i)),
        out_shape=jax.ShapeDtypeStruct((n_cols, m), out_dtype),
        scratch_shapes=[pltpu.VMEM((tn, k), w.dtype)], name=name,
        compiler_params=_cp("parallel", "arbitrary"),
    )(a, w)


def _glu_kernel(y_ref, yt_ref, w_ref, b_ref, o_ref, wbf_ref):
    @pl.when(pl.program_id(1) == 0)
    def _():
        wbf_ref[...] = w_ref[...].astype(BF16)

    z = jnp.dot(y_ref[...].astype(BF16), wbf_ref[...], preferred_element_type=F32) + b_ref[...]
    o_ref[...] = (yt_ref[...] * jax.nn.sigmoid(z)).astype(o_ref.dtype)


def _glu(y, w, b, layer, tm=1024, tn=512):
    m, k = y.shape
    n = w.shape[-1]
    tm = _tile(m, tm)
    tn = _tile(n, tn)
    return pl.pallas_call(
        _glu_kernel, grid=(n // tn, m // tm),
        in_specs=[pl.BlockSpec((tm, k), lambda j, i: (i, 0)),
                  pl.BlockSpec((tm, tn), lambda j, i: (i, j)),
                  _w_spec(w, layer, k, tn),
                  pl.BlockSpec((1, tn), lambda j, i: (0, j))],
        out_specs=pl.BlockSpec((tm, tn), lambda j, i: (i, j)),
        out_shape=jax.ShapeDtypeStruct((m, n), BF16),
        scratch_shapes=[pltpu.VMEM((k, tn), BF16)], name="s5_glu",
        compiler_params=_cp("parallel", "arbitrary"),
    )(y, y, w, b.reshape(1, n))


def _merge_kernel(a1_ref, a2_ref, w1_ref, w2_ref, ga_ref, gb_ref, o_ref, w1bf_ref, w2bf_ref):
    @pl.when(pl.program_id(1) == 0)
    def _():
        w1bf_ref[...] = w1_ref[...].astype(BF16)
        w2bf_ref[...] = w2_ref[...].astype(BF16)

    ya = jnp.dot(a1_ref[...], w1bf_ref[...], preferred_element_type=F32)
    yb = jnp.dot(a2_ref[...], w2bf_ref[...], preferred_element_type=F32)
    o_ref[...] = (jax.nn.sigmoid(ga_ref[...]) * ya + jax.nn.sigmoid(gb_ref[...]) * yb).astype(o_ref.dtype)


def _merge(a1, a2, w1, w2, gates, layer, tm=1024, tn=512):
    m, k1 = a1.shape
    k2 = a2.shape[1]
    n = w1.shape[-1]
    tm = _tile(m, tm)
    tn = _tile(n, tn)
    nb = n // tn
    return pl.pallas_call(
        _merge_kernel, grid=(nb, m // tm),
        in_specs=[pl.BlockSpec((tm, k1), lambda j, i: (i, 0)),
                  pl.BlockSpec((tm, k2), lambda j, i: (i, 0)),
                  _w_spec(w1, layer, k1, tn),
                  _w_spec(w2, layer, k2, tn),
                  pl.BlockSpec((tm, tn), lambda j, i: (i, j)),
                  pl.BlockSpec((tm, tn), lambda j, i: (i, j + nb))],
        out_specs=pl.BlockSpec((tm, tn), lambda j, i: (i, j)),
        out_shape=jax.ShapeDtypeStruct((m, n), BF16),
        scratch_shapes=[pltpu.VMEM((k1, tn), BF16), pltpu.VMEM((k2, tn), BF16)], name="branch_merge",
        compiler_params=_cp("parallel", "arbitrary"),
    )(a1, a2, w1, w2, gates, gates)


def _s5_discretize(lam_re, lam_im, b_re, b_im, c_re, c_im, d, log_step, t_chunk):
    hp = lax.Precision.HIGHEST
    g, p, h = b_re.shape
    na = LANES // h
    nx = g // na
    assert 2 * p == LANES
    lam = lax.complex(jnp.minimum(lam_re, -1e-4), lam_im)
    dt = jnp.exp(log_step)[:, None]
    lam_bar = jnp.exp(lam * dt)
    b_bar = ((lam_bar - 1.0) / lam)[:, :, None] * lax.complex(b_re, b_im)
    c = lax.complex(c_re, c_im)
    tau = jnp.arange(t_chunk + 1, dtype=F32)
    pw = jnp.exp((lam * dt)[:, None, :] * tau[None, :, None].astype(jnp.complex64))
    cp = c[:, None, :, :] * pw[:, :, None, :]
    kern = jnp.einsum('gtop,gpi->gtoi', cp[:, :t_chunk], b_bar, precision=hp).real
    kio = jnp.transpose(kern.reshape(nx, na, t_chunk, h, h), (0, 2, 1, 4, 3))
    kt = jnp.tile(kio.reshape(nx, t_chunk, LANES, h), (1, 1, 1, na))
    wcx = pw[:, t_chunk - 1 - np.arange(t_chunk)][:, :, :, None] * b_bar[:, None, :, :]
    wcx = jnp.transpose(wcx, (0, 1, 3, 2))
    w_ri = jnp.concatenate([wcx.real, wcx.imag], axis=-1).reshape(nx, na, t_chunk, h, 2 * p)
    wc = jnp.transpose(w_ri, (0, 2, 1, 3, 4)).reshape(nx, t_chunk, LANES, 2 * p)
    cl = jnp.transpose(cp[:, 1:t_chunk + 1], (0, 3, 1, 2))
    vc = jnp.concatenate([cl.real, -cl.imag], axis=1).reshape(nx, na, 2 * p, t_chunk * h)
    dd = jnp.broadcast_to(d.reshape(nx, 1, 1, na, h), (nx, 1, t_chunk, na, h)).reshape(nx, 1, t_chunk * LANES)
    a = pw[:, t_chunk]
    a1 = jnp.concatenate([a.real, a.real], axis=-1).reshape(1, g * 2 * p)
    a2 = jnp.concatenate([-a.imag, a.imag], axis=-1).reshape(1, g * 2 * p)
    return kt, wc, vc, dd, a1, a2


def _chunk_rows(u_ref, t, nc):
    return jnp.concatenate([u_ref[pl.ds(s, nc, stride=t), :] for s in range(t)], axis=1)


def _s5_state_in_kernel(u_ref, wc_ref, xa_ref, xb_ref, wa_ref, wb_ref, *, t, nc, h):
    na = LANES // h
    row_group = lax.broadcasted_iota(I32, (LANES, LANES), 0) // h
    for s in range(t):
        blk = wc_ref[s]
        swapped = pltpu.roll(blk, LANES // 2, axis=1)
        for b in range(na):
            own = row_group == b
            dst = (slice(s * LANES, (s + 1) * LANES), slice(b * LANES, (b + 1) * LANES))
            wa_ref[dst] = jnp.where(own, blk, 0.0).astype(BF16)
            wb_ref[dst] = jnp.where(own, swapped, 0.0).astype(BF16)
    ucat = _chunk_rows(u_ref, t, nc).astype(BF16)
    xa_ref[...] = jnp.dot(ucat, wa_ref[...], preferred_element_type=F32)
    xb_ref[...] = jnp.dot(ucat, wb_ref[...], preferred_element_type=F32)


def _s5_scan_kernel(xa_ref, xb_ref, a1_ref, a2_ref, s_ref, *, unroll):
    a1 = a1_ref[...]
    a2 = a2_ref[...]
    nc = xa_ref.shape[0]

    def body(cb, carry):
        s_a, s_b = carry
        base = pl.multiple_of(cb * unroll, unroll)
        xa = xa_ref[pl.ds(base, unroll), :]
        xb = xb_ref[pl.ds(base, unroll), :]
        rows = []
        for k in range(unroll):
            rows.append(s_a)
            s_a, s_b = (a1 * s_a + a2 * s_b + xa[k:k + 1, :],
                        a1 * s_b - a2 * s_a + xb[k:k + 1, :])
        s_ref[pl.ds(base, unroll), :] = jnp.concatenate(rows, axis=0)
        return s_a, s_b

    z = jnp.zeros(a1.shape, F32)
    lax.fori_loop(0, nc // unroll, body, (z, z))


def _s5_out_kernel(u_ref, sp_ref, kt_ref, vc_ref, dd_ref, y_ref, bd_ref, vm_ref, *, t, nc, h, n_parts):
    part = pl.program_id(1)
    tp = t // n_parts
    na = LANES // h
    w = tp * LANES
    same_group = (lax.broadcasted_iota(I32, (LANES, LANES), 0) // h
                  == lax.broadcasted_iota(I32, (LANES, LANES), 1) // h)
    row = lax.broadcasted_iota(I32, (LANES, w), 0)
    col = lax.broadcasted_iota(I32, (LANES, w), 1)
    landing_group = jnp.where((row // h == col // LANES) & (row % h == col % h), (col // h) % na, -1)
    ucat = _chunk_rows(u_ref, t, nc)
    for k in range(n_parts):
        @pl.when(part == k)
        def _(k=k):
            live = (k + 1) * tp
            for s in range(live):
                for tt in range(tp):
                    lag = k * tp + tt - s
                    dst = (slice(s * LANES, (s + 1) * LANES), slice(tt * LANES, (tt + 1) * LANES))
                    if lag >= 0:
                        bd_ref[dst] = jnp.where(same_group, kt_ref[lag], 0.0).astype(BF16)
                    else:
                        bd_ref[dst] = jnp.zeros((LANES, LANES), BF16)
            for a in range(na):
                place = jnp.where(landing_group == a, 1.0, 0.0).astype(BF16)
                src = vc_ref[a][:, k * tp * h:(k + 1) * tp * h].astype(BF16)
                vm_ref[a * LANES:(a + 1) * LANES, :] = jnp.dot(src, place, preferred_element_type=F32).astype(BF16)
            y = jnp.dot(ucat[:, :live * LANES].astype(BF16), bd_ref[:live * LANES, :], preferred_element_type=F32)
            y = y + jnp.dot(sp_ref[...].astype(BF16), vm_ref[...], preferred_element_type=F32)
            y = jax.nn.gelu(y + dd_ref[...] * ucat[:, k * w:(k + 1) * w])
            for s in range(tp):
                y_ref[pl.ds(k * tp + s, nc, stride=t), :] = y[:, s * LANES:(s + 1) * LANES]


def _s5_branch(u, prm):
    kt, wc, vc, dd, a1, a2 = prm
    L, width = u.shape
    nx, na, _, th = vc.shape
    t = kt.shape[1]
    h = th // t
    tah = t * LANES
    sw = na * LANES
    nc = L // t
    u_spec = pl.BlockSpec((L, LANES), lambda x: (0, x))
    st_spec = pl.BlockSpec((nc, sw), lambda x: (0, x))
    xa, xb = pl.pallas_call(
        functools.partial(_s5_state_in_kernel, t=t, nc=nc, h=h), grid=(nx,),
        in_specs=[u_spec,
                  pl.BlockSpec((None, t, LANES, LANES), lambda x: (x, 0, 0, 0))],
        out_specs=[st_spec, st_spec],
        out_shape=[jax.ShapeDtypeStruct((nc, nx * sw), F32)] * 2,
        scratch_shapes=[pltpu.VMEM((tah, sw), BF16), pltpu.VMEM((tah, sw), BF16)], name="s5_state_in",
        compiler_params=_cp("parallel"),
    )(u, wc)
    ws = _tile(nx * sw, 2048)
    unroll = _tile(nc, 8)
    sc_spec = pl.BlockSpec((nc, ws), lambda x: (0, x))
    a_spec = pl.BlockSpec((1, ws), lambda x: (0, x))
    sp = pl.pallas_call(
        functools.partial(_s5_scan_kernel, unroll=unroll), grid=(nx * sw // ws,),
        in_specs=[sc_spec, sc_spec, a_spec, a_spec],
        out_specs=sc_spec,
        out_shape=jax.ShapeDtypeStruct((nc, nx * sw), F32), name="s5_scan",
        compiler_params=_cp("parallel"),
    )(xa, xb, a1, a2)
    n_parts = 2
    tw = tah // n_parts
    return pl.pallas_call(
        functools.partial(_s5_out_kernel, t=t, nc=nc, h=h, n_parts=n_parts), grid=(nx, n_parts),
        in_specs=[pl.BlockSpec((L, LANES), lambda x, k: (0, x)),
                  pl.BlockSpec((nc, sw), lambda x, k: (0, x)),
                  pl.BlockSpec((None, t, LANES, LANES), lambda x, k: (x, 0, 0, 0)),
                  pl.BlockSpec((None, na, LANES, th), lambda x, k: (x, 0, 0, 0)),
                  pl.BlockSpec((None, 1, tw), lambda x, k: (x, 0, k))],
        out_specs=pl.BlockSpec((L, LANES), lambda x, k: (0, x)),
        out_shape=jax.ShapeDtypeStruct((L, width), F32),
        scratch_shapes=[pltpu.VMEM((tah, tw), BF16), pltpu.VMEM((sw, tw), BF16)], name="s5_out",
        compiler_params=_cp("parallel", "arbitrary"),
    )(u, sp, kt, vc, dd)


def _t5_bucket_starts(n_buckets):
    exact = n_buckets // 2
    dist = np.arange(0, 4 * REL_MAX_DIST).astype(np.int64)
    far = exact + (np.log(np.maximum(dist, 1).astype(np.float32) / np.float32(exact))
                   / np.float32(math.log(REL_MAX_DIST / exact)) * np.float32(n_buckets - exact)).astype(np.int64)
    bucket = np.where(dist < exact, dist, np.minimum(far, n_buckets - 1))
    return [int(np.argmax(bucket >= b)) for b in range(1, n_buckets)]


def _bias_tile_kernel(rb_ref, o_ref, *, tb, starts):
    delta = jnp.where(pl.program_id(0) == 3, -1, pl.program_id(0))
    head = pl.program_id(1)
    key = lax.broadcasted_iota(I32, (tb, tb), 0)
    qry = lax.broadcasted_iota(I32, (tb, tb), 1)
    dist = delta * tb + qry - key
    bias = jnp.full((tb, tb), rb_ref[0, head], F32)
    for b, start in enumerate(starts, start=1):
        bias = jnp.where(dist >= start, rb_ref[b, head], bias)
    o_ref[...] = jnp.where(dist >= 0, bias * LOG2E, NEG_BIG)


def _bias_tiles(rel_bias, tb):
    n_buckets, heads = rel_bias.shape
    starts = _t5_bucket_starts(n_buckets)
    assert 2 * tb - (tb - 1) >= starts[-1]
    return pl.pallas_call(
        functools.partial(_bias_tile_kernel, tb=tb, starts=starts), grid=(4, heads),
        in_specs=[pl.BlockSpec(memory_space=pltpu.SMEM)],
        out_specs=pl.BlockSpec((None, None, tb, tb), lambda dl, hd: (dl, hd, 0, 0)),
        out_shape=jax.ShapeDtypeStruct((4, heads, tb, tb), F32), name="rel_bias_tiles",
        compiler_params=_cp("parallel", "parallel"),
    )(rel_bias)


def _indexer_kernel(qt_ref, k2_ref, wt_ref, keys_ref, thr_ref, *, tq, ck, n_heads, topk, w_scale):
    i = pl.program_id(0)
    n_blocks = keys_ref.shape[0] // tq
    n_live = (i + 1) * (tq // ck)
    wv = wt_ref[...] * w_scale

    def score_chunk(c, carry):
        row = pl.multiple_of(c * ck, ck)
        kk = k2_ref[c]
        acc = jnp.zeros((ck, tq), F32)
        for pr in range(n_heads // 2):
            s2 = jnp.dot(kk, qt_ref[pr * LANES:(pr + 1) * LANES, :], preferred_element_type=F32)
            acc = acc + jnp.maximum(s2[:ck], 0.0) * wv[2 * pr:2 * pr + 1, :]
            acc = acc + jnp.maximum(s2[ck:], 0.0) * wv[2 * pr + 1:2 * pr + 2, :]
        kpos = row + lax.broadcasted_iota(I32, (ck, tq), 0)
        qpos = i * tq + lax.broadcasted_iota(I32, (ck, tq), 1)
        acc = jnp.where(acc == 0.0, 0.0, acc)
        score = jnp.where(kpos <= qpos, acc, -jnp.inf)
        bits = pltpu.bitcast(score, I32)
        keys_ref[pl.ds(row, ck), :] = jnp.where(bits < 0, bits ^ 0x7FFFFFFF, bits)
        return carry

    lax.fori_loop(0, n_live, score_chunk, 0)

    def fill_block(b, carry):
        row = pl.multiple_of(b * tq, tq)
        keys_ref[pl.ds(row, tq), :] = jnp.full((tq, tq), INT_MIN, I32)
        return carry

    lax.fori_loop(i + 1, n_blocks, fill_block, 0)

    acc_rows = 32

    def count_ge(cand):
        def body(b, cnt):
            row = pl.multiple_of(b * tq, tq)
            hit = jnp.where(keys_ref[pl.ds(row, tq), :] >= cand, 1.0, 0.0)
            return cnt + jnp.sum(hit.reshape(tq // acc_rows, acc_rows, tq), axis=0)
        cnt = lax.fori_loop(0, i + 1, body, jnp.zeros((acc_rows, tq), F32))
        return jnp.sum(cnt, axis=0, keepdims=True)

    kf = float(topk)
    zero = jnp.zeros((1, tq), I32)
    c_zero = count_ge(zero)
    thr0 = jnp.where(c_zero >= kf, zero, jnp.full((1, tq), INT_MIN, I32))
    cnt0 = jnp.where(c_zero >= kf, c_zero, ((i + 1) * tq).astype(F32))

    def undecided(st):
        it, _, cnt = st
        return (it < 31) & (jnp.max(cnt) > kf)

    def bit_step(st):
        it, thr, cnt = st
        cand = thr | jnp.left_shift(jnp.int32(1), 30 - it)
        c = count_ge(cand)
        keep = c >= kf
        return it + 1, jnp.where(keep, cand, thr), jnp.where(keep, c, cnt)

    _, thr, cnt = lax.while_loop(undecided, bit_step, (jnp.int32(0), thr0, cnt0))

    n_keys = keys_ref.shape[0]

    def position_cutoff(_):
        need = kf - count_ge(thr + 1)

        def count_eq_upto(pos):
            def body(b, acc):
                row = pl.multiple_of(b * tq, tq)
                kpos = row + lax.broadcasted_iota(I32, (tq, tq), 0)
                hit = jnp.where(keys_ref[pl.ds(row, tq), :] == thr, jnp.where(kpos <= pos, 1.0, 0.0), 0.0)
                return acc + jnp.sum(hit.reshape(tq // acc_rows, acc_rows, tq), axis=0)
            acc = lax.fori_loop(0, i + 1, body, jnp.zeros((acc_rows, tq), F32))
            return jnp.sum(acc, axis=0, keepdims=True)

        def halve(_, bounds):
            lo, hi = bounds
            mid = (lo + hi) >> 1
            enough = count_eq_upto(mid) >= need
            return jnp.where(enough, lo, mid), jnp.where(enough, mid, hi)

        lo0 = jnp.full((1, tq), -1, I32)
        hi0 = jnp.full((1, tq), n_keys - 1, I32)
        _, hi = lax.fori_loop(0, max(1, (n_keys - 1).bit_length()), halve, (lo0, hi0))
        return jnp.where(cnt > kf, hi, n_keys)

    jcut = lax.cond(jnp.max(cnt) > kf, position_cutoff, lambda _: jnp.full((1, tq), n_keys, I32), 0)
    sel_row = lax.broadcasted_iota(I32, thr_ref.shape, 0)
    thr_ref[...] = jnp.where(sel_row == 0, thr, jnp.where(sel_row == 1, jcut, 0))


def _indexer(qi_t, row_off, k2, w_t, topk, tq=256):
    L = qi_t.shape[1]
    n_heads = w_t.shape[0]
    qw = n_heads * IDX_DIM
    assert row_off % qw == 0
    tq = _tile(L, tq)
    ck = k2.shape[1] // 2
    assert tq % ck == 0
    w_scale = (n_heads ** -0.5) * (IDX_DIM ** -0.5)
    return pl.pallas_call(
        functools.partial(_indexer_kernel, tq=tq, ck=ck, n_heads=n_heads, topk=topk, w_scale=w_scale),
        grid=(L // tq,),
        in_specs=[pl.BlockSpec((qw, tq), lambda i: (row_off // qw, i)),
                  pl.BlockSpec(k2.shape, lambda i: (0, 0, 0)),
                  pl.BlockSpec((n_heads, tq), lambda i: (0, i))],
        out_specs=[pl.BlockSpec((L, tq), lambda i: (0, i)),
                   pl.BlockSpec((8, tq), lambda i: (0, i))],
        out_shape=[jax.ShapeDtypeStruct((L, L), I32), jax.ShapeDtypeStruct((8, L), I32)],
        name="dsa_indexer", compiler_params=_cp("parallel"),
    )(qi_t, k2, w_t)


def _attn_kernel(qi_ref, kj_ref, far_ref, qt_ref, k_ref, vt_ref, keys_ref, thr_ref, bias_a_ref, bias_b_ref, o_ref,
                 m_ref, l_ref, acc_ref, mb_ref, *, heads, dh, scale2):
    step = pl.program_id(0)
    qi = qi_ref[step]
    kj = kj_ref[step]

    @pl.when(kj == 0)
    def _():
        m_ref[...] = jnp.full(m_ref.shape, -jnp.inf, F32)
        l_ref[...] = jnp.zeros(l_ref.shape, F32)
        acc_ref[...] = jnp.zeros(acc_ref.shape, F32)

    thr = thr_ref[0:1, :]
    kpos = kj * keys_ref.shape[0] + lax.broadcasted_iota(I32, keys_ref.shape, 0)
    bar = jnp.where(kpos <= thr_ref[1:2, :], thr, thr + 1)
    mb_ref[...] = jnp.where(keys_ref[...] >= bar, 0.0, NEG_BIG)

    def scores(h):
        hs = slice(h * dh, (h + 1) * dh)
        return jnp.dot(k_ref[:, hs], qt_ref[hs, :], preferred_element_type=F32)

    def fold(reduce_fn, a, rows=32):
        return reduce_fn(a.reshape(a.shape[0] // rows, rows, a.shape[1]), axis=0)

    def run(near):
        ahead = 4
        pending = [scores(h) for h in range(min(ahead, heads))]
        for h in range(heads):
            hs = slice(h * dh, (h + 1) * dh)
            x = pending.pop(0) * scale2 + mb_ref[...]
            if h + ahead < heads:
                pending.append(scores(h + ahead))
            if near:
                x = x + jnp.concatenate([bias_a_ref[h], bias_b_ref[h]], axis=0)
            far = 0.0 if near else far_ref[h]
            m_prev = m_ref[h]
            m_new = jnp.maximum(m_prev, jnp.max(fold(jnp.max, x), axis=0, keepdims=True) + far)
            alpha = jnp.exp2(m_prev - m_new)
            p = jnp.exp2(x - (m_new[0:1, :] - far))
            l_ref[h] = alpha * l_ref[h] + jnp.sum(fold(jnp.sum, p), axis=0, keepdims=True)
            m_ref[h] = m_new
            acc_ref[hs, :] = alpha[0:1, :] * acc_ref[hs, :] + jnp.dot(vt_ref[hs, :], p.astype(BF16),
                                                                      preferred_element_type=F32)

    @pl.when(qi - 2 * kj - 1 < 2)
    def _():
        run(True)

    @pl.when(qi - 2 * kj - 1 >= 2)
    def _():
        run(False)

    @pl.when(kj == qi // 2)
    def _():
        for h in range(heads):
            hs = slice(h * dh, (h + 1) * dh)
            o_ref[:, hs] = jnp.transpose(acc_ref[hs, :] / l_ref[h][0:1, :]).astype(o_ref.dtype)


def _attention(q_t, v_t, k, keys, thr, bias_tiles, far_bias, tb):
    L = k.shape[0]
    heads = bias_tiles.shape[1]
    dh = ATT_HEAD_DIM
    aw = heads * dh
    nb = L // tb
    assert nb % 2 == 0
    pairs = [(i, j) for i in range(nb) for j in range(i // 2 + 1)]
    qi = jnp.asarray(np.array([p[0] for p in pairs], np.int32))
    kj = jnp.asarray(np.array([p[1] for p in pairs], np.int32))

    def tile_of(behind):
        return jnp.where(behind < 0, 3, jnp.minimum(behind, 2))

    bias_spec_a = pl.BlockSpec((None, heads, tb, tb), lambda s, qi, kj: (tile_of(qi[s] - 2 * kj[s]), 0, 0, 0))
    bias_spec_b = pl.BlockSpec((None, heads, tb, tb), lambda s, qi, kj: (tile_of(qi[s] - 2 * kj[s] - 1), 0, 0, 0))
    grid_spec = pltpu.PrefetchScalarGridSpec(
        num_scalar_prefetch=2, grid=(len(pairs),),
        in_specs=[pl.BlockSpec(memory_space=pltpu.SMEM),
                  pl.BlockSpec((aw, tb), lambda s, qi, kj: (0, qi[s])),
                  pl.BlockSpec((2 * tb, aw), lambda s, qi, kj: (kj[s], 0)),
                  pl.BlockSpec((aw, 2 * tb), lambda s, qi, kj: (0, kj[s])),
                  pl.BlockSpec((2 * tb, tb), lambda s, qi, kj: (kj[s], qi[s])),
                  pl.BlockSpec((8, tb), lambda s, qi, kj: (0, qi[s])),
                  bias_spec_a, bias_spec_b],
        out_specs=pl.BlockSpec((tb, aw), lambda s, qi, kj: (qi[s], 0)),
        scratch_shapes=[pltpu.VMEM((heads, 8, tb), F32),
                        pltpu.VMEM((heads, 8, tb), F32),
                        pltpu.VMEM((aw, tb), F32),
                        pltpu.VMEM((2 * tb, tb), F32)])
    return pl.pallas_call(
        functools.partial(_attn_kernel, heads=heads, dh=dh, scale2=dh ** -0.5 * LOG2E),
        grid_spec=grid_spec, out_shape=jax.ShapeDtypeStruct((L, aw), BF16), name="dsa_attention",
        compiler_params=_cp("arbitrary"),
    )(qi, kj, far_bias, q_t, k, v_t, keys, thr, bias_tiles, bias_tiles)


def _router_kernel(h_ref, w_ref, b_ref, g_ref, *, n_exp, top_k):
    logits = jnp.dot(h_ref[...], w_ref[...], preferred_element_type=F32) + b_ref[...]
    lane = lax.broadcasted_iota(I32, logits.shape, 1).astype(F32)
    work = jnp.where(lane < n_exp, logits, -jnp.inf)
    gates = jnp.zeros(logits.shape, F32)
    denom = None
    top = None
    for r in range(top_k):
        m = jnp.max(work, axis=1, keepdims=True)
        first = jnp.min(jnp.where(work == m, lane, float(LANES)), axis=1, keepdims=True)
        sel = lane == first
        if r == 0:
            top = m
        e = jnp.exp(m - top)
        gates = gates + jnp.where(sel, e, 0.0)
        denom = e if denom is None else denom + e
        work = jnp.where(sel, -jnp.inf, work)
    g_ref[...] = gates / denom


def _router(h, w_pad, b_pad, n_exp, tm=512):
    m, d = h.shape
    tm = _tile(m, tm)
    return pl.pallas_call(
        functools.partial(_router_kernel, n_exp=n_exp, top_k=TOP_K_EXPERTS), grid=(m // tm,),
        in_specs=[pl.BlockSpec((tm, d), lambda i: (i, 0)),
                  pl.BlockSpec((d, LANES), lambda i: (0, 0)),
                  pl.BlockSpec((1, LANES), lambda i: (0, 0))],
        out_specs=pl.BlockSpec((tm, LANES), lambda i: (i, 0)),
        out_shape=jax.ShapeDtypeStruct((m, LANES), F32), name="moe_router",
        compiler_params=_cp("parallel"),
    )(h, w_pad, b_pad)


def _moe_kernel(h_ref, g_ref, wgu_ref, bgu_ref, wdn_ref, bdn_ref, o_ref, *, ff, eb):
    step = pl.program_id(1)
    gates = g_ref[...]

    @pl.when(step == 0)
    def _():
        o_ref[...] = jnp.dot(gates.astype(BF16), bdn_ref[...], preferred_element_type=F32)

    lane = lax.broadcasted_iota(I32, gates.shape, 1)
    for k in range(eb):
        gu = jnp.dot(h_ref[...], wgu_ref[k], preferred_element_type=F32) + bgu_ref[k]
        gate = jnp.minimum(gu[:, :ff], SWIGLU_LIMIT)
        up = jnp.clip(gu[:, ff:], -SWIGLU_LIMIT, SWIGLU_LIMIT)
        act = (up + 1.0) * gate * jax.nn.sigmoid(SWIGLU_ALPHA * gate)
        ge = jnp.sum(jnp.where(lane == step * eb + k, gates, 0.0), axis=1, keepdims=True)
        o_ref[...] += jnp.dot((act * ge).astype(BF16), wdn_ref[k], preferred_element_type=F32)


def _moe_experts(h, gates, w_gu, b_gu, w_dn, b_dn_pad, tm=512, eb=2):
    m, d = h.shape
    n_exp, _, ff2 = w_gu.shape
    tm = _tile(m, tm)
    eb = _tile(n_exp, eb)
    return pl.pallas_call(
        functools.partial(_moe_kernel, ff=ff2 // 2, eb=eb), grid=(m // tm, n_exp // eb),
        in_specs=[pl.BlockSpec((tm, d), lambda i, e: (i, 0)),
                  pl.BlockSpec((tm, LANES), lambda i, e: (i, 0)),
                  pl.BlockSpec((eb, d, ff2), lambda i, e: (e, 0, 0)),
                  pl.BlockSpec((eb, 1, ff2), lambda i, e: (e, 0, 0)),
                  pl.BlockSpec((eb, ff2 // 2, d), lambda i, e: (e, 0, 0)),
                  pl.BlockSpec((LANES, d), lambda i, e: (0, 0))],
        out_specs=pl.BlockSpec((tm, d), lambda i, e: (i, 0)),
        out_shape=jax.ShapeDtypeStruct((m, d), F32), name="moe_experts",
        compiler_params=_cp("parallel", "arbitrary"),
    )(h, gates, w_gu, b_gu.reshape(n_exp, 1, ff2), w_dn, b_dn_pad)


def kernel(x, c, w_in, w_a_out, w_b_out, w_o, w_glu, b_glu, s5_lam_re, s5_lam_im, s5_b_re, s5_b_im, s5_c_re, s5_c_im, s5_d, s5_log_step, rel_bias, w_router, b_router, w_gu, b_gu, w_dn, b_dn, w_c1, b_c1, w_c2, b_c2, mod_table, ln_g, ln_b):
    bsz, L, d = x.shape
    assert bsz == 1 and c.shape[0] == 1
    depth = w_in.shape[0]
    g, p, h = s5_b_re.shape[1:]
    s5w = g * h
    heads = rel_bias.shape[1]
    aw = heads * ATT_HEAD_DIM
    iw = IDX_HEADS * IDX_DIM
    n_exp = w_router.shape[2]
    assert w_in.shape[2] == s5w + 3 * aw + iw + IDX_DIM + IDX_HEADS + 2 * d
    assert n_exp <= LANES and IDX_DIM + IDX_HEADS <= LANES and 2 * IDX_DIM == LANES
    o_q, o_k, o_v, o_qi = s5w, s5w + aw, s5w + 2 * aw, s5w + 3 * aw
    o_ki = s5w + 3 * aw + iw
    o_gate = o_ki + IDX_DIM + IDX_HEADS
    topk = min(TOPK_MAX, L // 4)
    tb = _tile(L, 256)
    ck = _tile(L, 128)
    dn_alpha = (2 * depth) ** 0.25

    mod = _conditioning(c, w_c1, b_c1, w_c2, b_c2, mod_table).reshape(depth, N_MOD, d)
    bias_tiles = _bias_tiles(rel_bias, tb)
    far_bias = rel_bias[-1] * LOG2E
    assert o_ki % LANES == 0 and (g * h) % LANES == 0 and LANES % h == 0

    xs = x.reshape(L, d)
    hcur = _modulate(xs, mod[0], sc_row=1, sh_row=0)
    for l in range(depth):
        w_l = w_in[l]
        w_main = w_l[:, :o_ki + LANES].astype(BF16)
        w_gate = w_l[:, o_gate:].astype(BF16)
        u = _matmul(hcur, w_main, out_dtype=F32, col_off=0, n_cols=s5w, name="proj_u")
        k = _matmul(hcur, w_main, out_dtype=BF16, col_off=o_k, n_cols=aw, name="proj_k")
        gates_ab = _matmul(hcur, w_gate, out_dtype=F32, name="proj_gates")
        misc = _matmul(hcur, w_main, out_dtype=F32, col_off=o_ki, n_cols=LANES, tn=LANES, name="proj_misc")
        q_t = _matmul_t(hcur, w_main, out_dtype=BF16, col_off=o_q, n_cols=aw, name="proj_q_t")
        vqi_t = _matmul_t(hcur, w_main, out_dtype=BF16, col_off=o_v, n_cols=aw + iw, name="proj_vqi_t")

        prm = _s5_discretize(s5_lam_re[l], s5_lam_im[l], s5_b_re[l], s5_b_im[l], s5_c_re[l], s5_c_im[l],
                             s5_d[l], s5_log_step[l], S5_CHUNK)
        y_s5 = _s5_branch(u, prm)
        ya_in = _glu(y_s5, w_glu, b_glu[l], l)

        k_idx = misc[:, :IDX_DIM].astype(BF16)
        zeros = jnp.zeros_like(k_idx)
        k_even = jnp.concatenate([k_idx, zeros], axis=1).reshape(L // ck, ck, LANES)
        k_odd = jnp.concatenate([zeros, k_idx], axis=1).reshape(L // ck, ck, LANES)
        k2 = jnp.concatenate([k_even, k_odd], axis=1)
        w_t = jnp.transpose(misc[:, IDX_DIM:IDX_DIM + IDX_HEADS])
        keys, thr = _indexer(vqi_t, aw, k2, w_t, topk)
        attn = _attention(q_t, vqi_t, k, keys, thr, bias_tiles, far_bias, tb)

        merged = _merge(ya_in, attn, w_a_out, w_b_out, gates_ab, l)
        y = _matmul(merged, w_o, layer=l, out_dtype=F32, name="proj_out")
        xs, h2 = _ln_modulate(xs, y, mod[l], ln_g[l], ln_b[l], mod[l], alpha=dn_alpha, gate_row=2, ln_row=0,
                              sc_row=4, sh_row=3, want_h=True)

        w_r = jnp.pad(w_router[l], ((0, 0), (0, LANES - n_exp))).astype(BF16)
        b_r = jnp.pad(b_router[l], (0, LANES - n_exp)).reshape(1, LANES)
        gates = _router(h2, w_r, b_r, n_exp)
        b_dn_pad = jnp.pad(b_dn[l], ((0, LANES - n_exp), (0, 0))).astype(BF16)
        y = _moe_experts(h2, gates, w_gu[l].astype(BF16), b_gu[l], w_dn[l].astype(BF16), b_dn_pad)
        last = l == depth - 1
        xs, hcur = _ln_modulate(xs, y, mod[l], ln_g[l], ln_b[l], mod[min(l + 1, depth - 1)], alpha=dn_alpha,
                                gate_row=5, ln_row=1, sc_row=1, sh_row=0, want_h=not last)
    return xs.reshape(bsz, L, d)
```

```python
import functools
import math

import numpy as np
import jax
import jax.numpy as jnp
from jax import lax
from jax.experimental import pallas as pl
from jax.experimental.pallas import tpu as pltpu

F32 = jnp.float32
BF16 = jnp.bfloat16
I32 = jnp.int32

ATT_HEAD_DIM = 128
IDX_HEADS = 32
IDX_DIM = 64
TOPK_MAX = 256
REL_MAX_DIST = 128
TOP_K_EXPERTS = 4
SWIGLU_LIMIT = 7.0
SWIGLU_ALPHA = 1.702
N_MOD = 6
LN_EPS = 1e-5

SUBLANES = 8
LANES = 128
VMEM_LIMIT_BYTES = 56 * 1024 * 1024
S5_CHUNK = 16
NEG_BIG = -1e30
LOG2E = math.log2(math.e)
INT_MIN = -(2 ** 31)


def _cp(*sem):
    return pltpu.CompilerParams(dimension_semantics=sem, vmem_limit_bytes=VMEM_LIMIT_BYTES)


def _tile(n, t):
    t = min(n, t)
    while n % t:
        t //= 2
    return t


def _cond1_kernel(c_ref, w_ref, b_ref, o_ref):
    a = jax.nn.silu(c_ref[...]).astype(BF16)
    t = jnp.dot(a, w_ref[...].astype(BF16), preferred_element_type=F32) + b_ref[...]
    o_ref[...] = jax.nn.silu(t)


def _cond2_kernel(t_ref, w_ref, b_ref, tab_ref, o_ref):
    acc = jnp.dot(t_ref[...].astype(BF16), w_ref[...].astype(BF16), preferred_element_type=F32)
    o_ref[...] = acc[0:1, :] + b_ref[...] + tab_ref[...]


def _conditioning(c, w_c1, b_c1, w_c2, b_c2, mod_table):
    d = c.shape[1]
    r = w_c1.shape[1]
    depth = mod_table.shape[0]
    n = w_c2.shape[1]
    c8 = jnp.broadcast_to(c, (SUBLANES, d))
    t = pl.pallas_call(
        _cond1_kernel, out_shape=jax.ShapeDtypeStruct((SUBLANES, r), F32), name="cond1",
        compiler_params=_cp(),
    )(c8, w_c1, b_c1.reshape(1, r))
    tn = _tile(n, 2048)
    return pl.pallas_call(
        _cond2_kernel, grid=(n // tn,),
        in_specs=[pl.BlockSpec((SUBLANES, r), lambda j: (0, 0)),
                  pl.BlockSpec((r, tn), lambda j: (0, j)),
                  pl.BlockSpec((1, tn), lambda j: (0, j)),
                  pl.BlockSpec((depth, tn), lambda j: (0, j))],
        out_specs=pl.BlockSpec((depth, tn), lambda j: (0, j)),
        out_shape=jax.ShapeDtypeStruct((depth, n), F32), name="cond2",
        compiler_params=_cp("parallel"),
    )(t, w_c2, b_c2.reshape(1, n), mod_table.reshape(depth, n))


def _modulate_kernel(x_ref, mod_ref, h_ref, *, sc_row, sh_row):
    sc = mod_ref[sc_row:sc_row + 1, :]
    sh = mod_ref[sh_row:sh_row + 1, :]
    h_ref[...] = (x_ref[...] * (1.0 + sc) + sh).astype(h_ref.dtype)


def _modulate(x, mod_l, sc_row, sh_row, tm=256):
    m, d = x.shape
    tm = _tile(m, tm)
    return pl.pallas_call(
        functools.partial(_modulate_kernel, sc_row=sc_row, sh_row=sh_row),
        grid=(m // tm,),
        in_specs=[pl.BlockSpec((tm, d), lambda i: (i, 0)),
                  pl.BlockSpec((N_MOD, d), lambda i: (0, 0))],
        out_specs=pl.BlockSpec((tm, d), lambda i: (i, 0)),
        out_shape=jax.ShapeDtypeStruct((m, d), BF16), name="modulate",
        compiler_params=_cp("parallel"),
    )(x, mod_l)


def _ln_kernel(x_ref, y_ref, mod_ref, g_ref, b_ref, nmod_ref, xo_ref, *h_refs,
               alpha, gate_row, ln_row, sc_row, sh_row):
    gate = mod_ref[gate_row:gate_row + 1, :]
    u = alpha * x_ref[...] + (1.0 + gate) * y_ref[...]
    mu = jnp.mean(u, axis=-1, keepdims=True)
    uc = u - mu
    var = jnp.mean(uc * uc, axis=-1, keepdims=True)
    xn = uc * lax.rsqrt(var + LN_EPS) * g_ref[ln_row:ln_row + 1, :] + b_ref[ln_row:ln_row + 1, :]
    xo_ref[...] = xn
    if h_refs:
        sc = nmod_ref[sc_row:sc_row + 1, :]
        sh = nmod_ref[sh_row:sh_row + 1, :]
        h_refs[0][...] = (xn * (1.0 + sc) + sh).astype(BF16)


def _ln_modulate(x, y, mod_l, ln_g_l, ln_b_l, next_mod, *, alpha, gate_row, ln_row, sc_row, sh_row,
                 want_h, tm=256):
    m, d = x.shape
    tm = _tile(m, tm)
    row = pl.BlockSpec((tm, d), lambda i: (i, 0))
    out_shape = [jax.ShapeDtypeStruct((m, d), F32)]
    out_specs = [row]
    if want_h:
        out_shape.append(jax.ShapeDtypeStruct((m, d), BF16))
        out_specs.append(row)
    res = pl.pallas_call(
        functools.partial(_ln_kernel, alpha=alpha, gate_row=gate_row, ln_row=ln_row,
                          sc_row=sc_row, sh_row=sh_row),
        grid=(m // tm,),
        in_specs=[row, row,
                  pl.BlockSpec((N_MOD, d), lambda i: (0, 0)),
                  pl.BlockSpec((2, d), lambda i: (0, 0)),
                  pl.BlockSpec((2, d), lambda i: (0, 0)),
                  pl.BlockSpec((N_MOD, d), lambda i: (0, 0))],
        out_specs=out_specs, out_shape=out_shape, name="ln_modulate",
        compiler_params=_cp("parallel"),
    )(x, y, mod_l, ln_g_l, ln_b_l, next_mod)
    return (res[0], res[1]) if want_h else (res[0], None)


def _mm_kernel(a_ref, w_ref, o_ref, wbf_ref):
    @pl.when(pl.program_id(1) == 0)
    def _():
        wbf_ref[...] = w_ref[...].astype(BF16)

    o_ref[...] = jnp.dot(a_ref[...], wbf_ref[...], preferred_element_type=F32).astype(o_ref.dtype)


def _w_spec(w, layer, k, tn, off=0):
    if w.ndim == 3:
        return pl.BlockSpec((None, k, tn), lambda j, i: (layer, 0, j + off))
    return pl.BlockSpec((k, tn), lambda j, i: (0, j + off))


def _matmul(a, w, *, out_dtype, layer=0, col_off=0, n_cols=None, tm=1024, tn=512, name="matmul"):
    m, k = a.shape
    n_cols = w.shape[-1] - col_off if n_cols is None else n_cols
    tm = _tile(m, tm)
    tn = math.gcd(_tile(n_cols, tn), col_off) if col_off else _tile(n_cols, tn)
    off = col_off // tn
    return pl.pallas_call(
        _mm_kernel, grid=(n_cols // tn, m // tm),
        in_specs=[pl.BlockSpec((tm, k), lambda j, i: (i, 0)),
                  _w_spec(w, layer, k, tn, off)],
        out_specs=pl.BlockSpec((tm, tn), lambda j, i: (i, j)),
        out_shape=jax.ShapeDtypeStruct((m, n_cols), out_dtype),
        scratch_shapes=[pltpu.VMEM((k, tn), BF16)], name=name,
        compiler_params=_cp("parallel", "arbitrary"),
    )(a, w)


def _mm_t_kernel(a_ref, w_ref, o_ref, wt_ref):
    @pl.when(pl.program_id(1) == 0)
    def _():
        wt_ref[...] = jnp.transpose(w_ref[...])

    o_ref[...] = lax.dot_general(wt_ref[...], a_ref[...], (((1,), (1,)), ((), ())),
                                 preferred_element_type=F32).astype(o_ref.dtype)


def _matmul_t(a, w, *, out_dtype, col_off, n_cols, tm=1024, tn=1024, name="matmul_t"):
    m, k = a.shape
    tm = _tile(m, tm)
    tn = math.gcd(_tile(n_cols, tn), col_off) if col_off else _tile(n_cols, tn)
    off = col_off // tn
    return pl.pallas_call(
        _mm_t_kernel, grid=(n_cols // tn, m // tm),
        in_specs=[pl.BlockSpec((tm, k), lambda j, i: (i, 0)),
                  pl.BlockSpec((k, tn), lambda j, i: (0, j + off))],
        out_specs=pl.BlockSpec((tn, tm), lambda j, i: (j, i)),
        out_shape=jax.ShapeDtypeStruct((n_cols, m), out_dtype),
        scratch_shapes=[pltpu.VMEM((tn, k), w.dtype)], name=name,
        compiler_params=_cp("parallel", "arbitrary"),
    )(a, w)


def _glu_kernel(y_ref, yt_ref, w_ref, b_ref, o_ref, wbf_ref):
    @pl.when(pl.program_id(1) == 0)
    def _():
        wbf_ref[...] = w_ref[...].astype(BF16)

    z = jnp.dot(y_ref[...].astype(BF16), wbf_ref[...], preferred_element_type=F32) + b_ref[...]
    o_ref[...] = (yt_ref[...] * jax.nn.sigmoid(z)).astype(o_ref.dtype)


def _glu(y, w, b, layer, tm=1024, tn=512):
    m, k = y.shape
    n = w.shape[-1]
    tm = _tile(m, tm)
    tn = _tile(n, tn)
    return pl.pallas_call(
        _glu_kernel, grid=(n // tn, m // tm),
        in_specs=[pl.BlockSpec((tm, k), lambda j, i: (i, 0)),
                  pl.BlockSpec((tm, tn), lambda j, i: (i, j)),
                  _w_spec(w, layer, k, tn),
                  pl.BlockSpec((1, tn), lambda j, i: (0, j))],
        out_specs=pl.BlockSpec((tm, tn), lambda j, i: (i, j)),
        out_shape=jax.ShapeDtypeStruct((m, n), BF16),
        scratch_shapes=[pltpu.VMEM((k, tn), BF16)], name="s5_glu",
        compiler_params=_cp("parallel", "arbitrary"),
    )(y, y, w, b.reshape(1, n))


def _merge_kernel(a1_ref, a2_ref, w1_ref, w2_ref, ga_ref, gb_ref, o_ref, w1bf_ref, w2bf_ref):
    @pl.when(pl.program_id(1) == 0)
    def _():
        w1bf_ref[...] = w1_ref[...].astype(BF16)
        w2bf_ref[...] = w2_ref[...].astype(BF16)

    ya = jnp.dot(a1_ref[...], w1bf_ref[...], preferred_element_type=F32)
    yb = jnp.dot(a2_ref[...], w2bf_ref[...], preferred_element_type=F32)
    o_ref[...] = (jax.nn.sigmoid(ga_ref[...]) * ya + jax.nn.sigmoid(gb_ref[...]) * yb).astype(o_ref.dtype)


def _merge(a1, a2, w1, w2, gates, layer, tm=1024, tn=512):
    m, k1 = a1.shape
    k2 = a2.shape[1]
    n = w1.shape[-1]
    tm = _tile(m, tm)
    tn = _tile(n, tn)
    nb = n // tn
    return pl.pallas_call(
        _merge_kernel, grid=(nb, m // tm),
        in_specs=[pl.BlockSpec((tm, k1), lambda j, i: (i, 0)),
                  pl.BlockSpec((tm, k2), lambda j, i: (i, 0)),
                  _w_spec(w1, layer, k1, tn),
                  _w_spec(w2, layer, k2, tn),
                  pl.BlockSpec((tm, tn), lambda j, i: (i, j)),
                  pl.BlockSpec((tm, tn), lambda j, i: (i, j + nb))],
        out_specs=pl.BlockSpec((tm, tn), lambda j, i: (i, j)),
        out_shape=jax.ShapeDtypeStruct((m, n), BF16),
        scratch_shapes=[pltpu.VMEM((k1, tn), BF16), pltpu.VMEM((k2, tn), BF16)], name="branch_merge",
        compiler_params=_cp("parallel", "arbitrary"),
    )(a1, a2, w1, w2, gates, gates)


def _s5_discretize(lam_re, lam_im, b_re, b_im, c_re, c_im, d, log_step, t_chunk):
    hp = lax.Precision.HIGHEST
    g, p, h = b_re.shape
    na = LANES // h
    nx = g // na
    assert 2 * p == LANES
    lam = lax.complex(jnp.minimum(lam_re, -1e-4), lam_im)
    dt = jnp.exp(log_step)[:, None]
    lam_bar = jnp.exp(lam * dt)
    b_bar = ((lam_bar - 1.0) / lam)[:, :, None] * lax.complex(b_re, b_im)
    c = lax.complex(c_re, c_im)
    tau = jnp.arange(t_chunk + 1, dtype=F32)
    pw = jnp.exp((lam * dt)[:, None, :] * tau[None, :, None].astype(jnp.complex64))
    cp = c[:, None, :, :] * pw[:, :, None, :]
    kern = jnp.einsum('gtop,gpi->gtoi', cp[:, :t_chunk], b_bar, precision=hp).real
    kio = jnp.transpose(kern.reshape(nx, na, t_chunk, h, h), (0, 2, 1, 4, 3))
    kt = jnp.tile(kio.reshape(nx, t_chunk, LANES, h), (1, 1, 1, na))
    wcx = pw[:, t_chunk - 1 - np.arange(t_chunk)][:, :, :, None] * b_bar[:, None, :, :]
    wcx = jnp.transpose(wcx, (0, 1, 3, 2))
    w_ri = jnp.concatenate([wcx.real, wcx.imag], axis=-1).reshape(nx, na, t_chunk, h, 2 * p)
    wc = jnp.transpose(w_ri, (0, 2, 1, 3, 4)).reshape(nx, t_chunk, LANES, 2 * p)
    cl = jnp.transpose(cp[:, 1:t_chunk + 1], (0, 3, 1, 2))
    vc = jnp.concatenate([cl.real, -cl.imag], axis=1).reshape(nx, na, 2 * p, t_chunk * h)
    dd = jnp.broadcast_to(d.reshape(nx, 1, 1, na, h), (nx, 1, t_chunk, na, h)).reshape(nx, 1, t_chunk * LANES)
    a = pw[:, t_chunk]
    a1 = jnp.concatenate([a.real, a.real], axis=-1).reshape(1, g * 2 * p)
    a2 = jnp.concatenate([-a.imag, a.imag], axis=-1).reshape(1, g * 2 * p)
    return kt, wc, vc, dd, a1, a2


def _chunk_rows(u_ref, t, nc):
    return jnp.concatenate([u_ref[pl.ds(s, nc, stride=t), :] for s in range(t)], axis=1)


def _s5_state_in_kernel(u_ref, wc_ref, xa_ref, xb_ref, wa_ref, wb_ref, *, t, nc, h):
    na = LANES // h
    row_group = lax.broadcasted_iota(I32, (LANES, LANES), 0) // h
    for s in range(t):
        blk = wc_ref[s]
        swapped = pltpu.roll(blk, LANES // 2, axis=1)
        for b in range(na):
            own = row_group == b
            dst = (slice(s * LANES, (s + 1) * LANES), slice(b * LANES, (b + 1) * LANES))
            wa_ref[dst] = jnp.where(own, blk, 0.0).astype(BF16)
            wb_ref[dst] = jnp.where(own, swapped, 0.0).astype(BF16)
    ucat = _chunk_rows(u_ref, t, nc).astype(BF16)
    xa_ref[...] = jnp.dot(ucat, wa_ref[...], preferred_element_type=F32)
    xb_ref[...] = jnp.dot(ucat, wb_ref[...], preferred_element_type=F32)


def _s5_scan_kernel(xa_ref, xb_ref, a1_ref, a2_ref, s_ref, *, unroll):
    a1 = a1_ref[...]
    a2 = a2_ref[...]
    nc = xa_ref.shape[0]

    def body(cb, carry):
        s_a, s_b = carry
        base = pl.multiple_of(cb * unroll, unroll)
        xa = xa_ref[pl.ds(base, unroll), :]
        xb = xb_ref[pl.ds(base, unroll), :]
        rows = []
        for k in range(unroll):
            rows.append(s_a)
            s_a, s_b = (a1 * s_a + a2 * s_b + xa[k:k + 1, :],
                        a1 * s_b - a2 * s_a + xb[k:k + 1, :])
        s_ref[pl.ds(base, unroll), :] = jnp.concatenate(rows, axis=0)
        return s_a, s_b

    z = jnp.zeros(a1.shape, F32)
    lax.fori_loop(0, nc // unroll, body, (z, z))


def _s5_out_kernel(u_ref, sp_ref, kt_ref, vc_ref, dd_ref, y_ref, bd_ref, vm_ref, *, t, nc, h, n_parts):
    part = pl.program_id(1)
    tp = t // n_parts
    na = LANES // h
    w = tp * LANES
    same_group = (lax.broadcasted_iota(I32, (LANES, LANES), 0) // h
                  == lax.broadcasted_iota(I32, (LANES, LANES), 1) // h)
    row = lax.broadcasted_iota(I32, (LANES, w), 0)
    col = lax.broadcasted_iota(I32, (LANES, w), 1)
    landing_group = jnp.where((row // h == col // LANES) & (row % h == col % h), (col // h) % na, -1)
    ucat = _chunk_rows(u_ref, t, nc)
    for k in range(n_parts):
        @pl.when(part == k)
        def _(k=k):
            live = (k + 1) * tp
            for s in range(live):
                for tt in range(tp):
                    lag = k * tp + tt - s
                    dst = (slice(s * LANES, (s + 1) * LANES), slice(tt * LANES, (tt + 1) * LANES))
                    if lag >= 0:
                        bd_ref[dst] = jnp.where(same_group, kt_ref[lag], 0.0).astype(BF16)
                    else:
                        bd_ref[dst] = jnp.zeros((LANES, LANES), BF16)
            for a in range(na):
                place = jnp.where(landing_group == a, 1.0, 0.0).astype(BF16)
                src = vc_ref[a][:, k * tp * h:(k + 1) * tp * h].astype(BF16)
                vm_ref[a * LANES:(a + 1) * LANES, :] = jnp.dot(src, place, preferred_element_type=F32).astype(BF16)
            y = jnp.dot(ucat[:, :live * LANES].astype(BF16), bd_ref[:live * LANES, :], preferred_element_type=F32)
            y = y + jnp.dot(sp_ref[...].astype(BF16), vm_ref[...], preferred_element_type=F32)
            y = jax.nn.gelu(y + dd_ref[...] * ucat[:, k * w:(k + 1) * w])
            for s in range(tp):
                y_ref[pl.ds(k * tp + s, nc, stride=t), :] = y[:, s * LANES:(s + 1) * LANES]


def _s5_branch(u, prm):
    kt, wc, vc, dd, a1, a2 = prm
    L, width = u.shape
    nx, na, _, th = vc.shape
    t = kt.shape[1]
    h = th // t
    tah = t * LANES
    sw = na * LANES
    nc = L // t
    u_spec = pl.BlockSpec((L, LANES), lambda x: (0, x))
    st_spec = pl.BlockSpec((nc, sw), lambda x: (0, x))
    xa, xb = pl.pallas_call(
        functools.partial(_s5_state_in_kernel, t=t, nc=nc, h=h), grid=(nx,),
        in_specs=[u_spec,
                  pl.BlockSpec((None, t, LANES, LANES), lambda x: (x, 0, 0, 0))],
        out_specs=[st_spec, st_spec],
        out_shape=[jax.ShapeDtypeStruct((nc, nx * sw), F32)] * 2,
        scratch_shapes=[pltpu.VMEM((tah, sw), BF16), pltpu.VMEM((tah, sw), BF16)], name="s5_state_in",
        compiler_params=_cp("parallel"),
    )(u, wc)
    ws = _tile(nx * sw, 2048)
    unroll = _tile(nc, SUBLANES)
    sc_spec = pl.BlockSpec((nc, ws), lambda x: (0, x))
    a_spec = pl.BlockSpec((1, ws), lambda x: (0, x))
    sp = pl.pallas_call(
        functools.partial(_s5_scan_kernel, unroll=unroll), grid=(nx * sw // ws,),
        in_specs=[sc_spec, sc_spec, a_spec, a_spec],
        out_specs=sc_spec,
        out_shape=jax.ShapeDtypeStruct((nc, nx * sw), F32), name="s5_scan",
        compiler_params=_cp("parallel"),
    )(xa, xb, a1, a2)
    n_parts = 2
    tw = tah // n_parts
    return pl.pallas_call(
        functools.partial(_s5_out_kernel, t=t, nc=nc, h=h, n_parts=n_parts), grid=(nx, n_parts),
        in_specs=[pl.BlockSpec((L, LANES), lambda x, k: (0, x)),
                  pl.BlockSpec((nc, sw), lambda x, k: (0, x)),
                  pl.BlockSpec((None, t, LANES, LANES), lambda x, k: (x, 0, 0, 0)),
                  pl.BlockSpec((None, na, LANES, th), lambda x, k: (x, 0, 0, 0)),
                  pl.BlockSpec((None, 1, tw), lambda x, k: (x, 0, k))],
        out_specs=pl.BlockSpec((L, LANES), lambda x, k: (0, x)),
        out_shape=jax.ShapeDtypeStruct((L, width), F32),
        scratch_shapes=[pltpu.VMEM((tah, tw), BF16), pltpu.VMEM((sw, tw), BF16)], name="s5_out",
        compiler_params=_cp("parallel", "arbitrary"),
    )(u, sp, kt, vc, dd)


def _t5_bucket_starts(n_buckets):
    exact = n_buckets // 2
    dist = np.arange(0, 4 * REL_MAX_DIST).astype(np.int64)
    far = exact + (np.log(np.maximum(dist, 1).astype(np.float32) / np.float32(exact))
                   / np.float32(math.log(REL_MAX_DIST / exact)) * np.float32(n_buckets - exact)).astype(np.int64)
    bucket = np.where(dist < exact, dist, np.minimum(far, n_buckets - 1))
    return [int(np.argmax(bucket >= b)) for b in range(1, n_buckets)]


def _bias_tile_kernel(rb_ref, o_ref, *, tb, starts):
    delta = jnp.where(pl.program_id(0) == 3, -1, pl.program_id(0))
    head = pl.program_id(1)
    key = lax.broadcasted_iota(I32, (tb, tb), 0)
    qry = lax.broadcasted_iota(I32, (tb, tb), 1)
    dist = delta * tb + qry - key
    bias = jnp.full((tb, tb), rb_ref[0, head], F32)
    for b, start in enumerate(starts, start=1):
        bias = jnp.where(dist >= start, rb_ref[b, head], bias)
    o_ref[...] = jnp.where(dist >= 0, bias * LOG2E, NEG_BIG)


def _bias_tiles(rel_bias, tb):
    n_buckets, heads = rel_bias.shape
    starts = _t5_bucket_starts(n_buckets)
    assert 2 * tb - (tb - 1) >= starts[-1]
    return pl.pallas_call(
        functools.partial(_bias_tile_kernel, tb=tb, starts=starts), grid=(4, heads),
        in_specs=[pl.BlockSpec(memory_space=pltpu.SMEM)],
        out_specs=pl.BlockSpec((None, None, tb, tb), lambda dl, hd: (dl, hd, 0, 0)),
        out_shape=jax.ShapeDtypeStruct((4, heads, tb, tb), F32), name="rel_bias_tiles",
        compiler_params=_cp("parallel", "parallel"),
    )(rel_bias)


def _indexer_kernel(qt_ref, k2_ref, wt_ref, keys_ref, thr_ref, *, tq, ck, n_heads, topk, w_scale):
    i = pl.program_id(0)
    n_blocks = keys_ref.shape[0] // tq
    n_live = (i + 1) * (tq // ck)
    wv = wt_ref[...] * w_scale

    def score_chunk(c, carry):
        row = pl.multiple_of(c * ck, ck)
        kk = k2_ref[c]
        acc = jnp.zeros((ck, tq), F32)
        for pr in range(n_heads // 2):
            s2 = jnp.dot(kk, qt_ref[pr * LANES:(pr + 1) * LANES, :], preferred_element_type=F32)
            acc = acc + jnp.maximum(s2[:ck], 0.0) * wv[2 * pr:2 * pr + 1, :]
            acc = acc + jnp.maximum(s2[ck:], 0.0) * wv[2 * pr + 1:2 * pr + 2, :]
        kpos = row + lax.broadcasted_iota(I32, (ck, tq), 0)
        qpos = i * tq + lax.broadcasted_iota(I32, (ck, tq), 1)
        acc = jnp.where(acc == 0.0, 0.0, acc)
        score = jnp.where(kpos <= qpos, acc, -jnp.inf)
        bits = pltpu.bitcast(score, I32)
        keys_ref[pl.ds(row, ck), :] = jnp.where(bits < 0, bits ^ 0x7FFFFFFF, bits)
        return carry

    lax.fori_loop(0, n_live, score_chunk, 0)

    def fill_block(b, carry):
        row = pl.multiple_of(b * tq, tq)
        keys_ref[pl.ds(row, tq), :] = jnp.full((tq, tq), INT_MIN, I32)
        return carry

    lax.fori_loop(i + 1, n_blocks, fill_block, 0)

    acc_rows = 32

    def count_ge(cand):
        def body(b, cnt):
            row = pl.multiple_of(b * tq, tq)
            hit = jnp.where(keys_ref[pl.ds(row, tq), :] >= cand, 1.0, 0.0)
            return cnt + jnp.sum(hit.reshape(tq // acc_rows, acc_rows, tq), axis=0)
        cnt = lax.fori_loop(0, i + 1, body, jnp.zeros((acc_rows, tq), F32))
        return jnp.sum(cnt, axis=0, keepdims=True)

    kf = float(topk)
    zero = jnp.zeros((1, tq), I32)
    c_zero = count_ge(zero)
    thr0 = jnp.where(c_zero >= kf, zero, jnp.full((1, tq), INT_MIN, I32))
    cnt0 = jnp.where(c_zero >= kf, c_zero, ((i + 1) * tq).astype(F32))

    def undecided(st):
        it, _, cnt = st
        return (it < 31) & (jnp.max(cnt) > kf)

    def bit_step(st):
        it, thr, cnt = st
        cand = thr | jnp.left_shift(jnp.int32(1), 30 - it)
        c = count_ge(cand)
        keep = c >= kf
        return it + 1, jnp.where(keep, cand, thr), jnp.where(keep, c, cnt)

    _, thr, cnt = lax.while_loop(undecided, bit_step, (jnp.int32(0), thr0, cnt0))

    n_keys = keys_ref.shape[0]

    def position_cutoff(_):
        need = kf - count_ge(thr + 1)

        def count_eq_upto(pos):
            def body(b, acc):
                row = pl.multiple_of(b * tq, tq)
                kpos = row + lax.broadcasted_iota(I32, (tq, tq), 0)
                hit = jnp.where(keys_ref[pl.ds(row, tq), :] == thr, jnp.where(kpos <= pos, 1.0, 0.0), 0.0)
                return acc + jnp.sum(hit.reshape(tq // acc_rows, acc_rows, tq), axis=0)
            acc = lax.fori_loop(0, i + 1, body, jnp.zeros((acc_rows, tq), F32))
            return jnp.sum(acc, axis=0, keepdims=True)

        def halve(_, bounds):
            lo, hi = bounds
            mid = (lo + hi) >> 1
            enough = count_eq_upto(mid) >= need
            return jnp.where(enough, lo, mid), jnp.where(enough, mid, hi)

        lo0 = jnp.full((1, tq), -1, I32)
        hi0 = jnp.full((1, tq), n_keys - 1, I32)
        _, hi = lax.fori_loop(0, max(1, (n_keys - 1).bit_length()), halve, (lo0, hi0))
        return jnp.where(cnt > kf, hi, n_keys)

    jcut = lax.cond(jnp.max(cnt) > kf, position_cutoff, lambda _: jnp.full((1, tq), n_keys, I32), 0)
    sel_row = lax.broadcasted_iota(I32, thr_ref.shape, 0)
    thr_ref[...] = jnp.where(sel_row == 0, thr, jnp.where(sel_row == 1, jcut, 0))


def _indexer(qi_t, row_off, k2, w_t, topk, tq=256):
    L = qi_t.shape[1]
    n_heads = w_t.shape[0]
    qw = n_heads * IDX_DIM
    assert row_off % qw == 0
    tq = _tile(L, tq)
    ck = k2.shape[1] // 2
    assert tq % ck == 0
    w_scale = (n_heads ** -0.5) * (IDX_DIM ** -0.5)
    return pl.pallas_call(
        functools.partial(_indexer_kernel, tq=tq, ck=ck, n_heads=n_heads, topk=topk, w_scale=w_scale),
        grid=(L // tq,),
        in_specs=[pl.BlockSpec((qw, tq), lambda i: (row_off // qw, i)),
                  pl.BlockSpec(k2.shape, lambda i: (0, 0, 0)),
                  pl.BlockSpec((n_heads, tq), lambda i: (0, i))],
        out_specs=[pl.BlockSpec((L, tq), lambda i: (0, i)),
                   pl.BlockSpec((SUBLANES, tq), lambda i: (0, i))],
        out_shape=[jax.ShapeDtypeStruct((L, L), I32), jax.ShapeDtypeStruct((SUBLANES, L), I32)],
        name="dsa_indexer", compiler_params=_cp("parallel"),
    )(qi_t, k2, w_t)


def _attn_kernel(qi_ref, kj_ref, far_ref, qt_ref, k_ref, vt_ref, keys_ref, thr_ref, bias_a_ref, bias_b_ref, o_ref,
                 m_ref, l_ref, acc_ref, mb_ref, *, heads, dh, scale2):
    step = pl.program_id(0)
    qi = qi_ref[step]
    kj = kj_ref[step]

    @pl.when(kj == 0)
    def _():
        m_ref[...] = jnp.full(m_ref.shape, -jnp.inf, F32)
        l_ref[...] = jnp.zeros(l_ref.shape, F32)
        acc_ref[...] = jnp.zeros(acc_ref.shape, F32)

    thr = thr_ref[0:1, :]
    kpos = kj * keys_ref.shape[0] + lax.broadcasted_iota(I32, keys_ref.shape, 0)
    bar = jnp.where(kpos <= thr_ref[1:2, :], thr, thr + 1)
    mb_ref[...] = jnp.where(keys_ref[...] >= bar, 0.0, NEG_BIG)

    def scores(h):
        hs = slice(h * dh, (h + 1) * dh)
        return jnp.dot(k_ref[:, hs], qt_ref[hs, :], preferred_element_type=F32)

    def fold(reduce_fn, a, rows=32):
        return reduce_fn(a.reshape(a.shape[0] // rows, rows, a.shape[1]), axis=0)

    def run(near):
        ahead = 4
        pending = [scores(h) for h in range(min(ahead, heads))]
        for h in range(heads):
            hs = slice(h * dh, (h + 1) * dh)
            x = pending.pop(0) * scale2 + mb_ref[...]
            if h + ahead < heads:
                pending.append(scores(h + ahead))
            if near:
                x = x + jnp.concatenate([bias_a_ref[h], bias_b_ref[h]], axis=0)
            far = 0.0 if near else far_ref[h]
            m_prev = m_ref[h]
            m_new = jnp.maximum(m_prev, jnp.max(fold(jnp.max, x), axis=0, keepdims=True) + far)
            alpha = jnp.exp2(m_prev - m_new)
            p = jnp.exp2(x - (m_new[0:1, :] - far))
            l_ref[h] = alpha * l_ref[h] + jnp.sum(fold(jnp.sum, p), axis=0, keepdims=True)
            m_ref[h] = m_new
            acc_ref[hs, :] = alpha[0:1, :] * acc_ref[hs, :] + jnp.dot(vt_ref[hs, :], p.astype(BF16),
                                                                      preferred_element_type=F32)

    @pl.when(qi - 2 * kj - 1 < 2)
    def _():
        run(True)

    @pl.when(qi - 2 * kj - 1 >= 2)
    def _():
        run(False)

    @pl.when(kj == qi // 2)
    def _():
        for h in range(heads):
            hs = slice(h * dh, (h + 1) * dh)
            o_ref[:, hs] = jnp.transpose(acc_ref[hs, :] / l_ref[h][0:1, :]).astype(o_ref.dtype)


def _attention(q_t, v_t, k, keys, thr, bias_tiles, far_bias, tb):
    L = k.shape[0]
    heads = bias_tiles.shape[1]
    dh = ATT_HEAD_DIM
    aw = heads * dh
    nb = L // tb
    assert nb % 2 == 0
    pairs = [(i, j) for i in range(nb) for j in range(i // 2 + 1)]
    qi = jnp.asarray(np.array([p[0] for p in pairs], np.int32))
    kj = jnp.asarray(np.array([p[1] for p in pairs], np.int32))

    def tile_of(behind):
        return jnp.where(behind < 0, 3, jnp.minimum(behind, 2))

    bias_spec_a = pl.BlockSpec((None, heads, tb, tb), lambda s, qi, kj: (tile_of(qi[s] - 2 * kj[s]), 0, 0, 0))
    bias_spec_b = pl.BlockSpec((None, heads, tb, tb), lambda s, qi, kj: (tile_of(qi[s] - 2 * kj[s] - 1), 0, 0, 0))
    grid_spec = pltpu.PrefetchScalarGridSpec(
        num_scalar_prefetch=2, grid=(len(pairs),),
        in_specs=[pl.BlockSpec(memory_space=pltpu.SMEM),
                  pl.BlockSpec((aw, tb), lambda s, qi, kj: (0, qi[s])),
                  pl.BlockSpec((2 * tb, aw), lambda s, qi, kj: (kj[s], 0)),
                  pl.BlockSpec((aw, 2 * tb), lambda s, qi, kj: (0, kj[s])),
                  pl.BlockSpec((2 * tb, tb), lambda s, qi, kj: (kj[s], qi[s])),
                  pl.BlockSpec((SUBLANES, tb), lambda s, qi, kj: (0, qi[s])),
                  bias_spec_a, bias_spec_b],
        out_specs=pl.BlockSpec((tb, aw), lambda s, qi, kj: (qi[s], 0)),
        scratch_shapes=[pltpu.VMEM((heads, SUBLANES, tb), F32),
                        pltpu.VMEM((heads, SUBLANES, tb), F32),
                        pltpu.VMEM((aw, tb), F32),
                        pltpu.VMEM((2 * tb, tb), F32)])
    return pl.pallas_call(
        functools.partial(_attn_kernel, heads=heads, dh=dh, scale2=dh ** -0.5 * LOG2E),
        grid_spec=grid_spec, out_shape=jax.ShapeDtypeStruct((L, aw), BF16), name="dsa_attention",
        compiler_params=_cp("arbitrary"),
    )(qi, kj, far_bias, q_t, k, v_t, keys, thr, bias_tiles, bias_tiles)


def _router_kernel(h_ref, w_ref, b_ref, g_ref, *, n_exp, top_k):
    logits = jnp.dot(h_ref[...], w_ref[...], preferred_element_type=F32) + b_ref[...]
    lane = lax.broadcasted_iota(I32, logits.shape, 1).astype(F32)
    work = jnp.where(lane < n_exp, logits, -jnp.inf)
    gates = jnp.zeros(logits.shape, F32)
    denom = None
    top = None
    for r in range(top_k):
        m = jnp.max(work, axis=1, keepdims=True)
        first = jnp.min(jnp.where(work == m, lane, float(LANES)), axis=1, keepdims=True)
        sel = lane == first
        if r == 0:
            top = m
        e = jnp.exp(m - top)
        gates = gates + jnp.where(sel, e, 0.0)
        denom = e if denom is None else denom + e
        work = jnp.where(sel, -jnp.inf, work)
    g_ref[...] = gates / denom


def _router(h, w_pad, b_pad, n_exp, tm=512):
    m, d = h.shape
    tm = _tile(m, tm)
    return pl.pallas_call(
        functools.partial(_router_kernel, n_exp=n_exp, top_k=TOP_K_EXPERTS), grid=(m // tm,),
        in_specs=[pl.BlockSpec((tm, d), lambda i: (i, 0)),
                  pl.BlockSpec((d, LANES), lambda i: (0, 0)),
                  pl.BlockSpec((1, LANES), lambda i: (0, 0))],
        out_specs=pl.BlockSpec((tm, LANES), lambda i: (i, 0)),
        out_shape=jax.ShapeDtypeStruct((m, LANES), F32), name="moe_router",
        compiler_params=_cp("parallel"),
    )(h, w_pad, b_pad)


def _moe_kernel(h_ref, g_ref, wgu_ref, bgu_ref, wdn_ref, bdn_ref, o_ref, *, ff, eb):
    step = pl.program_id(1)
    gates = g_ref[...]

    @pl.when(step == 0)
    def _():
        o_ref[...] = jnp.dot(gates.astype(BF16), bdn_ref[...], preferred_element_type=F32)

    lane = lax.broadcasted_iota(I32, gates.shape, 1)
    for k in range(eb):
        gu = jnp.dot(h_ref[...], wgu_ref[k], preferred_element_type=F32) + bgu_ref[k]
        gate = jnp.minimum(gu[:, :ff], SWIGLU_LIMIT)
        up = jnp.clip(gu[:, ff:], -SWIGLU_LIMIT, SWIGLU_LIMIT)
        act = (up + 1.0) * gate * jax.nn.sigmoid(SWIGLU_ALPHA * gate)
        ge = jnp.sum(jnp.where(lane == step * eb + k, gates, 0.0), axis=1, keepdims=True)
        o_ref[...] += jnp.dot((act * ge).astype(BF16), wdn_ref[k], preferred_element_type=F32)


def _moe_experts(h, gates, w_gu, b_gu, w_dn, b_dn_pad, tm=512, eb=2):
    m, d = h.shape
    n_exp, _, ff2 = w_gu.shape
    tm = _tile(m, tm)
    eb = _tile(n_exp, eb)
    return pl.pallas_call(
        functools.partial(_moe_kernel, ff=ff2 // 2, eb=eb), grid=(m // tm, n_exp // eb),
        in_specs=[pl.BlockSpec((tm, d), lambda i, e: (i, 0)),
                  pl.BlockSpec((tm, LANES), lambda i, e: (i, 0)),
                  pl.BlockSpec((eb, d, ff2), lambda i, e: (e, 0, 0)),
                  pl.BlockSpec((eb, 1, ff2), lambda i, e: (e, 0, 0)),
                  pl.BlockSpec((eb, ff2 // 2, d), lambda i, e: (e, 0, 0)),
                  pl.BlockSpec((LANES, d), lambda i, e: (0, 0))],
        out_specs=pl.BlockSpec((tm, d), lambda i, e: (i, 0)),
        out_shape=jax.ShapeDtypeStruct((m, d), F32), name="moe_experts",
        compiler_params=_cp("parallel", "arbitrary"),
    )(h, gates, w_gu, b_gu.reshape(n_exp, 1, ff2), w_dn, b_dn_pad)


def kernel(x, c, w_in, w_a_out, w_b_out, w_o, w_glu, b_glu, s5_lam_re, s5_lam_im, s5_b_re, s5_b_im, s5_c_re, s5_c_im, s5_d, s5_log_step, rel_bias, w_router, b_router, w_gu, b_gu, w_dn, b_dn, w_c1, b_c1, w_c2, b_c2, mod_table, ln_g, ln_b):
    bsz, L, d = x.shape
    assert bsz == 1 and c.shape[0] == 1
    depth = w_in.shape[0]
    g, p, h = s5_b_re.shape[1:]
    s5w = g * h
    heads = rel_bias.shape[1]
    aw = heads * ATT_HEAD_DIM
    iw = IDX_HEADS * IDX_DIM
    n_exp = w_router.shape[2]
    assert w_in.shape[2] == s5w + 3 * aw + iw + IDX_DIM + IDX_HEADS + 2 * d
    assert n_exp <= LANES and IDX_DIM + IDX_HEADS <= LANES and 2 * IDX_DIM == LANES
    o_q, o_k, o_v, o_qi = s5w, s5w + aw, s5w + 2 * aw, s5w + 3 * aw
    o_ki = s5w + 3 * aw + iw
    o_gate = o_ki + IDX_DIM + IDX_HEADS
    topk = min(TOPK_MAX, L // 4)
    tb = _tile(L, 256)
    ck = _tile(L, LANES)
    dn_alpha = (2 * depth) ** 0.25

    mod = _conditioning(c, w_c1, b_c1, w_c2, b_c2, mod_table).reshape(depth, N_MOD, d)
    bias_tiles = _bias_tiles(rel_bias, tb)
    far_bias = rel_bias[-1] * LOG2E
    assert o_ki % LANES == 0 and (g * h) % LANES == 0 and LANES % h == 0

    xs = x.reshape(L, d)
    hcur = _modulate(xs, mod[0], sc_row=1, sh_row=0)
    for l in range(depth):
        w_l = w_in[l]
        w_main = w_l[:, :o_ki + LANES].astype(BF16)
        w_gate = w_l[:, o_gate:].astype(BF16)
        u = _matmul(hcur, w_main, out_dtype=F32, col_off=0, n_cols=s5w, name="proj_u")
        k = _matmul(hcur, w_main, out_dtype=BF16, col_off=o_k, n_cols=aw, name="proj_k")
        gates_ab = _matmul(hcur, w_gate, out_dtype=F32, name="proj_gates")
        misc = _matmul(hcur, w_main, out_dtype=F32, col_off=o_ki, n_cols=LANES, tn=LANES, name="proj_misc")
        q_t = _matmul_t(hcur, w_main, out_dtype=BF16, col_off=o_q, n_cols=aw, name="proj_q_t")
        vqi_t = _matmul_t(hcur, w_main, out_dtype=BF16, col_off=o_v, n_cols=aw + iw, name="proj_vqi_t")

        prm = _s5_discretize(s5_lam_re[l], s5_lam_im[l], s5_b_re[l], s5_b_im[l], s5_c_re[l], s5_c_im[l],
                             s5_d[l], s5_log_step[l], S5_CHUNK)
        y_s5 = _s5_branch(u, prm)
        ya_in = _glu(y_s5, w_glu, b_glu[l], l)

        k_idx = misc[:, :IDX_DIM].astype(BF16)
        zeros = jnp.zeros_like(k_idx)
        k_even = jnp.concatenate([k_idx, zeros], axis=1).reshape(L // ck, ck, LANES)
        k_odd = jnp.concatenate([zeros, k_idx], axis=1).reshape(L // ck, ck, LANES)
        k2 = jnp.concatenate([k_even, k_odd], axis=1)
        w_t = jnp.transpose(misc[:, IDX_DIM:IDX_DIM + IDX_HEADS])
        keys, thr = _indexer(vqi_t, aw, k2, w_t, topk)
        attn = _attention(q_t, vqi_t, k, keys, thr, bias_tiles, far_bias, tb)

        merged = _merge(ya_in, attn, w_a_out, w_b_out, gates_ab, l)
        y = _matmul(merged, w_o, layer=l, out_dtype=F32, name="proj_out")
        xs, h2 = _ln_modulate(xs, y, mod[l], ln_g[l], ln_b[l], mod[l], alpha=dn_alpha, gate_row=2, ln_row=0,
                              sc_row=4, sh_row=3, want_h=True)

        w_r = jnp.pad(w_router[l], ((0, 0), (0, LANES - n_exp))).astype(BF16)
        b_r = jnp.pad(b_router[l], (0, LANES - n_exp)).reshape(1, LANES)
        gates = _router(h2, w_r, b_r, n_exp)
        b_dn_pad = jnp.pad(b_dn[l], ((0, LANES - n_exp), (0, 0))).astype(BF16)
        y = _moe_experts(h2, gates, w_gu[l].astype(BF16), b_gu[l], w_dn[l].astype(BF16), b_dn_pad)
        last = l == depth - 1
        xs, hcur = _ln_modulate(xs, y, mod[l], ln_g[l], ln_b[l], mod[min(l + 1, depth - 1)], alpha=dn_alpha,
                                gate_row=5, ln_row=1, sc_row=1, sh_row=0, want_h=not last)
    return xs.reshape(bsz, L, d)
```

```python
import functools
import math

import numpy as np
import jax
import jax.numpy as jnp
from jax import lax
from jax.experimental import pallas as pl
from jax.experimental.pallas import tpu as pltpu

F32 = jnp.float32
BF16 = jnp.bfloat16
I32 = jnp.int32

ATT_HEAD_DIM = 128
IDX_HEADS = 32
IDX_DIM = 64
TOPK_MAX = 256
REL_MAX_DIST = 128
TOP_K_EXPERTS = 4
SWIGLU_LIMIT = 7.0
SWIGLU_ALPHA = 1.702
N_MOD = 6
LN_EPS = 1e-5

SUBLANES = 8
LANES = 128
VMEM_LIMIT_BYTES = 56 * 1024 * 1024
S5_CHUNK = 16
NEG_BIG = -1e30
LOG2E = math.log2(math.e)
INT_MIN = -(2 ** 31)


def _cp(*sem):
    return pltpu.CompilerParams(dimension_semantics=sem, vmem_limit_bytes=VMEM_LIMIT_BYTES)


def _tile(n, t):
    t = min(n, t)
    while n % t:
        t //= 2
    return t


def _cond1_kernel(c_ref, w_ref, b_ref, o_ref):
    a = jax.nn.silu(c_ref[...]).astype(BF16)
    t = jnp.dot(a, w_ref[...].astype(BF16), preferred_element_type=F32) + b_ref[...]
    o_ref[...] = jax.nn.silu(t)


def _cond2_kernel(t_ref, w_ref, b_ref, tab_ref, o_ref):
    acc = jnp.dot(t_ref[...].astype(BF16), w_ref[...].astype(BF16), preferred_element_type=F32)
    o_ref[...] = acc[0:1, :] + b_ref[...] + tab_ref[...]


def _conditioning(c, w_c1, b_c1, w_c2, b_c2, mod_table):
    d = c.shape[1]
    r = w_c1.shape[1]
    depth = mod_table.shape[0]
    n = w_c2.shape[1]
    c8 = jnp.broadcast_to(c, (SUBLANES, d))
    t = pl.pallas_call(
        _cond1_kernel, out_shape=jax.ShapeDtypeStruct((SUBLANES, r), F32), name="cond1",
        compiler_params=_cp(),
    )(c8, w_c1, b_c1.reshape(1, r))
    tn = _tile(n, 2048)
    return pl.pallas_call(
        _cond2_kernel, grid=(n // tn,),
        in_specs=[pl.BlockSpec((SUBLANES, r), lambda j: (0, 0)),
                  pl.BlockSpec((r, tn), lambda j: (0, j)),
                  pl.BlockSpec((1, tn), lambda j: (0, j)),
                  pl.BlockSpec((depth, tn), lambda j: (0, j))],
        out_specs=pl.BlockSpec((depth, tn), lambda j: (0, j)),
        out_shape=jax.ShapeDtypeStruct((depth, n), F32), name="cond2",
        compiler_params=_cp("parallel"),
    )(t, w_c2, b_c2.reshape(1, n), mod_table.reshape(depth, n))


def _modulate_kernel(x_ref, mod_ref, h_ref, *, sc_row, sh_row):
    sc = mod_ref[sc_row:sc_row + 1, :]
    sh = mod_ref[sh_row:sh_row + 1, :]
    h_ref[...] = (x_ref[...] * (1.0 + sc) + sh).astype(h_ref.dtype)


def _modulate(x, mod_l, sc_row, sh_row, tm=256):
    m, d = x.shape
    tm = _tile(m, tm)
    return pl.pallas_call(
        functools.partial(_modulate_kernel, sc_row=sc_row, sh_row=sh_row),
        grid=(m // tm,),
        in_specs=[pl.BlockSpec((tm, d), lambda i: (i, 0)),
                  pl.BlockSpec((N_MOD, d), lambda i: (0, 0))],
        out_specs=pl.BlockSpec((tm, d), lambda i: (i, 0)),
        out_shape=jax.ShapeDtypeStruct((m, d), BF16), name="modulate",
        compiler_params=_cp("parallel"),
    )(x, mod_l)


def _ln_kernel(x_ref, y_ref, mod_ref, g_ref, b_ref, nmod_ref, xo_ref, *h_refs,
               alpha, gate_row, ln_row, sc_row, sh_row):
    gate = mod_ref[gate_row:gate_row + 1, :]
    u = alpha * x_ref[...] + (1.0 + gate) * y_ref[...]
    mu = jnp.mean(u, axis=-1, keepdims=True)
    uc = u - mu
    var = jnp.mean(uc * uc, axis=-1, keepdims=True)
    xn = uc * lax.rsqrt(var + LN_EPS) * g_ref[ln_row:ln_row + 1, :] + b_ref[ln_row:ln_row + 1, :]
    xo_ref[...] = xn
    if h_refs:
        sc = nmod_ref[sc_row:sc_row + 1, :]
        sh = nmod_ref[sh_row:sh_row + 1, :]
        h_refs[0][...] = (xn * (1.0 + sc) + sh).astype(BF16)


def _ln_modulate(x, y, mod_l, ln_g_l, ln_b_l, next_mod, *, alpha, gate_row, ln_row, sc_row, sh_row,
                 want_h, tm=256):
    m, d = x.shape
    tm = _tile(m, tm)
    row = pl.BlockSpec((tm, d), lambda i: (i, 0))
    out_shape = [jax.ShapeDtypeStruct((m, d), F32)]
    out_specs = [row]
    if want_h:
        out_shape.append(jax.ShapeDtypeStruct((m, d), BF16))
        out_specs.append(row)
    res = pl.pallas_call(
        functools.partial(_ln_kernel, alpha=alpha, gate_row=gate_row, ln_row=ln_row,
                          sc_row=sc_row, sh_row=sh_row),
        grid=(m // tm,),
        in_specs=[row, row,
                  pl.BlockSpec((N_MOD, d), lambda i: (0, 0)),
                  pl.BlockSpec((2, d), lambda i: (0, 0)),
                  pl.BlockSpec((2, d), lambda i: (0, 0)),
                  pl.BlockSpec((N_MOD, d), lambda i: (0, 0))],
        out_specs=out_specs, out_shape=out_shape, name="ln_modulate",
        compiler_params=_cp("parallel"),
    )(x, y, mod_l, ln_g_l, ln_b_l, next_mod)
    return (res[0], res[1]) if want_h else (res[0], None)


def _mm_kernel(a_ref, w_ref, o_ref, wbf_ref):
    @pl.when(pl.program_id(1) == 0)
    def _():
        wbf_ref[...] = w_ref[...].astype(BF16)

    o_ref[...] = jnp.dot(a_ref[...], wbf_ref[...], preferred_element_type=F32).astype(o_ref.dtype)


def _w_spec(w, layer, k, tn, off=0, scale=1):
    if w.ndim == 3:
        return pl.BlockSpec((None, k, tn), lambda j, i: (layer, 0, (j + off) * scale))
    return pl.BlockSpec((k, tn), lambda j, i: (0, (j + off) * scale))


def _matmul(a, w, *, out_dtype, layer=0, col_off=0, n_cols=None, tm=1024, tn=512, name="matmul"):
    m, k = a.shape
    n_cols = w.shape[-1] - col_off if n_cols is None else n_cols
    tm = _tile(m, tm)
    tn = math.gcd(_tile(n_cols, tn), col_off) if col_off else _tile(n_cols, tn)
    off = col_off // tn
    return pl.pallas_call(
        _mm_kernel, grid=(n_cols // tn, m // tm),
        in_specs=[pl.BlockSpec((tm, k), lambda j, i: (i, 0)),
                  _w_spec(w, layer, k, tn, off)],
        out_specs=pl.BlockSpec((tm, tn), lambda j, i: (i, j)),
        out_shape=jax.ShapeDtypeStruct((m, n_cols), out_dtype),
        scratch_shapes=[pltpu.VMEM((k, tn), BF16)], name=name,
        compiler_params=_cp("parallel", "arbitrary"),
    )(a, w)


def _mm_shift_kernel(a_ref, w_ref, wn_ref, o_ref, wbf_ref, *, shift):
    @pl.when(pl.program_id(1) == 0)
    def _():
        both = jnp.concatenate([w_ref[...], wn_ref[...]], axis=1)
        wbf_ref[...] = both[:, shift:shift + wbf_ref.shape[1]]

    o_ref[...] = jnp.dot(a_ref[...], wbf_ref[...], preferred_element_type=F32).astype(o_ref.dtype)


def _matmul_shifted(a, w, *, out_dtype, col_off, n_cols, layer=0, tm=1024, tn=512, name="matmul_shifted"):
    m, k = a.shape
    tm = _tile(m, tm)
    base, shift = col_off // LANES * LANES, col_off % LANES
    tn = math.gcd(_tile(n_cols, tn), base) if base else _tile(n_cols, tn)
    assert tn % LANES == 0
    return pl.pallas_call(
        functools.partial(_mm_shift_kernel, shift=shift), grid=(n_cols // tn, m // tm),
        in_specs=[pl.BlockSpec((tm, k), lambda j, i: (i, 0)),
                  _w_spec(w, layer, k, tn, base // tn),
                  _w_spec(w, layer, k, LANES, 1 + base // tn, tn // LANES)],
        out_specs=pl.BlockSpec((tm, tn), lambda j, i: (i, j)),
        out_shape=jax.ShapeDtypeStruct((m, n_cols), out_dtype),
        scratch_shapes=[pltpu.VMEM((k, tn), w.dtype)], name=name,
        compiler_params=_cp("parallel", "arbitrary"),
    )(a, w, w)


def _mm_t_kernel(a_ref, w_ref, o_ref, wt_ref):
    @pl.when(pl.program_id(1) == 0)
    def _():
        wt_ref[...] = jnp.transpose(w_ref[...])

    o_ref[...] = lax.dot_general(wt_ref[...], a_ref[...], (((1,), (1,)), ((), ())),
                                 preferred_element_type=F32).astype(o_ref.dtype)


def _matmul_t(a, w, *, out_dtype, col_off, n_cols, layer=0, tm=1024, tn=1024, name="matmul_t"):
    m, k = a.shape
    tm = _tile(m, tm)
    tn = math.gcd(_tile(n_cols, tn), col_off) if col_off else _tile(n_cols, tn)
    off = col_off // tn
    return pl.pallas_call(
        _mm_t_kernel, grid=(n_cols // tn, m // tm),
        in_specs=[pl.BlockSpec((tm, k), lambda j, i: (i, 0)),
                  _w_spec(w, layer, k, tn, off)],
        out_specs=pl.BlockSpec((tn, tm), lambda j, i: (j, i)),
        out_shape=jax.ShapeDtypeStruct((n_cols, m), out_dtype),
        scratch_shapes=[pltpu.VMEM((tn, k), w.dtype)], name=name,
        compiler_params=_cp("parallel", "arbitrary"),
    )(a, w)


def _glu_kernel(y_ref, yt_ref, w_ref, b_ref, o_ref, wbf_ref):
    @pl.when(pl.program_id(1) == 0)
    def _():
        wbf_ref[...] = w_ref[...].astype(BF16)

    z = jnp.dot(y_ref[...].astype(BF16), wbf_ref[...], preferred_element_type=F32) + b_ref[...]
    o_ref[...] = (yt_ref[...] * jax.nn.sigmoid(z)).astype(o_ref.dtype)


def _glu(y, w, b, layer, tm=1024, tn=512):
    m, k = y.shape
    n = w.shape[-1]
    tm = _tile(m, tm)
    tn = _tile(n, tn)
    return pl.pallas_call(
        _glu_kernel, grid=(n // tn, m // tm),
        in_specs=[pl.BlockSpec((tm, k), lambda j, i: (i, 0)),
                  pl.BlockSpec((tm, tn), lambda j, i: (i, j)),
                  _w_spec(w, layer, k, tn),
                  pl.BlockSpec((1, tn), lambda j, i: (0, j))],
        out_specs=pl.BlockSpec((tm, tn), lambda j, i: (i, j)),
        out_shape=jax.ShapeDtypeStruct((m, n), BF16),
        scratch_shapes=[pltpu.VMEM((k, tn), BF16)], name="s5_glu",
        compiler_params=_cp("parallel", "arbitrary"),
    )(y, y, w, b.reshape(1, n))


def _merge_kernel(a1_ref, a2_ref, w1_ref, w2_ref, ga_ref, gb_ref, o_ref, w1bf_ref, w2bf_ref):
    @pl.when(pl.program_id(1) == 0)
    def _():
        w1bf_ref[...] = w1_ref[...].astype(BF16)
        w2bf_ref[...] = w2_ref[...].astype(BF16)

    ya = jnp.dot(a1_ref[...], w1bf_ref[...], preferred_element_type=F32)
    yb = jnp.dot(a2_ref[...], w2bf_ref[...], preferred_element_type=F32)
    o_ref[...] = (jax.nn.sigmoid(ga_ref[...]) * ya + jax.nn.sigmoid(gb_ref[...]) * yb).astype(o_ref.dtype)


def _merge(a1, a2, w1, w2, gates, layer, tm=1024, tn=512):
    m, k1 = a1.shape
    k2 = a2.shape[1]
    n = w1.shape[-1]
    tm = _tile(m, tm)
    tn = _tile(n, tn)
    nb = n // tn
    return pl.pallas_call(
        _merge_kernel, grid=(nb, m // tm),
        in_specs=[pl.BlockSpec((tm, k1), lambda j, i: (i, 0)),
                  pl.BlockSpec((tm, k2), lambda j, i: (i, 0)),
                  _w_spec(w1, layer, k1, tn),
                  _w_spec(w2, layer, k2, tn),
                  pl.BlockSpec((tm, tn), lambda j, i: (i, j)),
                  pl.BlockSpec((tm, tn), lambda j, i: (i, j + nb))],
        out_specs=pl.BlockSpec((tm, tn), lambda j, i: (i, j)),
        out_shape=jax.ShapeDtypeStruct((m, n), BF16),
        scratch_shapes=[pltpu.VMEM((k1, tn), BF16), pltpu.VMEM((k2, tn), BF16)], name="branch_merge",
        compiler_params=_cp("parallel", "arbitrary"),
    )(a1, a2, w1, w2, gates, gates)


def _s5_discretize(lam_re, lam_im, b_re, b_im, c_re, c_im, d, log_step, t_chunk):
    hp = lax.Precision.HIGHEST
    g, p, h = b_re.shape
    na = LANES // h
    nx = g // na
    assert 2 * p == LANES
    lam = lax.complex(jnp.minimum(lam_re, -1e-4), lam_im)
    dt = jnp.exp(log_step)[:, None]
    lam_bar = jnp.exp(lam * dt)
    b_bar = ((lam_bar - 1.0) / lam)[:, :, None] * lax.complex(b_re, b_im)
    c = lax.complex(c_re, c_im)
    tau = jnp.arange(t_chunk + 1, dtype=F32)
    pw = jnp.exp((lam * dt)[:, None, :] * tau[None, :, None].astype(jnp.complex64))
    cp = c[:, None, :, :] * pw[:, :, None, :]
    kern = jnp.einsum('gtop,gpi->gtoi', cp[:, :t_chunk], b_bar, precision=hp).real
    kio = jnp.transpose(kern.reshape(nx, na, t_chunk, h, h), (0, 2, 1, 4, 3))
    kt = jnp.tile(kio.reshape(nx, t_chunk, LANES, h), (1, 1, 1, na))
    wcx = pw[:, t_chunk - 1 - np.arange(t_chunk)][:, :, :, None] * b_bar[:, None, :, :]
    wcx = jnp.transpose(wcx, (0, 1, 3, 2))
    w_ri = jnp.concatenate([wcx.real, wcx.imag], axis=-1).reshape(nx, na, t_chunk, h, 2 * p)
    wc = jnp.transpose(w_ri, (0, 2, 1, 3, 4)).reshape(nx, t_chunk, LANES, 2 * p)
    cl = jnp.transpose(cp[:, 1:t_chunk + 1], (0, 3, 1, 2))
    vc = jnp.concatenate([cl.real, -cl.imag], axis=1).reshape(nx, na, 2 * p, t_chunk * h)
    dd = jnp.broadcast_to(d.reshape(nx, 1, 1, na, h), (nx, 1, t_chunk, na, h)).reshape(nx, 1, t_chunk * LANES)
    a = pw[:, t_chunk]
    a1 = jnp.concatenate([a.real, a.real], axis=-1).reshape(1, g * 2 * p)
    a2 = jnp.concatenate([-a.imag, a.imag], axis=-1).reshape(1, g * 2 * p)
    return kt, wc, vc, dd, a1, a2


def _chunk_rows(u_ref, t, nc):
    return jnp.concatenate([u_ref[pl.ds(s, nc, stride=t), :] for s in range(t)], axis=1)


def _s5_state_in_kernel(u_ref, wc_ref, xa_ref, xb_ref, wa_ref, wb_ref, *, t, nc, h):
    na = LANES // h
    row_group = lax.broadcasted_iota(I32, (LANES, LANES), 0) // h
    for s in range(t):
        blk = wc_ref[s]
        swapped = pltpu.roll(blk, LANES // 2, axis=1)
        for b in range(na):
            own = row_group == b
            dst = (slice(s * LANES, (s + 1) * LANES), slice(b * LANES, (b + 1) * LANES))
            wa_ref[dst] = jnp.where(own, blk, 0.0).astype(BF16)
            wb_ref[dst] = jnp.where(own, swapped, 0.0).astype(BF16)
    ucat = _chunk_rows(u_ref, t, nc).astype(BF16)
    xa_ref[...] = jnp.dot(ucat, wa_ref[...], preferred_element_type=F32)
    xb_ref[...] = jnp.dot(ucat, wb_ref[...], preferred_element_type=F32)


def _s5_scan_kernel(xa_ref, xb_ref, a1_ref, a2_ref, s_ref, *, unroll):
    a1 = a1_ref[...]
    a2 = a2_ref[...]
    nc = xa_ref.shape[0]

    def body(cb, carry):
        s_a, s_b = carry
        base = pl.multiple_of(cb * unroll, unroll)
        xa = xa_ref[pl.ds(base, unroll), :]
        xb = xb_ref[pl.ds(base, unroll), :]
        rows = []
        for k in range(unroll):
            rows.append(s_a)
            s_a, s_b = (a1 * s_a + a2 * s_b + xa[k:k + 1, :],
                        a1 * s_b - a2 * s_a + xb[k:k + 1, :])
        s_ref[pl.ds(base, unroll), :] = jnp.concatenate(rows, axis=0)
        return s_a, s_b

    z = jnp.zeros(a1.shape, F32)
    lax.fori_loop(0, nc // unroll, body, (z, z))


def _s5_out_kernel(u_ref, sp_ref, kt_ref, vc_ref, dd_ref, y_ref, bd_ref, vm_ref, *, t, nc, h, n_parts):
    part = pl.program_id(1)
    tp = t // n_parts
    na = LANES // h
    w = tp * LANES
    same_group = (lax.broadcasted_iota(I32, (LANES, LANES), 0) // h
                  == lax.broadcasted_iota(I32, (LANES, LANES), 1) // h)
    row = lax.broadcasted_iota(I32, (LANES, w), 0)
    col = lax.broadcasted_iota(I32, (LANES, w), 1)
    landing_group = jnp.where((row // h == col // LANES) & (row % h == col % h), (col // h) % na, -1)
    ucat = _chunk_rows(u_ref, t, nc)
    for k in range(n_parts):
        @pl.when(part == k)
        def _(k=k):
            live = (k + 1) * tp
            for s in range(live):
                for tt in range(tp):
                    lag = k * tp + tt - s
                    dst = (slice(s * LANES, (s + 1) * LANES), slice(tt * LANES, (tt + 1) * LANES))
                    if lag >= 0:
                        bd_ref[dst] = jnp.where(same_group, kt_ref[lag], 0.0).astype(BF16)
                    else:
                        bd_ref[dst] = jnp.zeros((LANES, LANES), BF16)
            for a in range(na):
                place = jnp.where(landing_group == a, 1.0, 0.0).astype(BF16)
                src = vc_ref[a][:, k * tp * h:(k + 1) * tp * h].astype(BF16)
                vm_ref[a * LANES:(a + 1) * LANES, :] = jnp.dot(src, place, preferred_element_type=F32).astype(BF16)
            y = jnp.dot(ucat[:, :live * LANES].astype(BF16), bd_ref[:live * LANES, :], preferred_element_type=F32)
            y = y + jnp.dot(sp_ref[...].astype(BF16), vm_ref[...], preferred_element_type=F32)
            y = jax.nn.gelu(y + dd_ref[...] * ucat[:, k * w:(k + 1) * w])
            for s in range(tp):
                y_ref[pl.ds(k * tp + s, nc, stride=t), :] = y[:, s * LANES:(s + 1) * LANES]


def _s5_branch(u, prm):
    kt, wc, vc, dd, a1, a2 = prm
    L, width = u.shape
    nx, na, _, th = vc.shape
    t = kt.shape[1]
    h = th // t
    tah = t * LANES
    sw = na * LANES
    nc = L // t
    u_spec = pl.BlockSpec((L, LANES), lambda x: (0, x))
    st_spec = pl.BlockSpec((nc, sw), lambda x: (0, x))
    xa, xb = pl.pallas_call(
        functools.partial(_s5_state_in_kernel, t=t, nc=nc, h=h), grid=(nx,),
        in_specs=[u_spec,
                  pl.BlockSpec((None, t, LANES, LANES), lambda x: (x, 0, 0, 0))],
        out_specs=[st_spec, st_spec],
        out_shape=[jax.ShapeDtypeStruct((nc, nx * sw), F32)] * 2,
        scratch_shapes=[pltpu.VMEM((tah, sw), BF16), pltpu.VMEM((tah, sw), BF16)], name="s5_state_in",
        compiler_params=_cp("parallel"),
    )(u, wc)
    ws = _tile(nx * sw, 2048)
    unroll = _tile(nc, SUBLANES)
    sc_spec = pl.BlockSpec((nc, ws), lambda x: (0, x))
    a_spec = pl.BlockSpec((1, ws), lambda x: (0, x))
    sp = pl.pallas_call(
        functools.partial(_s5_scan_kernel, unroll=unroll), grid=(nx * sw // ws,),
        in_specs=[sc_spec, sc_spec, a_spec, a_spec],
        out_specs=sc_spec,
        out_shape=jax.ShapeDtypeStruct((nc, nx * sw), F32), name="s5_scan",
        compiler_params=_cp("parallel"),
    )(xa, xb, a1, a2)
    n_parts = 2
    tw = tah // n_parts
    return pl.pallas_call(
        functools.partial(_s5_out_kernel, t=t, nc=nc, h=h, n_parts=n_parts), grid=(nx, n_parts),
        in_specs=[pl.BlockSpec((L, LANES), lambda x, k: (0, x)),
                  pl.BlockSpec((nc, sw), lambda x, k: (0, x)),
                  pl.BlockSpec((None, t, LANES, LANES), lambda x, k: (x, 0, 0, 0)),
                  pl.BlockSpec((None, na, LANES, th), lambda x, k: (x, 0, 0, 0)),
                  pl.BlockSpec((None, 1, tw), lambda x, k: (x, 0, k))],
        out_specs=pl.BlockSpec((L, LANES), lambda x, k: (0, x)),
        out_shape=jax.ShapeDtypeStruct((L, width), F32),
        scratch_shapes=[pltpu.VMEM((tah, tw), BF16), pltpu.VMEM((sw, tw), BF16)], name="s5_out",
        compiler_params=_cp("parallel", "arbitrary"),
    )(u, sp, kt, vc, dd)


def _t5_bucket_starts(n_buckets):
    exact = n_buckets // 2
    dist = np.arange(0, 4 * REL_MAX_DIST).astype(np.int64)
    far = exact + (np.log(np.maximum(dist, 1).astype(np.float32) / np.float32(exact))
                   / np.float32(math.log(REL_MAX_DIST / exact)) * np.float32(n_buckets - exact)).astype(np.int64)
    bucket = np.where(dist < exact, dist, np.minimum(far, n_buckets - 1))
    return [int(np.argmax(bucket >= b)) for b in range(1, n_buckets)]


def _bias_tile_kernel(rb_ref, o_ref, *, tb, starts):
    delta = jnp.where(pl.program_id(0) == 3, -1, pl.program_id(0))
    head = pl.program_id(1)
    key = lax.broadcasted_iota(I32, (tb, tb), 0)
    qry = lax.broadcasted_iota(I32, (tb, tb), 1)
    dist = delta * tb + qry - key
    bias = jnp.full((tb, tb), rb_ref[0, head], F32)
    for b, start in enumerate(starts, start=1):
        bias = jnp.where(dist >= start, rb_ref[b, head], bias)
    o_ref[...] = jnp.where(dist >= 0, bias * LOG2E, NEG_BIG)


def _bias_tiles(rel_bias, tb):
    n_buckets, heads = rel_bias.shape
    starts = _t5_bucket_starts(n_buckets)
    assert 2 * tb - (tb - 1) >= starts[-1]
    return pl.pallas_call(
        functools.partial(_bias_tile_kernel, tb=tb, starts=starts), grid=(4, heads),
        in_specs=[pl.BlockSpec(memory_space=pltpu.SMEM)],
        out_specs=pl.BlockSpec((None, None, tb, tb), lambda dl, hd: (dl, hd, 0, 0)),
        out_shape=jax.ShapeDtypeStruct((4, heads, tb, tb), F32), name="rel_bias_tiles",
        compiler_params=_cp("parallel", "parallel"),
    )(rel_bias)


def _indexer_kernel(qt_ref, k2_ref, wt_ref, keys_ref, thr_ref, *, tq, ck, n_heads, topk, w_scale):
    i = pl.program_id(0)
    n_blocks = keys_ref.shape[0] // tq
    n_live = (i + 1) * (tq // ck)
    wv = wt_ref[...] * w_scale

    def score_chunk(c, carry):
        row = pl.multiple_of(c * ck, ck)
        kk = k2_ref[c]
        acc = jnp.zeros((ck, tq), F32)
        for pr in range(n_heads // 2):
            s2 = jnp.dot(kk, qt_ref[pr * LANES:(pr + 1) * LANES, :], preferred_element_type=F32)
            acc = acc + jnp.maximum(s2[:ck], 0.0) * wv[2 * pr:2 * pr + 1, :]
            acc = acc + jnp.maximum(s2[ck:], 0.0) * wv[2 * pr + 1:2 * pr + 2, :]
        kpos = row + lax.broadcasted_iota(I32, (ck, tq), 0)
        qpos = i * tq + lax.broadcasted_iota(I32, (ck, tq), 1)
        acc = jnp.where(acc == 0.0, 0.0, acc)
        score = jnp.where(kpos <= qpos, acc, -jnp.inf)
        bits = pltpu.bitcast(score, I32)
        keys_ref[pl.ds(row, ck), :] = jnp.where(bits < 0, bits ^ 0x7FFFFFFF, bits)
        return carry

    lax.fori_loop(0, n_live, score_chunk, 0)

    def fill_block(b, carry):
        row = pl.multiple_of(b * tq, tq)
        keys_ref[pl.ds(row, tq), :] = jnp.full((tq, tq), INT_MIN, I32)
        return carry

    lax.fori_loop(i + 1, n_blocks, fill_block, 0)

    acc_rows = 32

    def count_ge(cand):
        def body(b, cnt):
            row = pl.multiple_of(b * tq, tq)
            hit = jnp.where(keys_ref[pl.ds(row, tq), :] >= cand, 1.0, 0.0)
            return cnt + jnp.sum(hit.reshape(tq // acc_rows, acc_rows, tq), axis=0)
        cnt = lax.fori_loop(0, i + 1, body, jnp.zeros((acc_rows, tq), F32))
        return jnp.sum(cnt, axis=0, keepdims=True)

    kf = float(topk)
    zero = jnp.zeros((1, tq), I32)
    c_zero = count_ge(zero)
    thr0 = jnp.where(c_zero >= kf, zero, jnp.full((1, tq), INT_MIN, I32))
    cnt0 = jnp.where(c_zero >= kf, c_zero, ((i + 1) * tq).astype(F32))

    def undecided(st):
        it, _, cnt = st
        return (it < 31) & (jnp.max(cnt) > kf)

    def bit_step(st):
        it, thr, cnt = st
        cand = thr | jnp.left_shift(jnp.int32(1), 30 - it)
        c = count_ge(cand)
        keep = c >= kf
        return it + 1, jnp.where(keep, cand, thr), jnp.where(keep, c, cnt)

    _, thr, cnt = lax.while_loop(undecided, bit_step, (jnp.int32(0), thr0, cnt0))

    n_keys = keys_ref.shape[0]

    def position_cutoff(_):
        need = kf - count_ge(thr + 1)

        def count_eq_upto(pos):
            def body(b, acc):
                row = pl.multiple_of(b * tq, tq)
                kpos = row + lax.broadcasted_iota(I32, (tq, tq), 0)
                hit = jnp.where(keys_ref[pl.ds(row, tq), :] == thr, jnp.where(kpos <= pos, 1.0, 0.0), 0.0)
                return acc + jnp.sum(hit.reshape(tq // acc_rows, acc_rows, tq), axis=0)
            acc = lax.fori_loop(0, i + 1, body, jnp.zeros((acc_rows, tq), F32))
            return jnp.sum(acc, axis=0, keepdims=True)

        def halve(_, bounds):
            lo, hi = bounds
            mid = (lo + hi) >> 1
            enough = count_eq_upto(mid) >= need
            return jnp.where(enough, lo, mid), jnp.where(enough, mid, hi)

        lo0 = jnp.full((1, tq), -1, I32)
        hi0 = jnp.full((1, tq), n_keys - 1, I32)
        _, hi = lax.fori_loop(0, max(1, (n_keys - 1).bit_length()), halve, (lo0, hi0))
        return jnp.where(cnt > kf, hi, n_keys)

    jcut = lax.cond(jnp.max(cnt) > kf, position_cutoff, lambda _: jnp.full((1, tq), n_keys, I32), 0)
    sel_row = lax.broadcasted_iota(I32, thr_ref.shape, 0)
    thr_ref[...] = jnp.where(sel_row == 0, thr, jnp.where(sel_row == 1, jcut, 0))


def _indexer(qi_t, row_off, k2, w_t, topk, tq=256):
    L = qi_t.shape[1]
    n_heads = w_t.shape[0]
    qw = n_heads * IDX_DIM
    assert row_off % qw == 0
    tq = _tile(L, tq)
    ck = k2.shape[1] // 2
    assert tq % ck == 0
    w_scale = (n_heads ** -0.5) * (IDX_DIM ** -0.5)
    return pl.pallas_call(
        functools.partial(_indexer_kernel, tq=tq, ck=ck, n_heads=n_heads, topk=topk, w_scale=w_scale),
        grid=(L // tq,),
        in_specs=[pl.BlockSpec((qw, tq), lambda i: (row_off // qw, i)),
                  pl.BlockSpec(k2.shape, lambda i: (0, 0, 0)),
                  pl.BlockSpec((n_heads, tq), lambda i: (0, i))],
        out_specs=[pl.BlockSpec((L, tq), lambda i: (0, i)),
                   pl.BlockSpec((SUBLANES, tq), lambda i: (0, i))],
        out_shape=[jax.ShapeDtypeStruct((L, L), I32), jax.ShapeDtypeStruct((SUBLANES, L), I32)],
        name="dsa_indexer", compiler_params=_cp("parallel"),
    )(qi_t, k2, w_t)


def _attn_kernel(qi_ref, kj_ref, far_ref, qt_ref, k_ref, vt_ref, keys_ref, thr_ref, bias_a_ref, bias_b_ref, o_ref,
                 m_ref, l_ref, acc_ref, mb_ref, *, heads, dh, scale2):
    step = pl.program_id(0)
    qi = qi_ref[step]
    kj = kj_ref[step]

    @pl.when(kj == 0)
    def _():
        m_ref[...] = jnp.full(m_ref.shape, -jnp.inf, F32)
        l_ref[...] = jnp.zeros(l_ref.shape, F32)
        acc_ref[...] = jnp.zeros(acc_ref.shape, F32)

    thr = thr_ref[0:1, :]
    kpos = kj * keys_ref.shape[0] + lax.broadcasted_iota(I32, keys_ref.shape, 0)
    bar = jnp.where(kpos <= thr_ref[1:2, :], thr, thr + 1)
    mb_ref[...] = jnp.where(keys_ref[...] >= bar, 0.0, NEG_BIG)

    def scores(h):
        hs = slice(h * dh, (h + 1) * dh)
        return jnp.dot(k_ref[:, hs], qt_ref[hs, :], preferred_element_type=F32)

    def fold(reduce_fn, a, rows=32):
        return reduce_fn(a.reshape(a.shape[0] // rows, rows, a.shape[1]), axis=0)

    def run(near):
        ahead = 4
        pending = [scores(h) for h in range(min(ahead, heads))]
        for h in range(heads):
            hs = slice(h * dh, (h + 1) * dh)
            x = pending.pop(0) * scale2 + mb_ref[...]
            if h + ahead < heads:
                pending.append(scores(h + ahead))
            if near:
                x = x + jnp.concatenate([bias_a_ref[h], bias_b_ref[h]], axis=0)
            far = 0.0 if near else far_ref[h]
            m_prev = m_ref[h]
            m_new = jnp.maximum(m_prev, jnp.max(fold(jnp.max, x), axis=0, keepdims=True) + far)
            alpha = jnp.exp2(m_prev - m_new)
            p = jnp.exp2(x - (m_new[0:1, :] - far))
            l_ref[h] = alpha * l_ref[h] + jnp.sum(fold(jnp.sum, p), axis=0, keepdims=True)
            m_ref[h] = m_new
            acc_ref[hs, :] = alpha[0:1, :] * acc_ref[hs, :] + jnp.dot(vt_ref[hs, :], p.astype(BF16),
                                                                      preferred_element_type=F32)

    @pl.when(qi - 2 * kj - 1 < 2)
    def _():
        run(True)

    @pl.when(qi - 2 * kj - 1 >= 2)
    def _():
        run(False)

    @pl.when(kj == qi // 2)
    def _():
        for h in range(heads):
            hs = slice(h * dh, (h + 1) * dh)
            o_ref[:, hs] = jnp.transpose(acc_ref[hs, :] / l_ref[h][0:1, :]).astype(o_ref.dtype)


def _attention(q_t, v_t, k, keys, thr, bias_tiles, far_bias, tb):
    L = k.shape[0]
    heads = bias_tiles.shape[1]
    dh = ATT_HEAD_DIM
    aw = heads * dh
    nb = L // tb
    assert nb % 2 == 0
    pairs = [(i, j) for i in range(nb) for j in range(i // 2 + 1)]
    qi = jnp.asarray(np.array([p[0] for p in pairs], np.int32))
    kj = jnp.asarray(np.array([p[1] for p in pairs], np.int32))

    def tile_of(behind):
        return jnp.where(behind < 0, 3, jnp.minimum(behind, 2))

    bias_spec_a = pl.BlockSpec((None, heads, tb, tb), lambda s, qi, kj: (tile_of(qi[s] - 2 * kj[s]), 0, 0, 0))
    bias_spec_b = pl.BlockSpec((None, heads, tb, tb), lambda s, qi, kj: (tile_of(qi[s] - 2 * kj[s] - 1), 0, 0, 0))
    grid_spec = pltpu.PrefetchScalarGridSpec(
        num_scalar_prefetch=2, grid=(len(pairs),),
        in_specs=[pl.BlockSpec(memory_space=pltpu.SMEM),
                  pl.BlockSpec((aw, tb), lambda s, qi, kj: (0, qi[s])),
                  pl.BlockSpec((2 * tb, aw), lambda s, qi, kj: (kj[s], 0)),
                  pl.BlockSpec((aw, 2 * tb), lambda s, qi, kj: (0, kj[s])),
                  pl.BlockSpec((2 * tb, tb), lambda s, qi, kj: (kj[s], qi[s])),
                  pl.BlockSpec((SUBLANES, tb), lambda s, qi, kj: (0, qi[s])),
                  bias_spec_a, bias_spec_b],
        out_specs=pl.BlockSpec((tb, aw), lambda s, qi, kj: (qi[s], 0)),
        scratch_shapes=[pltpu.VMEM((heads, SUBLANES, tb), F32),
                        pltpu.VMEM((heads, SUBLANES, tb), F32),
                        pltpu.VMEM((aw, tb), F32),
                        pltpu.VMEM((2 * tb, tb), F32)])
    return pl.pallas_call(
        functools.partial(_attn_kernel, heads=heads, dh=dh, scale2=dh ** -0.5 * LOG2E),
        grid_spec=grid_spec, out_shape=jax.ShapeDtypeStruct((L, aw), BF16), name="dsa_attention",
        compiler_params=_cp("arbitrary"),
    )(qi, kj, far_bias, q_t, k, v_t, keys, thr, bias_tiles, bias_tiles)


def _router_kernel(h_ref, w_ref, b_ref, g_ref, *, n_exp, top_k):
    logits = jnp.dot(h_ref[...], w_ref[...], preferred_element_type=F32) + b_ref[...]
    lane = lax.broadcasted_iota(I32, logits.shape, 1).astype(F32)
    work = jnp.where(lane < n_exp, logits, -jnp.inf)
    gates = jnp.zeros(logits.shape, F32)
    denom = None
    top = None
    for r in range(top_k):
        m = jnp.max(work, axis=1, keepdims=True)
        first = jnp.min(jnp.where(work == m, lane, float(LANES)), axis=1, keepdims=True)
        sel = lane == first
        if r == 0:
            top = m
        e = jnp.exp(m - top)
        gates = gates + jnp.where(sel, e, 0.0)
        denom = e if denom is None else denom + e
        work = jnp.where(sel, -jnp.inf, work)
    g_ref[...] = gates / denom


def _router(h, w_pad, b_pad, n_exp, tm=512):
    m, d = h.shape
    tm = _tile(m, tm)
    return pl.pallas_call(
        functools.partial(_router_kernel, n_exp=n_exp, top_k=TOP_K_EXPERTS), grid=(m // tm,),
        in_specs=[pl.BlockSpec((tm, d), lambda i: (i, 0)),
                  pl.BlockSpec((d, LANES), lambda i: (0, 0)),
                  pl.BlockSpec((1, LANES), lambda i: (0, 0))],
        out_specs=pl.BlockSpec((tm, LANES), lambda i: (i, 0)),
        out_shape=jax.ShapeDtypeStruct((m, LANES), F32), name="moe_router",
        compiler_params=_cp("parallel"),
    )(h, w_pad, b_pad)


def _moe_kernel(h_ref, g_ref, wgu_ref, bgu_ref, wdn_ref, bdn_ref, o_ref, *, ff, eb):
    step = pl.program_id(1)
    gates = g_ref[...]

    @pl.when(step == 0)
    def _():
        o_ref[...] = jnp.dot(gates.astype(BF16), bdn_ref[...], preferred_element_type=F32)

    lane = lax.broadcasted_iota(I32, gates.shape, 1)
    for k in range(eb):
        gu = jnp.dot(h_ref[...], wgu_ref[k], preferred_element_type=F32) + bgu_ref[k]
        gate = jnp.minimum(gu[:, :ff], SWIGLU_LIMIT)
        up = jnp.clip(gu[:, ff:], -SWIGLU_LIMIT, SWIGLU_LIMIT)
        act = (up + 1.0) * gate * jax.nn.sigmoid(SWIGLU_ALPHA * gate)
        ge = jnp.sum(jnp.where(lane == step * eb + k, gates, 0.0), axis=1, keepdims=True)
        o_ref[...] += jnp.dot((act * ge).astype(BF16), wdn_ref[k], preferred_element_type=F32)


def _moe_experts(h, gates, w_gu, b_gu, w_dn, b_dn_pad, tm=512, eb=2):
    m, d = h.shape
    n_exp, _, ff2 = w_gu.shape
    tm = _tile(m, tm)
    eb = _tile(n_exp, eb)
    return pl.pallas_call(
        functools.partial(_moe_kernel, ff=ff2 // 2, eb=eb), grid=(m // tm, n_exp // eb),
        in_specs=[pl.BlockSpec((tm, d), lambda i, e: (i, 0)),
                  pl.BlockSpec((tm, LANES), lambda i, e: (i, 0)),
                  pl.BlockSpec((eb, d, ff2), lambda i, e: (e, 0, 0)),
                  pl.BlockSpec((eb, 1, ff2), lambda i, e: (e, 0, 0)),
                  pl.BlockSpec((eb, ff2 // 2, d), lambda i, e: (e, 0, 0)),
                  pl.BlockSpec((LANES, d), lambda i, e: (0, 0))],
        out_specs=pl.BlockSpec((tm, d), lambda i, e: (i, 0)),
        out_shape=jax.ShapeDtypeStruct((m, d), F32), name="moe_experts",
        compiler_params=_cp("parallel", "arbitrary"),
    )(h, gates, w_gu, b_gu.reshape(n_exp, 1, ff2), w_dn, b_dn_pad)


def kernel(x, c, w_in, w_a_out, w_b_out, w_o, w_glu, b_glu, s5_lam_re, s5_lam_im, s5_b_re, s5_b_im, s5_c_re, s5_c_im, s5_d, s5_log_step, rel_bias, w_router, b_router, w_gu, b_gu, w_dn, b_dn, w_c1, b_c1, w_c2, b_c2, mod_table, ln_g, ln_b):
    bsz, L, d = x.shape
    assert bsz == 1 and c.shape[0] == 1
    depth = w_in.shape[0]
    g, p, h = s5_b_re.shape[1:]
    s5w = g * h
    heads = rel_bias.shape[1]
    aw = heads * ATT_HEAD_DIM
    iw = IDX_HEADS * IDX_DIM
    n_exp = w_router.shape[2]
    assert w_in.shape[2] == s5w + 3 * aw + iw + IDX_DIM + IDX_HEADS + 2 * d
    assert n_exp <= LANES and IDX_DIM + IDX_HEADS <= LANES and 2 * IDX_DIM == LANES
    o_q, o_k, o_v, o_qi = s5w, s5w + aw, s5w + 2 * aw, s5w + 3 * aw
    o_ki = s5w + 3 * aw + iw
    o_gate = o_ki + IDX_DIM + IDX_HEADS
    topk = min(TOPK_MAX, L // 4)
    tb = _tile(L, 256)
    ck = _tile(L, LANES)
    dn_alpha = (2 * depth) ** 0.25

    mod = _conditioning(c, w_c1, b_c1, w_c2, b_c2, mod_table).reshape(depth, N_MOD, d)
    bias_tiles = _bias_tiles(rel_bias, tb)
    far_bias = rel_bias[-1] * LOG2E
    assert o_ki % LANES == 0 and (g * h) % LANES == 0 and LANES % h == 0

    w_proj = w_in.astype(BF16)

    xs = x.reshape(L, d)
    hcur = _modulate(xs, mod[0], sc_row=1, sh_row=0)
    for l in range(depth):
        u = _matmul(hcur, w_proj, layer=l, out_dtype=F32, col_off=0, n_cols=s5w, name="proj_u")
        k = _matmul(hcur, w_proj, layer=l, out_dtype=BF16, col_off=o_k, n_cols=aw, name="proj_k")
        gates_ab = _matmul_shifted(hcur, w_proj, layer=l, out_dtype=F32, col_off=o_gate, n_cols=2 * d,
                                   name="proj_gates")
        misc = _matmul(hcur, w_proj, layer=l, out_dtype=F32, col_off=o_ki, n_cols=LANES, tn=LANES, name="proj_misc")
        q_t = _matmul_t(hcur, w_proj, layer=l, out_dtype=BF16, col_off=o_q, n_cols=aw, name="proj_q_t")
        vqi_t = _matmul_t(hcur, w_proj, layer=l, out_dtype=BF16, col_off=o_v, n_cols=aw + iw,
                          name="proj_vqi_t")

        prm = _s5_discretize(s5_lam_re[l], s5_lam_im[l], s5_b_re[l], s5_b_im[l], s5_c_re[l], s5_c_im[l],
                             s5_d[l], s5_log_step[l], S5_CHUNK)
        y_s5 = _s5_branch(u, prm)
        ya_in = _glu(y_s5, w_glu, b_glu[l], l)

        k_idx = misc[:, :IDX_DIM].astype(BF16)
        zeros = jnp.zeros_like(k_idx)
        k_even = jnp.concatenate([k_idx, zeros], axis=1).reshape(L // ck, ck, LANES)
        k_odd = jnp.concatenate([zeros, k_idx], axis=1).reshape(L // ck, ck, LANES)
        k2 = jnp.concatenate([k_even, k_odd], axis=1)
        w_t = jnp.transpose(misc[:, IDX_DIM:IDX_DIM + IDX_HEADS])
        keys, thr = _indexer(vqi_t, aw, k2, w_t, topk)
        attn = _attention(q_t, vqi_t, k, keys, thr, bias_tiles, far_bias, tb)

        merged = _merge(ya_in, attn, w_a_out, w_b_out, gates_ab, l)
        y = _matmul(merged, w_o, layer=l, out_dtype=F32, name="proj_out")
        xs, h2 = _ln_modulate(xs, y, mod[l], ln_g[l], ln_b[l], mod[l], alpha=dn_alpha, gate_row=2, ln_row=0,
                              sc_row=4, sh_row=3, want_h=True)

        w_r = jnp.pad(w_router[l], ((0, 0), (0, LANES - n_exp))).astype(BF16)
        b_r = jnp.pad(b_router[l], (0, LANES - n_exp)).reshape(1, LANES)
        gates = _router(h2, w_r, b_r, n_exp)
        b_dn_pad = jnp.pad(b_dn[l], ((0, LANES - n_exp), (0, 0))).astype(BF16)
        y = _moe_experts(h2, gates, w_gu[l].astype(BF16), b_gu[l], w_dn[l].astype(BF16), b_dn_pad)
        last = l == depth - 1
        xs, hcur = _ln_modulate(xs, y, mod[l], ln_g[l], ln_b[l], mod[min(l + 1, depth - 1)], alpha=dn_alpha,
                                gate_row=5, ln_row=1, sc_row=1, sh_row=0, want_h=not last)
    return xs.reshape(bsz, L, d)
```

```python
import functools
import math

import numpy as np
import jax
import jax.numpy as jnp
from jax import lax
from jax.experimental import pallas as pl
from jax.experimental.pallas import tpu as pltpu

F32 = jnp.float32
BF16 = jnp.bfloat16
I32 = jnp.int32

ATT_HEAD_DIM = 128
IDX_HEADS = 32
IDX_DIM = 64
TOPK_MAX = 256
REL_MAX_DIST = 128
TOP_K_EXPERTS = 4
SWIGLU_LIMIT = 7.0
SWIGLU_ALPHA = 1.702
N_MOD = 6
LN_EPS = 1e-5

SUBLANES = 8
LANES = 128
VMEM_LIMIT_BYTES = 56 * 1024 * 1024
S5_CHUNK = 16
NEG_BIG = -1e30
LOG2E = math.log2(math.e)
INT_MIN = -(2 ** 31)


def _cp(*sem):
    return pltpu.CompilerParams(dimension_semantics=sem, vmem_limit_bytes=VMEM_LIMIT_BYTES)


def _tile(n, t):
    t = min(n, t)
    while n % t:
        t //= 2
    return t


def _cond1_kernel(c_ref, w_ref, b_ref, o_ref):
    a = jax.nn.silu(c_ref[...]).astype(BF16)
    t = jnp.dot(a, w_ref[...].astype(BF16), preferred_element_type=F32) + b_ref[...]
    o_ref[...] = jax.nn.silu(t)


def _cond2_kernel(t_ref, w_ref, b_ref, tab_ref, o_ref):
    acc = jnp.dot(t_ref[...].astype(BF16), w_ref[...].astype(BF16), preferred_element_type=F32)
    o_ref[...] = acc[0:1, :] + b_ref[...] + tab_ref[...]


def _conditioning(c, w_c1, b_c1, w_c2, b_c2, mod_table):
    d = c.shape[1]
    r = w_c1.shape[1]
    depth = mod_table.shape[0]
    n = w_c2.shape[1]
    c8 = jnp.broadcast_to(c, (SUBLANES, d))
    t = pl.pallas_call(
        _cond1_kernel, out_shape=jax.ShapeDtypeStruct((SUBLANES, r), F32), name="cond1",
        compiler_params=_cp(),
    )(c8, w_c1, b_c1.reshape(1, r))
    tn = _tile(n, 2048)
    return pl.pallas_call(
        _cond2_kernel, grid=(n // tn,),
        in_specs=[pl.BlockSpec((SUBLANES, r), lambda j: (0, 0)),
                  pl.BlockSpec((r, tn), lambda j: (0, j)),
                  pl.BlockSpec((1, tn), lambda j: (0, j)),
                  pl.BlockSpec((depth, tn), lambda j: (0, j))],
        out_specs=pl.BlockSpec((depth, tn), lambda j: (0, j)),
        out_shape=jax.ShapeDtypeStruct((depth, n), F32), name="cond2",
        compiler_params=_cp("parallel"),
    )(t, w_c2, b_c2.reshape(1, n), mod_table.reshape(depth, n))


def _modulate_kernel(x_ref, mod_ref, h_ref, *, sc_row, sh_row):
    sc = mod_ref[sc_row:sc_row + 1, :]
    sh = mod_ref[sh_row:sh_row + 1, :]
    h_ref[...] = (x_ref[...] * (1.0 + sc) + sh).astype(h_ref.dtype)


def _modulate(x, mod_l, sc_row, sh_row, tm=256):
    m, d = x.shape
    tm = _tile(m, tm)
    return pl.pallas_call(
        functools.partial(_modulate_kernel, sc_row=sc_row, sh_row=sh_row),
        grid=(m // tm,),
        in_specs=[pl.BlockSpec((tm, d), lambda i: (i, 0)),
                  pl.BlockSpec((N_MOD, d), lambda i: (0, 0))],
        out_specs=pl.BlockSpec((tm, d), lambda i: (i, 0)),
        out_shape=jax.ShapeDtypeStruct((m, d), BF16), name="modulate",
        compiler_params=_cp("parallel"),
    )(x, mod_l)


def _ln_kernel(x_ref, y_ref, mod_ref, g_ref, b_ref, nmod_ref, xo_ref, *h_refs,
               alpha, gate_row, ln_row, sc_row, sh_row):
    gate = mod_ref[gate_row:gate_row + 1, :]
    u = alpha * x_ref[...] + (1.0 + gate) * y_ref[...]
    mu = jnp.mean(u, axis=-1, keepdims=True)
    uc = u - mu
    var = jnp.mean(uc * uc, axis=-1, keepdims=True)
    xn = uc * lax.rsqrt(var + LN_EPS) * g_ref[ln_row:ln_row + 1, :] + b_ref[ln_row:ln_row + 1, :]
    xo_ref[...] = xn
    if h_refs:
        sc = nmod_ref[sc_row:sc_row + 1, :]
        sh = nmod_ref[sh_row:sh_row + 1, :]
        h_refs[0][...] = (xn * (1.0 + sc) + sh).astype(BF16)


def _ln_modulate(x, y, mod_l, ln_g_l, ln_b_l, next_mod, *, alpha, gate_row, ln_row, sc_row, sh_row,
                 want_h, tm=256):
    m, d = x.shape
    tm = _tile(m, tm)
    row = pl.BlockSpec((tm, d), lambda i: (i, 0))
    out_shape = [jax.ShapeDtypeStruct((m, d), F32)]
    out_specs = [row]
    if want_h:
        out_shape.append(jax.ShapeDtypeStruct((m, d), BF16))
        out_specs.append(row)
    res = pl.pallas_call(
        functools.partial(_ln_kernel, alpha=alpha, gate_row=gate_row, ln_row=ln_row,
                          sc_row=sc_row, sh_row=sh_row),
        grid=(m // tm,),
        in_specs=[row, row,
                  pl.BlockSpec((N_MOD, d), lambda i: (0, 0)),
                  pl.BlockSpec((2, d), lambda i: (0, 0)),
                  pl.BlockSpec((2, d), lambda i: (0, 0)),
                  pl.BlockSpec((N_MOD, d), lambda i: (0, 0))],
        out_specs=out_specs, out_shape=out_shape, name="ln_modulate",
        compiler_params=_cp("parallel"),
    )(x, y, mod_l, ln_g_l, ln_b_l, next_mod)
    return (res[0], res[1]) if want_h else (res[0], None)


def _mm_kernel(a_ref, w_ref, o_ref, wbf_ref):
    @pl.when(pl.program_id(1) == 0)
    def _():
        wbf_ref[...] = w_ref[...].astype(BF16)

    o_ref[...] = jnp.dot(a_ref[...], wbf_ref[...], preferred_element_type=F32).astype(o_ref.dtype)


def _w_spec(w, layer, k, tn, off=0, scale=1):
    if w.ndim == 3:
        return pl.BlockSpec((None, k, tn), lambda j, i: (layer, 0, (j + off) * scale))
    return pl.BlockSpec((k, tn), lambda j, i: (0, (j + off) * scale))


def _matmul(a, w, *, out_dtype, layer=0, col_off=0, n_cols=None, tm=1024, tn=512, name="matmul"):
    m, k = a.shape
    n_cols = w.shape[-1] - col_off if n_cols is None else n_cols
    tm = _tile(m, tm)
    tn = math.gcd(_tile(n_cols, tn), col_off) if col_off else _tile(n_cols, tn)
    off = col_off // tn
    return pl.pallas_call(
        _mm_kernel, grid=(n_cols // tn, m // tm),
        in_specs=[pl.BlockSpec((tm, k), lambda j, i: (i, 0)),
                  _w_spec(w, layer, k, tn, off)],
        out_specs=pl.BlockSpec((tm, tn), lambda j, i: (i, j)),
        out_shape=jax.ShapeDtypeStruct((m, n_cols), out_dtype),
        scratch_shapes=[pltpu.VMEM((k, tn), BF16)], name=name,
        compiler_params=_cp("parallel", "arbitrary"),
    )(a, w)


def _mm_shift_kernel(a_ref, w_ref, wn_ref, o_ref, wbf_ref, *, shift):
    @pl.when(pl.program_id(1) == 0)
    def _():
        both = jnp.concatenate([w_ref[...], wn_ref[...]], axis=1)
        wbf_ref[...] = both[:, shift:shift + wbf_ref.shape[1]]

    o_ref[...] = jnp.dot(a_ref[...], wbf_ref[...], preferred_element_type=F32).astype(o_ref.dtype)


def _matmul_shifted(a, w, *, out_dtype, col_off, n_cols, layer=0, tm=1024, tn=512, name="matmul_shifted"):
    m, k = a.shape
    tm = _tile(m, tm)
    base, shift = col_off // LANES * LANES, col_off % LANES
    tn = math.gcd(_tile(n_cols, tn), base) if base else _tile(n_cols, tn)
    assert tn % LANES == 0
    return pl.pallas_call(
        functools.partial(_mm_shift_kernel, shift=shift), grid=(n_cols // tn, m // tm),
        in_specs=[pl.BlockSpec((tm, k), lambda j, i: (i, 0)),
                  _w_spec(w, layer, k, tn, base // tn),
                  _w_spec(w, layer, k, LANES, 1 + base // tn, tn // LANES)],
        out_specs=pl.BlockSpec((tm, tn), lambda j, i: (i, j)),
        out_shape=jax.ShapeDtypeStruct((m, n_cols), out_dtype),
        scratch_shapes=[pltpu.VMEM((k, tn), w.dtype)], name=name,
        compiler_params=_cp("parallel", "arbitrary"),
    )(a, w, w)


def _mm_t_kernel(a_ref, w_ref, o_ref, wt_ref):
    @pl.when(pl.program_id(1) == 0)
    def _():
        wt_ref[...] = jnp.transpose(w_ref[...])

    o_ref[...] = lax.dot_general(wt_ref[...], a_ref[...], (((1,), (1,)), ((), ())),
                                 preferred_element_type=F32).astype(o_ref.dtype)


def _matmul_t(a, w, *, out_dtype, col_off, n_cols, layer=0, tm=1024, tn=1024, name="matmul_t"):
    m, k = a.shape
    tm = _tile(m, tm)
    tn = math.gcd(_tile(n_cols, tn), col_off) if col_off else _tile(n_cols, tn)
    off = col_off // tn
    return pl.pallas_call(
        _mm_t_kernel, grid=(n_cols // tn, m // tm),
        in_specs=[pl.BlockSpec((tm, k), lambda j, i: (i, 0)),
                  _w_spec(w, layer, k, tn, off)],
        out_specs=pl.BlockSpec((tn, tm), lambda j, i: (j, i)),
        out_shape=jax.ShapeDtypeStruct((n_cols, m), out_dtype),
        scratch_shapes=[pltpu.VMEM((tn, k), w.dtype)], name=name,
        compiler_params=_cp("parallel", "arbitrary"),
    )(a, w)


def _glu_kernel(y_ref, yt_ref, w_ref, b_ref, o_ref, wbf_ref):
    @pl.when(pl.program_id(1) == 0)
    def _():
        wbf_ref[...] = w_ref[...].astype(BF16)

    z = jnp.dot(y_ref[...].astype(BF16), wbf_ref[...], preferred_element_type=F32) + b_ref[...]
    o_ref[...] = (yt_ref[...] * jax.nn.sigmoid(z)).astype(o_ref.dtype)


def _glu(y, w, b, layer, tm=1024, tn=512):
    m, k = y.shape
    n = w.shape[-1]
    tm = _tile(m, tm)
    tn = _tile(n, tn)
    return pl.pallas_call(
        _glu_kernel, grid=(n // tn, m // tm),
        in_specs=[pl.BlockSpec((tm, k), lambda j, i: (i, 0)),
                  pl.BlockSpec((tm, tn), lambda j, i: (i, j)),
                  _w_spec(w, layer, k, tn),
                  pl.BlockSpec((1, tn), lambda j, i: (0, j))],
        out_specs=pl.BlockSpec((tm, tn), lambda j, i: (i, j)),
        out_shape=jax.ShapeDtypeStruct((m, n), BF16),
        scratch_shapes=[pltpu.VMEM((k, tn), BF16)], name="s5_glu",
        compiler_params=_cp("parallel", "arbitrary"),
    )(y, y, w, b.reshape(1, n))


def _merge_kernel(a1_ref, a2_ref, w1_ref, w2_ref, ga_ref, gb_ref, o_ref, w1bf_ref, w2bf_ref):
    @pl.when(pl.program_id(1) == 0)
    def _():
        w1bf_ref[...] = w1_ref[...].astype(BF16)
        w2bf_ref[...] = w2_ref[...].astype(BF16)

    ya = jnp.dot(a1_ref[...], w1bf_ref[...], preferred_element_type=F32)
    yb = jnp.dot(a2_ref[...], w2bf_ref[...], preferred_element_type=F32)
    o_ref[...] = (jax.nn.sigmoid(ga_ref[...]) * ya + jax.nn.sigmoid(gb_ref[...]) * yb).astype(o_ref.dtype)


def _merge(a1, a2, w1, w2, gates, layer, tm=1024, tn=512):
    m, k1 = a1.shape
    k2 = a2.shape[1]
    n = w1.shape[-1]
    tm = _tile(m, tm)
    tn = _tile(n, tn)
    nb = n // tn
    return pl.pallas_call(
        _merge_kernel, grid=(nb, m // tm),
        in_specs=[pl.BlockSpec((tm, k1), lambda j, i: (i, 0)),
                  pl.BlockSpec((tm, k2), lambda j, i: (i, 0)),
                  _w_spec(w1, layer, k1, tn),
                  _w_spec(w2, layer, k2, tn),
                  pl.BlockSpec((tm, tn), lambda j, i: (i, j)),
                  pl.BlockSpec((tm, tn), lambda j, i: (i, j + nb))],
        out_specs=pl.BlockSpec((tm, tn), lambda j, i: (i, j)),
        out_shape=jax.ShapeDtypeStruct((m, n), BF16),
        scratch_shapes=[pltpu.VMEM((k1, tn), BF16), pltpu.VMEM((k2, tn), BF16)], name="branch_merge",
        compiler_params=_cp("parallel", "arbitrary"),
    )(a1, a2, w1, w2, gates, gates)


def _s5_discretize(lam_re, lam_im, b_re, b_im, c_re, c_im, d, log_step, t_chunk):
    hp = lax.Precision.HIGHEST
    g, p, h = b_re.shape
    na = LANES // h
    nx = g // na
    assert 2 * p == LANES
    lam = lax.complex(jnp.minimum(lam_re, -1e-4), lam_im)
    dt = jnp.exp(log_step)[:, None]
    lam_bar = jnp.exp(lam * dt)
    b_bar = ((lam_bar - 1.0) / lam)[:, :, None] * lax.complex(b_re, b_im)
    c = lax.complex(c_re, c_im)
    tau = jnp.arange(t_chunk + 1, dtype=F32)
    pw = jnp.exp((lam * dt)[:, None, :] * tau[None, :, None].astype(jnp.complex64))
    cp = c[:, None, :, :] * pw[:, :, None, :]
    kern = jnp.einsum('gtop,gpi->gtoi', cp[:, :t_chunk], b_bar, precision=hp).real
    kio = jnp.transpose(kern.reshape(nx, na, t_chunk, h, h), (0, 2, 1, 4, 3))
    kt = jnp.tile(kio.reshape(nx, t_chunk, LANES, h), (1, 1, 1, na))
    wcx = pw[:, t_chunk - 1 - np.arange(t_chunk)][:, :, :, None] * b_bar[:, None, :, :]
    wcx = jnp.transpose(wcx, (0, 1, 3, 2))
    w_ri = jnp.concatenate([wcx.real, wcx.imag], axis=-1).reshape(nx, na, t_chunk, h, 2 * p)
    wc = jnp.transpose(w_ri, (0, 2, 1, 3, 4)).reshape(nx, t_chunk, LANES, 2 * p)
    cl = jnp.transpose(cp[:, 1:t_chunk + 1], (0, 3, 1, 2))
    vc = jnp.concatenate([cl.real, -cl.imag], axis=1).reshape(nx, na, 2 * p, t_chunk * h)
    dd = jnp.broadcast_to(d.reshape(nx, 1, 1, na, h), (nx, 1, t_chunk, na, h)).reshape(nx, 1, t_chunk * LANES)
    a = pw[:, t_chunk]
    a1 = jnp.concatenate([a.real, a.real], axis=-1).reshape(1, g * 2 * p)
    a2 = jnp.concatenate([-a.imag, a.imag], axis=-1).reshape(1, g * 2 * p)
    return kt, wc, vc, dd, a1, a2


def _chunk_rows(u_ref, t, nc):
    return jnp.concatenate([u_ref[pl.ds(s, nc, stride=t), :] for s in range(t)], axis=1)


def _s5_state_in_kernel(u_ref, wc_ref, xa_ref, xb_ref, wa_ref, wb_ref, *, t, nc, h):
    na = LANES // h
    row_group = lax.broadcasted_iota(I32, (LANES, LANES), 0) // h
    for s in range(t):
        blk = wc_ref[s]
        swapped = pltpu.roll(blk, LANES // 2, axis=1)
        for b in range(na):
            own = row_group == b
            dst = (slice(s * LANES, (s + 1) * LANES), slice(b * LANES, (b + 1) * LANES))
            wa_ref[dst] = jnp.where(own, blk, 0.0).astype(BF16)
            wb_ref[dst] = jnp.where(own, swapped, 0.0).astype(BF16)
    ucat = _chunk_rows(u_ref, t, nc).astype(BF16)
    xa_ref[...] = jnp.dot(ucat, wa_ref[...], preferred_element_type=F32)
    xb_ref[...] = jnp.dot(ucat, wb_ref[...], preferred_element_type=F32)


def _s5_scan_kernel(xa_ref, xb_ref, a1_ref, a2_ref, s_ref, *, unroll):
    a1 = a1_ref[...]
    a2 = a2_ref[...]
    nc = xa_ref.shape[0]

    def body(cb, carry):
        s_a, s_b = carry
        base = pl.multiple_of(cb * unroll, unroll)
        xa = xa_ref[pl.ds(base, unroll), :]
        xb = xb_ref[pl.ds(base, unroll), :]
        rows = []
        for k in range(unroll):
            rows.append(s_a)
            s_a, s_b = (a1 * s_a + a2 * s_b + xa[k:k + 1, :],
                        a1 * s_b - a2 * s_a + xb[k:k + 1, :])
        s_ref[pl.ds(base, unroll), :] = jnp.concatenate(rows, axis=0)
        return s_a, s_b

    z = jnp.zeros(a1.shape, F32)
    lax.fori_loop(0, nc // unroll, body, (z, z))


def _s5_out_kernel(u_ref, sp_ref, kt_ref, vc_ref, dd_ref, y_ref, bd_ref, vm_ref, *, t, nc, h, n_parts):
    part = pl.program_id(1)
    tp = t // n_parts
    na = LANES // h
    w = tp * LANES
    same_group = (lax.broadcasted_iota(I32, (LANES, LANES), 0) // h
                  == lax.broadcasted_iota(I32, (LANES, LANES), 1) // h)
    row = lax.broadcasted_iota(I32, (LANES, w), 0)
    col = lax.broadcasted_iota(I32, (LANES, w), 1)
    landing_group = jnp.where((row // h == col // LANES) & (row % h == col % h), (col // h) % na, -1)
    ucat = _chunk_rows(u_ref, t, nc)
    for k in range(n_parts):
        @pl.when(part == k)
        def _(k=k):
            live = (k + 1) * tp
            for s in range(live):
                for tt in range(tp):
                    lag = k * tp + tt - s
                    dst = (slice(s * LANES, (s + 1) * LANES), slice(tt * LANES, (tt + 1) * LANES))
                    if lag >= 0:
                        bd_ref[dst] = jnp.where(same_group, kt_ref[lag], 0.0).astype(BF16)
                    else:
                        bd_ref[dst] = jnp.zeros((LANES, LANES), BF16)
            for a in range(na):
                place = jnp.where(landing_group == a, 1.0, 0.0).astype(BF16)
                src = vc_ref[a][:, k * tp * h:(k + 1) * tp * h].astype(BF16)
                vm_ref[a * LANES:(a + 1) * LANES, :] = jnp.dot(src, place, preferred_element_type=F32).astype(BF16)
            y = jnp.dot(ucat[:, :live * LANES].astype(BF16), bd_ref[:live * LANES, :], preferred_element_type=F32)
            y = y + jnp.dot(sp_ref[...].astype(BF16), vm_ref[...], preferred_element_type=F32)
            y = jax.nn.gelu(y + dd_ref[...] * ucat[:, k * w:(k + 1) * w])
            for s in range(tp):
                y_ref[pl.ds(k * tp + s, nc, stride=t), :] = y[:, s * LANES:(s + 1) * LANES]


def _s5_branch(u, prm):
    kt, wc, vc, dd, a1, a2 = prm
    L, width = u.shape
    nx, na, _, th = vc.shape
    t = kt.shape[1]
    h = th // t
    tah = t * LANES
    sw = na * LANES
    nc = L // t
    u_spec = pl.BlockSpec((L, LANES), lambda x: (0, x))
    st_spec = pl.BlockSpec((nc, sw), lambda x: (0, x))
    xa, xb = pl.pallas_call(
        functools.partial(_s5_state_in_kernel, t=t, nc=nc, h=h), grid=(nx,),
        in_specs=[u_spec,
                  pl.BlockSpec((None, t, LANES, LANES), lambda x: (x, 0, 0, 0))],
        out_specs=[st_spec, st_spec],
        out_shape=[jax.ShapeDtypeStruct((nc, nx * sw), F32)] * 2,
        scratch_shapes=[pltpu.VMEM((tah, sw), BF16), pltpu.VMEM((tah, sw), BF16)], name="s5_state_in",
        compiler_params=_cp("parallel"),
    )(u, wc)
    ws = _tile(nx * sw, 2048)
    unroll = _tile(nc, SUBLANES)
    sc_spec = pl.BlockSpec((nc, ws), lambda x: (0, x))
    a_spec = pl.BlockSpec((1, ws), lambda x: (0, x))
    sp = pl.pallas_call(
        functools.partial(_s5_scan_kernel, unroll=unroll), grid=(nx * sw // ws,),
        in_specs=[sc_spec, sc_spec, a_spec, a_spec],
        out_specs=sc_spec,
        out_shape=jax.ShapeDtypeStruct((nc, nx * sw), F32), name="s5_scan",
        compiler_params=_cp("parallel"),
    )(xa, xb, a1, a2)
    n_parts = 2
    tw = tah // n_parts
    return pl.pallas_call(
        functools.partial(_s5_out_kernel, t=t, nc=nc, h=h, n_parts=n_parts), grid=(nx, n_parts),
        in_specs=[pl.BlockSpec((L, LANES), lambda x, k: (0, x)),
                  pl.BlockSpec((nc, sw), lambda x, k: (0, x)),
                  pl.BlockSpec((None, t, LANES, LANES), lambda x, k: (x, 0, 0, 0)),
                  pl.BlockSpec((None, na, LANES, th), lambda x, k: (x, 0, 0, 0)),
                  pl.BlockSpec((None, 1, tw), lambda x, k: (x, 0, k))],
        out_specs=pl.BlockSpec((L, LANES), lambda x, k: (0, x)),
        out_shape=jax.ShapeDtypeStruct((L, width), F32),
        scratch_shapes=[pltpu.VMEM((tah, tw), BF16), pltpu.VMEM((sw, tw), BF16)], name="s5_out",
        compiler_params=_cp("parallel", "arbitrary"),
    )(u, sp, kt, vc, dd)


def _t5_bucket_starts(n_buckets):
    exact = n_buckets // 2
    dist = np.arange(0, 4 * REL_MAX_DIST).astype(np.int64)
    far = exact + (np.log(np.maximum(dist, 1).astype(np.float32) / np.float32(exact))
                   / np.float32(math.log(REL_MAX_DIST / exact)) * np.float32(n_buckets - exact)).astype(np.int64)
    bucket = np.where(dist < exact, dist, np.minimum(far, n_buckets - 1))
    return [int(np.argmax(bucket >= b)) for b in range(1, n_buckets)]


def _bias_tile_kernel(rb_ref, o_ref, *, tb, starts):
    delta = jnp.where(pl.program_id(0) == 3, -1, pl.program_id(0))
    head = pl.program_id(1)
    key = lax.broadcasted_iota(I32, (tb, tb), 0)
    qry = lax.broadcasted_iota(I32, (tb, tb), 1)
    dist = delta * tb + qry - key
    bias = jnp.full((tb, tb), rb_ref[0, head], F32)
    for b, start in enumerate(starts, start=1):
        bias = jnp.where(dist >= start, rb_ref[b, head], bias)
    o_ref[...] = jnp.where(dist >= 0, bias * LOG2E, NEG_BIG)


def _bias_tiles(rel_bias, tb):
    n_buckets, heads = rel_bias.shape
    starts = _t5_bucket_starts(n_buckets)
    assert 2 * tb - (tb - 1) >= starts[-1]
    return pl.pallas_call(
        functools.partial(_bias_tile_kernel, tb=tb, starts=starts), grid=(4, heads),
        in_specs=[pl.BlockSpec(memory_space=pltpu.SMEM)],
        out_specs=pl.BlockSpec((None, None, tb, tb), lambda dl, hd: (dl, hd, 0, 0)),
        out_shape=jax.ShapeDtypeStruct((4, heads, tb, tb), F32), name="rel_bias_tiles",
        compiler_params=_cp("parallel", "parallel"),
    )(rel_bias)


def _indexer_kernel(qt_ref, k2_ref, wt_ref, keys_ref, thr_ref, *, tq, ck, n_heads, topk, w_scale):
    i = pl.program_id(0)
    n_blocks = keys_ref.shape[0] // tq
    n_live = (i + 1) * (tq // ck)
    wv = wt_ref[...] * w_scale

    def score_chunk(c, carry):
        row = pl.multiple_of(c * ck, ck)
        kk = k2_ref[c]
        acc = jnp.zeros((ck, tq), F32)
        for pr in range(n_heads // 2):
            s2 = jnp.dot(kk, qt_ref[pr * LANES:(pr + 1) * LANES, :], preferred_element_type=F32)
            acc = acc + jnp.maximum(s2[:ck], 0.0) * wv[2 * pr:2 * pr + 1, :]
            acc = acc + jnp.maximum(s2[ck:], 0.0) * wv[2 * pr + 1:2 * pr + 2, :]
        kpos = row + lax.broadcasted_iota(I32, (ck, tq), 0)
        qpos = i * tq + lax.broadcasted_iota(I32, (ck, tq), 1)
        acc = jnp.where(acc == 0.0, 0.0, acc)
        score = jnp.where(kpos <= qpos, acc, -jnp.inf)
        bits = pltpu.bitcast(score, I32)
        keys_ref[pl.ds(row, ck), :] = jnp.where(bits < 0, bits ^ 0x7FFFFFFF, bits)
        return carry

    lax.fori_loop(0, n_live, score_chunk, 0)

    def fill_block(b, carry):
        row = pl.multiple_of(b * tq, tq)
        keys_ref[pl.ds(row, tq), :] = jnp.full((tq, tq), INT_MIN, I32)
        return carry

    lax.fori_loop(i + 1, n_blocks, fill_block, 0)

    acc_rows = 32

    def count_ge(cand):
        def body(b, cnt):
            row = pl.multiple_of(b * tq, tq)
            hit = jnp.where(keys_ref[pl.ds(row, tq), :] >= cand, 1.0, 0.0)
            return cnt + jnp.sum(hit.reshape(tq // acc_rows, acc_rows, tq), axis=0)
        cnt = lax.fori_loop(0, i + 1, body, jnp.zeros((acc_rows, tq), F32))
        return jnp.sum(cnt, axis=0, keepdims=True)

    kf = float(topk)
    zero = jnp.zeros((1, tq), I32)
    c_zero = count_ge(zero)
    thr0 = jnp.where(c_zero >= kf, zero, jnp.full((1, tq), INT_MIN, I32))
    cnt0 = jnp.where(c_zero >= kf, c_zero, ((i + 1) * tq).astype(F32))

    def undecided(st):
        it, _, cnt = st
        return (it < 31) & (jnp.max(cnt) > kf)

    def bit_step(st):
        it, thr, cnt = st
        cand = thr | jnp.left_shift(jnp.int32(1), 30 - it)
        c = count_ge(cand)
        keep = c >= kf
        return it + 1, jnp.where(keep, cand, thr), jnp.where(keep, c, cnt)

    _, thr, cnt = lax.while_loop(undecided, bit_step, (jnp.int32(0), thr0, cnt0))

    n_keys = keys_ref.shape[0]

    def position_cutoff(_):
        need = kf - count_ge(thr + 1)

        def count_eq_upto(pos):
            def body(b, acc):
                row = pl.multiple_of(b * tq, tq)
                kpos = row + lax.broadcasted_iota(I32, (tq, tq), 0)
                hit = jnp.where(keys_ref[pl.ds(row, tq), :] == thr, jnp.where(kpos <= pos, 1.0, 0.0), 0.0)
                return acc + jnp.sum(hit.reshape(tq // acc_rows, acc_rows, tq), axis=0)
            acc = lax.fori_loop(0, i + 1, body, jnp.zeros((acc_rows, tq), F32))
            return jnp.sum(acc, axis=0, keepdims=True)

        def halve(_, bounds):
            lo, hi = bounds
            mid = (lo + hi) >> 1
            enough = count_eq_upto(mid) >= need
            return jnp.where(enough, lo, mid), jnp.where(enough, mid, hi)

        lo0 = jnp.full((1, tq), -1, I32)
        hi0 = jnp.full((1, tq), n_keys - 1, I32)
        _, hi = lax.fori_loop(0, max(1, (n_keys - 1).bit_length()), halve, (lo0, hi0))
        return jnp.where(cnt > kf, hi, n_keys)

    jcut = lax.cond(jnp.max(cnt) > kf, position_cutoff, lambda _: jnp.full((1, tq), n_keys, I32), 0)
    sel_row = lax.broadcasted_iota(I32, thr_ref.shape, 0)
    thr_ref[...] = jnp.where(sel_row == 0, thr, jnp.where(sel_row == 1, jcut, 0))


def _indexer(qi_t, row_off, k2, w_t, topk, tq=256):
    L = qi_t.shape[1]
    n_heads = w_t.shape[0]
    qw = n_heads * IDX_DIM
    assert row_off % qw == 0
    tq = _tile(L, tq)
    ck = k2.shape[1] // 2
    assert tq % ck == 0
    w_scale = (n_heads ** -0.5) * (IDX_DIM ** -0.5)
    return pl.pallas_call(
        functools.partial(_indexer_kernel, tq=tq, ck=ck, n_heads=n_heads, topk=topk, w_scale=w_scale),
        grid=(L // tq,),
        in_specs=[pl.BlockSpec((qw, tq), lambda i: (row_off // qw, i)),
                  pl.BlockSpec(k2.shape, lambda i: (0, 0, 0)),
                  pl.BlockSpec((n_heads, tq), lambda i: (0, i))],
        out_specs=[pl.BlockSpec((L, tq), lambda i: (0, i)),
                   pl.BlockSpec((SUBLANES, tq), lambda i: (0, i))],
        out_shape=[jax.ShapeDtypeStruct((L, L), I32), jax.ShapeDtypeStruct((SUBLANES, L), I32)],
        name="dsa_indexer", compiler_params=_cp("parallel"),
    )(qi_t, k2, w_t)


def _attn_kernel(qi_ref, kj_ref, far_ref, qt_ref, k_ref, vt_ref, keys_ref, thr_ref, bias_a_ref, bias_b_ref, o_ref,
                 m_ref, l_ref, acc_ref, mb_ref, *, heads, dh, scale2):
    step = pl.program_id(0)
    qi = qi_ref[step]
    kj = kj_ref[step]

    @pl.when(kj == 0)
    def _():
        m_ref[...] = jnp.full(m_ref.shape, -jnp.inf, F32)
        l_ref[...] = jnp.zeros(l_ref.shape, F32)
        acc_ref[...] = jnp.zeros(acc_ref.shape, F32)

    thr = thr_ref[0:1, :]
    kpos = kj * keys_ref.shape[0] + lax.broadcasted_iota(I32, keys_ref.shape, 0)
    bar = jnp.where(kpos <= thr_ref[1:2, :], thr, thr + 1)
    mb_ref[...] = jnp.where(keys_ref[...] >= bar, 0.0, NEG_BIG)

    def scores(h):
        hs = slice(h * dh, (h + 1) * dh)
        return jnp.dot(k_ref[:, hs], qt_ref[hs, :], preferred_element_type=F32)

    def fold(reduce_fn, a, rows=32):
        return reduce_fn(a.reshape(a.shape[0] // rows, rows, a.shape[1]), axis=0)

    def run(near):
        ahead = 4
        pending = [scores(h) for h in range(min(ahead, heads))]
        for h in range(heads):
            hs = slice(h * dh, (h + 1) * dh)
            x = pending.pop(0) * scale2 + mb_ref[...]
            if h + ahead < heads:
                pending.append(scores(h + ahead))
            if near:
                x = x + jnp.concatenate([bias_a_ref[h], bias_b_ref[h]], axis=0)
            far = 0.0 if near else far_ref[h]
            m_prev = m_ref[h]
            m_new = jnp.maximum(m_prev, jnp.max(fold(jnp.max, x), axis=0, keepdims=True) + far)
            alpha = jnp.exp2(m_prev - m_new)
            p = jnp.exp2(x - (m_new[0:1, :] - far))
            l_ref[h] = alpha * l_ref[h] + jnp.sum(fold(jnp.sum, p), axis=0, keepdims=True)
            m_ref[h] = m_new
            acc_ref[hs, :] = alpha[0:1, :] * acc_ref[hs, :] + jnp.dot(vt_ref[hs, :], p.astype(BF16),
                                                                      preferred_element_type=F32)

    @pl.when(qi - 2 * kj - 1 < 2)
    def _():
        run(True)

    @pl.when(qi - 2 * kj - 1 >= 2)
    def _():
        run(False)

    @pl.when(kj == qi // 2)
    def _():
        for h in range(heads):
            hs = slice(h * dh, (h + 1) * dh)
            o_ref[:, hs] = jnp.transpose(acc_ref[hs, :] / l_ref[h][0:1, :]).astype(o_ref.dtype)


def _attention(q_t, v_t, k, keys, thr, bias_tiles, far_bias, tb):
    L = k.shape[0]
    heads = bias_tiles.shape[1]
    dh = ATT_HEAD_DIM
    aw = heads * dh
    nb = L // tb
    assert nb % 2 == 0
    pairs = [(i, j) for i in range(nb) for j in range(i // 2 + 1)]
    qi = jnp.asarray(np.array([p[0] for p in pairs], np.int32))
    kj = jnp.asarray(np.array([p[1] for p in pairs], np.int32))

    def tile_of(behind):
        return jnp.where(behind < 0, 3, jnp.minimum(behind, 2))

    bias_spec_a = pl.BlockSpec((None, heads, tb, tb), lambda s, qi, kj: (tile_of(qi[s] - 2 * kj[s]), 0, 0, 0))
    bias_spec_b = pl.BlockSpec((None, heads, tb, tb), lambda s, qi, kj: (tile_of(qi[s] - 2 * kj[s] - 1), 0, 0, 0))
    grid_spec = pltpu.PrefetchScalarGridSpec(
        num_scalar_prefetch=2, grid=(len(pairs),),
        in_specs=[pl.BlockSpec(memory_space=pltpu.SMEM),
                  pl.BlockSpec((aw, tb), lambda s, qi, kj: (0, qi[s])),
                  pl.BlockSpec((2 * tb, aw), lambda s, qi, kj: (kj[s], 0)),
                  pl.BlockSpec((aw, 2 * tb), lambda s, qi, kj: (0, kj[s])),
                  pl.BlockSpec((2 * tb, tb), lambda s, qi, kj: (kj[s], qi[s])),
                  pl.BlockSpec((SUBLANES, tb), lambda s, qi, kj: (0, qi[s])),
                  bias_spec_a, bias_spec_b],
        out_specs=pl.BlockSpec((tb, aw), lambda s, qi, kj: (qi[s], 0)),
        scratch_shapes=[pltpu.VMEM((heads, SUBLANES, tb), F32),
                        pltpu.VMEM((heads, SUBLANES, tb), F32),
                        pltpu.VMEM((aw, tb), F32),
                        pltpu.VMEM((2 * tb, tb), F32)])
    return pl.pallas_call(
        functools.partial(_attn_kernel, heads=heads, dh=dh, scale2=dh ** -0.5 * LOG2E),
        grid_spec=grid_spec, out_shape=jax.ShapeDtypeStruct((L, aw), BF16), name="dsa_attention",
        compiler_params=_cp("arbitrary"),
    )(qi, kj, far_bias, q_t, k, v_t, keys, thr, bias_tiles, bias_tiles)


def _router_kernel(h_ref, w_ref, b_ref, g_ref, *, n_exp, top_k):
    logits = jnp.dot(h_ref[...], w_ref[...], preferred_element_type=F32) + b_ref[...]
    lane = lax.broadcasted_iota(I32, logits.shape, 1).astype(F32)
    work = jnp.where(lane < n_exp, logits, -jnp.inf)
    gates = jnp.zeros(logits.shape, F32)
    denom = None
    top = None
    for r in range(top_k):
        m = jnp.max(work, axis=1, keepdims=True)
        first = jnp.min(jnp.where(work == m, lane, float(LANES)), axis=1, keepdims=True)
        sel = lane == first
        if r == 0:
            top = m
        e = jnp.exp(m - top)
        gates = gates + jnp.where(sel, e, 0.0)
        denom = e if denom is None else denom + e
        work = jnp.where(sel, -jnp.inf, work)
    g_ref[...] = gates / denom


def _router(h, w_pad, b_pad, n_exp, tm=512):
    m, d = h.shape
    tm = _tile(m, tm)
    return pl.pallas_call(
        functools.partial(_router_kernel, n_exp=n_exp, top_k=TOP_K_EXPERTS), grid=(m // tm,),
        in_specs=[pl.BlockSpec((tm, d), lambda i: (i, 0)),
                  pl.BlockSpec((d, LANES), lambda i: (0, 0)),
                  pl.BlockSpec((1, LANES), lambda i: (0, 0))],
        out_specs=pl.BlockSpec((tm, LANES), lambda i: (i, 0)),
        out_shape=jax.ShapeDtypeStruct((m, LANES), F32), name="moe_router",
        compiler_params=_cp("parallel"),
    )(h, w_pad, b_pad)


def _moe_kernel(h_ref, g_ref, wgu_ref, bgu_ref, wdn_ref, bdn_ref, o_ref, *, ff, eb):
    step = pl.program_id(1)
    gates = g_ref[...]

    @pl.when(step == 0)
    def _():
        o_ref[...] = jnp.dot(gates.astype(BF16), bdn_ref[...], preferred_element_type=F32)

    lane = lax.broadcasted_iota(I32, gates.shape, 1)
    for k in range(eb):
        gu = jnp.dot(h_ref[...], wgu_ref[k], preferred_element_type=F32) + bgu_ref[k]
        gate = jnp.minimum(gu[:, :ff], SWIGLU_LIMIT)
        up = jnp.clip(gu[:, ff:], -SWIGLU_LIMIT, SWIGLU_LIMIT)
        act = (up + 1.0) * gate * jax.nn.sigmoid(SWIGLU_ALPHA * gate)
        ge = jnp.sum(jnp.where(lane == step * eb + k, gates, 0.0), axis=1, keepdims=True)
        o_ref[...] += jnp.dot((act * ge).astype(BF16), wdn_ref[k], preferred_element_type=F32)


def _moe_experts(h, gates, w_gu, b_gu, w_dn, b_dn_pad, layer, tm=512, eb=2):
    m, d = h.shape
    _, n_exp, _, ff2 = w_gu.shape
    tm = _tile(m, tm)
    eb = _tile(n_exp, eb)
    return pl.pallas_call(
        functools.partial(_moe_kernel, ff=ff2 // 2, eb=eb), grid=(m // tm, n_exp // eb),
        in_specs=[pl.BlockSpec((tm, d), lambda i, e: (i, 0)),
                  pl.BlockSpec((tm, LANES), lambda i, e: (i, 0)),
                  pl.BlockSpec((None, eb, d, ff2), lambda i, e: (layer, e, 0, 0)),
                  pl.BlockSpec((eb, 1, ff2), lambda i, e: (e, 0, 0)),
                  pl.BlockSpec((None, eb, ff2 // 2, d), lambda i, e: (layer, e, 0, 0)),
                  pl.BlockSpec((LANES, d), lambda i, e: (0, 0))],
        out_specs=pl.BlockSpec((tm, d), lambda i, e: (i, 0)),
        out_shape=jax.ShapeDtypeStruct((m, d), F32), name="moe_experts",
        compiler_params=_cp("parallel", "arbitrary"),
    )(h, gates, w_gu, b_gu.reshape(n_exp, 1, ff2), w_dn, b_dn_pad)


def kernel(x, c, w_in, w_a_out, w_b_out, w_o, w_glu, b_glu, s5_lam_re, s5_lam_im, s5_b_re, s5_b_im, s5_c_re, s5_c_im, s5_d, s5_log_step, rel_bias, w_router, b_router, w_gu, b_gu, w_dn, b_dn, w_c1, b_c1, w_c2, b_c2, mod_table, ln_g, ln_b):
    bsz, L, d = x.shape
    assert bsz == 1 and c.shape[0] == 1
    depth = w_in.shape[0]
    g, p, h = s5_b_re.shape[1:]
    s5w = g * h
    heads = rel_bias.shape[1]
    aw = heads * ATT_HEAD_DIM
    iw = IDX_HEADS * IDX_DIM
    n_exp = w_router.shape[2]
    assert w_in.shape[2] == s5w + 3 * aw + iw + IDX_DIM + IDX_HEADS + 2 * d
    assert n_exp <= LANES and IDX_DIM + IDX_HEADS <= LANES and 2 * IDX_DIM == LANES
    o_q, o_k, o_v, o_qi = s5w, s5w + aw, s5w + 2 * aw, s5w + 3 * aw
    o_ki = s5w + 3 * aw + iw
    o_gate = o_ki + IDX_DIM + IDX_HEADS
    topk = min(TOPK_MAX, L // 4)
    tb = _tile(L, 256)
    ck = _tile(L, LANES)
    dn_alpha = (2 * depth) ** 0.25

    mod = _conditioning(c, w_c1, b_c1, w_c2, b_c2, mod_table).reshape(depth, N_MOD, d)
    bias_tiles = _bias_tiles(rel_bias, tb)
    far_bias = rel_bias[-1] * LOG2E
    assert o_ki % LANES == 0 and (g * h) % LANES == 0 and LANES % h == 0

    w_proj = w_in.astype(BF16)
    w_gu_bf = w_gu.astype(BF16)
    w_dn_bf = w_dn.astype(BF16)

    xs = x.reshape(L, d)
    hcur = _modulate(xs, mod[0], sc_row=1, sh_row=0)
    for l in range(depth):
        u = _matmul(hcur, w_proj, layer=l, out_dtype=F32, col_off=0, n_cols=s5w, name="proj_u")
        k = _matmul(hcur, w_proj, layer=l, out_dtype=BF16, col_off=o_k, n_cols=aw, name="proj_k")
        gates_ab = _matmul_shifted(hcur, w_proj, layer=l, out_dtype=F32, col_off=o_gate, n_cols=2 * d,
                                   name="proj_gates")
        misc = _matmul(hcur, w_proj, layer=l, out_dtype=F32, col_off=o_ki, n_cols=LANES, tn=LANES, name="proj_misc")
        q_t = _matmul_t(hcur, w_proj, layer=l, out_dtype=BF16, col_off=o_q, n_cols=aw, name="proj_q_t")
        vqi_t = _matmul_t(hcur, w_proj, layer=l, out_dtype=BF16, col_off=o_v, n_cols=aw + iw,
                          name="proj_vqi_t")

        prm = _s5_discretize(s5_lam_re[l], s5_lam_im[l], s5_b_re[l], s5_b_im[l], s5_c_re[l], s5_c_im[l],
                             s5_d[l], s5_log_step[l], S5_CHUNK)
        y_s5 = _s5_branch(u, prm)
        ya_in = _glu(y_s5, w_glu, b_glu[l], l)

        k_idx = misc[:, :IDX_DIM].astype(BF16)
        zeros = jnp.zeros_like(k_idx)
        k_even = jnp.concatenate([k_idx, zeros], axis=1).reshape(L // ck, ck, LANES)
        k_odd = jnp.concatenate([zeros, k_idx], axis=1).reshape(L // ck, ck, LANES)
        k2 = jnp.concatenate([k_even, k_odd], axis=1)
        w_t = jnp.transpose(misc[:, IDX_DIM:IDX_DIM + IDX_HEADS])
        keys, thr = _indexer(vqi_t, aw, k2, w_t, topk)
        attn = _attention(q_t, vqi_t, k, keys, thr, bias_tiles, far_bias, tb)

        merged = _merge(ya_in, attn, w_a_out, w_b_out, gates_ab, l)
        y = _matmul(merged, w_o, layer=l, out_dtype=F32, name="proj_out")
        xs, h2 = _ln_modulate(xs, y, mod[l], ln_g[l], ln_b[l], mod[l], alpha=dn_alpha, gate_row=2, ln_row=0,
                              sc_row=4, sh_row=3, want_h=True)

        w_r = jnp.pad(w_router[l], ((0, 0), (0, LANES - n_exp))).astype(BF16)
        b_r = jnp.pad(b_router[l], (0, LANES - n_exp)).reshape(1, LANES)
        gates = _router(h2, w_r, b_r, n_exp)
        b_dn_pad = jnp.pad(b_dn[l], ((0, LANES - n_exp), (0, 0))).astype(BF16)
        y = _moe_experts(h2, gates, w_gu_bf, b_gu[l], w_dn_bf, b_dn_pad, l)
        last = l == depth - 1
        xs, hcur = _ln_modulate(xs, y, mod[l], ln_g[l], ln_b[l], mod[min(l + 1, depth - 1)], alpha=dn_alpha,
                                gate_row=5, ln_row=1, sc_row=1, sh_row=0, want_h=not last)
    return xs.reshape(bsz, L, d)
```

```python
import functools
import math

import numpy as np
import jax
import jax.numpy as jnp
from jax import lax
from jax.experimental import pallas as pl
from jax.experimental.pallas import tpu as pltpu

F32 = jnp.float32
BF16 = jnp.bfloat16
I32 = jnp.int32

ATT_HEAD_DIM = 128
IDX_HEADS = 32
IDX_DIM = 64
TOPK_MAX = 256
REL_MAX_DIST = 128
TOP_K_EXPERTS = 4
SWIGLU_LIMIT = 7.0
SWIGLU_ALPHA = 1.702
N_MOD = 6
LN_EPS = 1e-5

SUBLANES = 8
LANES = 128
VMEM_LIMIT_BYTES = 56 * 1024 * 1024
S5_CHUNK = 16
NEG_BIG = -1e30
LOG2E = math.log2(math.e)
INT_MIN = -(2 ** 31)


def _cp(*sem):
    return pltpu.CompilerParams(dimension_semantics=sem, vmem_limit_bytes=VMEM_LIMIT_BYTES)


def _tile(n, t):
    t = min(n, t)
    while n % t:
        t //= 2
    return t


def _cond1_kernel(c_ref, w_ref, b_ref, o_ref):
    a = jax.nn.silu(c_ref[...]).astype(BF16)
    t = jnp.dot(a, w_ref[...].astype(BF16), preferred_element_type=F32) + b_ref[...]
    o_ref[...] = jax.nn.silu(t)


def _cond2_kernel(t_ref, w_ref, b_ref, tab_ref, o_ref):
    acc = jnp.dot(t_ref[...].astype(BF16), w_ref[...].astype(BF16), preferred_element_type=F32)
    o_ref[...] = acc[0:1, :] + b_ref[...] + tab_ref[...]


def _conditioning(c, w_c1, b_c1, w_c2, b_c2, mod_table):
    d = c.shape[1]
    r = w_c1.shape[1]
    depth = mod_table.shape[0]
    n = w_c2.shape[1]
    c8 = jnp.broadcast_to(c, (SUBLANES, d))
    t = pl.pallas_call(
        _cond1_kernel, out_shape=jax.ShapeDtypeStruct((SUBLANES, r), F32), name="cond1",
        compiler_params=_cp(),
    )(c8, w_c1, b_c1.reshape(1, r))
    tn = _tile(n, 2048)
    return pl.pallas_call(
        _cond2_kernel, grid=(n // tn,),
        in_specs=[pl.BlockSpec((SUBLANES, r), lambda j: (0, 0)),
                  pl.BlockSpec((r, tn), lambda j: (0, j)),
                  pl.BlockSpec((1, tn), lambda j: (0, j)),
                  pl.BlockSpec((depth, tn), lambda j: (0, j))],
        out_specs=pl.BlockSpec((depth, tn), lambda j: (0, j)),
        out_shape=jax.ShapeDtypeStruct((depth, n), F32), name="cond2",
        compiler_params=_cp("parallel"),
    )(t, w_c2, b_c2.reshape(1, n), mod_table.reshape(depth, n))


def _modulate_kernel(x_ref, mod_ref, h_ref, *, sc_row, sh_row):
    sc = mod_ref[sc_row:sc_row + 1, :]
    sh = mod_ref[sh_row:sh_row + 1, :]
    h_ref[...] = (x_ref[...] * (1.0 + sc) + sh).astype(h_ref.dtype)


def _modulate(x, mod_l, sc_row, sh_row, tm=256):
    m, d = x.shape
    tm = _tile(m, tm)
    return pl.pallas_call(
        functools.partial(_modulate_kernel, sc_row=sc_row, sh_row=sh_row),
        grid=(m // tm,),
        in_specs=[pl.BlockSpec((tm, d), lambda i: (i, 0)),
                  pl.BlockSpec((N_MOD, d), lambda i: (0, 0))],
        out_specs=pl.BlockSpec((tm, d), lambda i: (i, 0)),
        out_shape=jax.ShapeDtypeStruct((m, d), BF16), name="modulate",
        compiler_params=_cp("parallel"),
    )(x, mod_l)


def _ln_kernel(x_ref, y_ref, mod_ref, g_ref, b_ref, nmod_ref, xo_ref, *h_refs,
               alpha, gate_row, ln_row, sc_row, sh_row):
    gate = mod_ref[gate_row:gate_row + 1, :]
    u = alpha * x_ref[...] + (1.0 + gate) * y_ref[...]
    mu = jnp.mean(u, axis=-1, keepdims=True)
    uc = u - mu
    var = jnp.mean(uc * uc, axis=-1, keepdims=True)
    xn = uc * lax.rsqrt(var + LN_EPS) * g_ref[ln_row:ln_row + 1, :] + b_ref[ln_row:ln_row + 1, :]
    xo_ref[...] = xn
    if h_refs:
        sc = nmod_ref[sc_row:sc_row + 1, :]
        sh = nmod_ref[sh_row:sh_row + 1, :]
        h_refs[0][...] = (xn * (1.0 + sc) + sh).astype(BF16)


def _ln_modulate(x, y, mod_l, ln_g_l, ln_b_l, next_mod, *, alpha, gate_row, ln_row, sc_row, sh_row,
                 want_h, tm=256):
    m, d = x.shape
    tm = _tile(m, tm)
    row = pl.BlockSpec((tm, d), lambda i: (i, 0))
    out_shape = [jax.ShapeDtypeStruct((m, d), F32)]
    out_specs = [row]
    if want_h:
        out_shape.append(jax.ShapeDtypeStruct((m, d), BF16))
        out_specs.append(row)
    res = pl.pallas_call(
        functools.partial(_ln_kernel, alpha=alpha, gate_row=gate_row, ln_row=ln_row,
                          sc_row=sc_row, sh_row=sh_row),
        grid=(m // tm,),
        in_specs=[row, row,
                  pl.BlockSpec((N_MOD, d), lambda i: (0, 0)),
                  pl.BlockSpec((2, d), lambda i: (0, 0)),
                  pl.BlockSpec((2, d), lambda i: (0, 0)),
                  pl.BlockSpec((N_MOD, d), lambda i: (0, 0))],
        out_specs=out_specs, out_shape=out_shape, name="ln_modulate",
        compiler_params=_cp("parallel"),
    )(x, y, mod_l, ln_g_l, ln_b_l, next_mod)
    return (res[0], res[1]) if want_h else (res[0], None)


def _mm_kernel(a_ref, w_ref, o_ref, wbf_ref):
    @pl.when(pl.program_id(1) == 0)
    def _():
        wbf_ref[...] = w_ref[...].astype(BF16)

    o_ref[...] = jnp.dot(a_ref[...], wbf_ref[...], preferred_element_type=F32).astype(o_ref.dtype)


def _w_spec(w, layer, k, tn, off=0, scale=1):
    if w.ndim == 3:
        return pl.BlockSpec((None, k, tn), lambda j, i: (layer, 0, (j + off) * scale))
    return pl.BlockSpec((k, tn), lambda j, i: (0, (j + off) * scale))


def _matmul(a, w, *, out_dtype, layer=0, col_off=0, n_cols=None, tm=1024, tn=512, name="matmul"):
    m, k = a.shape
    n_cols = w.shape[-1] - col_off if n_cols is None else n_cols
    tm = _tile(m, tm)
    tn = math.gcd(_tile(n_cols, tn), col_off) if col_off else _tile(n_cols, tn)
    off = col_off // tn
    return pl.pallas_call(
        _mm_kernel, grid=(n_cols // tn, m // tm),
        in_specs=[pl.BlockSpec((tm, k), lambda j, i: (i, 0)),
                  _w_spec(w, layer, k, tn, off)],
        out_specs=pl.BlockSpec((tm, tn), lambda j, i: (i, j)),
        out_shape=jax.ShapeDtypeStruct((m, n_cols), out_dtype),
        scratch_shapes=[pltpu.VMEM((k, tn), BF16)], name=name,
        compiler_params=_cp("parallel", "arbitrary"),
    )(a, w)


def _mm_shift_kernel(a_ref, w_ref, wn_ref, o_ref, wbf_ref, *, shift):
    @pl.when(pl.program_id(1) == 0)
    def _():
        both = jnp.concatenate([w_ref[...], wn_ref[...]], axis=1)
        wbf_ref[...] = both[:, shift:shift + wbf_ref.shape[1]]

    o_ref[...] = jnp.dot(a_ref[...], wbf_ref[...], preferred_element_type=F32).astype(o_ref.dtype)


def _matmul_shifted(a, w, *, out_dtype, col_off, n_cols, layer=0, tm=1024, tn=512, name="matmul_shifted"):
    m, k = a.shape
    tm = _tile(m, tm)
    base, shift = col_off // LANES * LANES, col_off % LANES
    tn = math.gcd(_tile(n_cols, tn), base) if base else _tile(n_cols, tn)
    assert tn % LANES == 0
    return pl.pallas_call(
        functools.partial(_mm_shift_kernel, shift=shift), grid=(n_cols // tn, m // tm),
        in_specs=[pl.BlockSpec((tm, k), lambda j, i: (i, 0)),
                  _w_spec(w, layer, k, tn, base // tn),
                  _w_spec(w, layer, k, LANES, 1 + base // tn, tn // LANES)],
        out_specs=pl.BlockSpec((tm, tn), lambda j, i: (i, j)),
        out_shape=jax.ShapeDtypeStruct((m, n_cols), out_dtype),
        scratch_shapes=[pltpu.VMEM((k, tn), w.dtype)], name=name,
        compiler_params=_cp("parallel", "arbitrary"),
    )(a, w, w)


def _mm_t_kernel(a_ref, w_ref, o_ref, wt_ref):
    @pl.when(pl.program_id(1) == 0)
    def _():
        wt_ref[...] = jnp.transpose(w_ref[...])

    o_ref[...] = lax.dot_general(wt_ref[...], a_ref[...], (((1,), (1,)), ((), ())),
                                 preferred_element_type=F32).astype(o_ref.dtype)


def _matmul_t(a, w, *, out_dtype, col_off, n_cols, layer=0, tm=1024, tn=1024, name="matmul_t"):
    m, k = a.shape
    tm = _tile(m, tm)
    tn = math.gcd(_tile(n_cols, tn), col_off) if col_off else _tile(n_cols, tn)
    off = col_off // tn
    return pl.pallas_call(
        _mm_t_kernel, grid=(n_cols // tn, m // tm),
        in_specs=[pl.BlockSpec((tm, k), lambda j, i: (i, 0)),
                  _w_spec(w, layer, k, tn, off)],
        out_specs=pl.BlockSpec((tn, tm), lambda j, i: (j, i)),
        out_shape=jax.ShapeDtypeStruct((n_cols, m), out_dtype),
        scratch_shapes=[pltpu.VMEM((tn, k), w.dtype)], name=name,
        compiler_params=_cp("parallel", "arbitrary"),
    )(a, w)


def _glu_kernel(y_ref, yt_ref, w_ref, b_ref, o_ref, wbf_ref):
    @pl.when(pl.program_id(1) == 0)
    def _():
        wbf_ref[...] = w_ref[...].astype(BF16)

    z = jnp.dot(y_ref[...].astype(BF16), wbf_ref[...], preferred_element_type=F32) + b_ref[...]
    o_ref[...] = (yt_ref[...] * jax.nn.sigmoid(z)).astype(o_ref.dtype)


def _glu(y, w, b, layer, tm=1024, tn=512):
    m, k = y.shape
    n = w.shape[-1]
    tm = _tile(m, tm)
    tn = _tile(n, tn)
    return pl.pallas_call(
        _glu_kernel, grid=(n // tn, m // tm),
        in_specs=[pl.BlockSpec((tm, k), lambda j, i: (i, 0)),
                  pl.BlockSpec((tm, tn), lambda j, i: (i, j)),
                  _w_spec(w, layer, k, tn),
                  pl.BlockSpec((1, tn), lambda j, i: (0, j))],
        out_specs=pl.BlockSpec((tm, tn), lambda j, i: (i, j)),
        out_shape=jax.ShapeDtypeStruct((m, n), BF16),
        scratch_shapes=[pltpu.VMEM((k, tn), BF16)], name="s5_glu",
        compiler_params=_cp("parallel", "arbitrary"),
    )(y, y, w, b.reshape(1, n))


def _merge_kernel(a1_ref, a2_ref, w1_ref, w2_ref, ga_ref, gb_ref, o_ref, w1bf_ref, w2bf_ref):
    @pl.when(pl.program_id(1) == 0)
    def _():
        w1bf_ref[...] = w1_ref[...].astype(BF16)
        w2bf_ref[...] = w2_ref[...].astype(BF16)

    ya = jnp.dot(a1_ref[...], w1bf_ref[...], preferred_element_type=F32)
    yb = jnp.dot(a2_ref[...], w2bf_ref[...], preferred_element_type=F32)
    o_ref[...] = (jax.nn.sigmoid(ga_ref[...]) * ya + jax.nn.sigmoid(gb_ref[...]) * yb).astype(o_ref.dtype)


def _merge(a1, a2, w1, w2, gates, layer, tm=1024, tn=512):
    m, k1 = a1.shape
    k2 = a2.shape[1]
    n = w1.shape[-1]
    tm = _tile(m, tm)
    tn = _tile(n, tn)
    nb = n // tn
    return pl.pallas_call(
        _merge_kernel, grid=(nb, m // tm),
        in_specs=[pl.BlockSpec((tm, k1), lambda j, i: (i, 0)),
                  pl.BlockSpec((tm, k2), lambda j, i: (i, 0)),
                  _w_spec(w1, layer, k1, tn),
                  _w_spec(w2, layer, k2, tn),
                  pl.BlockSpec((tm, tn), lambda j, i: (i, j)),
                  pl.BlockSpec((tm, tn), lambda j, i: (i, j + nb))],
        out_specs=pl.BlockSpec((tm, tn), lambda j, i: (i, j)),
        out_shape=jax.ShapeDtypeStruct((m, n), BF16),
        scratch_shapes=[pltpu.VMEM((k1, tn), BF16), pltpu.VMEM((k2, tn), BF16)], name="branch_merge",
        compiler_params=_cp("parallel", "arbitrary"),
    )(a1, a2, w1, w2, gates, gates)


def _s5_discretize(lam_re, lam_im, b_re, b_im, c_re, c_im, d, log_step, t_chunk):
    hp = lax.Precision.HIGHEST
    g, p, h = b_re.shape
    na = LANES // h
    nx = g // na
    assert 2 * p == LANES
    lam = lax.complex(jnp.minimum(lam_re, -1e-4), lam_im)
    dt = jnp.exp(log_step)[:, None]
    lam_bar = jnp.exp(lam * dt)
    b_bar = ((lam_bar - 1.0) / lam)[:, :, None] * lax.complex(b_re, b_im)
    c = lax.complex(c_re, c_im)
    tau = jnp.arange(t_chunk + 1, dtype=F32)
    pw = jnp.exp((lam * dt)[:, None, :] * tau[None, :, None].astype(jnp.complex64))
    cp = c[:, None, :, :] * pw[:, :, None, :]
    kern = jnp.einsum('gtop,gpi->gtoi', cp[:, :t_chunk], b_bar, precision=hp).real
    kio = jnp.transpose(kern.reshape(nx, na, t_chunk, h, h), (0, 2, 1, 4, 3))
    kt = jnp.tile(kio.reshape(nx, t_chunk, LANES, h), (1, 1, 1, na))
    wcx = pw[:, t_chunk - 1 - np.arange(t_chunk)][:, :, :, None] * b_bar[:, None, :, :]
    wcx = jnp.transpose(wcx, (0, 1, 3, 2))
    w_ri = jnp.concatenate([wcx.real, wcx.imag], axis=-1).reshape(nx, na, t_chunk, h, 2 * p)
    wc = jnp.transpose(w_ri, (0, 2, 1, 3, 4)).reshape(nx, t_chunk, LANES, 2 * p)
    cl = jnp.transpose(cp[:, 1:t_chunk + 1], (0, 3, 1, 2))
    vc = jnp.concatenate([cl.real, -cl.imag], axis=1).reshape(nx, na, 2 * p, t_chunk * h)
    dd = jnp.broadcast_to(d.reshape(nx, 1, 1, na, h), (nx, 1, t_chunk, na, h)).reshape(nx, 1, t_chunk * LANES)
    a = pw[:, t_chunk]
    a1 = jnp.concatenate([a.real, a.real], axis=-1).reshape(1, g * 2 * p)
    a2 = jnp.concatenate([-a.imag, a.imag], axis=-1).reshape(1, g * 2 * p)
    return kt, wc, vc, dd, a1, a2


def _chunk_rows(u_ref, t, nc):
    return jnp.concatenate([u_ref[pl.ds(s, nc, stride=t), :] for s in range(t)], axis=1)


def _s5_state_in_kernel(u_ref, wc_ref, xa_ref, xb_ref, wa_ref, wb_ref, *, t, nc, h):
    na = LANES // h
    row_group = lax.broadcasted_iota(I32, (LANES, LANES), 0) // h
    for s in range(t):
        blk = wc_ref[s]
        swapped = pltpu.roll(blk, LANES // 2, axis=1)
        for b in range(na):
            own = row_group == b
            dst = (slice(s * LANES, (s + 1) * LANES), slice(b * LANES, (b + 1) * LANES))
            wa_ref[dst] = jnp.where(own, blk, 0.0).astype(BF16)
            wb_ref[dst] = jnp.where(own, swapped, 0.0).astype(BF16)
    ucat = _chunk_rows(u_ref, t, nc).astype(BF16)
    xa_ref[...] = jnp.dot(ucat, wa_ref[...], preferred_element_type=F32)
    xb_ref[...] = jnp.dot(ucat, wb_ref[...], preferred_element_type=F32)


def _s5_scan_kernel(xa_ref, xb_ref, a1_ref, a2_ref, s_ref, *, unroll):
    a1 = a1_ref[...]
    a2 = a2_ref[...]
    nc = xa_ref.shape[0]

    def body(cb, carry):
        s_a, s_b = carry
        base = pl.multiple_of(cb * unroll, unroll)
        xa = xa_ref[pl.ds(base, unroll), :]
        xb = xb_ref[pl.ds(base, unroll), :]
        rows = []
        for k in range(unroll):
            rows.append(s_a)
            s_a, s_b = (a1 * s_a + a2 * s_b + xa[k:k + 1, :],
                        a1 * s_b - a2 * s_a + xb[k:k + 1, :])
        s_ref[pl.ds(base, unroll), :] = jnp.concatenate(rows, axis=0)
        return s_a, s_b

    z = jnp.zeros(a1.shape, F32)
    lax.fori_loop(0, nc // unroll, body, (z, z))


def _s5_out_kernel(u_ref, sp_ref, kt_ref, vc_ref, dd_ref, y_ref, bd_ref, vm_ref, *, t, nc, h, n_parts):
    part = pl.program_id(1)
    tp = t // n_parts
    na = LANES // h
    w = tp * LANES
    same_group = (lax.broadcasted_iota(I32, (LANES, LANES), 0) // h
                  == lax.broadcasted_iota(I32, (LANES, LANES), 1) // h)
    row = lax.broadcasted_iota(I32, (LANES, w), 0)
    col = lax.broadcasted_iota(I32, (LANES, w), 1)
    landing_group = jnp.where((row // h == col // LANES) & (row % h == col % h), (col // h) % na, -1)
    ucat = _chunk_rows(u_ref, t, nc)
    for k in range(n_parts):
        @pl.when(part == k)
        def _(k=k):
            live = (k + 1) * tp
            for s in range(live):
                for tt in range(tp):
                    lag = k * tp + tt - s
                    dst = (slice(s * LANES, (s + 1) * LANES), slice(tt * LANES, (tt + 1) * LANES))
                    if lag >= 0:
                        bd_ref[dst] = jnp.where(same_group, kt_ref[lag], 0.0).astype(BF16)
                    else:
                        bd_ref[dst] = jnp.zeros((LANES, LANES), BF16)
            for a in range(na):
                place = jnp.where(landing_group == a, 1.0, 0.0).astype(BF16)
                src = vc_ref[a][:, k * tp * h:(k + 1) * tp * h].astype(BF16)
                vm_ref[a * LANES:(a + 1) * LANES, :] = jnp.dot(src, place, preferred_element_type=F32).astype(BF16)
            y = jnp.dot(ucat[:, :live * LANES].astype(BF16), bd_ref[:live * LANES, :], preferred_element_type=F32)
            y = y + jnp.dot(sp_ref[...].astype(BF16), vm_ref[...], preferred_element_type=F32)
            y = jax.nn.gelu(y + dd_ref[...] * ucat[:, k * w:(k + 1) * w])
            for s in range(tp):
                y_ref[pl.ds(k * tp + s, nc, stride=t), :] = y[:, s * LANES:(s + 1) * LANES]


def _s5_branch(u, prm):
    kt, wc, vc, dd, a1, a2 = prm
    L, width = u.shape
    nx, na, _, th = vc.shape
    t = kt.shape[1]
    h = th // t
    tah = t * LANES
    sw = na * LANES
    nc = L // t
    u_spec = pl.BlockSpec((L, LANES), lambda x: (0, x))
    st_spec = pl.BlockSpec((nc, sw), lambda x: (0, x))
    xa, xb = pl.pallas_call(
        functools.partial(_s5_state_in_kernel, t=t, nc=nc, h=h), grid=(nx,),
        in_specs=[u_spec,
                  pl.BlockSpec((None, t, LANES, LANES), lambda x: (x, 0, 0, 0))],
        out_specs=[st_spec, st_spec],
        out_shape=[jax.ShapeDtypeStruct((nc, nx * sw), F32)] * 2,
        scratch_shapes=[pltpu.VMEM((tah, sw), BF16), pltpu.VMEM((tah, sw), BF16)], name="s5_state_in",
        compiler_params=_cp("parallel"),
    )(u, wc)
    ws = _tile(nx * sw, 2048)
    unroll = _tile(nc, SUBLANES)
    sc_spec = pl.BlockSpec((nc, ws), lambda x: (0, x))
    a_spec = pl.BlockSpec((1, ws), lambda x: (0, x))
    sp = pl.pallas_call(
        functools.partial(_s5_scan_kernel, unroll=unroll), grid=(nx * sw // ws,),
        in_specs=[sc_spec, sc_spec, a_spec, a_spec],
        out_specs=sc_spec,
        out_shape=jax.ShapeDtypeStruct((nc, nx * sw), F32), name="s5_scan",
        compiler_params=_cp("parallel"),
    )(xa, xb, a1, a2)
    n_parts = 2
    tw = tah // n_parts
    return pl.pallas_call(
        functools.partial(_s5_out_kernel, t=t, nc=nc, h=h, n_parts=n_parts), grid=(nx, n_parts),
        in_specs=[pl.BlockSpec((L, LANES), lambda x, k: (0, x)),
                  pl.BlockSpec((nc, sw), lambda x, k: (0, x)),
                  pl.BlockSpec((None, t, LANES, LANES), lambda x, k: (x, 0, 0, 0)),
                  pl.BlockSpec((None, na, LANES, th), lambda x, k: (x, 0, 0, 0)),
                  pl.BlockSpec((None, 1, tw), lambda x, k: (x, 0, k))],
        out_specs=pl.BlockSpec((L, LANES), lambda x, k: (0, x)),
        out_shape=jax.ShapeDtypeStruct((L, width), F32),
        scratch_shapes=[pltpu.VMEM((tah, tw), BF16), pltpu.VMEM((sw, tw), BF16)], name="s5_out",
        compiler_params=_cp("parallel", "arbitrary"),
    )(u, sp, kt, vc, dd)


def _t5_bucket_starts(n_buckets):
    exact = n_buckets // 2
    dist = np.arange(0, 4 * REL_MAX_DIST).astype(np.int64)
    far = exact + (np.log(np.maximum(dist, 1).astype(np.float32) / np.float32(exact))
                   / np.float32(math.log(REL_MAX_DIST / exact)) * np.float32(n_buckets - exact)).astype(np.int64)
    bucket = np.where(dist < exact, dist, np.minimum(far, n_buckets - 1))
    return [int(np.argmax(bucket >= b)) for b in range(1, n_buckets)]


def _bias_tile_kernel(rb_ref, o_ref, *, tb, starts):
    delta = jnp.where(pl.program_id(0) == 3, -1, pl.program_id(0))
    head = pl.program_id(1)
    key = lax.broadcasted_iota(I32, (tb, tb), 0)
    qry = lax.broadcasted_iota(I32, (tb, tb), 1)
    dist = delta * tb + qry - key
    bias = jnp.full((tb, tb), rb_ref[0, head], F32)
    for b, start in enumerate(starts, start=1):
        bias = jnp.where(dist >= start, rb_ref[b, head], bias)
    o_ref[...] = jnp.where(dist >= 0, bias * LOG2E, NEG_BIG)


def _bias_tiles(rel_bias, tb):
    n_buckets, heads = rel_bias.shape
    starts = _t5_bucket_starts(n_buckets)
    assert 2 * tb - (tb - 1) >= starts[-1]
    return pl.pallas_call(
        functools.partial(_bias_tile_kernel, tb=tb, starts=starts), grid=(4, heads),
        in_specs=[pl.BlockSpec(memory_space=pltpu.SMEM)],
        out_specs=pl.BlockSpec((None, None, tb, tb), lambda dl, hd: (dl, hd, 0, 0)),
        out_shape=jax.ShapeDtypeStruct((4, heads, tb, tb), F32), name="rel_bias_tiles",
        compiler_params=_cp("parallel", "parallel"),
    )(rel_bias)


def _indexer_kernel(qt_ref, k2_ref, wt_ref, keys_ref, thr_ref, *, tq, ck, n_heads, topk, w_scale):
    i = pl.program_id(0)
    n_blocks = keys_ref.shape[0] // tq
    n_live = (i + 1) * (tq // ck)
    wv = wt_ref[...] * w_scale

    def score_chunk(c, carry):
        row = pl.multiple_of(c * ck, ck)
        kk = k2_ref[c]
        acc = jnp.zeros((ck, tq), F32)
        for pr in range(n_heads // 2):
            s2 = jnp.dot(kk, qt_ref[pr * LANES:(pr + 1) * LANES, :], preferred_element_type=F32)
            acc = acc + jnp.maximum(s2[:ck], 0.0) * wv[2 * pr:2 * pr + 1, :]
            acc = acc + jnp.maximum(s2[ck:], 0.0) * wv[2 * pr + 1:2 * pr + 2, :]
        kpos = row + lax.broadcasted_iota(I32, (ck, tq), 0)
        qpos = i * tq + lax.broadcasted_iota(I32, (ck, tq), 1)
        acc = jnp.where(acc == 0.0, 0.0, acc)
        score = jnp.where(kpos <= qpos, acc, -jnp.inf)
        bits = pltpu.bitcast(score, I32)
        keys_ref[pl.ds(row, ck), :] = jnp.where(bits < 0, bits ^ 0x7FFFFFFF, bits)
        return carry

    lax.fori_loop(0, n_live, score_chunk, 0)

    def fill_block(b, carry):
        row = pl.multiple_of(b * tq, tq)
        keys_ref[pl.ds(row, tq), :] = jnp.full((tq, tq), INT_MIN, I32)
        return carry

    lax.fori_loop(i + 1, n_blocks, fill_block, 0)

    acc_rows = 32

    def count_ge(cand):
        def body(b, cnt):
            row = pl.multiple_of(b * tq, tq)
            hit = jnp.where(keys_ref[pl.ds(row, tq), :] >= cand, 1.0, 0.0)
            return cnt + jnp.sum(hit.reshape(tq // acc_rows, acc_rows, tq), axis=0)
        cnt = lax.fori_loop(0, i + 1, body, jnp.zeros((acc_rows, tq), F32))
        return jnp.sum(cnt, axis=0, keepdims=True)

    kf = float(topk)
    zero = jnp.zeros((1, tq), I32)
    c_zero = count_ge(zero)
    thr0 = jnp.where(c_zero >= kf, zero, jnp.full((1, tq), INT_MIN, I32))
    cnt0 = jnp.where(c_zero >= kf, c_zero, ((i + 1) * tq).astype(F32))

    def undecided(st):
        it, _, cnt = st
        return (it < 31) & (jnp.max(cnt) > kf)

    def bit_step(st):
        it, thr, cnt = st
        cand = thr | jnp.left_shift(jnp.int32(1), 30 - it)
        c = count_ge(cand)
        keep = c >= kf
        return it + 1, jnp.where(keep, cand, thr), jnp.where(keep, c, cnt)

    _, thr, cnt = lax.while_loop(undecided, bit_step, (jnp.int32(0), thr0, cnt0))

    n_keys = keys_ref.shape[0]

    def position_cutoff(_):
        need = kf - count_ge(thr + 1)

        def count_eq_upto(pos):
            def body(b, acc):
                row = pl.multiple_of(b * tq, tq)
                kpos = row + lax.broadcasted_iota(I32, (tq, tq), 0)
                hit = jnp.where(keys_ref[pl.ds(row, tq), :] == thr, jnp.where(kpos <= pos, 1.0, 0.0), 0.0)
                return acc + jnp.sum(hit.reshape(tq // acc_rows, acc_rows, tq), axis=0)
            acc = lax.fori_loop(0, i + 1, body, jnp.zeros((acc_rows, tq), F32))
            return jnp.sum(acc, axis=0, keepdims=True)

        def halve(_, bounds):
            lo, hi = bounds
            mid = (lo + hi) >> 1
            enough = count_eq_upto(mid) >= need
            return jnp.where(enough, lo, mid), jnp.where(enough, mid, hi)

        lo0 = jnp.full((1, tq), -1, I32)
        hi0 = jnp.full((1, tq), n_keys - 1, I32)
        _, hi = lax.fori_loop(0, max(1, (n_keys - 1).bit_length()), halve, (lo0, hi0))
        return jnp.where(cnt > kf, hi, n_keys)

    jcut = lax.cond(jnp.max(cnt) > kf, position_cutoff, lambda _: jnp.full((1, tq), n_keys, I32), 0)
    sel_row = lax.broadcasted_iota(I32, thr_ref.shape, 0)
    thr_ref[...] = jnp.where(sel_row == 0, thr, jnp.where(sel_row == 1, jcut, 0))


def _indexer(qi_t, row_off, k2, w_t, topk, tq=256):
    L = qi_t.shape[1]
    n_heads = w_t.shape[0]
    qw = n_heads * IDX_DIM
    assert row_off % qw == 0
    tq = _tile(L, tq)
    ck = k2.shape[1] // 2
    assert tq % ck == 0
    w_scale = (n_heads ** -0.5) * (IDX_DIM ** -0.5)
    return pl.pallas_call(
        functools.partial(_indexer_kernel, tq=tq, ck=ck, n_heads=n_heads, topk=topk, w_scale=w_scale),
        grid=(L // tq,),
        in_specs=[pl.BlockSpec((qw, tq), lambda i: (row_off // qw, i)),
                  pl.BlockSpec(k2.shape, lambda i: (0, 0, 0)),
                  pl.BlockSpec((n_heads, tq), lambda i: (0, i))],
        out_specs=[pl.BlockSpec((L, tq), lambda i: (0, i)),
                   pl.BlockSpec((SUBLANES, tq), lambda i: (0, i))],
        out_shape=[jax.ShapeDtypeStruct((L, L), I32), jax.ShapeDtypeStruct((SUBLANES, L), I32)],
        name="dsa_indexer", compiler_params=_cp("parallel"),
    )(qi_t, k2, w_t)


def _attn_kernel(qi_ref, kj_ref, far_ref, qt_ref, k_ref, vt_ref, keys_ref, thr_ref, bias_a_ref, bias_b_ref, o_ref,
                 m_ref, l_ref, acc_ref, mb_ref, *, heads, dh, scale2):
    step = pl.program_id(0)
    qi = qi_ref[step]
    kj = kj_ref[step]

    @pl.when(kj == 0)
    def _():
        m_ref[...] = jnp.full(m_ref.shape, -jnp.inf, F32)
        l_ref[...] = jnp.zeros(l_ref.shape, F32)
        acc_ref[...] = jnp.zeros(acc_ref.shape, F32)

    thr = thr_ref[0:1, :]
    kpos = kj * keys_ref.shape[0] + lax.broadcasted_iota(I32, keys_ref.shape, 0)
    bar = jnp.where(kpos <= thr_ref[1:2, :], thr, thr + 1)
    mb_ref[...] = jnp.where(keys_ref[...] >= bar, 0.0, NEG_BIG)

    def scores(h):
        hs = slice(h * dh, (h + 1) * dh)
        return jnp.dot(k_ref[:, hs], qt_ref[hs, :], preferred_element_type=F32)

    def fold(reduce_fn, a, rows=32):
        return reduce_fn(a.reshape(a.shape[0] // rows, rows, a.shape[1]), axis=0)

    def run(near):
        ahead = 4
        pending = [scores(h) for h in range(min(ahead, heads))]
        for h in range(heads):
            hs = slice(h * dh, (h + 1) * dh)
            x = pending.pop(0) * scale2 + mb_ref[...]
            if h + ahead < heads:
                pending.append(scores(h + ahead))
            if near:
                x = x + jnp.concatenate([bias_a_ref[h], bias_b_ref[h]], axis=0)
            far = 0.0 if near else far_ref[h]
            m_prev = m_ref[h]
            m_new = jnp.maximum(m_prev, jnp.max(fold(jnp.max, x), axis=0, keepdims=True) + far)
            alpha = jnp.exp2(m_prev - m_new)
            p = jnp.exp2(x - (m_new[0:1, :] - far))
            l_ref[h] = alpha * l_ref[h] + jnp.sum(fold(jnp.sum, p), axis=0, keepdims=True)
            m_ref[h] = m_new
            acc_ref[hs, :] = alpha[0:1, :] * acc_ref[hs, :] + jnp.dot(vt_ref[hs, :], p.astype(BF16),
                                                                      preferred_element_type=F32)

    @pl.when(qi - 2 * kj - 1 < 2)
    def _():
        run(True)

    @pl.when(qi - 2 * kj - 1 >= 2)
    def _():
        run(False)

    @pl.when(kj == qi // 2)
    def _():
        for h in range(heads):
            hs = slice(h * dh, (h + 1) * dh)
            o_ref[:, hs] = jnp.transpose(acc_ref[hs, :] / l_ref[h][0:1, :]).astype(o_ref.dtype)


def _attention(q_t, v_t, k, keys, thr, bias_tiles, far_bias, tb):
    L = k.shape[0]
    heads = bias_tiles.shape[1]
    dh = ATT_HEAD_DIM
    aw = heads * dh
    nb = L // tb
    assert nb % 2 == 0
    pairs = [(i, j) for i in range(nb) for j in range(i // 2 + 1)]
    qi = jnp.asarray(np.array([p[0] for p in pairs], np.int32))
    kj = jnp.asarray(np.array([p[1] for p in pairs], np.int32))

    def tile_of(behind):
        return jnp.where(behind < 0, 3, jnp.minimum(behind, 2))

    bias_spec_a = pl.BlockSpec((None, heads, tb, tb), lambda s, qi, kj: (tile_of(qi[s] - 2 * kj[s]), 0, 0, 0))
    bias_spec_b = pl.BlockSpec((None, heads, tb, tb), lambda s, qi, kj: (tile_of(qi[s] - 2 * kj[s] - 1), 0, 0, 0))
    grid_spec = pltpu.PrefetchScalarGridSpec(
        num_scalar_prefetch=2, grid=(len(pairs),),
        in_specs=[pl.BlockSpec(memory_space=pltpu.SMEM),
                  pl.BlockSpec((aw, tb), lambda s, qi, kj: (0, qi[s])),
                  pl.BlockSpec((2 * tb, aw), lambda s, qi, kj: (kj[s], 0)),
                  pl.BlockSpec((aw, 2 * tb), lambda s, qi, kj: (0, kj[s])),
                  pl.BlockSpec((2 * tb, tb), lambda s, qi, kj: (kj[s], qi[s])),
                  pl.BlockSpec((SUBLANES, tb), lambda s, qi, kj: (0, qi[s])),
                  bias_spec_a, bias_spec_b],
        out_specs=pl.BlockSpec((tb, aw), lambda s, qi, kj: (qi[s], 0)),
        scratch_shapes=[pltpu.VMEM((heads, SUBLANES, tb), F32),
                        pltpu.VMEM((heads, SUBLANES, tb), F32),
                        pltpu.VMEM((aw, tb), F32),
                        pltpu.VMEM((2 * tb, tb), F32)])
    return pl.pallas_call(
        functools.partial(_attn_kernel, heads=heads, dh=dh, scale2=dh ** -0.5 * LOG2E),
        grid_spec=grid_spec, out_shape=jax.ShapeDtypeStruct((L, aw), BF16), name="dsa_attention",
        compiler_params=_cp("arbitrary"),
    )(qi, kj, far_bias, q_t, k, v_t, keys, thr, bias_tiles, bias_tiles)


def _router_kernel(h_ref, w_ref, b_ref, g_ref, *, n_exp, top_k):
    logits = jnp.dot(h_ref[...], w_ref[...], preferred_element_type=F32) + b_ref[...]
    lane = lax.broadcasted_iota(I32, logits.shape, 1).astype(F32)
    work = jnp.where(lane < n_exp, logits, -jnp.inf)
    gates = jnp.zeros(logits.shape, F32)
    denom = None
    top = None
    for r in range(top_k):
        m = jnp.max(work, axis=1, keepdims=True)
        first = jnp.min(jnp.where(work == m, lane, float(LANES)), axis=1, keepdims=True)
        sel = lane == first
        if r == 0:
            top = m
        e = jnp.exp(m - top)
        gates = gates + jnp.where(sel, e, 0.0)
        denom = e if denom is None else denom + e
        work = jnp.where(sel, -jnp.inf, work)
    g_ref[...] = gates / denom


def _router(h, w_pad, b_pad, n_exp, tm=512):
    m, d = h.shape
    tm = _tile(m, tm)
    return pl.pallas_call(
        functools.partial(_router_kernel, n_exp=n_exp, top_k=TOP_K_EXPERTS), grid=(m // tm,),
        in_specs=[pl.BlockSpec((tm, d), lambda i: (i, 0)),
                  pl.BlockSpec((d, LANES), lambda i: (0, 0)),
                  pl.BlockSpec((1, LANES), lambda i: (0, 0))],
        out_specs=pl.BlockSpec((tm, LANES), lambda i: (i, 0)),
        out_shape=jax.ShapeDtypeStruct((m, LANES), F32), name="moe_router",
        compiler_params=_cp("parallel"),
    )(h, w_pad, b_pad)


def _moe_kernel(h_ref, g_ref, wgu_ref, bgu_ref, wdn_ref, bdn_ref, o_ref, *, ff, eb):
    step = pl.program_id(1)
    gates = g_ref[...]

    @pl.when(step == 0)
    def _():
        o_ref[...] = jnp.dot(gates.astype(BF16), bdn_ref[...], preferred_element_type=F32)

    lane = lax.broadcasted_iota(I32, gates.shape, 1)
    for k in range(eb):
        gu = jnp.dot(h_ref[...], wgu_ref[k], preferred_element_type=F32) + bgu_ref[k]
        gate = jnp.minimum(gu[:, :ff], SWIGLU_LIMIT)
        up = jnp.clip(gu[:, ff:], -SWIGLU_LIMIT, SWIGLU_LIMIT)
        act = (up + 1.0) * gate * jax.nn.sigmoid(SWIGLU_ALPHA * gate)
        ge = jnp.sum(jnp.where(lane == step * eb + k, gates, 0.0), axis=1, keepdims=True)
        o_ref[...] += jnp.dot((act * ge).astype(BF16), wdn_ref[k], preferred_element_type=F32)


def _moe_experts(h, gates, w_gu, b_gu, w_dn, b_dn_pad, layer, tm=512, eb=2):
    m, d = h.shape
    _, n_exp, _, ff2 = w_gu.shape
    tm = _tile(m, tm)
    eb = _tile(n_exp, eb)
    return pl.pallas_call(
        functools.partial(_moe_kernel, ff=ff2 // 2, eb=eb), grid=(m // tm, n_exp // eb),
        in_specs=[pl.BlockSpec((tm, d), lambda i, e: (i, 0)),
                  pl.BlockSpec((tm, LANES), lambda i, e: (i, 0)),
                  pl.BlockSpec((None, eb, d, ff2), lambda i, e: (layer, e, 0, 0)),
                  pl.BlockSpec((eb, 1, ff2), lambda i, e: (e, 0, 0)),
                  pl.BlockSpec((None, eb, ff2 // 2, d), lambda i, e: (layer, e, 0, 0)),
                  pl.BlockSpec((LANES, d), lambda i, e: (0, 0))],
        out_specs=pl.BlockSpec((tm, d), lambda i, e: (i, 0)),
        out_shape=jax.ShapeDtypeStruct((m, d), F32), name="moe_experts",
        compiler_params=_cp("parallel", "arbitrary"),
    )(h, gates, w_gu, b_gu.reshape(n_exp, 1, ff2), w_dn, b_dn_pad)


def kernel(x, c, w_in, w_a_out, w_b_out, w_o, w_glu, b_glu, s5_lam_re, s5_lam_im, s5_b_re, s5_b_im, s5_c_re, s5_c_im, s5_d, s5_log_step, rel_bias, w_router, b_router, w_gu, b_gu, w_dn, b_dn, w_c1, b_c1, w_c2, b_c2, mod_table, ln_g, ln_b):
    bsz, L, d = x.shape
    assert bsz == 1 and c.shape[0] == 1
    depth = w_in.shape[0]
    g, p, h = s5_b_re.shape[1:]
    s5w = g * h
    heads = rel_bias.shape[1]
    aw = heads * ATT_HEAD_DIM
    iw = IDX_HEADS * IDX_DIM
    n_exp = w_router.shape[2]
    assert w_in.shape[2] == s5w + 3 * aw + iw + IDX_DIM + IDX_HEADS + 2 * d
    assert n_exp <= LANES and IDX_DIM + IDX_HEADS <= LANES and 2 * IDX_DIM == LANES
    o_q, o_k, o_v, o_qi = s5w, s5w + aw, s5w + 2 * aw, s5w + 3 * aw
    o_ki = s5w + 3 * aw + iw
    o_gate = o_ki + IDX_DIM + IDX_HEADS
    topk = min(TOPK_MAX, L // 4)
    tb = _tile(L, 256)
    ck = _tile(L, LANES)
    dn_alpha = (2 * depth) ** 0.25

    mod = _conditioning(c, w_c1, b_c1, w_c2, b_c2, mod_table).reshape(depth, N_MOD, d)
    bias_tiles = _bias_tiles(rel_bias, tb)
    far_bias = rel_bias[-1] * LOG2E
    assert o_ki % LANES == 0 and (g * h) % LANES == 0 and LANES % h == 0

    w_proj = w_in.astype(BF16)
    w_gu_bf = w_gu.astype(BF16)
    w_dn_bf = w_dn.astype(BF16)
    s5_tables = jax.vmap(functools.partial(_s5_discretize, t_chunk=S5_CHUNK))(
        s5_lam_re, s5_lam_im, s5_b_re, s5_b_im, s5_c_re, s5_c_im, s5_d, s5_log_step)

    xs = x.reshape(L, d)
    hcur = _modulate(xs, mod[0], sc_row=1, sh_row=0)
    for l in range(depth):
        u = _matmul(hcur, w_proj, layer=l, out_dtype=F32, col_off=0, n_cols=s5w, name="proj_u")
        k = _matmul(hcur, w_proj, layer=l, out_dtype=BF16, col_off=o_k, n_cols=aw, name="proj_k")
        gates_ab = _matmul_shifted(hcur, w_proj, layer=l, out_dtype=F32, col_off=o_gate, n_cols=2 * d,
                                   name="proj_gates")
        misc = _matmul(hcur, w_proj, layer=l, out_dtype=F32, col_off=o_ki, n_cols=LANES, tn=LANES, name="proj_misc")
        q_t = _matmul_t(hcur, w_proj, layer=l, out_dtype=BF16, col_off=o_q, n_cols=aw, name="proj_q_t")
        vqi_t = _matmul_t(hcur, w_proj, layer=l, out_dtype=BF16, col_off=o_v, n_cols=aw + iw,
                          name="proj_vqi_t")

        y_s5 = _s5_branch(u, tuple(table[l] for table in s5_tables))
        ya_in = _glu(y_s5, w_glu, b_glu[l], l)

        k_idx = misc[:, :IDX_DIM].astype(BF16)
        zeros = jnp.zeros_like(k_idx)
        k_even = jnp.concatenate([k_idx, zeros], axis=1).reshape(L // ck, ck, LANES)
        k_odd = jnp.concatenate([zeros, k_idx], axis=1).reshape(L // ck, ck, LANES)
        k2 = jnp.concatenate([k_even, k_odd], axis=1)
        w_t = jnp.transpose(misc[:, IDX_DIM:IDX_DIM + IDX_HEADS])
        keys, thr = _indexer(vqi_t, aw, k2, w_t, topk)
        attn = _attention(q_t, vqi_t, k, keys, thr, bias_tiles, far_bias, tb)

        merged = _merge(ya_in, attn, w_a_out, w_b_out, gates_ab, l)
        y = _matmul(merged, w_o, layer=l, out_dtype=F32, name="proj_out")
        xs, h2 = _ln_modulate(xs, y, mod[l], ln_g[l], ln_b[l], mod[l], alpha=dn_alpha, gate_row=2, ln_row=0,
                              sc_row=4, sh_row=3, want_h=True)

        w_r = jnp.pad(w_router[l], ((0, 0), (0, LANES - n_exp))).astype(BF16)
        b_r = jnp.pad(b_router[l], (0, LANES - n_exp)).reshape(1, LANES)
        gates = _router(h2, w_r, b_r, n_exp)
        b_dn_pad = jnp.pad(b_dn[l], ((0, LANES - n_exp), (0, 0))).astype(BF16)
        y = _moe_experts(h2, gates, w_gu_bf, b_gu[l], w_dn_bf, b_dn_pad, l)
        last = l == depth - 1
        xs, hcur = _ln_modulate(xs, y, mod[l], ln_g[l], ln_b[l], mod[min(l + 1, depth - 1)], alpha=dn_alpha,
                                gate_row=5, ln_row=1, sc_row=1, sh_row=0, want_h=not last)
    return xs.reshape(bsz, L, d)
```
